```python
import math
import jax
import jax.numpy as jnp
from jax import lax
import numpy as np

D_MODEL = 2048
BATCH = 4
SEQ = 2048
DEPTH = 2

EPS = 1e-6
EXPAND = 2
MIX = EXPAND * D_MODEL
N_EVEN = (DEPTH + 1) // 2
N_ODD = DEPTH // 2

CONV_W = MIX // 2
CONV_K = 3

HEAD_DIM = 128
N_HEADS = (MIX // 2) // HEAD_DIM
N_KV = 4
GQA = N_HEADS // N_KV
ROT_DIM = HEAD_DIM // 4
ROPE_THETA = 500000.0
CMP_LEN = 32
CMP_STRIDE = 16
SLC_LEN = 64
N_SEL = 8
WINDOW = 512
Q_BLOCK = 64

L0_SIZES = (CONV_W, CONV_W, CONV_W, CONV_W,
            N_HEADS * HEAD_DIM,
            N_KV * HEAD_DIM, N_KV * HEAD_DIM, N_KV * HEAD_DIM,
            N_KV * HEAD_DIM, N_KV * HEAD_DIM, N_KV * HEAD_DIM,
            3 * N_HEADS,
            N_HEADS * HEAD_DIM)
L0_IN = 4 * CONV_W + 2 * N_HEADS * HEAD_DIM + 6 * N_KV * HEAD_DIM + 3 * N_HEADS

SGU_W = MIX
CHUNK = 128
N_GROUPS = 16
GROUP_W = SGU_W // N_GROUPS

kernel_name = "hybrid_conv_nsa_chunked_gmlp"


def rms_norm(x, g):
    x32 = x.astype(jnp.float32)
    y = x32 * lax.rsqrt(jnp.mean(x32 * x32, axis=-1, keepdims=True) + EPS)
    return (y * g.astype(jnp.float32)).astype(x.dtype)


def layer_norm(x, g, b):
    x32 = x.astype(jnp.float32)
    mu = jnp.mean(x32, axis=-1, keepdims=True)
    xc = x32 - mu
    y = xc * lax.rsqrt(jnp.mean(xc * xc, axis=-1, keepdims=True) + EPS)
    return (y * g.astype(jnp.float32) + b.astype(jnp.float32)).astype(x.dtype)


def masked_softmax(s, mask):
    s = jnp.where(mask, s.astype(jnp.float32), -jnp.inf)
    m = jnp.max(s, axis=-1, keepdims=True)
    m = jnp.where(jnp.isfinite(m), m, 0.0)
    e = jnp.exp(s - m)
    den = jnp.sum(e, axis=-1, keepdims=True)
    return e / jnp.where(den > 0, den, 1.0)


def partial_rotary(x, pos):
    half = ROT_DIM // 2
    inv_freq = jnp.power(ROPE_THETA, -jnp.arange(half, dtype=jnp.float32) * 2.0 / ROT_DIM)
    ang = pos.astype(jnp.float32)[:, None] * inv_freq[None, :]
    cos = jnp.cos(ang).astype(x.dtype)
    sin = jnp.sin(ang).astype(x.dtype)
    x1 = x[..., :half]
    x2 = x[..., half:ROT_DIM]
    return jnp.concatenate([x1 * cos - x2 * sin, x2 * cos + x1 * sin, x[..., ROT_DIM:]], axis=-1)


def split_cols(z, sizes):
    offs = []
    acc = 0
    for s in sizes[:-1]:
        acc += s
        offs.append(acc)
    return jnp.split(z, offs, axis=-1)


def short_conv(h, w):
    return lax.conv_general_dilated(
        h, w[:, None, :].astype(h.dtype), window_strides=(1,),
        padding=[(CONV_K - 1, 0)], dimension_numbers=("NWC", "WIO", "NWC"),
        feature_group_count=h.shape[-1])


def compress(k, pos_emb, w1, b1, w2):
    T = k.shape[2]
    n_cmp = (T - CMP_LEN) // CMP_STRIDE + 1
    idx = jnp.arange(n_cmp)[:, None] * CMP_STRIDE + jnp.arange(CMP_LEN)[None, :]
    blocks = k[:, :, idx] + pos_emb
    flat = blocks.reshape(blocks.shape[0], blocks.shape[1], n_cmp, CMP_LEN * HEAD_DIM)
    return jax.nn.silu(flat @ w1 + b1) @ w2


def nsa_mixer(q, k_cmp, v_cmp, k_slc, v_slc, k_win, v_win, gate_logits,
              ck_pos, ck_w1, ck_b1, ck_w2, cv_pos, cv_w1, cv_b1, cv_w2):
    Bsz, T = q.shape[0], q.shape[1]
    pos = jnp.arange(T)
    scale = HEAD_DIM ** -0.5
    q = q.reshape(Bsz, T, N_KV, GQA, HEAD_DIM).transpose(0, 2, 3, 1, 4)

    def kv(a):
        return a.reshape(Bsz, T, N_KV, HEAD_DIM).transpose(0, 2, 1, 3)

    k_cmp, v_cmp, k_slc, v_slc, k_win, v_win = (kv(a) for a in (k_cmp, v_cmp, k_slc, v_slc, k_win, v_win))
    q_rot = partial_rotary(q, pos)
    k_slc = partial_rotary(k_slc, pos)
    k_win = partial_rotary(k_win, pos)

    kc = compress(k_cmp, ck_pos, ck_w1, ck_b1, ck_w2)
    vc = compress(v_cmp, cv_pos, cv_w1, cv_b1, cv_w2)
    n_cmp = kc.shape[2]
    cmp_start = jnp.arange(n_cmp) * CMP_STRIDE
    cmp_mask = (cmp_start + CMP_LEN - 1)[None, :] <= pos[:, None]
    s_c = jnp.einsum("bgntd,bgcd->bgntc", q, kc) * scale
    p_c = masked_softmax(s_c, cmp_mask)
    o_c = jnp.einsum("bgntc,bgcd->bgntd", p_c.astype(vc.dtype), vc)

    n_blk = T // SLC_LEN
    n_sel = min(N_SEL, n_blk)
    blk = jnp.arange(n_blk)
    blk_start = blk * SLC_LEN
    overlap = ((cmp_start[:, None] < blk_start[None, :] + SLC_LEN)
               & (cmp_start[:, None] + CMP_LEN > blk_start[None, :])).astype(jnp.float32)
    imp = jnp.einsum("bgtc,cj->bgtj", jnp.sum(p_c, axis=2), overlap)
    forced = (blk[None, :] == 0) | (blk[None, :] == (pos // SLC_LEN)[:, None])
    causal_blk = blk_start[None, :] <= pos[:, None]
    imp = jnp.where(forced, jnp.inf, jnp.where(causal_blk, imp, -jnp.inf))
    sel_val, sel_idx = lax.top_k(imp, n_sel)
    sel_ok = sel_val > -jnp.inf

    n_qb = T // Q_BLOCK
    q_blocks = jnp.moveaxis(q_rot.reshape(Bsz, N_KV, GQA, n_qb, Q_BLOCK, HEAD_DIM), 3, 0)
    idx_blocks = jnp.moveaxis(sel_idx.reshape(Bsz, N_KV, n_qb, Q_BLOCK, n_sel), 2, 0)
    ok_blocks = jnp.moveaxis(sel_ok.reshape(Bsz, N_KV, n_qb, Q_BLOCK, n_sel), 2, 0)

    kb = k_slc.reshape(Bsz, N_KV, n_blk, SLC_LEN, HEAD_DIM)
    vb = v_slc.reshape(Bsz, N_KV, n_blk, SLC_LEN, HEAD_DIM)
    take = jax.vmap(jax.vmap(lambda a, i: a[i]))
    k_win_pad = jnp.pad(k_win, ((0, 0), (0, 0), (WINDOW, 0), (0, 0)))
    v_win_pad = jnp.pad(v_win, ((0, 0), (0, 0), (WINDOW, 0), (0, 0)))
    offs = jnp.arange(SLC_LEN)
    n_keys = n_sel * SLC_LEN

    def block_step(args):
        qb, ib, okb, i = args
        start = i * Q_BLOCK
        tb = start + jnp.arange(Q_BLOCK)
        kg = take(kb, ib).reshape(Bsz, N_KV, Q_BLOCK, n_keys, HEAD_DIM)
        vg = take(vb, ib).reshape(Bsz, N_KV, Q_BLOCK, n_keys, HEAD_DIM)
        kpos = (ib[..., None] * SLC_LEN + offs).reshape(Bsz, N_KV, Q_BLOCK, n_keys)
        okk = jnp.broadcast_to(okb[..., None], okb.shape + (SLC_LEN,)).reshape(Bsz, N_KV, Q_BLOCK, n_keys)
        smask = (okk & (kpos <= tb[:, None]))[:, :, None]
        s_s = jnp.einsum("bgnqd,bgqmd->bgnqm", qb, kg) * scale
        p_s = masked_softmax(s_s, smask)
        o_s = jnp.einsum("bgnqm,bgqmd->bgnqd", p_s.astype(vg.dtype), vg)
        kw = lax.dynamic_slice_in_dim(k_win_pad, start, WINDOW + Q_BLOCK, axis=2)
        vw = lax.dynamic_slice_in_dim(v_win_pad, start, WINDOW + Q_BLOCK, axis=2)
        wpos = start - WINDOW + jnp.arange(WINDOW + Q_BLOCK)
        diff = tb[:, None] - wpos[None, :]
        wmask = (wpos[None, :] >= 0) & (diff >= 0) & (diff < WINDOW)
        s_w = jnp.einsum("bgnqd,bgkd->bgnqk", qb, kw) * scale
        p_w = masked_softmax(s_w, wmask)
        o_w = jnp.einsum("bgnqk,bgkd->bgnqd", p_w.astype(vw.dtype), vw)
        return (o_s, o_w)

    o_s, o_w = lax.map(block_step, (q_blocks, idx_blocks, ok_blocks, jnp.arange(n_qb)))
    o_s = jnp.moveaxis(o_s, 0, 3).reshape(Bsz, N_KV, GQA, T, HEAD_DIM)
    o_w = jnp.moveaxis(o_w, 0, 3).reshape(Bsz, N_KV, GQA, T, HEAD_DIM)

    g = jax.nn.sigmoid(gate_logits.astype(jnp.float32)).astype(q.dtype)
    g = g.reshape(Bsz, T, 3, N_KV, GQA).transpose(2, 0, 3, 4, 1)[..., None]
    o = g[0] * o_c + g[1] * o_s + g[2] * o_w
    return o.transpose(0, 3, 1, 2, 4).reshape(Bsz, T, N_HEADS * HEAD_DIM)


def conv_nsa_layer(h, w_in, conv_w, ck_pos, ck_w1, ck_b1, ck_w2,
                   cv_pos, cv_w1, cv_b1, cv_w2, w_out):
    z = h @ w_in
    (cb, cc, ch, cg, q, kc, vc, ks, vs, kw, vw, gl, ng) = split_cols(z, L0_SIZES)
    y_conv = cb * short_conv(cc * ch, conv_w) * jax.nn.silu(cg)
    y_nsa = nsa_mixer(q, kc, vc, ks, vs, kw, vw, gl,
                      ck_pos, ck_w1, ck_b1, ck_w2, cv_pos, cv_w1, cv_b1, cv_w2) * jax.nn.silu(ng)
    return jnp.concatenate([y_conv, y_nsa], axis=-1) @ w_out


def chunked_gmlp_layer(h, w_in, ln_g, ln_b, w_s, b_s, w_out):
    Bsz, T = h.shape[0], h.shape[1]
    u, v, zg = jnp.split(h @ w_in, 3, axis=-1)
    v = layer_norm(v, ln_g, ln_b)
    v = v.reshape(Bsz, T // CHUNK, CHUNK, N_GROUPS, GROUP_W)
    tri = jnp.tril(jnp.ones((CHUNK, CHUNK), dtype=bool))
    ws = jnp.where(tri, w_s, 0.0)
    mix = jnp.einsum("hts,bcshd->bcthd", ws, v) + b_s.T[None, None, :, :, None]
    mix = mix.reshape(Bsz, T, SGU_W)
    return (u * mix * jax.nn.silu(zg)) @ w_out


def setup_inputs(seed: int = 0) -> dict:
    key = jax.random.key(seed)
    ks = jax.random.split(key, 24)
    f32 = jnp.float32
    ne, no = N_EVEN, N_ODD

    def nrm(k, shape, scale):
        return jax.random.normal(k, shape, f32) * scale

    return {
        "x": nrm(ks[0], (BATCH, SEQ, D_MODEL), 1.0),
        "norm_even": 1.0 + nrm(ks[1], (ne, D_MODEL), 0.02),
        "w_in_even": nrm(ks[2], (ne, D_MODEL, L0_IN), D_MODEL ** -0.5),
        "conv_w": nrm(ks[3], (ne, CONV_K, CONV_W), CONV_K ** -0.5),
        "cmp_k_pos": nrm(ks[4], (ne, CMP_LEN, HEAD_DIM), 0.1),
        "cmp_k_w1": nrm(ks[5], (ne, CMP_LEN * HEAD_DIM, HEAD_DIM), (CMP_LEN * HEAD_DIM) ** -0.5),
        "cmp_k_b1": nrm(ks[6], (ne, HEAD_DIM), 0.01),
        "cmp_k_w2": nrm(ks[7], (ne, HEAD_DIM, HEAD_DIM), HEAD_DIM ** -0.5),
        "cmp_v_pos": nrm(ks[8], (ne, CMP_LEN, HEAD_DIM), 0.1),
        "cmp_v_w1": nrm(ks[9], (ne, CMP_LEN * HEAD_DIM, HEAD_DIM), (CMP_LEN * HEAD_DIM) ** -0.5),
        "cmp_v_b1": nrm(ks[10], (ne, HEAD_DIM), 0.01),
        "cmp_v_w2": nrm(ks[11], (ne, HEAD_DIM, HEAD_DIM), HEAD_DIM ** -0.5),
        "w_out_even": nrm(ks[12], (ne, MIX, D_MODEL), MIX ** -0.5),
        "norm_odd": 1.0 + nrm(ks[13], (no, D_MODEL), 0.02),
        "w_in_odd": nrm(ks[14], (no, D_MODEL, 3 * SGU_W), D_MODEL ** -0.5),
        "sgu_ln_g": 1.0 + nrm(ks[15], (no, SGU_W), 0.02),
        "sgu_ln_b": nrm(ks[16], (no, SGU_W), 0.01),
        "sgu_w_s": nrm(ks[17], (no, N_GROUPS, CHUNK, CHUNK), CHUNK ** -0.5),
        "sgu_b_s": 1.0 + nrm(ks[18], (no, N_GROUPS, CHUNK), 0.02),
        "w_out_odd": nrm(ks[19], (no, SGU_W, D_MODEL), SGU_W ** -0.5),
        "norm_final": 1.0 + nrm(ks[20], (D_MODEL,), 0.02),
    }


def reference(x, norm_even, w_in_even, conv_w, cmp_k_pos, cmp_k_w1, cmp_k_b1, cmp_k_w2,
              cmp_v_pos, cmp_v_w1, cmp_v_b1, cmp_v_w2, w_out_even,
              norm_odd, w_in_odd, sgu_ln_g, sgu_ln_b, sgu_w_s, sgu_b_s, w_out_odd,
              norm_final):
    for layer in range(DEPTH):
        j = layer // 2
        if layer % 2 == 0:
            h = rms_norm(x, norm_even[j])
            x = x + conv_nsa_layer(h, w_in_even[j], conv_w[j],
                                   cmp_k_pos[j], cmp_k_w1[j], cmp_k_b1[j], cmp_k_w2[j],
                                   cmp_v_pos[j], cmp_v_w1[j], cmp_v_b1[j], cmp_v_w2[j],
                                   w_out_even[j])
        else:
            h = rms_norm(x, norm_odd[j])
            x = x + chunked_gmlp_layer(h, w_in_odd[j], sgu_ln_g[j], sgu_ln_b[j],
                                       sgu_w_s[j], sgu_b_s[j], w_out_odd[j])
    return rms_norm(x, norm_final)
```

```python
import functools
import math

import jax
import jax.numpy as jnp
from jax import lax
from jax.experimental import pallas as pl
from jax.experimental.pallas import tpu as pltpu

F32 = jnp.float32
BF16 = jnp.bfloat16

D_MODEL = 2048
MIX = 2 * D_MODEL
CONV_W = MIX // 2
CONV_K = 3
HEAD_DIM = 128
N_HEADS = 16
N_KV = 4
GQA = N_HEADS // N_KV
ROT_DIM = HEAD_DIM // 4
ROPE_THETA = 500000.0
CMP_LEN = 32
CMP_STRIDE = 16
SLC_LEN = 64
N_SEL = 8
WINDOW = 512
SGU_W = MIX
CHUNK = 128
N_GROUPS = 16
GROUP_W = SGU_W // N_GROUPS
EPS = 1e-6

LANES = 128
HALO = 16
VMEM_LIMIT = 56 * 1024 * 1024

NEG = -1e30


def _cparams(n_axes):
    return pltpu.CompilerParams(
        dimension_semantics=("arbitrary",) * n_axes, vmem_limit_bytes=VMEM_LIMIT)


def _rmsnorm_kernel(x_ref, g_ref, o_ref):
    x = x_ref[...]
    y = x * lax.rsqrt(jnp.mean(x * x, axis=-1, keepdims=True) + EPS)
    o_ref[...] = (y * g_ref[...]).astype(o_ref.dtype)


def _rmsnorm_proj_kernel(x_ref, g_ref, w_ref, o_ref, p_ref):
    x = x_ref[...]
    y = x * lax.rsqrt(jnp.mean(x * x, axis=-1, keepdims=True) + EPS)
    hb = (y * g_ref[...]).astype(BF16)
    o_ref[...] = hb
    p_ref[...] = jnp.dot(hb, w_ref[...], preferred_element_type=F32)


def rmsnorm(x2d, g, out_dtype, tm=512):
    m, d = x2d.shape
    return pl.pallas_call(
        _rmsnorm_kernel,
        out_shape=jax.ShapeDtypeStruct((m, d), out_dtype),
        grid=(m // tm,),
        in_specs=[pl.BlockSpec((tm, d), lambda i: (i, 0)),
                  pl.BlockSpec((1, d), lambda i: (0, 0))],
        out_specs=pl.BlockSpec((tm, d), lambda i: (i, 0)),
        compiler_params=_cparams(1),
        name="rmsnorm",
    )(x2d, g.reshape(1, d))


def rmsnorm_proj(x2d, g, w, tm=512):
    m, d = x2d.shape
    n = w.shape[1]
    return pl.pallas_call(
        _rmsnorm_proj_kernel,
        out_shape=(jax.ShapeDtypeStruct((m, d), BF16), jax.ShapeDtypeStruct((m, n), F32)),
        grid=(m // tm,),
        in_specs=[pl.BlockSpec((tm, d), lambda i: (i, 0)),
                  pl.BlockSpec((1, d), lambda i: (0, 0)),
                  pl.BlockSpec((d, n), lambda i: (0, 0))],
        out_specs=(pl.BlockSpec((tm, d), lambda i: (i, 0)),
                   pl.BlockSpec((tm, n), lambda i: (i, 0))),
        compiler_params=_cparams(1),
        name="rmsnorm_gates",
    )(x2d, g.reshape(1, d), w)


def _matmul_kernel(a_ref, w_ref, o_ref):
    o_ref[...] = jnp.dot(a_ref[...], w_ref[...], preferred_element_type=F32)


def matmul(a, w, tm=1024, tn=512, name="proj"):
    m, k = a.shape
    n = w.shape[1]
    return pl.pallas_call(
        _matmul_kernel,
        out_shape=jax.ShapeDtypeStruct((m, n), F32),
        grid=(m // tm, n // tn),
        in_specs=[pl.BlockSpec((tm, k), lambda i, j: (i, 0)),
                  pl.BlockSpec((k, tn), lambda i, j: (0, j))],
        out_specs=pl.BlockSpec((tm, tn), lambda i, j: (i, j)),
        compiler_params=_cparams(2),
        name=name,
    )(a, w)


def _conv_proj_kernel(h_ref, hp_ref, wb_ref, wc_ref, wh_ref, wg_ref, cw_ref, o_ref, *, tiles_per_seq):
    i = pl.program_id(0)
    h = h_ref[...]
    cb = jnp.dot(h, wb_ref[...], preferred_element_type=F32)
    cc = jnp.dot(h, wc_ref[...], preferred_element_type=F32)
    ch = jnp.dot(h, wh_ref[...], preferred_element_type=F32)
    cg = jnp.dot(h, wg_ref[...], preferred_element_type=F32)
    p = cc * ch
    hp = hp_ref[...]
    pp = (jnp.dot(hp, wc_ref[...], preferred_element_type=F32)
          * jnp.dot(hp, wh_ref[...], preferred_element_type=F32))
    pp = jnp.where(i % tiles_per_seq == 0, 0.0, pp)
    row = lax.broadcasted_iota(jnp.int32, p.shape, 0)
    p1 = pltpu.roll(p, 1, axis=0)
    p2 = pltpu.roll(p, 2, axis=0)
    p1 = jnp.where(row == 0, pp[HALO - 1:HALO, :], p1)
    p2 = jnp.where(row == 0, pp[HALO - 2:HALO - 1, :],
                   jnp.where(row == 1, pp[HALO - 1:HALO, :], p2))
    cw = cw_ref[...]
    conv = cw[0:1, :] * p2 + cw[1:2, :] * p1 + cw[2:3, :] * p
    o_ref[...] = (cb * conv * (cg * jax.nn.sigmoid(cg))).astype(o_ref.dtype)


def conv_proj(h, wb, wc, wh, wg, conv_w, seq, tm=1024, tn=256):
    m, k = h.shape
    n = wb.shape[1]
    tiles_per_seq = seq // tm
    halo_per_tile = tm // HALO
    wspec = pl.BlockSpec((k, tn), lambda i, j: (0, j))
    return pl.pallas_call(
        functools.partial(_conv_proj_kernel, tiles_per_seq=tiles_per_seq),
        out_shape=jax.ShapeDtypeStruct((m, n), BF16),
        grid=(m // tm, n // tn),
        in_specs=[pl.BlockSpec((tm, k), lambda i, j: (i, 0)),
                  pl.BlockSpec((HALO, k), lambda i, j: (jnp.maximum(i * halo_per_tile - 1, 0), 0)),
                  wspec, wspec, wspec, wspec,
                  pl.BlockSpec((CONV_K, tn), lambda i, j: (0, j))],
        out_specs=pl.BlockSpec((tm, tn), lambda i, j: (i, j)),
        compiler_params=_cparams(2),
        name="conv_proj",
    )(h, h, wb, wc, wh, wg, conv_w)


def _rotary(x, c, sa, sb):
    half = ROT_DIM // 2
    return x * c + pltpu.roll(x, half, axis=1) * sa + pltpu.roll(x, LANES - half, axis=1) * sb


def _compress(src_ref, pos_ref, w1_ref, b1_ref, w2_ref):
    n_rows = src_ref.shape[0] // CMP_STRIDE
    p_acc = jnp.zeros((n_rows, HEAD_DIM), F32)
    q_acc = jnp.zeros((n_rows, HEAD_DIM), F32)
    for r in range(CMP_STRIDE):
        s_r = src_ref[pl.ds(r, n_rows, stride=CMP_STRIDE), :]
        a_r = (s_r + pos_ref[r:r + 1, :]).astype(BF16)
        b_r = (s_r + pos_ref[CMP_STRIDE + r:CMP_STRIDE + r + 1, :]).astype(BF16)
        p_acc += jnp.dot(a_r, w1_ref[r], preferred_element_type=F32)
        q_acc += jnp.dot(b_r, w1_ref[CMP_STRIDE + r], preferred_element_type=F32)
    hid = p_acc + pltpu.roll(q_acc, n_rows - 1, axis=0) + b1_ref[...]
    act = (hid * jax.nn.sigmoid(hid)).astype(BF16)
    return jnp.dot(act, w2_ref[...], preferred_element_type=F32)


def _nsa_prep_kernel(kc_ref, vc_ref, ks_ref, vs_ref, kw_ref, vw_ref, c_ref, sa_ref, sb_ref,
                     kpos_ref, kw1_ref, kb1_ref, kw2_ref, vpos_ref, vw1_ref, vb1_ref, vw2_ref,
                     ksr_ref, vsb_ref, kwr_ref, vwb_ref, kcmp_ref, vcmp_ref):
    c, sa, sb = c_ref[...], sa_ref[...], sb_ref[...]
    ksr_ref[...] = _rotary(ks_ref[...], c, sa, sb).astype(BF16)
    kwr_ref[...] = _rotary(kw_ref[...], c, sa, sb).astype(BF16)
    vsb_ref[...] = vs_ref[...].astype(BF16)
    vwb_ref[...] = vw_ref[...].astype(BF16)
    kcmp_ref[...] = _compress(kc_ref, kpos_ref, kw1_ref, kb1_ref, kw2_ref).astype(BF16)
    vcmp_ref[...] = _compress(vc_ref, vpos_ref, vw1_ref, vb1_ref, vw2_ref).astype(BF16)


def nsa_prep(z, tabs, cmp_k, cmp_v, batch, seq, col0):
    def zspec(which):
        return pl.BlockSpec((seq, HEAD_DIM), lambda b, g, w=which: (b, col0 + w * N_KV + g))

    tab = pl.BlockSpec((seq, HEAD_DIM), lambda b, g: (0, 0))

    def wspecs():
        return [pl.BlockSpec((CMP_LEN, HEAD_DIM), lambda b, g: (0, 0)),
                pl.BlockSpec((CMP_LEN, HEAD_DIM, HEAD_DIM), lambda b, g: (0, 0, 0)),
                pl.BlockSpec((1, HEAD_DIM), lambda b, g: (0, 0)),
                pl.BlockSpec((HEAD_DIM, HEAD_DIM), lambda b, g: (0, 0))]

    kv_out = pl.BlockSpec((None, None, seq, HEAD_DIM), lambda b, g: (b, g, 0, 0))
    n_cmp_rows = seq // CMP_STRIDE
    cmp_out = pl.BlockSpec((None, None, n_cmp_rows, HEAD_DIM), lambda b, g: (b, g, 0, 0))
    kv_shape = jax.ShapeDtypeStruct((batch, N_KV, seq, HEAD_DIM), BF16)
    cmp_shape = jax.ShapeDtypeStruct((batch, N_KV, n_cmp_rows, HEAD_DIM), BF16)
    return pl.pallas_call(
        _nsa_prep_kernel,
        out_shape=(kv_shape, kv_shape, kv_shape, kv_shape, cmp_shape, cmp_shape),
        grid=(batch, N_KV),
        in_specs=[zspec(0), zspec(1), zspec(2), zspec(3), zspec(4), zspec(5), tab, tab, tab]
        + wspecs() + wspecs(),
        out_specs=(kv_out, kv_out, kv_out, kv_out, cmp_out, cmp_out),
        compiler_params=_cparams(2),
        name="nsa_prep",
    )(z, z, z, z, z, z, *tabs, *cmp_k, *cmp_v)


def _nsa_attn_kernel(q_ref, ng_ref, gl_ref, c_ref, sa_ref, sb_ref, ov_ref,
                     ksr_ref, vsb_ref, kwr_ref, vwb_ref, kcmp_ref, vcmp_ref,
                     o_ref, m_ref, l_ref, acc_ref, *, tq, tk):
    g = pl.program_id(1)
    i = pl.program_id(2)
    q0 = i * tq
    rows = GQA * tq
    scale = HEAD_DIM ** -0.5
    nt = (((1,), (1,)), ((), ()))

    q = q_ref[...]
    heads = [q[:, n * HEAD_DIM:(n + 1) * HEAD_DIM] for n in range(GQA)]
    q_all = jnp.concatenate(heads, axis=0)
    c, sa, sb = c_ref[...], sa_ref[...], sb_ref[...]
    qr_all = jnp.concatenate([_rotary(hd, c, sa, sb) for hd in heads], axis=0).astype(BF16)

    tpos_q = q0 + lax.broadcasted_iota(jnp.int32, (tq, 1), 0)
    tpos = jnp.concatenate([tpos_q] * GQA, axis=0)

    n_cmp = kcmp_ref.shape[0]
    s_c = lax.dot_general(q_all.astype(BF16), kcmp_ref[...], nt, preferred_element_type=F32) * scale
    cidx = lax.broadcasted_iota(jnp.int32, (rows, n_cmp), 1)
    valid_c = (cidx * CMP_STRIDE + (CMP_LEN - 1)) <= tpos
    s_c = jnp.where(valid_c, s_c, NEG)
    m_c = jnp.max(s_c, axis=-1, keepdims=True)
    e_c = jnp.where(valid_c, jnp.exp(s_c - m_c), 0.0)
    den_c = jnp.sum(e_c, axis=-1, keepdims=True)
    p_c = e_c / jnp.where(den_c > 0, den_c, 1.0)
    o_c = jnp.dot(p_c.astype(BF16), vcmp_ref[...], preferred_element_type=F32)

    p_sum = p_c[0:tq]
    for n in range(1, GQA):
        p_sum = p_sum + p_c[n * tq:(n + 1) * tq]
    n_blk = ov_ref.shape[0]
    imp = lax.dot_general(ov_ref[...], p_sum, nt, preferred_element_type=F32,
                          precision=lax.Precision.HIGHEST)
    jblk = lax.broadcasted_iota(jnp.int32, (n_blk, tq), 0)
    tlane = q0 + lax.broadcasted_iota(jnp.int32, (n_blk, tq), 1)
    forced = (jblk == 0) | (jblk == (tlane >> 6))
    imp = jnp.where(forced, jnp.inf, jnp.where(jblk * SLC_LEN <= tlane, imp, -jnp.inf))
    cnt = jnp.zeros((n_blk, tq), F32)
    for r in range(n_blk):
        row = imp[r:r + 1, :]
        tie = jnp.where(jblk > r, 1.0, 0.0)
        cnt = cnt + jnp.where(row > imp, 1.0, jnp.where(row == imp, tie, 0.0))
    sel_t = jnp.where(cnt < N_SEL, jnp.where(imp > -jnp.inf, 1.0, 0.0), 0.0)
    sel_pad = jnp.concatenate([sel_t, jnp.zeros((LANES - n_blk, tq), F32)], axis=0)
    sel = sel_pad.T.astype(BF16)

    m_ref[...] = jnp.full(m_ref.shape, NEG, F32)
    l_ref[...] = jnp.zeros(l_ref.shape, F32)
    acc_ref[...] = jnp.zeros(acc_ref.shape, F32)
    n_chunks = (q0 + tq + tk - 1) // tk

    def sel_body(ci, carry):
        k0 = pl.multiple_of(ci * tk, tk)
        kblk = ksr_ref[pl.ds(k0, tk), :]
        vblk = vsb_ref[pl.ds(k0, tk), :]
        kpos = k0 + lax.broadcasted_iota(jnp.int32, (LANES, tk), 1)
        bsel = lax.broadcasted_iota(jnp.int32, (LANES, tk), 0)
        expand = jnp.where((kpos >> 6) == bsel, 1.0, 0.0).astype(BF16)
        maskf = jnp.dot(sel, expand, preferred_element_type=F32)
        kpos_q = k0 + lax.broadcasted_iota(jnp.int32, (tq, tk), 1)
        bias = jnp.where(kpos_q <= tpos_q, (maskf - 1.0) * (-NEG), NEG)
        s = lax.dot_general(qr_all, kblk, nt, preferred_element_type=F32) * scale
        s = s + jnp.concatenate([bias] * GQA, axis=0)
        m_old = m_ref[...]
        m_new = jnp.maximum(m_old, jnp.max(s, axis=-1, keepdims=True))
        alpha = jnp.exp(m_old - m_new)
        p = jnp.exp(s - m_new)
        l_ref[...] = alpha * l_ref[...] + jnp.sum(p, axis=-1, keepdims=True)
        acc_ref[...] = alpha * acc_ref[...] + jnp.dot(p.astype(BF16), vblk, preferred_element_type=F32)
        m_ref[...] = m_new
        return carry

    lax.fori_loop(0, n_chunks, sel_body, 0)
    o_s = acc_ref[...] / l_ref[...]

    wlen = WINDOW + tq
    w0 = pl.multiple_of(jnp.maximum(q0 - WINDOW, 0), tq)
    kwin = kwr_ref[pl.ds(w0, wlen), :]
    vwin = vwb_ref[pl.ds(w0, wlen), :]
    s_w = lax.dot_general(qr_all, kwin, nt, preferred_element_type=F32) * scale
    dist = tpos - (w0 + lax.broadcasted_iota(jnp.int32, (rows, wlen), 1))
    valid_w = (dist >= 0) & (dist < WINDOW)
    s_w = jnp.where(valid_w, s_w, NEG)
    m_w = jnp.max(s_w, axis=-1, keepdims=True)
    e_w = jnp.exp(s_w - m_w)
    o_w = jnp.dot(e_w.astype(BF16), vwin, preferred_element_type=F32) / jnp.sum(e_w, axis=-1, keepdims=True)

    gates = jax.nn.sigmoid(gl_ref[...])
    lane = lax.broadcasted_iota(jnp.int32, gates.shape, 1)
    ng = ng_ref[...]
    for n in range(GQA):
        head = g * GQA + n

        def gate(j):
            return jnp.sum(jnp.where(lane == j * N_HEADS + head, gates, 0.0), axis=-1, keepdims=True)

        sl = slice(n * tq, (n + 1) * tq)
        o = gate(0) * o_c[sl] + gate(1) * o_s[sl] + gate(2) * o_w[sl]
        ngh = ng[:, n * HEAD_DIM:(n + 1) * HEAD_DIM]
        o_ref[:, n * HEAD_DIM:(n + 1) * HEAD_DIM] = (o * (ngh * jax.nn.sigmoid(ngh))).astype(o_ref.dtype)


def nsa_attention(z, gl, tabs, overlap_t, prep, batch, seq, ng_col, tq=128, tk=256):
    ksr, vsb, kwr, vwb, kcmp, vcmp = prep
    nq = seq // tq
    rows = GQA * tq
    qw = GQA * HEAD_DIM
    kv = pl.BlockSpec((None, None, seq, HEAD_DIM), lambda b, g, i: (b, g, 0, 0))
    cmp = pl.BlockSpec((None, None, kcmp.shape[2], HEAD_DIM), lambda b, g, i: (b, g, 0, 0))
    tab = pl.BlockSpec((tq, HEAD_DIM), lambda b, g, i: (i, 0))
    return pl.pallas_call(
        functools.partial(_nsa_attn_kernel, tq=tq, tk=tk),
        out_shape=jax.ShapeDtypeStruct((batch * seq, N_HEADS * HEAD_DIM), BF16),
        grid=(batch, N_KV, nq),
        in_specs=[pl.BlockSpec((tq, qw), lambda b, g, i: (b * nq + i, g)),
                  pl.BlockSpec((tq, qw), lambda b, g, i: (b * nq + i, ng_col + g)),
                  pl.BlockSpec((tq, LANES), lambda b, g, i: (b * nq + i, 0)),
                  tab, tab, tab,
                  pl.BlockSpec(overlap_t.shape, lambda b, g, i: (0, 0)),
                  kv, kv, kv, kv, cmp, cmp],
        out_specs=pl.BlockSpec((tq, qw), lambda b, g, i: (b * nq + i, g)),
        scratch_shapes=[pltpu.VMEM((rows, 1), F32), pltpu.VMEM((rows, 1), F32),
                        pltpu.VMEM((rows, HEAD_DIM), F32)],
        compiler_params=_cparams(3),
        name="nsa_attention",
    )(z, z, gl, *tabs, overlap_t, ksr, vsb, kwr, vwb, kcmp, vcmp)


def _out_proj_kernel(a0_ref, a1_ref, w0_ref, w1_ref, x_ref, o_ref):
    acc = jnp.dot(a0_ref[...], w0_ref[...], preferred_element_type=F32)
    acc += jnp.dot(a1_ref[...], w1_ref[...], preferred_element_type=F32)
    o_ref[...] = x_ref[...] + acc


def out_proj(a0, a0_blk, a1, a1_blk, w, x2d, tm=1024, tn=512, name="out_proj"):
    m = x2d.shape[0]
    kh = w.shape[0] // 2
    n = w.shape[1]
    return pl.pallas_call(
        _out_proj_kernel,
        out_shape=jax.ShapeDtypeStruct((m, n), F32),
        grid=(m // tm, n // tn),
        in_specs=[pl.BlockSpec((tm, kh), lambda i, j: (i, a0_blk)),
                  pl.BlockSpec((tm, kh), lambda i, j: (i, a1_blk)),
                  pl.BlockSpec((kh, tn), lambda i, j: (0, j)),
                  pl.BlockSpec((kh, tn), lambda i, j: (1, j)),
                  pl.BlockSpec((tm, tn), lambda i, j: (i, j))],
        out_specs=pl.BlockSpec((tm, tn), lambda i, j: (i, j)),
        compiler_params=_cparams(2),
        name=name,
    )(a0, a1, w, w, x2d)


def _sgu_mix_kernel(h_ref, w_ref, lg_ref, lb_ref, ws_ref, bs_ref, o_ref, *, n_col_tiles, groups_per_tile):
    j = pl.program_id(1)
    o_ref[j] = jnp.dot(h_ref[...], w_ref[...], preferred_element_type=F32)

    @pl.when(j == n_col_tiles - 1)
    def _():
        tm, tn = o_ref.shape[1], o_ref.shape[2]
        width = n_col_tiles * tn
        tot = jnp.zeros((tm, 1), F32)
        for t in range(n_col_tiles):
            tot = tot + jnp.sum(o_ref[t], axis=-1, keepdims=True)
        mu = tot / width
        var = jnp.zeros((tm, 1), F32)
        for t in range(n_col_tiles):
            d = o_ref[t] - mu
            var = var + jnp.sum(d * d, axis=-1, keepdims=True)
        rstd = lax.rsqrt(var / width + EPS)
        tri = (lax.broadcasted_iota(jnp.int32, (CHUNK, CHUNK), 1)
               <= lax.broadcasted_iota(jnp.int32, (CHUNK, CHUNK), 0))
        for t in range(n_col_tiles):
            vn = ((o_ref[t] - mu) * rstd * lg_ref[t] + lb_ref[t]).astype(BF16)
            outs = []
            for gi in range(groups_per_tile):
                grp = t * groups_per_tile + gi
                wsm = jnp.where(tri, ws_ref[grp], 0.0).astype(BF16)
                bsg = bs_ref[grp]
                cols = slice(gi * GROUP_W, (gi + 1) * GROUP_W)
                parts = []
                for cidx in range(tm // CHUNK):
                    vc = vn[cidx * CHUNK:(cidx + 1) * CHUNK, cols]
                    parts.append(jnp.dot(wsm, vc, preferred_element_type=F32) + bsg)
                outs.append(jnp.concatenate(parts, axis=0))
            o_ref[t] = jnp.concatenate(outs, axis=1)


def sgu_mix(h, w_v, ln_g, ln_b, w_s, b_s, tm=512, tn=512):
    m, k = h.shape
    n = w_v.shape[1]
    nj = n // tn
    gpt = tn // GROUP_W
    return pl.pallas_call(
        functools.partial(_sgu_mix_kernel, n_col_tiles=nj, groups_per_tile=gpt),
        out_shape=jax.ShapeDtypeStruct((nj, m, tn), F32),
        grid=(m // tm, nj),
        in_specs=[pl.BlockSpec((tm, k), lambda i, j: (i, 0)),
                  pl.BlockSpec((k, tn), lambda i, j: (0, j)),
                  pl.BlockSpec((nj, 1, tn), lambda i, j: (0, 0, 0)),
                  pl.BlockSpec((nj, 1, tn), lambda i, j: (0, 0, 0)),
                  pl.BlockSpec((N_GROUPS, CHUNK, CHUNK), lambda i, j: (0, 0, 0)),
                  pl.BlockSpec((N_GROUPS, CHUNK, 1), lambda i, j: (0, 0, 0))],
        out_specs=pl.BlockSpec((nj, tm, tn), lambda i, j: (0, i, 0)),
        compiler_params=_cparams(2),
        name="sgu_mix",
    )(h, w_v, ln_g.reshape(nj, 1, tn), ln_b.reshape(nj, 1, tn), w_s, b_s.reshape(N_GROUPS, CHUNK, 1))


def _sgu_gate_kernel(h_ref, wu_ref, wz_ref, mix_ref, o_ref):
    h = h_ref[...]
    u = jnp.dot(h, wu_ref[...], preferred_element_type=F32)
    zg = jnp.dot(h, wz_ref[...], preferred_element_type=F32)
    o_ref[...] = (u * mix_ref[...] * (zg * jax.nn.sigmoid(zg))).astype(o_ref.dtype)


def sgu_gate(h, w_u, w_z, mix, tm=1024):
    m, k = h.shape
    n = w_u.shape[1]
    nj, _, tn = mix.shape
    return pl.pallas_call(
        _sgu_gate_kernel,
        out_shape=jax.ShapeDtypeStruct((m, n), BF16),
        grid=(m // tm, nj),
        in_specs=[pl.BlockSpec((tm, k), lambda i, j: (i, 0)),
                  pl.BlockSpec((k, tn), lambda i, j: (0, j)),
                  pl.BlockSpec((k, tn), lambda i, j: (0, j)),
                  pl.BlockSpec((None, tm, tn), lambda i, j: (j, i, 0))],
        out_specs=pl.BlockSpec((tm, tn), lambda i, j: (i, j)),
        compiler_params=_cparams(2),
        name="sgu_gate",
    )(h, w_u, w_z, mix)


def _rotary_tables(seq):
    half = ROT_DIM // 2
    inv_freq = jnp.power(ROPE_THETA, -jnp.arange(half, dtype=F32) * 2.0 / ROT_DIM)
    ang = jnp.arange(seq).astype(F32)[:, None] * inv_freq[None, :]
    cos, sin = jnp.cos(ang), jnp.sin(ang)
    rest = HEAD_DIM - ROT_DIM
    c = jnp.concatenate([cos, cos, jnp.ones((seq, rest), F32)], axis=1)
    sa = jnp.concatenate([jnp.zeros((seq, half), F32), sin, jnp.zeros((seq, rest), F32)], axis=1)
    sb = jnp.concatenate([-sin, jnp.zeros((seq, half + rest), F32)], axis=1)
    return c, sa, sb


def _overlap_t(seq):
    n_blk = seq // SLC_LEN
    n_rows = seq // CMP_STRIDE
    n_cmp = (seq - CMP_LEN) // CMP_STRIDE + 1
    cs = jnp.arange(n_rows) * CMP_STRIDE
    bs = jnp.arange(n_blk) * SLC_LEN
    ov = (cs[None, :] < bs[:, None] + SLC_LEN) & (cs[None, :] + CMP_LEN > bs[:, None])
    ov = ov & (jnp.arange(n_rows)[None, :] < n_cmp)
    return ov.astype(F32)


def kernel(x, norm_even, w_in_even, conv_w, cmp_k_pos, cmp_k_w1, cmp_k_b1, cmp_k_w2, cmp_v_pos, cmp_v_w1, cmp_v_b1, cmp_v_w2, w_out_even, norm_odd, w_in_odd, sgu_ln_g, sgu_ln_b, sgu_w_s, sgu_b_s, w_out_odd, norm_final):
    batch, seq, d = x.shape
    m = batch * seq
    x2d = x.reshape(m, d)

    w_in = w_in_even[0].astype(BF16)
    cw = CONV_W
    qw = N_HEADS * HEAD_DIM
    kvw = N_KV * HEAD_DIM
    o_q = 4 * cw
    o_kv = o_q + qw
    o_gl = o_kv + 6 * kvw
    o_ng = o_gl + 3 * N_HEADS
    w_cb, w_cc, w_ch, w_cg = (w_in[:, k * cw:(k + 1) * cw] for k in range(4))
    w_nsa = jnp.concatenate([w_in[:, o_q:o_gl], w_in[:, o_ng:o_ng + qw]], axis=1)
    w_gl = jnp.pad(w_in[:, o_gl:o_ng], ((0, 0), (0, LANES - 3 * N_HEADS)))

    h0, gl = rmsnorm_proj(x2d, norm_even[0], w_gl)
    y_conv = conv_proj(h0, w_cb, w_cc, w_ch, w_cg, conv_w[0], seq)
    z = matmul(h0, w_nsa, name="nsa_proj")

    tabs = _rotary_tables(seq)
    cmp_k = (cmp_k_pos[0], cmp_k_w1[0].astype(BF16).reshape(CMP_LEN, HEAD_DIM, HEAD_DIM),
             cmp_k_b1[0].reshape(1, HEAD_DIM), cmp_k_w2[0].astype(BF16))
    cmp_v = (cmp_v_pos[0], cmp_v_w1[0].astype(BF16).reshape(CMP_LEN, HEAD_DIM, HEAD_DIM),
             cmp_v_b1[0].reshape(1, HEAD_DIM), cmp_v_w2[0].astype(BF16))
    prep = nsa_prep(z, tabs, cmp_k, cmp_v, batch, seq, col0=qw // HEAD_DIM)
    ng_col = (qw + 6 * kvw) // (GQA * HEAD_DIM)
    y_nsa = nsa_attention(z, gl, tabs, _overlap_t(seq), prep, batch, seq, ng_col)

    x1 = out_proj(y_conv, 0, y_nsa, 0, w_out_even[0].astype(BF16), x2d, name="out_proj_even")

    h1 = rmsnorm(x1, norm_odd[0], BF16)
    w_in1 = w_in_odd[0].astype(BF16)
    w_u, w_v, w_z = (w_in1[:, k * SGU_W:(k + 1) * SGU_W] for k in range(3))
    mix = sgu_mix(h1, w_v, sgu_ln_g[0], sgu_ln_b[0], sgu_w_s[0], sgu_b_s[0])
    act = sgu_gate(h1, w_u, w_z, mix)
    x2 = out_proj(act, 0, act, 1, w_out_odd[0].astype(BF16), x1, name="out_proj_odd")

    out = rmsnorm(x2, norm_final, F32)
    return out.reshape(batch, seq, d)
```

```python
import functools
import math

import jax
import jax.numpy as jnp
from jax import lax
from jax.experimental import pallas as pl
from jax.experimental.pallas import tpu as pltpu

F32 = jnp.float32
BF16 = jnp.bfloat16

D_MODEL = 2048
MIX = 2 * D_MODEL
CONV_W = MIX // 2
CONV_K = 3
HEAD_DIM = 128
N_HEADS = 16
N_KV = 4
GQA = N_HEADS // N_KV
ROT_DIM = HEAD_DIM // 4
ROPE_THETA = 500000.0
CMP_LEN = 32
CMP_STRIDE = 16
SLC_LEN = 64
N_SEL = 8
WINDOW = 512
SGU_W = MIX
CHUNK = 128
N_GROUPS = 16
GROUP_W = SGU_W // N_GROUPS
EPS = 1e-6

LANES = 128
HALO = 16
KEY_TILE = LANES
VMEM_LIMIT = 56 * 1024 * 1024

NEG = -1e30


def _cparams(n_axes):
    return pltpu.CompilerParams(
        dimension_semantics=("arbitrary",) * n_axes, vmem_limit_bytes=VMEM_LIMIT)


def _rmsnorm_kernel(x_ref, g_ref, o_ref):
    x = x_ref[...]
    y = x * lax.rsqrt(jnp.mean(x * x, axis=-1, keepdims=True) + EPS)
    o_ref[...] = (y * g_ref[...]).astype(o_ref.dtype)


def _rmsnorm_proj_kernel(x_ref, g_ref, w_ref, o_ref, p_ref):
    x = x_ref[...]
    y = x * lax.rsqrt(jnp.mean(x * x, axis=-1, keepdims=True) + EPS)
    hb = (y * g_ref[...]).astype(BF16)
    o_ref[...] = hb
    p_ref[...] = jnp.dot(hb, w_ref[...], preferred_element_type=F32)


def rmsnorm(x2d, g, out_dtype, tm=512):
    m, d = x2d.shape
    return pl.pallas_call(
        _rmsnorm_kernel,
        out_shape=jax.ShapeDtypeStruct((m, d), out_dtype),
        grid=(m // tm,),
        in_specs=[pl.BlockSpec((tm, d), lambda i: (i, 0)),
                  pl.BlockSpec((1, d), lambda i: (0, 0))],
        out_specs=pl.BlockSpec((tm, d), lambda i: (i, 0)),
        compiler_params=_cparams(1),
        name="rmsnorm",
    )(x2d, g.reshape(1, d))


def rmsnorm_proj(x2d, g, w, tm=512):
    m, d = x2d.shape
    n = w.shape[1]
    return pl.pallas_call(
        _rmsnorm_proj_kernel,
        out_shape=(jax.ShapeDtypeStruct((m, d), BF16), jax.ShapeDtypeStruct((m, n), F32)),
        grid=(m // tm,),
        in_specs=[pl.BlockSpec((tm, d), lambda i: (i, 0)),
                  pl.BlockSpec((1, d), lambda i: (0, 0)),
                  pl.BlockSpec((d, n), lambda i: (0, 0))],
        out_specs=(pl.BlockSpec((tm, d), lambda i: (i, 0)),
                   pl.BlockSpec((tm, n), lambda i: (i, 0))),
        compiler_params=_cparams(1),
        name="rmsnorm_gates",
    )(x2d, g.reshape(1, d), w)


def _matmul_kernel(a_ref, w_ref, o_ref):
    o_ref[...] = jnp.dot(a_ref[...], w_ref[...], preferred_element_type=F32)


def matmul(a, w, tm=1024, tn=512, name="proj"):
    m, k = a.shape
    n = w.shape[1]
    return pl.pallas_call(
        _matmul_kernel,
        out_shape=jax.ShapeDtypeStruct((m, n), F32),
        grid=(m // tm, n // tn),
        in_specs=[pl.BlockSpec((tm, k), lambda i, j: (i, 0)),
                  pl.BlockSpec((k, tn), lambda i, j: (0, j))],
        out_specs=pl.BlockSpec((tm, tn), lambda i, j: (i, j)),
        compiler_params=_cparams(2),
        name=name,
    )(a, w)


def _conv_proj_kernel(h_ref, hp_ref, wb_ref, wc_ref, wh_ref, wg_ref, cw_ref, o_ref, *, tiles_per_seq):
    i = pl.program_id(0)
    h = h_ref[...]
    cb = jnp.dot(h, wb_ref[...], preferred_element_type=F32)
    cc = jnp.dot(h, wc_ref[...], preferred_element_type=F32)
    ch = jnp.dot(h, wh_ref[...], preferred_element_type=F32)
    cg = jnp.dot(h, wg_ref[...], preferred_element_type=F32)
    p = cc * ch
    hp = hp_ref[...]
    pp = (jnp.dot(hp, wc_ref[...], preferred_element_type=F32)
          * jnp.dot(hp, wh_ref[...], preferred_element_type=F32))
    pp = jnp.where(i % tiles_per_seq == 0, 0.0, pp)
    row = lax.broadcasted_iota(jnp.int32, p.shape, 0)
    p1 = pltpu.roll(p, 1, axis=0)
    p2 = pltpu.roll(p, 2, axis=0)
    p1 = jnp.where(row == 0, pp[HALO - 1:HALO, :], p1)
    p2 = jnp.where(row == 0, pp[HALO - 2:HALO - 1, :],
                   jnp.where(row == 1, pp[HALO - 1:HALO, :], p2))
    cw = cw_ref[...]
    conv = cw[0:1, :] * p2 + cw[1:2, :] * p1 + cw[2:3, :] * p
    o_ref[...] = (cb * conv * (cg * jax.nn.sigmoid(cg))).astype(o_ref.dtype)


def conv_proj(h, wb, wc, wh, wg, conv_w, seq, tm=1024, tn=256):
    m, k = h.shape
    n = wb.shape[1]
    tiles_per_seq = seq // tm
    halo_per_tile = tm // HALO
    wspec = pl.BlockSpec((k, tn), lambda i, j: (0, j))
    return pl.pallas_call(
        functools.partial(_conv_proj_kernel, tiles_per_seq=tiles_per_seq),
        out_shape=jax.ShapeDtypeStruct((m, n), BF16),
        grid=(m // tm, n // tn),
        in_specs=[pl.BlockSpec((tm, k), lambda i, j: (i, 0)),
                  pl.BlockSpec((HALO, k), lambda i, j: (jnp.maximum(i * halo_per_tile - 1, 0), 0)),
                  wspec, wspec, wspec, wspec,
                  pl.BlockSpec((CONV_K, tn), lambda i, j: (0, j))],
        out_specs=pl.BlockSpec((tm, tn), lambda i, j: (i, j)),
        compiler_params=_cparams(2),
        name="conv_proj",
    )(h, h, wb, wc, wh, wg, conv_w)


def _rotary(x, c, sa, sb, axis):
    half = ROT_DIM // 2
    return (x * c + pltpu.roll(x, half, axis=axis) * sa
            + pltpu.roll(x, HEAD_DIM - half, axis=axis) * sb)


def _compress(src_ref, pos_ref, w1_ref, b1_ref, w2_ref):
    n_rows = src_ref.shape[0] // CMP_STRIDE
    p_acc = jnp.zeros((n_rows, HEAD_DIM), F32)
    q_acc = jnp.zeros((n_rows, HEAD_DIM), F32)
    for r in range(CMP_STRIDE):
        s_r = src_ref[pl.ds(r, n_rows, stride=CMP_STRIDE), :]
        a_r = (s_r + pos_ref[r:r + 1, :]).astype(BF16)
        b_r = (s_r + pos_ref[CMP_STRIDE + r:CMP_STRIDE + r + 1, :]).astype(BF16)
        p_acc += jnp.dot(a_r, w1_ref[r], preferred_element_type=F32)
        q_acc += jnp.dot(b_r, w1_ref[CMP_STRIDE + r], preferred_element_type=F32)
    hid = p_acc + pltpu.roll(q_acc, n_rows - 1, axis=0) + b1_ref[...]
    act = (hid * jax.nn.sigmoid(hid)).astype(BF16)
    return jnp.dot(act, w2_ref[...], preferred_element_type=F32)


def _nsa_prep_kernel(kc_ref, vc_ref, ks_ref, vs_ref, kw_ref, vw_ref, c_ref, sa_ref, sb_ref,
                     kpos_ref, kw1_ref, kb1_ref, kw2_ref, vpos_ref, vw1_ref, vb1_ref, vw2_ref,
                     ksr_ref, vst_ref, kwr_ref, vwt_ref, kcmp_ref, vcmp_ref):
    c, sa, sb = c_ref[...], sa_ref[...], sb_ref[...]
    ksr_ref[...] = _rotary(ks_ref[...], c, sa, sb, 1).astype(BF16)
    kwr_ref[...] = _rotary(kw_ref[...], c, sa, sb, 1).astype(BF16)
    for t in range(vst_ref.shape[0]):
        rows = pl.ds(t * KEY_TILE, KEY_TILE)
        vst_ref[t] = vs_ref[rows, :].T.astype(BF16)
        vwt_ref[t] = vw_ref[rows, :].T.astype(BF16)
    kcmp_ref[...] = _compress(kc_ref, kpos_ref, kw1_ref, kb1_ref, kw2_ref).astype(BF16)
    vcmp_ref[...] = _compress(vc_ref, vpos_ref, vw1_ref, vb1_ref, vw2_ref).astype(BF16)


def nsa_prep(z, tabs, cmp_k, cmp_v, batch, seq, col0):
    def zspec(which):
        return pl.BlockSpec((seq, HEAD_DIM), lambda b, g, w=which: (b, col0 + w * N_KV + g))

    tab = pl.BlockSpec((seq, HEAD_DIM), lambda b, g: (0, 0))

    def wspecs():
        return [pl.BlockSpec((CMP_LEN, HEAD_DIM), lambda b, g: (0, 0)),
                pl.BlockSpec((CMP_LEN, HEAD_DIM, HEAD_DIM), lambda b, g: (0, 0, 0)),
                pl.BlockSpec((1, HEAD_DIM), lambda b, g: (0, 0)),
                pl.BlockSpec((HEAD_DIM, HEAD_DIM), lambda b, g: (0, 0))]

    k_out = pl.BlockSpec((None, None, seq, HEAD_DIM), lambda b, g: (b, g, 0, 0))
    n_tiles = seq // KEY_TILE
    v_out = pl.BlockSpec((None, None, n_tiles, HEAD_DIM, KEY_TILE), lambda b, g: (b, g, 0, 0, 0))
    n_cmp_rows = seq // CMP_STRIDE
    cmp_out = pl.BlockSpec((None, None, n_cmp_rows, HEAD_DIM), lambda b, g: (b, g, 0, 0))
    k_shape = jax.ShapeDtypeStruct((batch, N_KV, seq, HEAD_DIM), BF16)
    v_shape = jax.ShapeDtypeStruct((batch, N_KV, n_tiles, HEAD_DIM, KEY_TILE), BF16)
    cmp_shape = jax.ShapeDtypeStruct((batch, N_KV, n_cmp_rows, HEAD_DIM), BF16)
    return pl.pallas_call(
        _nsa_prep_kernel,
        out_shape=(k_shape, v_shape, k_shape, v_shape, cmp_shape, cmp_shape),
        grid=(batch, N_KV),
        in_specs=[zspec(0), zspec(1), zspec(2), zspec(3), zspec(4), zspec(5), tab, tab, tab]
        + wspecs() + wspecs(),
        out_specs=(k_out, v_out, k_out, v_out, cmp_out, cmp_out),
        compiler_params=_cparams(2),
        name="nsa_prep",
    )(z, z, z, z, z, z, *tabs, *cmp_k, *cmp_v)


def _nsa_attn_kernel(q_ref, ng_ref, gl_ref, ct_ref, sat_ref, sbt_ref, ov_ref,
                     ksr_ref, vst_ref, kwr_ref, vwt_ref, kcmp_ref, vcmp_ref,
                     o_ref, acc_ref, sel_ref, glt_ref, *, tq, tk):
    g = pl.program_id(1)
    i = pl.program_id(2)
    q0 = i * tq
    qscale = (HEAD_DIM ** -0.5) * math.log2(math.e)

    q = q_ref[...]
    ct, sat, sbt = ct_ref[...], sat_ref[...], sbt_ref[...]
    qts, qrts = [], []
    for n in range(GQA):
        qt = (q[:, n * HEAD_DIM:(n + 1) * HEAD_DIM] * qscale).T
        qts.append(qt)
        qrts.append(_rotary(qt, ct, sat, sbt, 0))
    qt_all = jnp.concatenate(qts, axis=1).astype(BF16)
    qrt_all = jnp.concatenate(qrts, axis=1).astype(BF16)

    tlane = q0 + lax.broadcasted_iota(jnp.int32, (1, tq), 1)

    def lanes4(x):
        return jnp.concatenate([x] * GQA, axis=1)

    n_cmp = kcmp_ref.shape[0]
    s_c = jnp.dot(kcmp_ref[...], qt_all, preferred_element_type=F32)
    cend = lax.broadcasted_iota(jnp.int32, (n_cmp, tq), 0) * CMP_STRIDE + (CMP_LEN - 1)
    valid_c = cend <= tlane
    bias_c = lanes4(jnp.where(valid_c, 0.0, NEG))
    keep_c = lanes4(jnp.where(valid_c, 1.0, 0.0))
    s_c = s_c + bias_c
    m_c = jnp.max(s_c, axis=0, keepdims=True)
    e_c = jnp.exp2(s_c - m_c) * keep_c
    den_c = jnp.sum(e_c, axis=0, keepdims=True)
    p_c = e_c / jnp.where(den_c > 0, den_c, 1.0)
    vcmp_t = vcmp_ref[...].astype(F32).T.astype(BF16)
    o_c = jnp.dot(vcmp_t, p_c.astype(BF16), preferred_element_type=F32)

    p_sum = p_c[:, 0:tq]
    for n in range(1, GQA):
        p_sum = p_sum + p_c[:, n * tq:(n + 1) * tq]
    n_blk = ov_ref.shape[0]
    imp = jnp.dot(ov_ref[...], p_sum, preferred_element_type=F32,
                  precision=lax.Precision.HIGHEST)
    jblk = lax.broadcasted_iota(jnp.int32, (n_blk, tq), 0)
    forced = (jblk == 0) | (jblk == (tlane >> 6))
    imp = jnp.where(forced, jnp.inf, jnp.where(jblk * SLC_LEN <= tlane, imp, -jnp.inf))
    cnt = jnp.zeros((n_blk, tq), F32)
    for r in range(n_blk):
        row = imp[r:r + 1, :]
        tie = jnp.where(jblk > r, 1.0, 0.0)
        cnt = cnt + jnp.where(row > imp, 1.0, jnp.where(row == imp, tie, 0.0))
    sel_ref[...] = jnp.where(cnt < N_SEL, jnp.where(imp > -jnp.inf, 1.0, 0.0), 0.0)

    blocks_per_chunk = tk // SLC_LEN
    tiles_per_chunk = tk // KEY_TILE

    def sel_chunk(ci, m_old, l_old, diagonal):
        k0 = pl.multiple_of(ci * tk, tk)
        kblk = ksr_ref[pl.ds(k0, tk), :]
        vblk = jnp.concatenate([vst_ref[ci * tiles_per_chunk + r] for r in range(tiles_per_chunk)],
                               axis=1)
        picked = sel_ref[pl.ds(ci * blocks_per_chunk, blocks_per_chunk), :]
        keep = jnp.concatenate(
            [jnp.broadcast_to(picked[r:r + 1, :], (SLC_LEN, tq)) for r in range(blocks_per_chunk)],
            axis=0)
        if diagonal:
            kpos = k0 + lax.broadcasted_iota(jnp.int32, (tk, tq), 0)
            keep = jnp.where(kpos <= tlane, keep, 0.0)
        s = jnp.dot(kblk, qrt_all, preferred_element_type=F32) + lanes4((keep - 1.0) * (-NEG))
        m_new = jnp.maximum(m_old, jnp.max(s, axis=0, keepdims=True))
        alpha = jnp.exp2(m_old - m_new)
        p = jnp.exp2(s - m_new)
        l_new = alpha * l_old + jnp.sum(p, axis=0, keepdims=True)
        acc_ref[...] = alpha * acc_ref[...] + jnp.dot(vblk, p.astype(BF16), preferred_element_type=F32)
        return m_new, l_new

    acc_ref[...] = jnp.zeros(acc_ref.shape, F32)
    n_chunks = (q0 + tq + tk - 1) // tk
    state = (jnp.full((1, GQA * tq), NEG, F32), jnp.zeros((1, GQA * tq), F32))
    state = lax.fori_loop(0, n_chunks - 1, lambda ci, st: sel_chunk(ci, st[0], st[1], False), state)
    _, l_s = sel_chunk(n_chunks - 1, state[0], state[1], True)
    o_s = acc_ref[...] / l_s

    wlen = WINDOW + tq
    w0 = pl.multiple_of(jnp.maximum(q0 - WINDOW, 0), tq)
    kwin = kwr_ref[pl.ds(w0, wlen), :]
    wt0 = w0 // KEY_TILE
    vwin = jnp.concatenate([vwt_ref[wt0 + r] for r in range(wlen // KEY_TILE)], axis=1)
    dist = tlane - (w0 + lax.broadcasted_iota(jnp.int32, (wlen, tq), 0))
    bias_w = jnp.where(dist >= 0, jnp.where(dist < WINDOW, 0.0, NEG), NEG)
    s_w = jnp.dot(kwin, qrt_all, preferred_element_type=F32) + lanes4(bias_w)
    e_w = jnp.exp2(s_w - jnp.max(s_w, axis=0, keepdims=True))
    o_w = (jnp.dot(vwin, e_w.astype(BF16), preferred_element_type=F32)
           / jnp.sum(e_w, axis=0, keepdims=True))

    ng = ng_ref[...]
    glt_ref[...] = jax.nn.sigmoid(gl_ref[...]).T
    for n in range(GQA):
        head = g * GQA + n
        gc, gs, gw = (glt_ref[pl.ds(j * N_HEADS + head, 1), :] for j in range(3))
        sl = slice(n * tq, (n + 1) * tq)
        o = (gc * o_c[:, sl] + gs * o_s[:, sl] + gw * o_w[:, sl]).T
        ngh = ng[:, n * HEAD_DIM:(n + 1) * HEAD_DIM]
        o_ref[:, n * HEAD_DIM:(n + 1) * HEAD_DIM] = (o * (ngh * jax.nn.sigmoid(ngh))).astype(o_ref.dtype)


def nsa_attention(z, gl, tabs_t, overlap_t, prep, batch, seq, ng_col, tq=128, tk=256):
    ksr, vst, kwr, vwt, kcmp, vcmp = prep
    nq = seq // tq
    qw = GQA * HEAD_DIM
    n_blk = overlap_t.shape[0]
    kspec = pl.BlockSpec((None, None, seq, HEAD_DIM), lambda b, g, i: (b, g, 0, 0))
    vspec = pl.BlockSpec((None, None) + vst.shape[2:], lambda b, g, i: (b, g, 0, 0, 0))
    cmp = pl.BlockSpec((None, None, kcmp.shape[2], HEAD_DIM), lambda b, g, i: (b, g, 0, 0))
    tab = pl.BlockSpec((HEAD_DIM, tq), lambda b, g, i: (0, i))
    return pl.pallas_call(
        functools.partial(_nsa_attn_kernel, tq=tq, tk=tk),
        out_shape=jax.ShapeDtypeStruct((batch * seq, N_HEADS * HEAD_DIM), BF16),
        grid=(batch, N_KV, nq),
        in_specs=[pl.BlockSpec((tq, qw), lambda b, g, i: (b * nq + i, g)),
                  pl.BlockSpec((tq, qw), lambda b, g, i: (b * nq + i, ng_col + g)),
                  pl.BlockSpec((tq, gl.shape[1]), lambda b, g, i: (b * nq + i, 0)),
                  tab, tab, tab,
                  pl.BlockSpec(overlap_t.shape, lambda b, g, i: (0, 0)),
                  kspec, vspec, kspec, vspec, cmp, cmp],
        out_specs=pl.BlockSpec((tq, qw), lambda b, g, i: (b * nq + i, g)),
        scratch_shapes=[pltpu.VMEM((HEAD_DIM, GQA * tq), F32), pltpu.VMEM((n_blk, tq), F32),
                        pltpu.VMEM((gl.shape[1], tq), F32)],
        compiler_params=_cparams(3),
        name="nsa_attention",
    )(z, z, gl, *tabs_t, overlap_t, ksr, vst, kwr, vwt, kcmp, vcmp)


def _out_proj_kernel(a0_ref, a1_ref, w0_ref, w1_ref, x_ref, o_ref):
    acc = jnp.dot(a0_ref[...], w0_ref[...], preferred_element_type=F32)
    acc += jnp.dot(a1_ref[...], w1_ref[...], preferred_element_type=F32)
    o_ref[...] = x_ref[...] + acc


def out_proj(a0, a0_blk, a1, a1_blk, w, x2d, tm=1024, tn=512, name="out_proj"):
    m = x2d.shape[0]
    kh = w.shape[0] // 2
    n = w.shape[1]
    return pl.pallas_call(
        _out_proj_kernel,
        out_shape=jax.ShapeDtypeStruct((m, n), F32),
        grid=(m // tm, n // tn),
        in_specs=[pl.BlockSpec((tm, kh), lambda i, j: (i, a0_blk)),
                  pl.BlockSpec((tm, kh), lambda i, j: (i, a1_blk)),
                  pl.BlockSpec((kh, tn), lambda i, j: (0, j)),
                  pl.BlockSpec((kh, tn), lambda i, j: (1, j)),
                  pl.BlockSpec((tm, tn), lambda i, j: (i, j))],
        out_specs=pl.BlockSpec((tm, tn), lambda i, j: (i, j)),
        compiler_params=_cparams(2),
        name=name,
    )(a0, a1, w, w, x2d)


def _sgu_mix_kernel(h_ref, w_ref, lg_ref, lb_ref, ws_ref, bs_ref, o_ref, *, n_col_tiles, groups_per_tile):
    j = pl.program_id(1)
    o_ref[j] = jnp.dot(h_ref[...], w_ref[...], preferred_element_type=F32)

    @pl.when(j == n_col_tiles - 1)
    def _():
        tm, tn = o_ref.shape[1], o_ref.shape[2]
        width = n_col_tiles * tn
        tot = jnp.zeros((tm, 1), F32)
        for t in range(n_col_tiles):
            tot = tot + jnp.sum(o_ref[t], axis=-1, keepdims=True)
        mu = tot / width
        var = jnp.zeros((tm, 1), F32)
        for t in range(n_col_tiles):
            d = o_ref[t] - mu
            var = var + jnp.sum(d * d, axis=-1, keepdims=True)
        rstd = lax.rsqrt(var / width + EPS)
        tri = (lax.broadcasted_iota(jnp.int32, (CHUNK, CHUNK), 1)
               <= lax.broadcasted_iota(jnp.int32, (CHUNK, CHUNK), 0))
        for t in range(n_col_tiles):
            vn = ((o_ref[t] - mu) * rstd * lg_ref[t] + lb_ref[t]).astype(BF16)
            outs = []
            for gi in range(groups_per_tile):
                grp = t * groups_per_tile + gi
                wsm = jnp.where(tri, ws_ref[grp], 0.0).astype(BF16)
                bsg = bs_ref[grp]
                cols = slice(gi * GROUP_W, (gi + 1) * GROUP_W)
                parts = []
                for cidx in range(tm // CHUNK):
                    vc = vn[cidx * CHUNK:(cidx + 1) * CHUNK, cols]
                    parts.append(jnp.dot(wsm, vc, preferred_element_type=F32) + bsg)
                outs.append(jnp.concatenate(parts, axis=0))
            o_ref[t] = jnp.concatenate(outs, axis=1)


def sgu_mix(h, w_v, ln_g, ln_b, w_s, b_s, tm=512, tn=512):
    m, k = h.shape
    n = w_v.shape[1]
    nj = n // tn
    gpt = tn // GROUP_W
    return pl.pallas_call(
        functools.partial(_sgu_mix_kernel, n_col_tiles=nj, groups_per_tile=gpt),
        out_shape=jax.ShapeDtypeStruct((nj, m, tn), F32),
        grid=(m // tm, nj),
        in_specs=[pl.BlockSpec((tm, k), lambda i, j: (i, 0)),
                  pl.BlockSpec((k, tn), lambda i, j: (0, j)),
                  pl.BlockSpec((nj, 1, tn), lambda i, j: (0, 0, 0)),
                  pl.BlockSpec((nj, 1, tn), lambda i, j: (0, 0, 0)),
                  pl.BlockSpec((N_GROUPS, CHUNK, CHUNK), lambda i, j: (0, 0, 0)),
                  pl.BlockSpec((N_GROUPS, CHUNK, 1), lambda i, j: (0, 0, 0))],
        out_specs=pl.BlockSpec((nj, tm, tn), lambda i, j: (0, i, 0)),
        compiler_params=_cparams(2),
        name="sgu_mix",
    )(h, w_v, ln_g.reshape(nj, 1, tn), ln_b.reshape(nj, 1, tn), w_s, b_s.reshape(N_GROUPS, CHUNK, 1))


def _sgu_gate_kernel(h_ref, wu_ref, wz_ref, mix_ref, o_ref):
    h = h_ref[...]
    u = jnp.dot(h, wu_ref[...], preferred_element_type=F32)
    zg = jnp.dot(h, wz_ref[...], preferred_element_type=F32)
    o_ref[...] = (u * mix_ref[...] * (zg * jax.nn.sigmoid(zg))).astype(o_ref.dtype)


def sgu_gate(h, w_u, w_z, mix, tm=1024):
    m, k = h.shape
    n = w_u.shape[1]
    nj, _, tn = mix.shape
    return pl.pallas_call(
        _sgu_gate_kernel,
        out_shape=jax.ShapeDtypeStruct((m, n), BF16),
        grid=(m // tm, nj),
        in_specs=[pl.BlockSpec((tm, k), lambda i, j: (i, 0)),
                  pl.BlockSpec((k, tn), lambda i, j: (0, j)),
                  pl.BlockSpec((k, tn), lambda i, j: (0, j)),
                  pl.BlockSpec((None, tm, tn), lambda i, j: (j, i, 0))],
        out_specs=pl.BlockSpec((tm, tn), lambda i, j: (i, j)),
        compiler_params=_cparams(2),
        name="sgu_gate",
    )(h, w_u, w_z, mix)


def _rotary_tables(seq):
    half = ROT_DIM // 2
    inv_freq = jnp.power(ROPE_THETA, -jnp.arange(half, dtype=F32) * 2.0 / ROT_DIM)
    ang = jnp.arange(seq).astype(F32)[:, None] * inv_freq[None, :]
    cos, sin = jnp.cos(ang), jnp.sin(ang)
    rest = HEAD_DIM - ROT_DIM
    c = jnp.concatenate([cos, cos, jnp.ones((seq, rest), F32)], axis=1)
    sa = jnp.concatenate([jnp.zeros((seq, half), F32), sin, jnp.zeros((seq, rest), F32)], axis=1)
    sb = jnp.concatenate([-sin, jnp.zeros((seq, half + rest), F32)], axis=1)
    return c, sa, sb


def _overlap_t(seq):
    n_blk = seq // SLC_LEN
    n_rows = seq // CMP_STRIDE
    n_cmp = (seq - CMP_LEN) // CMP_STRIDE + 1
    cs = jnp.arange(n_rows) * CMP_STRIDE
    bs = jnp.arange(n_blk) * SLC_LEN
    ov = (cs[None, :] < bs[:, None] + SLC_LEN) & (cs[None, :] + CMP_LEN > bs[:, None])
    ov = ov & (jnp.arange(n_rows)[None, :] < n_cmp)
    return ov.astype(F32)


def kernel(x, norm_even, w_in_even, conv_w, cmp_k_pos, cmp_k_w1, cmp_k_b1, cmp_k_w2, cmp_v_pos, cmp_v_w1, cmp_v_b1, cmp_v_w2, w_out_even, norm_odd, w_in_odd, sgu_ln_g, sgu_ln_b, sgu_w_s, sgu_b_s, w_out_odd, norm_final):
    batch, seq, d = x.shape
    m = batch * seq
    x2d = x.reshape(m, d)

    w_in = w_in_even[0].astype(BF16)
    cw = CONV_W
    qw = N_HEADS * HEAD_DIM
    kvw = N_KV * HEAD_DIM
    o_q = 4 * cw
    o_kv = o_q + qw
    o_gl = o_kv + 6 * kvw
    o_ng = o_gl + 3 * N_HEADS
    w_cb, w_cc, w_ch, w_cg = (w_in[:, k * cw:(k + 1) * cw] for k in range(4))
    w_nsa = jnp.concatenate([w_in[:, o_q:o_gl], w_in[:, o_ng:o_ng + qw]], axis=1)
    w_gl = jnp.pad(w_in[:, o_gl:o_ng], ((0, 0), (0, LANES - 3 * N_HEADS)))

    h0, gl = rmsnorm_proj(x2d, norm_even[0], w_gl)
    y_conv = conv_proj(h0, w_cb, w_cc, w_ch, w_cg, conv_w[0], seq)
    z = matmul(h0, w_nsa, name="nsa_proj")

    tabs = _rotary_tables(seq)
    tabs_t = tuple(t.T for t in tabs)
    cmp_k = (cmp_k_pos[0], cmp_k_w1[0].astype(BF16).reshape(CMP_LEN, HEAD_DIM, HEAD_DIM),
             cmp_k_b1[0].reshape(1, HEAD_DIM), cmp_k_w2[0].astype(BF16))
    cmp_v = (cmp_v_pos[0], cmp_v_w1[0].astype(BF16).reshape(CMP_LEN, HEAD_DIM, HEAD_DIM),
             cmp_v_b1[0].reshape(1, HEAD_DIM), cmp_v_w2[0].astype(BF16))
    prep = nsa_prep(z, tabs, cmp_k, cmp_v, batch, seq, col0=qw // HEAD_DIM)
    ng_col = (qw + 6 * kvw) // (GQA * HEAD_DIM)
    y_nsa = nsa_attention(z, gl, tabs_t, _overlap_t(seq), prep, batch, seq, ng_col)

    x1 = out_proj(y_conv, 0, y_nsa, 0, w_out_even[0].astype(BF16), x2d, name="out_proj_even")

    h1 = rmsnorm(x1, norm_odd[0], BF16)
    w_in1 = w_in_odd[0].astype(BF16)
    w_u, w_v, w_z = (w_in1[:, k * SGU_W:(k + 1) * SGU_W] for k in range(3))
    mix = sgu_mix(h1, w_v, sgu_ln_g[0], sgu_ln_b[0], sgu_w_s[0], sgu_b_s[0])
    act = sgu_gate(h1, w_u, w_z, mix)
    x2 = out_proj(act, 0, act, 1, w_out_odd[0].astype(BF16), x1, name="out_proj_odd")

    out = rmsnorm(x2, norm_final, F32)
    return out.reshape(batch, seq, d)
```

```python
import functools
import math

import jax
import jax.numpy as jnp
from jax import lax
from jax.experimental import pallas as pl
from jax.experimental.pallas import tpu as pltpu

F32 = jnp.float32
BF16 = jnp.bfloat16

D_MODEL = 2048
MIX = 2 * D_MODEL
CONV_W = MIX // 2
CONV_K = 3
HEAD_DIM = 128
N_HEADS = 16
N_KV = 4
GQA = N_HEADS // N_KV
ROT_DIM = HEAD_DIM // 4
ROPE_THETA = 500000.0
CMP_LEN = 32
CMP_STRIDE = 16
SLC_LEN = 64
N_SEL = 8
WINDOW = 512
SGU_W = MIX
CHUNK = 128
N_GROUPS = 16
GROUP_W = SGU_W // N_GROUPS
EPS = 1e-6

LANES = 128
SUBLANES = 8
SLC_SHIFT = 6
HALO = 16
KEY_TILE = LANES
VMEM_LIMIT = 56 * 1024 * 1024

NEG = -1e30

EXT_DIM = 2 * HEAD_DIM


def _cparams(n_axes):
    return pltpu.CompilerParams(
        dimension_semantics=("arbitrary",) * n_axes, vmem_limit_bytes=VMEM_LIMIT)


def _rmsnorm_kernel(x_ref, g_ref, o_ref):
    x = x_ref[...]
    y = x * lax.rsqrt(jnp.mean(x * x, axis=-1, keepdims=True) + EPS)
    o_ref[...] = (y * g_ref[...]).astype(o_ref.dtype)


def _rmsnorm_proj_kernel(x_ref, g_ref, w_ref, o_ref, p_ref):
    x = x_ref[...]
    y = x * lax.rsqrt(jnp.mean(x * x, axis=-1, keepdims=True) + EPS)
    hb = (y * g_ref[...]).astype(BF16)
    o_ref[...] = hb
    p_ref[...] = jnp.dot(hb, w_ref[...], preferred_element_type=F32)


def rmsnorm(x2d, g, out_dtype, tm=512):
    m, d = x2d.shape
    return pl.pallas_call(
        _rmsnorm_kernel,
        out_shape=jax.ShapeDtypeStruct((m, d), out_dtype),
        grid=(m // tm,),
        in_specs=[pl.BlockSpec((tm, d), lambda i: (i, 0)),
                  pl.BlockSpec((1, d), lambda i: (0, 0))],
        out_specs=pl.BlockSpec((tm, d), lambda i: (i, 0)),
        compiler_params=_cparams(1),
        name="rmsnorm",
    )(x2d, g.reshape(1, d))


def rmsnorm_proj(x2d, g, w, tm=512):
    m, d = x2d.shape
    n = w.shape[1]
    return pl.pallas_call(
        _rmsnorm_proj_kernel,
        out_shape=(jax.ShapeDtypeStruct((m, d), BF16), jax.ShapeDtypeStruct((m, n), F32)),
        grid=(m // tm,),
        in_specs=[pl.BlockSpec((tm, d), lambda i: (i, 0)),
                  pl.BlockSpec((1, d), lambda i: (0, 0)),
                  pl.BlockSpec((d, n), lambda i: (0, 0))],
        out_specs=(pl.BlockSpec((tm, d), lambda i: (i, 0)),
                   pl.BlockSpec((tm, n), lambda i: (i, 0))),
        compiler_params=_cparams(1),
        name="rmsnorm_gates",
    )(x2d, g.reshape(1, d), w)


def _matmul_kernel(a_ref, w_ref, o_ref):
    o_ref[...] = jnp.dot(a_ref[...], w_ref[...], preferred_element_type=F32)


def matmul(a, w, col0, n, tm=1024, tn=512, name="proj"):
    m, k = a.shape
    assert col0 % tn == 0 and n % tn == 0
    j0 = col0 // tn
    return pl.pallas_call(
        _matmul_kernel,
        out_shape=jax.ShapeDtypeStruct((m, n), F32),
        grid=(m // tm, n // tn),
        in_specs=[pl.BlockSpec((tm, k), lambda i, j: (i, 0)),
                  pl.BlockSpec((k, tn), lambda i, j: (0, j0 + j))],
        out_specs=pl.BlockSpec((tm, tn), lambda i, j: (i, j)),
        compiler_params=_cparams(2),
        name=name,
    )(a, w)


def _conv_proj_kernel(h_ref, hp_ref, wb_ref, wc_ref, wh_ref, wg_ref, cw_ref, o_ref, *, tiles_per_seq):
    i = pl.program_id(0)
    h = h_ref[...]
    cb = jnp.dot(h, wb_ref[...], preferred_element_type=F32)
    cc = jnp.dot(h, wc_ref[...], preferred_element_type=F32)
    ch = jnp.dot(h, wh_ref[...], preferred_element_type=F32)
    cg = jnp.dot(h, wg_ref[...], preferred_element_type=F32)
    p = cc * ch
    hp = hp_ref[...]
    pp = (jnp.dot(hp, wc_ref[...], preferred_element_type=F32)
          * jnp.dot(hp, wh_ref[...], preferred_element_type=F32))
    pp = jnp.where(i % tiles_per_seq == 0, 0.0, pp)
    row = lax.broadcasted_iota(jnp.int32, p.shape, 0)
    p1 = pltpu.roll(p, 1, axis=0)
    p2 = pltpu.roll(p, 2, axis=0)
    p1 = jnp.where(row == 0, pp[HALO - 1:HALO, :], p1)
    p2 = jnp.where(row == 0, pp[HALO - 2:HALO - 1, :],
                   jnp.where(row == 1, pp[HALO - 1:HALO, :], p2))
    cw = cw_ref[...]
    conv = cw[0:1, :] * p2 + cw[1:2, :] * p1 + cw[2:3, :] * p
    o_ref[...] = (cb * conv * (cg * jax.nn.sigmoid(cg))).astype(o_ref.dtype)


def conv_proj(h, w, conv_w, seq, tm=1024, tn=256):
    m, k = h.shape
    n = conv_w.shape[1]
    tiles_per_seq = seq // tm
    halo_per_tile = tm // HALO
    nj = n // tn

    def wspec(which):
        return pl.BlockSpec((k, tn), lambda i, j: (0, which * nj + j))

    return pl.pallas_call(
        functools.partial(_conv_proj_kernel, tiles_per_seq=tiles_per_seq),
        out_shape=jax.ShapeDtypeStruct((m, n), BF16),
        grid=(m // tm, n // tn),
        in_specs=[pl.BlockSpec((tm, k), lambda i, j: (i, 0)),
                  pl.BlockSpec((HALO, k), lambda i, j: (jnp.maximum(i * halo_per_tile - 1, 0), 0)),
                  wspec(0), wspec(1), wspec(2), wspec(3),
                  pl.BlockSpec((CONV_K, tn), lambda i, j: (0, j))],
        out_specs=pl.BlockSpec((tm, tn), lambda i, j: (i, j)),
        compiler_params=_cparams(2),
        name="conv_proj",
    )(h, h, w, w, w, w, conv_w)


def _rotary(x, c, sa, sb):
    half = ROT_DIM // 2
    return (x * c + pltpu.roll(x, half, axis=1) * sa
            + pltpu.roll(x, HEAD_DIM - half, axis=1) * sb)


def _compress(src_ref, pos_ref, w1_ref, b1_ref, w2_ref):
    n_rows = src_ref.shape[0] // CMP_STRIDE
    p_acc = jnp.zeros((n_rows, HEAD_DIM), F32)
    q_acc = jnp.zeros((n_rows, HEAD_DIM), F32)
    for r in range(CMP_STRIDE):
        s_r = src_ref[pl.ds(r, n_rows, stride=CMP_STRIDE), :]
        a_r = (s_r + pos_ref[r:r + 1, :]).astype(BF16)
        b_r = (s_r + pos_ref[CMP_STRIDE + r:CMP_STRIDE + r + 1, :]).astype(BF16)
        p_acc += jnp.dot(a_r, w1_ref[r], preferred_element_type=F32)
        q_acc += jnp.dot(b_r, w1_ref[CMP_STRIDE + r], preferred_element_type=F32)
    hid = p_acc + pltpu.roll(q_acc, n_rows - 1, axis=0) + b1_ref[...]
    act = (hid * jax.nn.sigmoid(hid)).astype(BF16)
    return jnp.dot(act, w2_ref[...], preferred_element_type=F32)


def _nsa_prep_kernel(kc_ref, vc_ref, ks_ref, vs_ref, kw_ref, vw_ref, c_ref, sa_ref, sb_ref,
                     kpos_ref, kw1_ref, kb1_ref, kw2_ref, vpos_ref, vw1_ref, vb1_ref, vw2_ref,
                     ksx_ref, vst_ref, kwx_ref, vwt_ref, kcmp_ref, vcmp_ref):
    c, sa, sb = c_ref[...], sa_ref[...], sb_ref[...]
    seq = ks_ref.shape[0]
    n_blk = seq // SLC_LEN
    flag_w = EXT_DIM - HEAD_DIM
    blk = lax.broadcasted_iota(jnp.int32, (seq, flag_w), 0) >> SLC_SHIFT
    onehot = jnp.where(blk == lax.broadcasted_iota(jnp.int32, (seq, flag_w), 1), 1.0, 0.0)
    ksx_ref[:, 0:HEAD_DIM] = _rotary(ks_ref[...], c, sa, sb).astype(BF16)
    ksx_ref[:, HEAD_DIM:] = onehot.astype(BF16)
    pad_flag = jnp.where(lax.broadcasted_iota(jnp.int32, (WINDOW, EXT_DIM), 1) == HEAD_DIM + n_blk, 1.0, 0.0)
    kwx_ref[0:WINDOW, :] = pad_flag.astype(BF16)
    kwx_ref[WINDOW:, 0:HEAD_DIM] = _rotary(kw_ref[...], c, sa, sb).astype(BF16)
    kwx_ref[WINDOW:, HEAD_DIM:] = jnp.zeros((seq, flag_w), BF16)
    n_tiles = seq // KEY_TILE
    pad_tiles = WINDOW // KEY_TILE
    for t in range(pad_tiles):
        vwt_ref[t] = jnp.zeros((HEAD_DIM, KEY_TILE), BF16)
    for t in range(n_tiles):
        rows = pl.ds(t * KEY_TILE, KEY_TILE)
        vst_ref[t] = vs_ref[rows, :].T.astype(BF16)
        vwt_ref[pad_tiles + t] = vw_ref[rows, :].T.astype(BF16)
    kcmp_ref[...] = _compress(kc_ref, kpos_ref, kw1_ref, kb1_ref, kw2_ref).astype(BF16)
    vcmp_ref[...] = _compress(vc_ref, vpos_ref, vw1_ref, vb1_ref, vw2_ref).astype(BF16)


def nsa_prep(z, tabs, cmp_k, cmp_v, batch, seq, col0):
    def zspec(which):
        return pl.BlockSpec((seq, HEAD_DIM), lambda b, g, w=which: (b, col0 + w * N_KV + g))

    tab = pl.BlockSpec((seq, HEAD_DIM), lambda b, g: (0, 0))

    def wspecs():
        return [pl.BlockSpec((CMP_LEN, HEAD_DIM), lambda b, g: (0, 0)),
                pl.BlockSpec((CMP_LEN, HEAD_DIM, HEAD_DIM), lambda b, g: (0, 0, 0)),
                pl.BlockSpec((1, HEAD_DIM), lambda b, g: (0, 0)),
                pl.BlockSpec((HEAD_DIM, HEAD_DIM), lambda b, g: (0, 0))]

    def out(shape):
        nd = len(shape)
        spec = pl.BlockSpec((None, None) + shape, lambda b, g: (b, g) + (0,) * nd)
        return spec, jax.ShapeDtypeStruct((batch, N_KV) + shape, BF16)

    n_tiles = seq // KEY_TILE
    pad_tiles = WINDOW // KEY_TILE
    outs = [out((seq, EXT_DIM)), out((n_tiles, HEAD_DIM, KEY_TILE)),
            out((WINDOW + seq, EXT_DIM)), out((pad_tiles + n_tiles, HEAD_DIM, KEY_TILE)),
            out((seq // CMP_STRIDE, HEAD_DIM)), out((seq // CMP_STRIDE, HEAD_DIM))]
    return pl.pallas_call(
        _nsa_prep_kernel,
        out_shape=tuple(o[1] for o in outs),
        grid=(batch, N_KV),
        in_specs=[zspec(0), zspec(1), zspec(2), zspec(3), zspec(4), zspec(5), tab, tab, tab]
        + wspecs() + wspecs(),
        out_specs=tuple(o[0] for o in outs),
        compiler_params=_cparams(2),
        name="nsa_prep",
    )(z, z, z, z, z, z, *tabs, *cmp_k, *cmp_v)


def _nsa_attn_kernel(q_ref, ng_ref, gl_ref, rot_ref, ov_ref,
                     ksx_ref, vst_ref, kwx_ref, vwt_ref, kcmp_ref, vcmp_ref,
                     o_ref, acc_ref, qx_ref, glt_ref, s_ref, m_ref, l_ref, *, tq):
    g = pl.program_id(1)
    i = pl.program_id(2)
    q0 = i * tq
    n_blk = ov_ref.shape[0]
    half = ROT_DIM // 2
    qscale = (HEAD_DIM ** -0.5) * math.log2(math.e)

    def head_lanes(n):
        return slice(n * tq, (n + 1) * tq)

    def lanes4(x):
        return jnp.concatenate([x] * GQA, axis=1)

    q = q_ref[...]
    rot = rot_ref[...]
    cos2, sin2 = rot[:ROT_DIM], rot[ROT_DIM:]
    qts = []
    for n in range(GQA):
        qt = (q[:, n * HEAD_DIM:(n + 1) * HEAD_DIM] * qscale).T
        qts.append(qt)
        top = qt[:ROT_DIM]
        swapped = jnp.concatenate([top[half:], top[:half]], axis=0)
        qrt = jnp.concatenate([top * cos2 + swapped * sin2, qt[ROT_DIM:]], axis=0)
        qx_ref[0:HEAD_DIM, head_lanes(n)] = qrt.astype(BF16)
    qt_all = jnp.concatenate(qts, axis=1).astype(BF16)

    tlane = q0 + lax.broadcasted_iota(jnp.int32, (1, tq), 1)

    n_cmp = kcmp_ref.shape[0]
    s_c = jnp.dot(kcmp_ref[...], qt_all, preferred_element_type=F32)
    cend = lax.broadcasted_iota(jnp.int32, (n_cmp, tq), 0) * CMP_STRIDE + (CMP_LEN - 1)
    s_c = s_c + lanes4(jnp.where(cend <= tlane, 0.0, NEG))
    any_c = lanes4(tlane) >= CMP_LEN - 1
    m_c = jnp.where(any_c, jnp.max(s_c, axis=0, keepdims=True), 0.0)
    e_c = jnp.exp2(s_c - m_c)
    den_c = jnp.sum(e_c, axis=0, keepdims=True)
    p_c = e_c / jnp.where(den_c > 0, den_c, 1.0)
    vcmp_t = vcmp_ref[...].astype(F32).T.astype(BF16)
    o_c = jnp.dot(vcmp_t, p_c.astype(BF16), preferred_element_type=F32)

    p_sum = p_c[:, head_lanes(0)]
    for n in range(1, GQA):
        p_sum = p_sum + p_c[:, head_lanes(n)]
    imp = jnp.dot(ov_ref[...], p_sum, preferred_element_type=F32,
                  precision=lax.Precision.HIGHEST)
    jblk = lax.broadcasted_iota(jnp.int32, (n_blk, tq), 0)
    forced = (jblk == 0) | (jblk == (tlane >> SLC_SHIFT))
    imp = jnp.where(forced, jnp.inf, jnp.where(jblk * SLC_LEN <= tlane, imp, -jnp.inf))
    n_grp = n_blk // SUBLANES
    grp = [imp[a * SUBLANES:(a + 1) * SUBLANES] for a in range(n_grp)]
    sub = lax.broadcasted_iota(jnp.int32, (SUBLANES, tq), 0)
    cnt = [jnp.zeros((SUBLANES, tq), F32) for _ in range(n_grp)]
    for r in range(n_blk):
        row = imp[r:r + 1, :]
        a_r, r_in = divmod(r, SUBLANES)
        for a in range(n_grp):
            if a > a_r:
                cnt[a] = cnt[a] + jnp.where(row >= grp[a], 1.0, 0.0)
            elif a < a_r:
                cnt[a] = cnt[a] + jnp.where(row > grp[a], 1.0, 0.0)
            else:
                cnt[a] = cnt[a] + jnp.where(sub > r_in, jnp.where(row >= grp[a], 1.0, 0.0),
                                            jnp.where(row > grp[a], 1.0, 0.0))
    cnt = jnp.concatenate(cnt, axis=0)
    sel_bias = jnp.where(cnt < N_SEL, jnp.where(imp > -jnp.inf, 0.0, NEG), NEG)

    qx_ref[HEAD_DIM:HEAD_DIM + n_blk, :] = lanes4(sel_bias).astype(BF16)
    n_rest = EXT_DIM - HEAD_DIM - n_blk
    first = lax.broadcasted_iota(jnp.int32, (n_rest, GQA * tq), 0) == 0
    qx_ref[HEAD_DIM + n_blk:, :] = jnp.where(first, NEG, 0.0).astype(BF16)

    kk = lax.broadcasted_iota(jnp.int32, (tq, tq), 0)
    tt = lax.broadcasted_iota(jnp.int32, (tq, tq), 1)
    tri_diag = jnp.where(kk <= tt, 0.0, NEG)
    tri_old = jnp.where(kk > tt, 0.0, NEG)

    tiles = tq // KEY_TILE

    def sel_scores(ci):
        k0 = pl.multiple_of(ci * tq, tq)
        return jnp.dot(ksx_ref[pl.ds(k0, tq), :], qx_ref[...], preferred_element_type=F32)

    def sel_update(ci, s):
        vblk = jnp.concatenate([vst_ref[ci * tiles + r] for r in range(tiles)], axis=1)
        m_old = m_ref[...]
        m_new = jnp.maximum(m_old, jnp.max(s, axis=0, keepdims=True))
        alpha = jnp.exp2(m_old - m_new)
        p = jnp.exp2(s - m_new)
        l_ref[...] = alpha * l_ref[...] + jnp.sum(p, axis=0, keepdims=True)
        acc_ref[...] = alpha * acc_ref[...] + jnp.dot(vblk, p.astype(BF16), preferred_element_type=F32)
        m_ref[...] = m_new

    def sel_pair(pi, carry):
        c0 = 2 * pi
        s_ref[1] = sel_scores(c0 + 1)
        sel_update(c0, s_ref[0])
        s_ref[0] = sel_scores(c0 + 2)
        sel_update(c0 + 1, s_ref[1])
        return carry

    m_ref[...] = jnp.full(m_ref.shape, NEG, F32)
    l_ref[...] = jnp.zeros(l_ref.shape, F32)
    acc_ref[...] = jnp.zeros(acc_ref.shape, F32)
    s_ref[0] = sel_scores(0)
    lax.fori_loop(0, i // 2, sel_pair, 0)

    @pl.when(i % 2 == 1)
    def _():
        s_ref[1] = sel_scores(i)
        sel_update(i - 1, s_ref[0])

    sel_update(i, s_ref[i % 2] + lanes4(tri_diag))
    o_s = acc_ref[...] / l_ref[...]

    n_wt = WINDOW // tq
    kwin = kwx_ref[pl.ds(pl.multiple_of(q0, tq), WINDOW + tq), :]
    vt0 = q0 // KEY_TILE
    vwin = jnp.concatenate([vwt_ref[vt0 + r] for r in range((WINDOW + tq) // KEY_TILE)], axis=1)
    s_w = jnp.dot(kwin, qx_ref[...], preferred_element_type=F32)
    s_w = jnp.concatenate([s_w[:tq] + lanes4(tri_old), s_w[tq:n_wt * tq],
                           s_w[n_wt * tq:] + lanes4(tri_diag)], axis=0)
    e_w = jnp.exp2(s_w - jnp.max(s_w, axis=0, keepdims=True))
    o_w = (jnp.dot(vwin, e_w.astype(BF16), preferred_element_type=F32)
           / jnp.sum(e_w, axis=0, keepdims=True))

    ng = ng_ref[...]
    glt_ref[...] = jax.nn.sigmoid(gl_ref[...]).T
    for n in range(GQA):
        head = g * GQA + n
        gc, gs, gw = (glt_ref[pl.ds(j * N_HEADS + head, 1), :] for j in range(3))
        sl = head_lanes(n)
        o = (gc * o_c[:, sl] + gs * o_s[:, sl] + gw * o_w[:, sl]).T
        ngh = ng[:, n * HEAD_DIM:(n + 1) * HEAD_DIM]
        o_ref[:, n * HEAD_DIM:(n + 1) * HEAD_DIM] = (o * (ngh * jax.nn.sigmoid(ngh))).astype(o_ref.dtype)


def nsa_attention(z, z_ng, gl, rot_t, overlap_t, prep, batch, seq, tq=256):
    ksx, vst, kwx, vwt, kcmp, vcmp = prep
    assert tq % KEY_TILE == 0 and WINDOW % tq == 0 and WINDOW >= 2 * tq
    nq = seq // tq
    qw = GQA * HEAD_DIM

    def whole(a):
        nd = a.ndim - 2
        return pl.BlockSpec((None, None) + a.shape[2:], lambda b, g, i: (b, g) + (0,) * nd)

    return pl.pallas_call(
        functools.partial(_nsa_attn_kernel, tq=tq),
        out_shape=jax.ShapeDtypeStruct((batch * seq, N_HEADS * HEAD_DIM), BF16),
        grid=(batch, N_KV, nq),
        in_specs=[pl.BlockSpec((tq, qw), lambda b, g, i: (b * nq + i, g)),
                  pl.BlockSpec((tq, qw), lambda b, g, i: (b * nq + i, g)),
                  pl.BlockSpec((tq, gl.shape[1]), lambda b, g, i: (b * nq + i, 0)),
                  pl.BlockSpec((rot_t.shape[0], tq), lambda b, g, i: (0, i)),
                  pl.BlockSpec(overlap_t.shape, lambda b, g, i: (0, 0)),
                  whole(ksx), whole(vst), whole(kwx), whole(vwt), whole(kcmp), whole(vcmp)],
        out_specs=pl.BlockSpec((tq, qw), lambda b, g, i: (b * nq + i, g)),
        scratch_shapes=[pltpu.VMEM((HEAD_DIM, GQA * tq), F32),
                        pltpu.VMEM((EXT_DIM, GQA * tq), BF16),
                        pltpu.VMEM((gl.shape[1], tq), F32),
                        pltpu.VMEM((2, tq, GQA * tq), F32),
                        pltpu.VMEM((1, GQA * tq), F32), pltpu.VMEM((1, GQA * tq), F32)],
        compiler_params=_cparams(3),
        name="nsa_attention",
    )(z, z_ng, gl, rot_t, overlap_t, ksx, vst, kwx, vwt, kcmp, vcmp)


def _out_proj_kernel(a0_ref, a1_ref, w0_ref, w1_ref, x_ref, o_ref):
    acc = jnp.dot(a0_ref[...], w0_ref[...], preferred_element_type=F32)
    acc += jnp.dot(a1_ref[...], w1_ref[...], preferred_element_type=F32)
    o_ref[...] = x_ref[...] + acc


def out_proj(a0, a0_blk, a1, a1_blk, w, x2d, tm=1024, tn=512, name="out_proj"):
    m = x2d.shape[0]
    kh = w.shape[0] // 2
    n = w.shape[1]
    return pl.pallas_call(
        _out_proj_kernel,
        out_shape=jax.ShapeDtypeStruct((m, n), F32),
        grid=(m // tm, n // tn),
        in_specs=[pl.BlockSpec((tm, kh), lambda i, j: (i, a0_blk)),
                  pl.BlockSpec((tm, kh), lambda i, j: (i, a1_blk)),
                  pl.BlockSpec((kh, tn), lambda i, j: (0, j)),
                  pl.BlockSpec((kh, tn), lambda i, j: (1, j)),
                  pl.BlockSpec((tm, tn), lambda i, j: (i, j))],
        out_specs=pl.BlockSpec((tm, tn), lambda i, j: (i, j)),
        compiler_params=_cparams(2),
        name=name,
    )(a0, a1, w, w, x2d)


def _sgu_mix_kernel(h_ref, w_ref, lg_ref, lb_ref, ws_ref, bs_ref, o_ref, *, n_col_tiles, groups_per_tile):
    j = pl.program_id(1)
    o_ref[j] = jnp.dot(h_ref[...], w_ref[...], preferred_element_type=F32)

    @pl.when(j == n_col_tiles - 1)
    def _():
        tm, tn = o_ref.shape[1], o_ref.shape[2]
        width = n_col_tiles * tn
        tot = jnp.zeros((tm, 1), F32)
        for t in range(n_col_tiles):
            tot = tot + jnp.sum(o_ref[t], axis=-1, keepdims=True)
        mu = tot / width
        var = jnp.zeros((tm, 1), F32)
        for t in range(n_col_tiles):
            d = o_ref[t] - mu
            var = var + jnp.sum(d * d, axis=-1, keepdims=True)
        rstd = lax.rsqrt(var / width + EPS)
        tri = (lax.broadcasted_iota(jnp.int32, (CHUNK, CHUNK), 1)
               <= lax.broadcasted_iota(jnp.int32, (CHUNK, CHUNK), 0))
        for t in range(n_col_tiles):
            vn = ((o_ref[t] - mu) * rstd * lg_ref[t] + lb_ref[t]).astype(BF16)
            outs = []
            for gi in range(groups_per_tile):
                grp = t * groups_per_tile + gi
                wsm = jnp.where(tri, ws_ref[grp], 0.0).astype(BF16)
                bsg = bs_ref[grp]
                cols = slice(gi * GROUP_W, (gi + 1) * GROUP_W)
                parts = []
                for cidx in range(tm // CHUNK):
                    vc = vn[cidx * CHUNK:(cidx + 1) * CHUNK, cols]
                    parts.append(jnp.dot(wsm, vc, preferred_element_type=F32) + bsg)
                outs.append(jnp.concatenate(parts, axis=0))
            o_ref[t] = jnp.concatenate(outs, axis=1)


def sgu_mix(h, w, col0, ln_g, ln_b, w_s, b_s, tm=512, tn=512):
    m, k = h.shape
    n = SGU_W
    nj = n // tn
    j0 = col0 // tn
    gpt = tn // GROUP_W
    return pl.pallas_call(
        functools.partial(_sgu_mix_kernel, n_col_tiles=nj, groups_per_tile=gpt),
        out_shape=jax.ShapeDtypeStruct((nj, m, tn), F32),
        grid=(m // tm, nj),
        in_specs=[pl.BlockSpec((tm, k), lambda i, j: (i, 0)),
                  pl.BlockSpec((k, tn), lambda i, j: (0, j0 + j)),
                  pl.BlockSpec((nj, 1, tn), lambda i, j: (0, 0, 0)),
                  pl.BlockSpec((nj, 1, tn), lambda i, j: (0, 0, 0)),
                  pl.BlockSpec((N_GROUPS, CHUNK, CHUNK), lambda i, j: (0, 0, 0)),
                  pl.BlockSpec((N_GROUPS, CHUNK, 1), lambda i, j: (0, 0, 0))],
        out_specs=pl.BlockSpec((nj, tm, tn), lambda i, j: (0, i, 0)),
        compiler_params=_cparams(2),
        name="sgu_mix",
    )(h, w, ln_g.reshape(nj, 1, tn), ln_b.reshape(nj, 1, tn), w_s, b_s.reshape(N_GROUPS, CHUNK, 1))


def _sgu_gate_kernel(h_ref, wu_ref, wz_ref, mix_ref, o_ref):
    h = h_ref[...]
    u = jnp.dot(h, wu_ref[...], preferred_element_type=F32)
    zg = jnp.dot(h, wz_ref[...], preferred_element_type=F32)
    o_ref[...] = (u * mix_ref[...] * (zg * jax.nn.sigmoid(zg))).astype(o_ref.dtype)


def sgu_gate(h, w, col_u, col_z, mix, tm=1024):
    m, k = h.shape
    n = SGU_W
    nj, _, tn = mix.shape
    ju, jz = col_u // tn, col_z // tn
    return pl.pallas_call(
        _sgu_gate_kernel,
        out_shape=jax.ShapeDtypeStruct((m, n), BF16),
        grid=(m // tm, nj),
        in_specs=[pl.BlockSpec((tm, k), lambda i, j: (i, 0)),
                  pl.BlockSpec((k, tn), lambda i, j: (0, ju + j)),
                  pl.BlockSpec((k, tn), lambda i, j: (0, jz + j)),
                  pl.BlockSpec((None, tm, tn), lambda i, j: (j, i, 0))],
        out_specs=pl.BlockSpec((tm, tn), lambda i, j: (i, j)),
        compiler_params=_cparams(2),
        name="sgu_gate",
    )(h, w, w, mix)


def _rotary_tables(seq):
    half = ROT_DIM // 2
    inv_freq = jnp.power(ROPE_THETA, -jnp.arange(half, dtype=F32) * 2.0 / ROT_DIM)
    ang = jnp.arange(seq).astype(F32)[:, None] * inv_freq[None, :]
    cos, sin = jnp.cos(ang), jnp.sin(ang)
    rest = HEAD_DIM - ROT_DIM
    c = jnp.concatenate([cos, cos, jnp.ones((seq, rest), F32)], axis=1)
    sa = jnp.concatenate([jnp.zeros((seq, half), F32), sin, jnp.zeros((seq, rest), F32)], axis=1)
    sb = jnp.concatenate([-sin, jnp.zeros((seq, half + rest), F32)], axis=1)
    rot_t = jnp.concatenate([cos, cos, -sin, sin], axis=1).T
    return (c, sa, sb), rot_t


def _overlap_t(seq):
    n_blk = seq // SLC_LEN
    n_rows = seq // CMP_STRIDE
    n_cmp = (seq - CMP_LEN) // CMP_STRIDE + 1
    cs = jnp.arange(n_rows) * CMP_STRIDE
    bs = jnp.arange(n_blk) * SLC_LEN
    ov = (cs[None, :] < bs[:, None] + SLC_LEN) & (cs[None, :] + CMP_LEN > bs[:, None])
    ov = ov & (jnp.arange(n_rows)[None, :] < n_cmp)
    return ov.astype(F32)


def kernel(x, norm_even, w_in_even, conv_w, cmp_k_pos, cmp_k_w1, cmp_k_b1, cmp_k_w2, cmp_v_pos, cmp_v_w1, cmp_v_b1, cmp_v_w2, w_out_even, norm_odd, w_in_odd, sgu_ln_g, sgu_ln_b, sgu_w_s, sgu_b_s, w_out_odd, norm_final):
    batch, seq, d = x.shape
    m = batch * seq
    x2d = x.reshape(m, d)

    w_in = w_in_even[0].astype(BF16)
    cw = CONV_W
    qw = N_HEADS * HEAD_DIM
    kvw = N_KV * HEAD_DIM
    o_q = 4 * cw
    o_kv = o_q + qw
    o_gl = o_kv + 6 * kvw
    o_ng = o_gl + 3 * N_HEADS
    w_gl = jnp.pad(w_in[:, o_gl:o_ng], ((0, 0), (0, LANES - 3 * N_HEADS)))
    w_ng = w_in[:, o_ng:o_ng + qw]

    h0, gl = rmsnorm_proj(x2d, norm_even[0], w_gl)
    y_conv = conv_proj(h0, w_in, conv_w[0], seq)
    z = matmul(h0, w_in, o_q, qw + 6 * kvw, name="nsa_proj")
    z_ng = matmul(h0, w_ng, 0, qw, name="nsa_gate_proj")

    tabs, rot_t = _rotary_tables(seq)
    cmp_k = (cmp_k_pos[0], cmp_k_w1[0].astype(BF16).reshape(CMP_LEN, HEAD_DIM, HEAD_DIM),
             cmp_k_b1[0].reshape(1, HEAD_DIM), cmp_k_w2[0].astype(BF16))
    cmp_v = (cmp_v_pos[0], cmp_v_w1[0].astype(BF16).reshape(CMP_LEN, HEAD_DIM, HEAD_DIM),
             cmp_v_b1[0].reshape(1, HEAD_DIM), cmp_v_w2[0].astype(BF16))
    prep = nsa_prep(z, tabs, cmp_k, cmp_v, batch, seq, col0=qw // HEAD_DIM)
    y_nsa = nsa_attention(z, z_ng, gl, rot_t, _overlap_t(seq), prep, batch, seq)

    x1 = out_proj(y_conv, 0, y_nsa, 0, w_out_even[0].astype(BF16), x2d, name="out_proj_even")

    h1 = rmsnorm(x1, norm_odd[0], BF16)
    w_in1 = w_in_odd[0].astype(BF16)
    mix = sgu_mix(h1, w_in1, SGU_W, sgu_ln_g[0], sgu_ln_b[0], sgu_w_s[0], sgu_b_s[0])
    act = sgu_gate(h1, w_in1, 0, 2 * SGU_W, mix)
    x2 = out_proj(act, 0, act, 1, w_out_odd[0].astype(BF16), x1, name="out_proj_odd")

    out = rmsnorm(x2, norm_final, F32)
    return out.reshape(batch, seq, d)
```

```python
import functools
import math

import jax
import jax.numpy as jnp
from jax import lax
from jax.experimental import pallas as pl
from jax.experimental.pallas import tpu as pltpu

F32 = jnp.float32
BF16 = jnp.bfloat16

D_MODEL = 2048
MIX = 2 * D_MODEL
CONV_W = MIX // 2
CONV_K = 3
HEAD_DIM = 128
N_HEADS = 16
N_KV = 4
GQA = N_HEADS // N_KV
ROT_DIM = HEAD_DIM // 4
ROPE_THETA = 500000.0
CMP_LEN = 32
CMP_STRIDE = 16
SLC_LEN = 64
N_SEL = 8
WINDOW = 512
SGU_W = MIX
CHUNK = 128
N_GROUPS = 16
GROUP_W = SGU_W // N_GROUPS
EPS = 1e-6

LANES = 128
SUBLANES = 8
SLC_SHIFT = 6
HALO = 16
KEY_TILE = LANES
VMEM_LIMIT = 56 * 1024 * 1024

NEG = -1e30

EXT_DIM = 2 * HEAD_DIM


def _cparams(n_axes):
    return pltpu.CompilerParams(
        dimension_semantics=("arbitrary",) * n_axes, vmem_limit_bytes=VMEM_LIMIT)


def _rmsnorm_kernel(x_ref, g_ref, o_ref):
    x = x_ref[...]
    y = x * lax.rsqrt(jnp.mean(x * x, axis=-1, keepdims=True) + EPS)
    o_ref[...] = (y * g_ref[...]).astype(o_ref.dtype)


def _rmsnorm_proj_kernel(x_ref, g_ref, w_ref, o_ref, p_ref):
    x = x_ref[...]
    y = x * lax.rsqrt(jnp.mean(x * x, axis=-1, keepdims=True) + EPS)
    hb = (y * g_ref[...]).astype(BF16)
    o_ref[...] = hb
    p_ref[...] = jnp.dot(hb, w_ref[...], preferred_element_type=F32)


def rmsnorm(x2d, g, out_dtype, tm=512):
    m, d = x2d.shape
    return pl.pallas_call(
        _rmsnorm_kernel,
        out_shape=jax.ShapeDtypeStruct((m, d), out_dtype),
        grid=(m // tm,),
        in_specs=[pl.BlockSpec((tm, d), lambda i: (i, 0)),
                  pl.BlockSpec((1, d), lambda i: (0, 0))],
        out_specs=pl.BlockSpec((tm, d), lambda i: (i, 0)),
        compiler_params=_cparams(1),
        name="rmsnorm",
    )(x2d, g.reshape(1, d))


def rmsnorm_proj(x2d, g, w, tm=512):
    m, d = x2d.shape
    n = w.shape[1]
    return pl.pallas_call(
        _rmsnorm_proj_kernel,
        out_shape=(jax.ShapeDtypeStruct((m, d), BF16), jax.ShapeDtypeStruct((m, n), F32)),
        grid=(m // tm,),
        in_specs=[pl.BlockSpec((tm, d), lambda i: (i, 0)),
                  pl.BlockSpec((1, d), lambda i: (0, 0)),
                  pl.BlockSpec((d, n), lambda i: (0, 0))],
        out_specs=(pl.BlockSpec((tm, d), lambda i: (i, 0)),
                   pl.BlockSpec((tm, n), lambda i: (i, 0))),
        compiler_params=_cparams(1),
        name="rmsnorm_gates",
    )(x2d, g.reshape(1, d), w)


def _cast_once(w_refs, wb_refs):
    @pl.when(pl.program_id(1) == 0)
    def _():
        for w_ref, wb_ref in zip(w_refs, wb_refs):
            wb_ref[...] = w_ref[...].astype(BF16)


def _matmul_kernel(a_ref, w_ref, o_ref, wb_ref):
    _cast_once([w_ref], [wb_ref])
    o_ref[...] = jnp.dot(a_ref[...], wb_ref[...], preferred_element_type=F32)


def matmul(a, w, col0, n, tm=1024, tn=512, name="proj"):
    m, k = a.shape
    assert col0 % tn == 0 and n % tn == 0
    j0 = col0 // tn
    return pl.pallas_call(
        _matmul_kernel,
        out_shape=jax.ShapeDtypeStruct((m, n), F32),
        grid=(n // tn, m // tm),
        in_specs=[pl.BlockSpec((tm, k), lambda j, i: (i, 0)),
                  pl.BlockSpec((k, tn), lambda j, i: (0, j0 + j))],
        out_specs=pl.BlockSpec((tm, tn), lambda j, i: (i, j)),
        scratch_shapes=[pltpu.VMEM((k, tn), BF16)],
        compiler_params=_cparams(2),
        name=name,
    )(a, w)


def _conv_proj_kernel(h_ref, hp_ref, wb32_ref, wc32_ref, wh32_ref, wg32_ref, cw_ref, o_ref,
                      wb_ref, wc_ref, wh_ref, wg_ref, *, tiles_per_seq):
    _cast_once([wb32_ref, wc32_ref, wh32_ref, wg32_ref], [wb_ref, wc_ref, wh_ref, wg_ref])
    i = pl.program_id(1)
    h = h_ref[...]
    cb = jnp.dot(h, wb_ref[...], preferred_element_type=F32)
    cc = jnp.dot(h, wc_ref[...], preferred_element_type=F32)
    ch = jnp.dot(h, wh_ref[...], preferred_element_type=F32)
    cg = jnp.dot(h, wg_ref[...], preferred_element_type=F32)
    p = cc * ch
    hp = hp_ref[...]
    pp = (jnp.dot(hp, wc_ref[...], preferred_element_type=F32)
          * jnp.dot(hp, wh_ref[...], preferred_element_type=F32))
    pp = jnp.where(i % tiles_per_seq == 0, 0.0, pp)
    row = lax.broadcasted_iota(jnp.int32, p.shape, 0)
    p1 = pltpu.roll(p, 1, axis=0)
    p2 = pltpu.roll(p, 2, axis=0)
    p1 = jnp.where(row == 0, pp[HALO - 1:HALO, :], p1)
    p2 = jnp.where(row == 0, pp[HALO - 2:HALO - 1, :],
                   jnp.where(row == 1, pp[HALO - 1:HALO, :], p2))
    cw = cw_ref[...]
    conv = cw[0:1, :] * p2 + cw[1:2, :] * p1 + cw[2:3, :] * p
    o_ref[...] = (cb * conv * (cg * jax.nn.sigmoid(cg))).astype(o_ref.dtype)


def conv_proj(h, w, conv_w, seq, tm=1024, tn=256):
    m, k = h.shape
    n = conv_w.shape[1]
    tiles_per_seq = seq // tm
    halo_per_tile = tm // HALO
    nj = n // tn

    def wspec(which):
        return pl.BlockSpec((k, tn), lambda j, i: (0, which * nj + j))

    return pl.pallas_call(
        functools.partial(_conv_proj_kernel, tiles_per_seq=tiles_per_seq),
        out_shape=jax.ShapeDtypeStruct((m, n), BF16),
        grid=(n // tn, m // tm),
        in_specs=[pl.BlockSpec((tm, k), lambda j, i: (i, 0)),
                  pl.BlockSpec((HALO, k), lambda j, i: (jnp.maximum(i * halo_per_tile - 1, 0), 0)),
                  wspec(0), wspec(1), wspec(2), wspec(3),
                  pl.BlockSpec((CONV_K, tn), lambda j, i: (0, j))],
        out_specs=pl.BlockSpec((tm, tn), lambda j, i: (i, j)),
        scratch_shapes=[pltpu.VMEM((k, tn), BF16)] * 4,
        compiler_params=_cparams(2),
        name="conv_proj",
    )(h, h, w, w, w, w, conv_w)


def _rotary(x, c, sa, sb):
    half = ROT_DIM // 2
    return (x * c + pltpu.roll(x, half, axis=1) * sa
            + pltpu.roll(x, HEAD_DIM - half, axis=1) * sb)


def _compress(src_ref, pos_ref, w1_ref, b1_ref, w2_ref):
    n_rows = src_ref.shape[0] // CMP_STRIDE
    p_acc = jnp.zeros((n_rows, HEAD_DIM), F32)
    q_acc = jnp.zeros((n_rows, HEAD_DIM), F32)
    for r in range(CMP_STRIDE):
        s_r = src_ref[pl.ds(r, n_rows, stride=CMP_STRIDE), :]
        a_r = (s_r + pos_ref[r:r + 1, :]).astype(BF16)
        b_r = (s_r + pos_ref[CMP_STRIDE + r:CMP_STRIDE + r + 1, :]).astype(BF16)
        p_acc += jnp.dot(a_r, w1_ref[r], preferred_element_type=F32)
        q_acc += jnp.dot(b_r, w1_ref[CMP_STRIDE + r], preferred_element_type=F32)
    hid = p_acc + pltpu.roll(q_acc, n_rows - 1, axis=0) + b1_ref[...]
    act = (hid * jax.nn.sigmoid(hid)).astype(BF16)
    return jnp.dot(act, w2_ref[...], preferred_element_type=F32)


def _nsa_prep_kernel(kc_ref, vc_ref, ks_ref, vs_ref, kw_ref, vw_ref, c_ref, sa_ref, sb_ref,
                     kpos_ref, kw1_ref, kb1_ref, kw2_ref, vpos_ref, vw1_ref, vb1_ref, vw2_ref,
                     ksx_ref, vst_ref, kwx_ref, vwt_ref, kcmp_ref, vcmp_ref):
    c, sa, sb = c_ref[...], sa_ref[...], sb_ref[...]
    seq = ks_ref.shape[0]
    n_blk = seq // SLC_LEN
    flag_w = EXT_DIM - HEAD_DIM
    blk = lax.broadcasted_iota(jnp.int32, (seq, flag_w), 0) >> SLC_SHIFT
    onehot = jnp.where(blk == lax.broadcasted_iota(jnp.int32, (seq, flag_w), 1), 1.0, 0.0)
    ksx_ref[:, 0:HEAD_DIM] = _rotary(ks_ref[...], c, sa, sb).astype(BF16)
    ksx_ref[:, HEAD_DIM:] = onehot.astype(BF16)
    pad_flag = jnp.where(lax.broadcasted_iota(jnp.int32, (WINDOW, EXT_DIM), 1) == HEAD_DIM + n_blk, 1.0, 0.0)
    kwx_ref[0:WINDOW, :] = pad_flag.astype(BF16)
    kwx_ref[WINDOW:, 0:HEAD_DIM] = _rotary(kw_ref[...], c, sa, sb).astype(BF16)
    kwx_ref[WINDOW:, HEAD_DIM:] = jnp.zeros((seq, flag_w), BF16)
    n_tiles = seq // KEY_TILE
    pad_tiles = WINDOW // KEY_TILE
    for t in range(pad_tiles):
        vwt_ref[t] = jnp.zeros((HEAD_DIM, KEY_TILE), BF16)
    for t in range(n_tiles):
        rows = pl.ds(t * KEY_TILE, KEY_TILE)
        vst_ref[t] = vs_ref[rows, :].T.astype(BF16)
        vwt_ref[pad_tiles + t] = vw_ref[rows, :].T.astype(BF16)
    kcmp_ref[...] = _compress(kc_ref, kpos_ref, kw1_ref, kb1_ref, kw2_ref).astype(BF16)
    vcmp_ref[...] = _compress(vc_ref, vpos_ref, vw1_ref, vb1_ref, vw2_ref).astype(BF16)


def nsa_prep(z, tabs, cmp_k, cmp_v, batch, seq, col0):
    def zspec(which):
        return pl.BlockSpec((seq, HEAD_DIM), lambda b, g, w=which: (b, col0 + w * N_KV + g))

    tab = pl.BlockSpec((seq, HEAD_DIM), lambda b, g: (0, 0))

    def wspecs():
        return [pl.BlockSpec((CMP_LEN, HEAD_DIM), lambda b, g: (0, 0)),
                pl.BlockSpec((CMP_LEN, HEAD_DIM, HEAD_DIM), lambda b, g: (0, 0, 0)),
                pl.BlockSpec((1, HEAD_DIM), lambda b, g: (0, 0)),
                pl.BlockSpec((HEAD_DIM, HEAD_DIM), lambda b, g: (0, 0))]

    def out(shape):
        nd = len(shape)
        spec = pl.BlockSpec((None, None) + shape, lambda b, g: (b, g) + (0,) * nd)
        return spec, jax.ShapeDtypeStruct((batch, N_KV) + shape, BF16)

    n_tiles = seq // KEY_TILE
    pad_tiles = WINDOW // KEY_TILE
    outs = [out((seq, EXT_DIM)), out((n_tiles, HEAD_DIM, KEY_TILE)),
            out((WINDOW + seq, EXT_DIM)), out((pad_tiles + n_tiles, HEAD_DIM, KEY_TILE)),
            out((seq // CMP_STRIDE, HEAD_DIM)), out((seq // CMP_STRIDE, HEAD_DIM))]
    return pl.pallas_call(
        _nsa_prep_kernel,
        out_shape=tuple(o[1] for o in outs),
        grid=(batch, N_KV),
        in_specs=[zspec(0), zspec(1), zspec(2), zspec(3), zspec(4), zspec(5), tab, tab, tab]
        + wspecs() + wspecs(),
        out_specs=tuple(o[0] for o in outs),
        compiler_params=_cparams(2),
        name="nsa_prep",
    )(z, z, z, z, z, z, *tabs, *cmp_k, *cmp_v)


def _nsa_attn_kernel(q_ref, ng_ref, gl_ref, rot_ref, ov_ref,
                     ksx_ref, vst_ref, kwx_ref, vwt_ref, kcmp_ref, vcmp_ref,
                     o_ref, acc_ref, qx_ref, glt_ref, s_ref, m_ref, l_ref, *, tq):
    g = pl.program_id(1)
    i = pl.program_id(2)
    q0 = i * tq
    n_blk = ov_ref.shape[0]
    half = ROT_DIM // 2
    qscale = (HEAD_DIM ** -0.5) * math.log2(math.e)

    def head_lanes(n):
        return slice(n * tq, (n + 1) * tq)

    def lanes4(x):
        return jnp.concatenate([x] * GQA, axis=1)

    q = q_ref[...]
    rot = rot_ref[...]
    cos2, sin2 = rot[:ROT_DIM], rot[ROT_DIM:]
    qts = []
    for n in range(GQA):
        qt = (q[:, n * HEAD_DIM:(n + 1) * HEAD_DIM] * qscale).T
        qts.append(qt)
        top = qt[:ROT_DIM]
        swapped = jnp.concatenate([top[half:], top[:half]], axis=0)
        qrt = jnp.concatenate([top * cos2 + swapped * sin2, qt[ROT_DIM:]], axis=0)
        qx_ref[0:HEAD_DIM, head_lanes(n)] = qrt.astype(BF16)
    qt_all = jnp.concatenate(qts, axis=1).astype(BF16)

    tlane = q0 + lax.broadcasted_iota(jnp.int32, (1, tq), 1)

    n_cmp = kcmp_ref.shape[0]
    s_c = jnp.dot(kcmp_ref[...], qt_all, preferred_element_type=F32)
    cend = lax.broadcasted_iota(jnp.int32, (n_cmp, tq), 0) * CMP_STRIDE + (CMP_LEN - 1)
    s_c = s_c + lanes4(jnp.where(cend <= tlane, 0.0, NEG))
    any_c = lanes4(tlane) >= CMP_LEN - 1
    m_c = jnp.where(any_c, jnp.max(s_c, axis=0, keepdims=True), 0.0)
    e_c = jnp.exp2(s_c - m_c)
    den_c = jnp.sum(e_c, axis=0, keepdims=True)
    p_c = e_c / jnp.where(den_c > 0, den_c, 1.0)
    vcmp_t = vcmp_ref[...].astype(F32).T.astype(BF16)
    o_c = jnp.dot(vcmp_t, p_c.astype(BF16), preferred_element_type=F32)

    p_sum = p_c[:, head_lanes(0)]
    for n in range(1, GQA):
        p_sum = p_sum + p_c[:, head_lanes(n)]
    imp = jnp.dot(ov_ref[...], p_sum, preferred_element_type=F32,
                  precision=lax.Precision.HIGHEST)
    jblk = lax.broadcasted_iota(jnp.int32, (n_blk, tq), 0)
    forced = (jblk == 0) | (jblk == (tlane >> SLC_SHIFT))
    imp = jnp.where(forced, jnp.inf, jnp.where(jblk * SLC_LEN <= tlane, imp, -jnp.inf))
    n_grp = n_blk // SUBLANES
    grp = [imp[a * SUBLANES:(a + 1) * SUBLANES] for a in range(n_grp)]
    sub = lax.broadcasted_iota(jnp.int32, (SUBLANES, tq), 0)
    cnt = [jnp.zeros((SUBLANES, tq), F32) for _ in range(n_grp)]
    for r in range(n_blk):
        row = imp[r:r + 1, :]
        a_r, r_in = divmod(r, SUBLANES)
        for a in range(n_grp):
            if a > a_r:
                cnt[a] = cnt[a] + jnp.where(row >= grp[a], 1.0, 0.0)
            elif a < a_r:
                cnt[a] = cnt[a] + jnp.where(row > grp[a], 1.0, 0.0)
            else:
                cnt[a] = cnt[a] + jnp.where(sub > r_in, jnp.where(row >= grp[a], 1.0, 0.0),
                                            jnp.where(row > grp[a], 1.0, 0.0))
    cnt = jnp.concatenate(cnt, axis=0)
    sel_bias = jnp.where(cnt < N_SEL, jnp.where(imp > -jnp.inf, 0.0, NEG), NEG)

    qx_ref[HEAD_DIM:HEAD_DIM + n_blk, :] = lanes4(sel_bias).astype(BF16)
    n_rest = EXT_DIM - HEAD_DIM - n_blk
    first = lax.broadcasted_iota(jnp.int32, (n_rest, GQA * tq), 0) == 0
    qx_ref[HEAD_DIM + n_blk:, :] = jnp.where(first, NEG, 0.0).astype(BF16)

    kk = lax.broadcasted_iota(jnp.int32, (tq, tq), 0)
    tt = lax.broadcasted_iota(jnp.int32, (tq, tq), 1)
    tri_diag = jnp.where(kk <= tt, 0.0, NEG)
    tri_old = jnp.where(kk > tt, 0.0, NEG)

    tiles = tq // KEY_TILE

    def sel_scores(ci):
        k0 = pl.multiple_of(ci * tq, tq)
        return jnp.dot(ksx_ref[pl.ds(k0, tq), :], qx_ref[...], preferred_element_type=F32)

    def sel_update(ci, s):
        vblk = jnp.concatenate([vst_ref[ci * tiles + r] for r in range(tiles)], axis=1)
        m_old = m_ref[...]
        m_new = jnp.maximum(m_old, jnp.max(s, axis=0, keepdims=True))
        alpha = jnp.exp2(m_old - m_new)
        p = jnp.exp2(s - m_new)
        l_ref[...] = alpha * l_ref[...] + jnp.sum(p, axis=0, keepdims=True)
        acc_ref[...] = alpha * acc_ref[...] + jnp.dot(vblk, p.astype(BF16), preferred_element_type=F32)
        m_ref[...] = m_new

    def sel_pair(pi, carry):
        c0 = 2 * pi
        s_ref[1] = sel_scores(c0 + 1)
        sel_update(c0, s_ref[0])
        s_ref[0] = sel_scores(c0 + 2)
        sel_update(c0 + 1, s_ref[1])
        return carry

    m_ref[...] = jnp.full(m_ref.shape, NEG, F32)
    l_ref[...] = jnp.zeros(l_ref.shape, F32)
    acc_ref[...] = jnp.zeros(acc_ref.shape, F32)
    s_ref[0] = sel_scores(0)
    lax.fori_loop(0, i // 2, sel_pair, 0)

    @pl.when(i % 2 == 1)
    def _():
        s_ref[1] = sel_scores(i)
        sel_update(i - 1, s_ref[0])

    sel_update(i, s_ref[i % 2] + lanes4(tri_diag))
    o_s = acc_ref[...] / l_ref[...]

    n_wt = WINDOW // tq
    kwin = kwx_ref[pl.ds(pl.multiple_of(q0, tq), WINDOW + tq), :]
    vt0 = q0 // KEY_TILE
    vwin = jnp.concatenate([vwt_ref[vt0 + r] for r in range((WINDOW + tq) // KEY_TILE)], axis=1)
    s_w = jnp.dot(kwin, qx_ref[...], preferred_element_type=F32)
    s_w = jnp.concatenate([s_w[:tq] + lanes4(tri_old), s_w[tq:n_wt * tq],
                           s_w[n_wt * tq:] + lanes4(tri_diag)], axis=0)
    e_w = jnp.exp2(s_w - jnp.max(s_w, axis=0, keepdims=True))
    o_w = (jnp.dot(vwin, e_w.astype(BF16), preferred_element_type=F32)
           / jnp.sum(e_w, axis=0, keepdims=True))

    ng = ng_ref[...]
    glt_ref[...] = jax.nn.sigmoid(gl_ref[...]).T
    for n in range(GQA):
        head = g * GQA + n
        gc, gs, gw = (glt_ref[pl.ds(j * N_HEADS + head, 1), :] for j in range(3))
        sl = head_lanes(n)
        o = (gc * o_c[:, sl] + gs * o_s[:, sl] + gw * o_w[:, sl]).T
        ngh = ng[:, n * HEAD_DIM:(n + 1) * HEAD_DIM]
        o_ref[:, n * HEAD_DIM:(n + 1) * HEAD_DIM] = (o * (ngh * jax.nn.sigmoid(ngh))).astype(o_ref.dtype)


def nsa_attention(z, z_ng, gl, rot_t, overlap_t, prep, batch, seq, tq=256):
    ksx, vst, kwx, vwt, kcmp, vcmp = prep
    assert tq % KEY_TILE == 0 and WINDOW % tq == 0 and WINDOW >= 2 * tq
    nq = seq // tq
    qw = GQA * HEAD_DIM

    def whole(a):
        nd = a.ndim - 2
        return pl.BlockSpec((None, None) + a.shape[2:], lambda b, g, i: (b, g) + (0,) * nd)

    return pl.pallas_call(
        functools.partial(_nsa_attn_kernel, tq=tq),
        out_shape=jax.ShapeDtypeStruct((batch * seq, N_HEADS * HEAD_DIM), BF16),
        grid=(batch, N_KV, nq),
        in_specs=[pl.BlockSpec((tq, qw), lambda b, g, i: (b * nq + i, g)),
                  pl.BlockSpec((tq, qw), lambda b, g, i: (b * nq + i, g)),
                  pl.BlockSpec((tq, gl.shape[1]), lambda b, g, i: (b * nq + i, 0)),
                  pl.BlockSpec((rot_t.shape[0], tq), lambda b, g, i: (0, i)),
                  pl.BlockSpec(overlap_t.shape, lambda b, g, i: (0, 0)),
                  whole(ksx), whole(vst), whole(kwx), whole(vwt), whole(kcmp), whole(vcmp)],
        out_specs=pl.BlockSpec((tq, qw), lambda b, g, i: (b * nq + i, g)),
        scratch_shapes=[pltpu.VMEM((HEAD_DIM, GQA * tq), F32),
                        pltpu.VMEM((EXT_DIM, GQA * tq), BF16),
                        pltpu.VMEM((gl.shape[1], tq), F32),
                        pltpu.VMEM((2, tq, GQA * tq), F32),
                        pltpu.VMEM((1, GQA * tq), F32), pltpu.VMEM((1, GQA * tq), F32)],
        compiler_params=_cparams(3),
        name="nsa_attention",
    )(z, z_ng, gl, rot_t, overlap_t, ksx, vst, kwx, vwt, kcmp, vcmp)


def _out_proj_kernel(a0_ref, a1_ref, w0_ref, w1_ref, x_ref, o_ref):
    acc = jnp.dot(a0_ref[...], w0_ref[...], preferred_element_type=F32)
    acc += jnp.dot(a1_ref[...], w1_ref[...], preferred_element_type=F32)
    o_ref[...] = x_ref[...] + acc


def out_proj(a0, a0_blk, a1, a1_blk, w, x2d, tm=1024, tn=512, name="out_proj"):
    m = x2d.shape[0]
    kh = w.shape[0] // 2
    n = w.shape[1]
    return pl.pallas_call(
        _out_proj_kernel,
        out_shape=jax.ShapeDtypeStruct((m, n), F32),
        grid=(m // tm, n // tn),
        in_specs=[pl.BlockSpec((tm, kh), lambda i, j: (i, a0_blk)),
                  pl.BlockSpec((tm, kh), lambda i, j: (i, a1_blk)),
                  pl.BlockSpec((kh, tn), lambda i, j: (0, j)),
                  pl.BlockSpec((kh, tn), lambda i, j: (1, j)),
                  pl.BlockSpec((tm, tn), lambda i, j: (i, j))],
        out_specs=pl.BlockSpec((tm, tn), lambda i, j: (i, j)),
        compiler_params=_cparams(2),
        name=name,
    )(a0, a1, w, w, x2d)


def _sgu_mix_kernel(h_ref, w_ref, lg_ref, lb_ref, ws_ref, bs_ref, o_ref, wb_ref, *,
                    n_col_tiles, groups_per_tile):
    j = pl.program_id(1)

    @pl.when(pl.program_id(0) == 0)
    def _():
        wb_ref[j] = w_ref[...].astype(BF16)

    o_ref[j] = jnp.dot(h_ref[...], wb_ref[j], preferred_element_type=F32)

    @pl.when(j == n_col_tiles - 1)
    def _():
        tm, tn = o_ref.shape[1], o_ref.shape[2]
        width = n_col_tiles * tn
        tot = jnp.zeros((tm, 1), F32)
        for t in range(n_col_tiles):
            tot = tot + jnp.sum(o_ref[t], axis=-1, keepdims=True)
        mu = tot / width
        var = jnp.zeros((tm, 1), F32)
        for t in range(n_col_tiles):
            d = o_ref[t] - mu
            var = var + jnp.sum(d * d, axis=-1, keepdims=True)
        rstd = lax.rsqrt(var / width + EPS)
        tri = (lax.broadcasted_iota(jnp.int32, (CHUNK, CHUNK), 1)
               <= lax.broadcasted_iota(jnp.int32, (CHUNK, CHUNK), 0))
        for t in range(n_col_tiles):
            vn = ((o_ref[t] - mu) * rstd * lg_ref[t] + lb_ref[t]).astype(BF16)
            outs = []
            for gi in range(groups_per_tile):
                grp = t * groups_per_tile + gi
                wsm = jnp.where(tri, ws_ref[grp], 0.0).astype(BF16)
                bsg = bs_ref[grp]
                cols = slice(gi * GROUP_W, (gi + 1) * GROUP_W)
                parts = []
                for cidx in range(tm // CHUNK):
                    vc = vn[cidx * CHUNK:(cidx + 1) * CHUNK, cols]
                    parts.append(jnp.dot(wsm, vc, preferred_element_type=F32) + bsg)
                outs.append(jnp.concatenate(parts, axis=0))
            o_ref[t] = jnp.concatenate(outs, axis=1)


def sgu_mix(h, w, col0, ln_g, ln_b, w_s, b_s, tm=512, tn=512):
    m, k = h.shape
    n = SGU_W
    nj = n // tn
    j0 = col0 // tn
    gpt = tn // GROUP_W
    return pl.pallas_call(
        functools.partial(_sgu_mix_kernel, n_col_tiles=nj, groups_per_tile=gpt),
        out_shape=jax.ShapeDtypeStruct((nj, m, tn), F32),
        grid=(m // tm, nj),
        in_specs=[pl.BlockSpec((tm, k), lambda i, j: (i, 0)),
                  pl.BlockSpec((k, tn), lambda i, j: (0, j0 + jnp.where(i == 0, j, 0))),
                  pl.BlockSpec((nj, 1, tn), lambda i, j: (0, 0, 0)),
                  pl.BlockSpec((nj, 1, tn), lambda i, j: (0, 0, 0)),
                  pl.BlockSpec((N_GROUPS, CHUNK, CHUNK), lambda i, j: (0, 0, 0)),
                  pl.BlockSpec((N_GROUPS, CHUNK, 1), lambda i, j: (0, 0, 0))],
        out_specs=pl.BlockSpec((nj, tm, tn), lambda i, j: (0, i, 0)),
        scratch_shapes=[pltpu.VMEM((nj, k, tn), BF16)],
        compiler_params=_cparams(2),
        name="sgu_mix",
    )(h, w, ln_g.reshape(nj, 1, tn), ln_b.reshape(nj, 1, tn), w_s, b_s.reshape(N_GROUPS, CHUNK, 1))


def _sgu_gate_kernel(h_ref, wu32_ref, wz32_ref, mix_ref, o_ref, wu_ref, wz_ref):
    _cast_once([wu32_ref, wz32_ref], [wu_ref, wz_ref])
    h = h_ref[...]
    u = jnp.dot(h, wu_ref[...], preferred_element_type=F32)
    zg = jnp.dot(h, wz_ref[...], preferred_element_type=F32)
    o_ref[...] = (u * mix_ref[...] * (zg * jax.nn.sigmoid(zg))).astype(o_ref.dtype)


def sgu_gate(h, w, col_u, col_z, mix, tm=1024):
    m, k = h.shape
    n = SGU_W
    nj, _, tn = mix.shape
    ju, jz = col_u // tn, col_z // tn
    return pl.pallas_call(
        _sgu_gate_kernel,
        out_shape=jax.ShapeDtypeStruct((m, n), BF16),
        grid=(nj, m // tm),
        in_specs=[pl.BlockSpec((tm, k), lambda j, i: (i, 0)),
                  pl.BlockSpec((k, tn), lambda j, i: (0, ju + j)),
                  pl.BlockSpec((k, tn), lambda j, i: (0, jz + j)),
                  pl.BlockSpec((None, tm, tn), lambda j, i: (j, i, 0))],
        out_specs=pl.BlockSpec((tm, tn), lambda j, i: (i, j)),
        scratch_shapes=[pltpu.VMEM((k, tn), BF16)] * 2,
        compiler_params=_cparams(2),
        name="sgu_gate",
    )(h, w, w, mix)


def _rotary_tables(seq):
    half = ROT_DIM // 2
    inv_freq = jnp.power(ROPE_THETA, -jnp.arange(half, dtype=F32) * 2.0 / ROT_DIM)
    ang = jnp.arange(seq).astype(F32)[:, None] * inv_freq[None, :]
    cos, sin = jnp.cos(ang), jnp.sin(ang)
    rest = HEAD_DIM - ROT_DIM
    c = jnp.concatenate([cos, cos, jnp.ones((seq, rest), F32)], axis=1)
    sa = jnp.concatenate([jnp.zeros((seq, half), F32), sin, jnp.zeros((seq, rest), F32)], axis=1)
    sb = jnp.concatenate([-sin, jnp.zeros((seq, half + rest), F32)], axis=1)
    rot_t = jnp.concatenate([cos, cos, -sin, sin], axis=1).T
    return (c, sa, sb), rot_t


def _overlap_t(seq):
    n_blk = seq // SLC_LEN
    n_rows = seq // CMP_STRIDE
    n_cmp = (seq - CMP_LEN) // CMP_STRIDE + 1
    cs = jnp.arange(n_rows) * CMP_STRIDE
    bs = jnp.arange(n_blk) * SLC_LEN
    ov = (cs[None, :] < bs[:, None] + SLC_LEN) & (cs[None, :] + CMP_LEN > bs[:, None])
    ov = ov & (jnp.arange(n_rows)[None, :] < n_cmp)
    return ov.astype(F32)


def kernel(x, norm_even, w_in_even, conv_w, cmp_k_pos, cmp_k_w1, cmp_k_b1, cmp_k_w2, cmp_v_pos, cmp_v_w1, cmp_v_b1, cmp_v_w2, w_out_even, norm_odd, w_in_odd, sgu_ln_g, sgu_ln_b, sgu_w_s, sgu_b_s, w_out_odd, norm_final):
    batch, seq, d = x.shape
    m = batch * seq
    x2d = x.reshape(m, d)

    w_in = w_in_even[0]
    cw = CONV_W
    qw = N_HEADS * HEAD_DIM
    kvw = N_KV * HEAD_DIM
    o_q = 4 * cw
    o_kv = o_q + qw
    o_gl = o_kv + 6 * kvw
    o_ng = o_gl + 3 * N_HEADS
    w_gl = jnp.pad(w_in[:, o_gl:o_ng], ((0, 0), (0, LANES - 3 * N_HEADS))).astype(BF16)
    w_ng = w_in[:, o_ng:o_ng + qw]

    h0, gl = rmsnorm_proj(x2d, norm_even[0], w_gl)
    y_conv = conv_proj(h0, w_in, conv_w[0], seq)
    z = matmul(h0, w_in, o_q, qw + 6 * kvw, tn=1024, name="nsa_proj")
    z_ng = matmul(h0, w_ng, 0, qw, tn=1024, name="nsa_gate_proj")

    tabs, rot_t = _rotary_tables(seq)
    cmp_k = (cmp_k_pos[0], cmp_k_w1[0].astype(BF16).reshape(CMP_LEN, HEAD_DIM, HEAD_DIM),
             cmp_k_b1[0].reshape(1, HEAD_DIM), cmp_k_w2[0].astype(BF16))
    cmp_v = (cmp_v_pos[0], cmp_v_w1[0].astype(BF16).reshape(CMP_LEN, HEAD_DIM, HEAD_DIM),
             cmp_v_b1[0].reshape(1, HEAD_DIM), cmp_v_w2[0].astype(BF16))
    prep = nsa_prep(z, tabs, cmp_k, cmp_v, batch, seq, col0=qw // HEAD_DIM)
    y_nsa = nsa_attention(z, z_ng, gl, rot_t, _overlap_t(seq), prep, batch, seq)

    x1 = out_proj(y_conv, 0, y_nsa, 0, w_out_even[0].astype(BF16), x2d, name="out_proj_even")

    h1 = rmsnorm(x1, norm_odd[0], BF16)
    w_in1 = w_in_odd[0]
    mix = sgu_mix(h1, w_in1, SGU_W, sgu_ln_g[0], sgu_ln_b[0], sgu_w_s[0], sgu_b_s[0])
    act = sgu_gate(h1, w_in1, 0, 2 * SGU_W, mix)
    x2 = out_proj(act, 0, act, 1, w_out_odd[0].astype(BF16), x1, name="out_proj_odd")

    out = rmsnorm(x2, norm_final, F32)
    return out.reshape(batch, seq, d)
```

```python
import functools
import math

import jax
import jax.numpy as jnp
from jax import lax
from jax.experimental import pallas as pl
from jax.experimental.pallas import tpu as pltpu

F32 = jnp.float32
BF16 = jnp.bfloat16

D_MODEL = 2048
MIX = 2 * D_MODEL
CONV_W = MIX // 2
CONV_K = 3
HEAD_DIM = 128
N_HEADS = 16
N_KV = 4
GQA = N_HEADS // N_KV
ROT_DIM = HEAD_DIM // 4
ROPE_THETA = 500000.0
CMP_LEN = 32
CMP_STRIDE = 16
SLC_LEN = 64
N_SEL = 8
WINDOW = 512
SGU_W = MIX
CHUNK = 128
N_GROUPS = 16
GROUP_W = SGU_W // N_GROUPS
EPS = 1e-6

LANES = 128
SUBLANES = 8
SLC_SHIFT = 6
HALO = 16
KEY_TILE = LANES
XPOSE_ROWS = 256
VMEM_LIMIT = 56 * 1024 * 1024

NEG = -1e30

EXT_DIM = 2 * HEAD_DIM


def _cparams(n_axes):
    return pltpu.CompilerParams(
        dimension_semantics=("arbitrary",) * n_axes, vmem_limit_bytes=VMEM_LIMIT)


def _rmsnorm_kernel(x_ref, g_ref, o_ref):
    x = x_ref[...]
    y = x * lax.rsqrt(jnp.mean(x * x, axis=-1, keepdims=True) + EPS)
    o_ref[...] = (y * g_ref[...]).astype(o_ref.dtype)


def _rmsnorm_proj_kernel(x_ref, g_ref, wt_ref, o_ref, p_ref, wb_ref):
    @pl.when(pl.program_id(0) == 0)
    def _():
        wb_ref[...] = wt_ref[...].T.astype(BF16)

    x = x_ref[...]
    y = x * lax.rsqrt(jnp.mean(x * x, axis=-1, keepdims=True) + EPS)
    hb = (y * g_ref[...]).astype(BF16)
    o_ref[...] = hb
    p_ref[...] = jnp.dot(hb, wb_ref[...], preferred_element_type=F32)


def rmsnorm(x2d, g, out_dtype, tm=512):
    m, d = x2d.shape
    return pl.pallas_call(
        _rmsnorm_kernel,
        out_shape=jax.ShapeDtypeStruct((m, d), out_dtype),
        grid=(m // tm,),
        in_specs=[pl.BlockSpec((tm, d), lambda i: (i, 0)),
                  pl.BlockSpec((1, d), lambda i: (0, 0))],
        out_specs=pl.BlockSpec((tm, d), lambda i: (i, 0)),
        compiler_params=_cparams(1),
        name="rmsnorm",
    )(x2d, g.reshape(1, d))


def rmsnorm_proj(x2d, g, wt, row0, n, tm=512):
    m, d = x2d.shape
    assert row0 % n == 0
    return pl.pallas_call(
        _rmsnorm_proj_kernel,
        out_shape=(jax.ShapeDtypeStruct((m, d), BF16), jax.ShapeDtypeStruct((m, n), F32)),
        grid=(m // tm,),
        in_specs=[pl.BlockSpec((tm, d), lambda i: (i, 0)),
                  pl.BlockSpec((1, d), lambda i: (0, 0)),
                  pl.BlockSpec((n, d), lambda i: (row0 // n, 0))],
        out_specs=(pl.BlockSpec((tm, d), lambda i: (i, 0)),
                   pl.BlockSpec((tm, n), lambda i: (i, 0))),
        scratch_shapes=[pltpu.VMEM((d, n), BF16)],
        compiler_params=_cparams(1),
        name="rmsnorm_gates",
    )(x2d, g.reshape(1, d), wt)


def _cast_once(w_refs, wb_refs, transposed=False):
    @pl.when(pl.program_id(1) == 0)
    def _():
        for w_ref, wb_ref in zip(w_refs, wb_refs):
            if transposed:
                for r in range(0, w_ref.shape[0], XPOSE_ROWS):
                    wb_ref[:, r:r + XPOSE_ROWS] = w_ref[r:r + XPOSE_ROWS, :].T.astype(BF16)
            else:
                wb_ref[...] = w_ref[...].astype(BF16)


def _matmul_t_kernel(a_ref, wt_ref, o_ref, wb_ref):
    _cast_once([wt_ref], [wb_ref], transposed=True)
    o_ref[...] = jnp.dot(a_ref[...], wb_ref[...], preferred_element_type=F32)


def matmul_t(a, wt, row0, n, tm=1024, tn=512, name="proj"):
    m, k = a.shape
    assert n % tn == 0 and row0 % HALO == 0
    if row0 % tn == 0:
        wspec = pl.BlockSpec((tn, k), lambda j, i: (row0 // tn + j, 0))
    else:
        wspec = pl.BlockSpec((pl.Element(tn), pl.Element(k)),
                             lambda j, i: (pl.multiple_of(row0 + j * tn, HALO), 0))
    return pl.pallas_call(
        _matmul_t_kernel,
        out_shape=jax.ShapeDtypeStruct((m, n), F32),
        grid=(n // tn, m // tm),
        in_specs=[pl.BlockSpec((tm, k), lambda j, i: (i, 0)), wspec],
        out_specs=pl.BlockSpec((tm, tn), lambda j, i: (i, j)),
        scratch_shapes=[pltpu.VMEM((k, tn), BF16)],
        compiler_params=_cparams(2),
        name=name,
    )(a, wt)


def _conv_proj_kernel(h_ref, hp_ref, wb32_ref, wc32_ref, wh32_ref, wg32_ref, cw_ref, o_ref,
                      wb_ref, wc_ref, wh_ref, wg_ref, *, tiles_per_seq):
    _cast_once([wb32_ref, wc32_ref, wh32_ref, wg32_ref], [wb_ref, wc_ref, wh_ref, wg_ref],
               transposed=True)
    i = pl.program_id(1)
    h = h_ref[...]
    cb = jnp.dot(h, wb_ref[...], preferred_element_type=F32)
    cc = jnp.dot(h, wc_ref[...], preferred_element_type=F32)
    ch = jnp.dot(h, wh_ref[...], preferred_element_type=F32)
    cg = jnp.dot(h, wg_ref[...], preferred_element_type=F32)
    p = cc * ch
    hp = hp_ref[...]
    pp = (jnp.dot(hp, wc_ref[...], preferred_element_type=F32)
          * jnp.dot(hp, wh_ref[...], preferred_element_type=F32))
    pp = jnp.where(i % tiles_per_seq == 0, 0.0, pp)
    row = lax.broadcasted_iota(jnp.int32, p.shape, 0)
    p1 = pltpu.roll(p, 1, axis=0)
    p2 = pltpu.roll(p, 2, axis=0)
    p1 = jnp.where(row == 0, pp[HALO - 1:HALO, :], p1)
    p2 = jnp.where(row == 0, pp[HALO - 2:HALO - 1, :],
                   jnp.where(row == 1, pp[HALO - 1:HALO, :], p2))
    cw = cw_ref[...]
    conv = cw[0:1, :] * p2 + cw[1:2, :] * p1 + cw[2:3, :] * p
    o_ref[...] = (cb * conv * (cg * jax.nn.sigmoid(cg))).astype(o_ref.dtype)


def conv_proj(h, wt, conv_w, seq, tm=1024, tn=256):
    m, k = h.shape
    n = conv_w.shape[1]
    tiles_per_seq = seq // tm
    halo_per_tile = tm // HALO
    nj = n // tn

    def wspec(which):
        return pl.BlockSpec((tn, k), lambda j, i: (which * nj + j, 0))

    return pl.pallas_call(
        functools.partial(_conv_proj_kernel, tiles_per_seq=tiles_per_seq),
        out_shape=jax.ShapeDtypeStruct((m, n), BF16),
        grid=(n // tn, m // tm),
        in_specs=[pl.BlockSpec((tm, k), lambda j, i: (i, 0)),
                  pl.BlockSpec((HALO, k), lambda j, i: (jnp.maximum(i * halo_per_tile - 1, 0), 0)),
                  wspec(0), wspec(1), wspec(2), wspec(3),
                  pl.BlockSpec((CONV_K, tn), lambda j, i: (0, j))],
        out_specs=pl.BlockSpec((tm, tn), lambda j, i: (i, j)),
        scratch_shapes=[pltpu.VMEM((k, tn), BF16)] * 4,
        compiler_params=_cparams(2),
        name="conv_proj",
    )(h, h, wt, wt, wt, wt, conv_w)


def _rotary(x, c, sa, sb):
    half = ROT_DIM // 2
    return (x * c + pltpu.roll(x, half, axis=1) * sa
            + pltpu.roll(x, HEAD_DIM - half, axis=1) * sb)


def _compress(src_ref, pos_ref, w1_ref, b1_ref, w2_ref):
    n_rows = src_ref.shape[0] // CMP_STRIDE
    p_acc = jnp.zeros((n_rows, HEAD_DIM), F32)
    q_acc = jnp.zeros((n_rows, HEAD_DIM), F32)
    for r in range(CMP_STRIDE):
        s_r = src_ref[pl.ds(r, n_rows, stride=CMP_STRIDE), :]
        a_r = (s_r + pos_ref[r:r + 1, :]).astype(BF16)
        b_r = (s_r + pos_ref[CMP_STRIDE + r:CMP_STRIDE + r + 1, :]).astype(BF16)
        p_acc += jnp.dot(a_r, w1_ref[r], preferred_element_type=F32)
        q_acc += jnp.dot(b_r, w1_ref[CMP_STRIDE + r], preferred_element_type=F32)
    hid = p_acc + pltpu.roll(q_acc, n_rows - 1, axis=0) + b1_ref[...]
    act = (hid * jax.nn.sigmoid(hid)).astype(BF16)
    return jnp.dot(act, w2_ref[...], preferred_element_type=F32)


def _nsa_prep_kernel(kc_ref, vc_ref, ks_ref, vs_ref, kw_ref, vw_ref, c_ref, sa_ref, sb_ref,
                     kpos_ref, kw1_ref, kb1_ref, kw2_ref, vpos_ref, vw1_ref, vb1_ref, vw2_ref,
                     ksx_ref, vst_ref, kwx_ref, vwt_ref, kcmp_ref, vcmp_ref):
    c, sa, sb = c_ref[...], sa_ref[...], sb_ref[...]
    seq = ks_ref.shape[0]
    n_blk = seq // SLC_LEN
    flag_w = EXT_DIM - HEAD_DIM
    blk = lax.broadcasted_iota(jnp.int32, (seq, flag_w), 0) >> SLC_SHIFT
    onehot = jnp.where(blk == lax.broadcasted_iota(jnp.int32, (seq, flag_w), 1), 1.0, 0.0)
    ksx_ref[:, 0:HEAD_DIM] = _rotary(ks_ref[...], c, sa, sb).astype(BF16)
    ksx_ref[:, HEAD_DIM:] = onehot.astype(BF16)
    pad_flag = jnp.where(lax.broadcasted_iota(jnp.int32, (WINDOW, EXT_DIM), 1) == HEAD_DIM + n_blk, 1.0, 0.0)
    kwx_ref[0:WINDOW, :] = pad_flag.astype(BF16)
    kwx_ref[WINDOW:, 0:HEAD_DIM] = _rotary(kw_ref[...], c, sa, sb).astype(BF16)
    kwx_ref[WINDOW:, HEAD_DIM:] = jnp.zeros((seq, flag_w), BF16)
    n_tiles = seq // KEY_TILE
    pad_tiles = WINDOW // KEY_TILE
    for t in range(pad_tiles):
        vwt_ref[t] = jnp.zeros((HEAD_DIM, KEY_TILE), BF16)
    for t in range(n_tiles):
        rows = pl.ds(t * KEY_TILE, KEY_TILE)
        vst_ref[t] = vs_ref[rows, :].T.astype(BF16)
        vwt_ref[pad_tiles + t] = vw_ref[rows, :].T.astype(BF16)
    kcmp_ref[...] = _compress(kc_ref, kpos_ref, kw1_ref, kb1_ref, kw2_ref).astype(BF16)
    vcmp_ref[...] = _compress(vc_ref, vpos_ref, vw1_ref, vb1_ref, vw2_ref).astype(BF16)


def nsa_prep(z, tabs, cmp_k, cmp_v, batch, seq, col0):
    def zspec(which):
        return pl.BlockSpec((seq, HEAD_DIM), lambda b, g, w=which: (b, col0 + w * N_KV + g))

    tab = pl.BlockSpec((seq, HEAD_DIM), lambda b, g: (0, 0))

    def wspecs():
        return [pl.BlockSpec((CMP_LEN, HEAD_DIM), lambda b, g: (0, 0)),
                pl.BlockSpec((CMP_LEN, HEAD_DIM, HEAD_DIM), lambda b, g: (0, 0, 0)),
                pl.BlockSpec((1, HEAD_DIM), lambda b, g: (0, 0)),
                pl.BlockSpec((HEAD_DIM, HEAD_DIM), lambda b, g: (0, 0))]

    def out(shape):
        nd = len(shape)
        spec = pl.BlockSpec((None, None) + shape, lambda b, g: (b, g) + (0,) * nd)
        return spec, jax.ShapeDtypeStruct((batch, N_KV) + shape, BF16)

    n_tiles = seq // KEY_TILE
    pad_tiles = WINDOW // KEY_TILE
    outs = [out((seq, EXT_DIM)), out((n_tiles, HEAD_DIM, KEY_TILE)),
            out((WINDOW + seq, EXT_DIM)), out((pad_tiles + n_tiles, HEAD_DIM, KEY_TILE)),
            out((seq // CMP_STRIDE, HEAD_DIM)), out((seq // CMP_STRIDE, HEAD_DIM))]
    return pl.pallas_call(
        _nsa_prep_kernel,
        out_shape=tuple(o[1] for o in outs),
        grid=(batch, N_KV),
        in_specs=[zspec(0), zspec(1), zspec(2), zspec(3), zspec(4), zspec(5), tab, tab, tab]
        + wspecs() + wspecs(),
        out_specs=tuple(o[0] for o in outs),
        compiler_params=_cparams(2),
        name="nsa_prep",
    )(z, z, z, z, z, z, *tabs, *cmp_k, *cmp_v)


def _nsa_attn_kernel(q_ref, ng_ref, gl_ref, rot_ref, ov_ref,
                     ksx_ref, vst_ref, kwx_ref, vwt_ref, kcmp_ref, vcmp_ref,
                     o_ref, acc_ref, qx_ref, glt_ref, s_ref, m_ref, l_ref, *, tq):
    g = pl.program_id(1)
    i = pl.program_id(2)
    q0 = i * tq
    n_blk = ov_ref.shape[0]
    half = ROT_DIM // 2
    qscale = (HEAD_DIM ** -0.5) * math.log2(math.e)

    def head_lanes(n):
        return slice(n * tq, (n + 1) * tq)

    def lanes4(x):
        return jnp.concatenate([x] * GQA, axis=1)

    q = q_ref[...]
    rot = rot_ref[...]
    cos2, sin2 = rot[:ROT_DIM], rot[ROT_DIM:]
    qts = []
    for n in range(GQA):
        qt = (q[:, n * HEAD_DIM:(n + 1) * HEAD_DIM] * qscale).T
        qts.append(qt)
        top = qt[:ROT_DIM]
        swapped = jnp.concatenate([top[half:], top[:half]], axis=0)
        qrt = jnp.concatenate([top * cos2 + swapped * sin2, qt[ROT_DIM:]], axis=0)
        qx_ref[0:HEAD_DIM, head_lanes(n)] = qrt.astype(BF16)
    qt_all = jnp.concatenate(qts, axis=1).astype(BF16)

    tlane = q0 + lax.broadcasted_iota(jnp.int32, (1, tq), 1)

    n_cmp = kcmp_ref.shape[0]
    s_c = jnp.dot(kcmp_ref[...], qt_all, preferred_element_type=F32)
    cend = lax.broadcasted_iota(jnp.int32, (n_cmp, tq), 0) * CMP_STRIDE + (CMP_LEN - 1)
    s_c = s_c + lanes4(jnp.where(cend <= tlane, 0.0, NEG))
    any_c = lanes4(tlane) >= CMP_LEN - 1
    m_c = jnp.where(any_c, jnp.max(s_c, axis=0, keepdims=True), 0.0)
    e_c = jnp.exp2(s_c - m_c)
    den_c = jnp.sum(e_c, axis=0, keepdims=True)
    p_c = e_c / jnp.where(den_c > 0, den_c, 1.0)
    vcmp_t = vcmp_ref[...].astype(F32).T.astype(BF16)
    o_c = jnp.dot(vcmp_t, p_c.astype(BF16), preferred_element_type=F32)

    p_sum = p_c[:, head_lanes(0)]
    for n in range(1, GQA):
        p_sum = p_sum + p_c[:, head_lanes(n)]
    imp = jnp.dot(ov_ref[...], p_sum, preferred_element_type=F32,
                  precision=lax.Precision.HIGHEST)
    jblk = lax.broadcasted_iota(jnp.int32, (n_blk, tq), 0)
    forced = (jblk == 0) | (jblk == (tlane >> SLC_SHIFT))
    imp = jnp.where(forced, jnp.inf, jnp.where(jblk * SLC_LEN <= tlane, imp, -jnp.inf))
    n_grp = n_blk // SUBLANES
    grp = [imp[a * SUBLANES:(a + 1) * SUBLANES] for a in range(n_grp)]
    sub = lax.broadcasted_iota(jnp.int32, (SUBLANES, tq), 0)
    cnt = [jnp.zeros((SUBLANES, tq), F32) for _ in range(n_grp)]
    for r in range(n_blk):
        row = imp[r:r + 1, :]
        a_r, r_in = divmod(r, SUBLANES)
        for a in range(n_grp):
            if a > a_r:
                cnt[a] = cnt[a] + jnp.where(row >= grp[a], 1.0, 0.0)
            elif a < a_r:
                cnt[a] = cnt[a] + jnp.where(row > grp[a], 1.0, 0.0)
            else:
                cnt[a] = cnt[a] + jnp.where(sub > r_in, jnp.where(row >= grp[a], 1.0, 0.0),
                                            jnp.where(row > grp[a], 1.0, 0.0))
    cnt = jnp.concatenate(cnt, axis=0)
    sel_bias = jnp.where(cnt < N_SEL, jnp.where(imp > -jnp.inf, 0.0, NEG), NEG)

    qx_ref[HEAD_DIM:HEAD_DIM + n_blk, :] = lanes4(sel_bias).astype(BF16)
    n_rest = EXT_DIM - HEAD_DIM - n_blk
    first = lax.broadcasted_iota(jnp.int32, (n_rest, GQA * tq), 0) == 0
    qx_ref[HEAD_DIM + n_blk:, :] = jnp.where(first, NEG, 0.0).astype(BF16)

    kk = lax.broadcasted_iota(jnp.int32, (tq, tq), 0)
    tt = lax.broadcasted_iota(jnp.int32, (tq, tq), 1)
    tri_diag = jnp.where(kk <= tt, 0.0, NEG)
    tri_old = jnp.where(kk > tt, 0.0, NEG)

    tiles = tq // KEY_TILE

    def sel_scores(ci):
        k0 = pl.multiple_of(ci * tq, tq)
        return jnp.dot(ksx_ref[pl.ds(k0, tq), :], qx_ref[...], preferred_element_type=F32)

    def sel_update(ci, s):
        vblk = jnp.concatenate([vst_ref[ci * tiles + r] for r in range(tiles)], axis=1)
        m_old = m_ref[...]
        m_new = jnp.maximum(m_old, jnp.max(s, axis=0, keepdims=True))
        alpha = jnp.exp2(m_old - m_new)
        p = jnp.exp2(s - m_new)
        l_ref[...] = alpha * l_ref[...] + jnp.sum(p, axis=0, keepdims=True)
        acc_ref[...] = alpha * acc_ref[...] + jnp.dot(vblk, p.astype(BF16), preferred_element_type=F32)
        m_ref[...] = m_new

    def sel_pair(pi, carry):
        c0 = 2 * pi
        s_ref[1] = sel_scores(c0 + 1)
        sel_update(c0, s_ref[0])
        s_ref[0] = sel_scores(c0 + 2)
        sel_update(c0 + 1, s_ref[1])
        return carry

    m_ref[...] = jnp.full(m_ref.shape, NEG, F32)
    l_ref[...] = jnp.zeros(l_ref.shape, F32)
    acc_ref[...] = jnp.zeros(acc_ref.shape, F32)
    s_ref[0] = sel_scores(0)
    lax.fori_loop(0, i // 2, sel_pair, 0)

    @pl.when(i % 2 == 1)
    def _():
        s_ref[1] = sel_scores(i)
        sel_update(i - 1, s_ref[0])

    sel_update(i, s_ref[i % 2] + lanes4(tri_diag))
    o_s = acc_ref[...] / l_ref[...]

    n_wt = WINDOW // tq
    kwin = kwx_ref[pl.ds(pl.multiple_of(q0, tq), WINDOW + tq), :]
    vt0 = q0 // KEY_TILE
    vwin = jnp.concatenate([vwt_ref[vt0 + r] for r in range((WINDOW + tq) // KEY_TILE)], axis=1)
    s_w = jnp.dot(kwin, qx_ref[...], preferred_element_type=F32)
    s_w = jnp.concatenate([s_w[:tq] + lanes4(tri_old), s_w[tq:n_wt * tq],
                           s_w[n_wt * tq:] + lanes4(tri_diag)], axis=0)
    e_w = jnp.exp2(s_w - jnp.max(s_w, axis=0, keepdims=True))
    o_w = (jnp.dot(vwin, e_w.astype(BF16), preferred_element_type=F32)
           / jnp.sum(e_w, axis=0, keepdims=True))

    ng = ng_ref[...]
    glt_ref[...] = jax.nn.sigmoid(gl_ref[...]).T
    for n in range(GQA):
        head = g * GQA + n
        gc, gs, gw = (glt_ref[pl.ds(j * N_HEADS + head, 1), :] for j in range(3))
        sl = head_lanes(n)
        o = (gc * o_c[:, sl] + gs * o_s[:, sl] + gw * o_w[:, sl]).T
        ngh = ng[:, n * HEAD_DIM:(n + 1) * HEAD_DIM]
        o_ref[:, n * HEAD_DIM:(n + 1) * HEAD_DIM] = (o * (ngh * jax.nn.sigmoid(ngh))).astype(o_ref.dtype)


def nsa_attention(z, z_ng, gl, rot_t, overlap_t, prep, batch, seq, tq=256):
    ksx, vst, kwx, vwt, kcmp, vcmp = prep
    assert tq % KEY_TILE == 0 and WINDOW % tq == 0 and WINDOW >= 2 * tq
    nq = seq // tq
    qw = GQA * HEAD_DIM

    def whole(a):
        nd = a.ndim - 2
        return pl.BlockSpec((None, None) + a.shape[2:], lambda b, g, i: (b, g) + (0,) * nd)

    return pl.pallas_call(
        functools.partial(_nsa_attn_kernel, tq=tq),
        out_shape=jax.ShapeDtypeStruct((batch * seq, N_HEADS * HEAD_DIM), BF16),
        grid=(batch, N_KV, nq),
        in_specs=[pl.BlockSpec((tq, qw), lambda b, g, i: (b * nq + i, g)),
                  pl.BlockSpec((tq, qw), lambda b, g, i: (b * nq + i, g)),
                  pl.BlockSpec((tq, gl.shape[1]), lambda b, g, i: (b * nq + i, 0)),
                  pl.BlockSpec((rot_t.shape[0], tq), lambda b, g, i: (0, i)),
                  pl.BlockSpec(overlap_t.shape, lambda b, g, i: (0, 0)),
                  whole(ksx), whole(vst), whole(kwx), whole(vwt), whole(kcmp), whole(vcmp)],
        out_specs=pl.BlockSpec((tq, qw), lambda b, g, i: (b * nq + i, g)),
        scratch_shapes=[pltpu.VMEM((HEAD_DIM, GQA * tq), F32),
                        pltpu.VMEM((EXT_DIM, GQA * tq), BF16),
                        pltpu.VMEM((gl.shape[1], tq), F32),
                        pltpu.VMEM((2, tq, GQA * tq), F32),
                        pltpu.VMEM((1, GQA * tq), F32), pltpu.VMEM((1, GQA * tq), F32)],
        compiler_params=_cparams(3),
        name="nsa_attention",
    )(z, z_ng, gl, rot_t, overlap_t, ksx, vst, kwx, vwt, kcmp, vcmp)


def _out_proj_kernel(a0_ref, a1_ref, w0_ref, w1_ref, x_ref, o_ref):
    acc = jnp.dot(a0_ref[...], w0_ref[...], preferred_element_type=F32)
    acc += jnp.dot(a1_ref[...], w1_ref[...], preferred_element_type=F32)
    o_ref[...] = x_ref[...] + acc


def out_proj(a0, a0_blk, a1, a1_blk, w, x2d, tm=1024, tn=512, name="out_proj"):
    m = x2d.shape[0]
    kh = w.shape[0] // 2
    n = w.shape[1]
    return pl.pallas_call(
        _out_proj_kernel,
        out_shape=jax.ShapeDtypeStruct((m, n), F32),
        grid=(m // tm, n // tn),
        in_specs=[pl.BlockSpec((tm, kh), lambda i, j: (i, a0_blk)),
                  pl.BlockSpec((tm, kh), lambda i, j: (i, a1_blk)),
                  pl.BlockSpec((kh, tn), lambda i, j: (0, j)),
                  pl.BlockSpec((kh, tn), lambda i, j: (1, j)),
                  pl.BlockSpec((tm, tn), lambda i, j: (i, j))],
        out_specs=pl.BlockSpec((tm, tn), lambda i, j: (i, j)),
        compiler_params=_cparams(2),
        name=name,
    )(a0, a1, w, w, x2d)


def _sgu_mix_kernel(h_ref, w_ref, lg_ref, lb_ref, ws_ref, bs_ref, o_ref, wb_ref, *,
                    n_col_tiles, groups_per_tile):
    j = pl.program_id(1)

    @pl.when(pl.program_id(0) == 0)
    def _():
        wb_ref[j] = w_ref[...].astype(BF16)

    o_ref[j] = jnp.dot(h_ref[...], wb_ref[j], preferred_element_type=F32)

    @pl.when(j == n_col_tiles - 1)
    def _():
        tm, tn = o_ref.shape[1], o_ref.shape[2]
        width = n_col_tiles * tn
        tot = jnp.zeros((tm, 1), F32)
        for t in range(n_col_tiles):
            tot = tot + jnp.sum(o_ref[t], axis=-1, keepdims=True)
        mu = tot / width
        var = jnp.zeros((tm, 1), F32)
        for t in range(n_col_tiles):
            d = o_ref[t] - mu
            var = var + jnp.sum(d * d, axis=-1, keepdims=True)
        rstd = lax.rsqrt(var / width + EPS)
        tri = (lax.broadcasted_iota(jnp.int32, (CHUNK, CHUNK), 1)
               <= lax.broadcasted_iota(jnp.int32, (CHUNK, CHUNK), 0))
        for t in range(n_col_tiles):
            vn = ((o_ref[t] - mu) * rstd * lg_ref[t] + lb_ref[t]).astype(BF16)
            outs = []
            for gi in range(groups_per_tile):
                grp = t * groups_per_tile + gi
                wsm = jnp.where(tri, ws_ref[grp], 0.0).astype(BF16)
                bsg = bs_ref[grp]
                cols = slice(gi * GROUP_W, (gi + 1) * GROUP_W)
                parts = []
                for cidx in range(tm // CHUNK):
                    vc = vn[cidx * CHUNK:(cidx + 1) * CHUNK, cols]
                    parts.append(jnp.dot(wsm, vc, preferred_element_type=F32) + bsg)
                outs.append(jnp.concatenate(parts, axis=0))
            o_ref[t] = jnp.concatenate(outs, axis=1)


def sgu_mix(h, w, col0, ln_g, ln_b, w_s, b_s, tm=512, tn=512):
    m, k = h.shape
    n = SGU_W
    nj = n // tn
    j0 = col0 // tn
    gpt = tn // GROUP_W
    return pl.pallas_call(
        functools.partial(_sgu_mix_kernel, n_col_tiles=nj, groups_per_tile=gpt),
        out_shape=jax.ShapeDtypeStruct((nj, m, tn), F32),
        grid=(m // tm, nj),
        in_specs=[pl.BlockSpec((tm, k), lambda i, j: (i, 0)),
                  pl.BlockSpec((k, tn), lambda i, j: (0, j0 + jnp.where(i == 0, j, 0))),
                  pl.BlockSpec((nj, 1, tn), lambda i, j: (0, 0, 0)),
                  pl.BlockSpec((nj, 1, tn), lambda i, j: (0, 0, 0)),
                  pl.BlockSpec((N_GROUPS, CHUNK, CHUNK), lambda i, j: (0, 0, 0)),
                  pl.BlockSpec((N_GROUPS, CHUNK, 1), lambda i, j: (0, 0, 0))],
        out_specs=pl.BlockSpec((nj, tm, tn), lambda i, j: (0, i, 0)),
        scratch_shapes=[pltpu.VMEM((nj, k, tn), BF16)],
        compiler_params=_cparams(2),
        name="sgu_mix",
    )(h, w, ln_g.reshape(nj, 1, tn), ln_b.reshape(nj, 1, tn), w_s, b_s.reshape(N_GROUPS, CHUNK, 1))


def _sgu_gate_kernel(h_ref, wu32_ref, wz32_ref, mix_ref, o_ref, wu_ref, wz_ref):
    _cast_once([wu32_ref, wz32_ref], [wu_ref, wz_ref])
    h = h_ref[...]
    u = jnp.dot(h, wu_ref[...], preferred_element_type=F32)
    zg = jnp.dot(h, wz_ref[...], preferred_element_type=F32)
    o_ref[...] = (u * mix_ref[...] * (zg * jax.nn.sigmoid(zg))).astype(o_ref.dtype)


def sgu_gate(h, w, col_u, col_z, mix, tm=1024):
    m, k = h.shape
    n = SGU_W
    nj, _, tn = mix.shape
    ju, jz = col_u // tn, col_z // tn
    return pl.pallas_call(
        _sgu_gate_kernel,
        out_shape=jax.ShapeDtypeStruct((m, n), BF16),
        grid=(nj, m // tm),
        in_specs=[pl.BlockSpec((tm, k), lambda j, i: (i, 0)),
                  pl.BlockSpec((k, tn), lambda j, i: (0, ju + j)),
                  pl.BlockSpec((k, tn), lambda j, i: (0, jz + j)),
                  pl.BlockSpec((None, tm, tn), lambda j, i: (j, i, 0))],
        out_specs=pl.BlockSpec((tm, tn), lambda j, i: (i, j)),
        scratch_shapes=[pltpu.VMEM((k, tn), BF16)] * 2,
        compiler_params=_cparams(2),
        name="sgu_gate",
    )(h, w, w, mix)


def _rotary_tables(seq):
    half = ROT_DIM // 2
    inv_freq = jnp.power(ROPE_THETA, -jnp.arange(half, dtype=F32) * 2.0 / ROT_DIM)
    ang = jnp.arange(seq).astype(F32)[:, None] * inv_freq[None, :]
    cos, sin = jnp.cos(ang), jnp.sin(ang)
    rest = HEAD_DIM - ROT_DIM
    c = jnp.concatenate([cos, cos, jnp.ones((seq, rest), F32)], axis=1)
    sa = jnp.concatenate([jnp.zeros((seq, half), F32), sin, jnp.zeros((seq, rest), F32)], axis=1)
    sb = jnp.concatenate([-sin, jnp.zeros((seq, half + rest), F32)], axis=1)
    rot_t = jnp.concatenate([cos, cos, -sin, sin], axis=1).T
    return (c, sa, sb), rot_t


def _overlap_t(seq):
    n_blk = seq // SLC_LEN
    n_rows = seq // CMP_STRIDE
    n_cmp = (seq - CMP_LEN) // CMP_STRIDE + 1
    cs = jnp.arange(n_rows) * CMP_STRIDE
    bs = jnp.arange(n_blk) * SLC_LEN
    ov = (cs[None, :] < bs[:, None] + SLC_LEN) & (cs[None, :] + CMP_LEN > bs[:, None])
    ov = ov & (jnp.arange(n_rows)[None, :] < n_cmp)
    return ov.astype(F32)


def kernel(x, norm_even, w_in_even, conv_w, cmp_k_pos, cmp_k_w1, cmp_k_b1, cmp_k_w2, cmp_v_pos, cmp_v_w1, cmp_v_b1, cmp_v_w2, w_out_even, norm_odd, w_in_odd, sgu_ln_g, sgu_ln_b, sgu_w_s, sgu_b_s, w_out_odd, norm_final):
    batch, seq, d = x.shape
    m = batch * seq
    x2d = x.reshape(m, d)

    wt_in = jnp.swapaxes(w_in_even[0], 0, 1)
    cw = CONV_W
    qw = N_HEADS * HEAD_DIM
    kvw = N_KV * HEAD_DIM
    o_q = 4 * cw
    o_kv = o_q + qw
    o_gl = o_kv + 6 * kvw
    o_ng = o_gl + 3 * N_HEADS

    h0, gl = rmsnorm_proj(x2d, norm_even[0], wt_in, o_gl, LANES)
    y_conv = conv_proj(h0, wt_in, conv_w[0], seq)
    z = matmul_t(h0, wt_in, o_q, qw + 6 * kvw, tn=1024, name="nsa_proj")
    z_ng = matmul_t(h0, wt_in, o_ng, qw, tn=1024, name="nsa_gate_proj")

    tabs, rot_t = _rotary_tables(seq)
    cmp_k = (cmp_k_pos[0], cmp_k_w1[0].astype(BF16).reshape(CMP_LEN, HEAD_DIM, HEAD_DIM),
             cmp_k_b1[0].reshape(1, HEAD_DIM), cmp_k_w2[0].astype(BF16))
    cmp_v = (cmp_v_pos[0], cmp_v_w1[0].astype(BF16).reshape(CMP_LEN, HEAD_DIM, HEAD_DIM),
             cmp_v_b1[0].reshape(1, HEAD_DIM), cmp_v_w2[0].astype(BF16))
    prep = nsa_prep(z, tabs, cmp_k, cmp_v, batch, seq, col0=qw // HEAD_DIM)
    y_nsa = nsa_attention(z, z_ng, gl, rot_t, _overlap_t(seq), prep, batch, seq)

    x1 = out_proj(y_conv, 0, y_nsa, 0, w_out_even[0].astype(BF16), x2d, name="out_proj_even")

    h1 = rmsnorm(x1, norm_odd[0], BF16)
    w_in1 = w_in_odd[0]
    mix = sgu_mix(h1, w_in1, SGU_W, sgu_ln_g[0], sgu_ln_b[0], sgu_w_s[0], sgu_b_s[0])
    act = sgu_gate(h1, w_in1, 0, 2 * SGU_W, mix)
    x2 = out_proj(act, 0, act, 1, w_out_odd[0].astype(BF16), x1, name="out_proj_odd")

    out = rmsnorm(x2, norm_final, F32)
    return out.reshape(batch, seq, d)
```

```python
import functools
import math

import jax
import jax.numpy as jnp
from jax import lax
from jax.experimental import pallas as pl
from jax.experimental.pallas import tpu as pltpu

F32 = jnp.float32
BF16 = jnp.bfloat16

D_MODEL = 2048
MIX = 2 * D_MODEL
CONV_W = MIX // 2
CONV_K = 3
HEAD_DIM = 128
N_HEADS = 16
N_KV = 4
GQA = N_HEADS // N_KV
ROT_DIM = HEAD_DIM // 4
ROPE_THETA = 500000.0
CMP_LEN = 32
CMP_STRIDE = 16
SLC_LEN = 64
N_SEL = 8
WINDOW = 512
SGU_W = MIX
CHUNK = 128
N_GROUPS = 16
GROUP_W = SGU_W // N_GROUPS
EPS = 1e-6

LANES = 128
SUBLANES = 8
SLC_SHIFT = 6
HALO = 16
KEY_TILE = LANES
XPOSE_ROWS = 256
VMEM_LIMIT = 56 * 1024 * 1024

NEG = -1e30

EXT_DIM = 2 * HEAD_DIM


def _cparams(n_axes):
    return pltpu.CompilerParams(
        dimension_semantics=("arbitrary",) * n_axes, vmem_limit_bytes=VMEM_LIMIT)


def _rmsnorm_kernel(x_ref, g_ref, o_ref):
    x = x_ref[...]
    y = x * lax.rsqrt(jnp.mean(x * x, axis=-1, keepdims=True) + EPS)
    o_ref[...] = (y * g_ref[...]).astype(o_ref.dtype)


def _rmsnorm_proj_kernel(x_ref, g_ref, wt_ref, o_ref, p_ref, wb_ref):
    @pl.when(pl.program_id(0) == 0)
    def _():
        wb_ref[...] = wt_ref[...].T.astype(BF16)

    x = x_ref[...]
    y = x * lax.rsqrt(jnp.mean(x * x, axis=-1, keepdims=True) + EPS)
    hb = (y * g_ref[...]).astype(BF16)
    o_ref[...] = hb
    p_ref[...] = jnp.dot(hb, wb_ref[...], preferred_element_type=F32)


def rmsnorm(x2d, g, out_dtype, tm=512):
    m, d = x2d.shape
    return pl.pallas_call(
        _rmsnorm_kernel,
        out_shape=jax.ShapeDtypeStruct((m, d), out_dtype),
        grid=(m // tm,),
        in_specs=[pl.BlockSpec((tm, d), lambda i: (i, 0)),
                  pl.BlockSpec((1, d), lambda i: (0, 0))],
        out_specs=pl.BlockSpec((tm, d), lambda i: (i, 0)),
        compiler_params=_cparams(1),
        name="rmsnorm",
    )(x2d, g.reshape(1, d))


def rmsnorm_proj(x2d, g, wt, row0, n, tm=512):
    m, d = x2d.shape
    assert row0 % n == 0
    return pl.pallas_call(
        _rmsnorm_proj_kernel,
        out_shape=(jax.ShapeDtypeStruct((m, d), BF16), jax.ShapeDtypeStruct((m, n), F32)),
        grid=(m // tm,),
        in_specs=[pl.BlockSpec((tm, d), lambda i: (i, 0)),
                  pl.BlockSpec((1, d), lambda i: (0, 0)),
                  pl.BlockSpec((n, d), lambda i: (row0 // n, 0))],
        out_specs=(pl.BlockSpec((tm, d), lambda i: (i, 0)),
                   pl.BlockSpec((tm, n), lambda i: (i, 0))),
        scratch_shapes=[pltpu.VMEM((d, n), BF16)],
        compiler_params=_cparams(1),
        name="rmsnorm_gates",
    )(x2d, g.reshape(1, d), wt)


def _cast_once(w_refs, wb_refs, transposed=False):
    @pl.when(pl.program_id(1) == 0)
    def _():
        for w_ref, wb_ref in zip(w_refs, wb_refs):
            if transposed:
                for r in range(0, w_ref.shape[0], XPOSE_ROWS):
                    wb_ref[:, r:r + XPOSE_ROWS] = w_ref[r:r + XPOSE_ROWS, :].T.astype(BF16)
            else:
                wb_ref[...] = w_ref[...].astype(BF16)


def _matmul_t_kernel(a_ref, wt_ref, o_ref, wb_ref):
    _cast_once([wt_ref], [wb_ref], transposed=True)
    o_ref[...] = jnp.dot(a_ref[...], wb_ref[...], preferred_element_type=F32)


def matmul_t(a, wt, row0, n, tm=1024, tn=512, name="proj"):
    m, k = a.shape
    assert n % tn == 0 and row0 % HALO == 0
    if row0 % tn == 0:
        wspec = pl.BlockSpec((tn, k), lambda j, i: (row0 // tn + j, 0))
    else:
        wspec = pl.BlockSpec((pl.Element(tn), pl.Element(k)),
                             lambda j, i: (pl.multiple_of(row0 + j * tn, HALO), 0))
    return pl.pallas_call(
        _matmul_t_kernel,
        out_shape=jax.ShapeDtypeStruct((m, n), F32),
        grid=(n // tn, m // tm),
        in_specs=[pl.BlockSpec((tm, k), lambda j, i: (i, 0)), wspec],
        out_specs=pl.BlockSpec((tm, tn), lambda j, i: (i, j)),
        scratch_shapes=[pltpu.VMEM((k, tn), BF16)],
        compiler_params=_cparams(2),
        name=name,
    )(a, wt)


def _conv_proj_kernel(h_ref, hp_ref, wb32_ref, wc32_ref, wh32_ref, wg32_ref, cw_ref, o_ref,
                      wb_ref, wc_ref, wh_ref, wg_ref, *, tiles_per_seq):
    _cast_once([wb32_ref, wc32_ref, wh32_ref, wg32_ref], [wb_ref, wc_ref, wh_ref, wg_ref],
               transposed=True)
    i = pl.program_id(1)
    cw = cw_ref[...]
    hp = hp_ref[...]
    pp = (jnp.dot(hp, wc_ref[...], preferred_element_type=F32)
          * jnp.dot(hp, wh_ref[...], preferred_element_type=F32))
    pp = jnp.where(i % tiles_per_seq == 0, 0.0, pp)
    prev1, prev2 = pp[HALO - 1:HALO, :], pp[HALO - 2:HALO - 1, :]
    h = h_ref[...]
    cb = jnp.dot(h, wb_ref[...], preferred_element_type=F32)
    cc = jnp.dot(h, wc_ref[...], preferred_element_type=F32)
    ch = jnp.dot(h, wh_ref[...], preferred_element_type=F32)
    cg = jnp.dot(h, wg_ref[...], preferred_element_type=F32)
    p = cc * ch
    row = lax.broadcasted_iota(jnp.int32, p.shape, 0)
    p1 = jnp.where(row == 0, prev1, pltpu.roll(p, 1, axis=0))
    p2 = jnp.where(row == 0, prev2, jnp.where(row == 1, prev1, pltpu.roll(p, 2, axis=0)))
    conv = cw[0:1, :] * p2 + cw[1:2, :] * p1 + cw[2:3, :] * p
    o_ref[...] = (cb * conv * (cg * jax.nn.sigmoid(cg))).astype(o_ref.dtype)


def conv_proj(h, wt, conv_w, seq, tm=1024, tn=256):
    m, k = h.shape
    n = conv_w.shape[1]
    tiles_per_seq = seq // tm
    halo_per_tile = tm // HALO
    nj = n // tn

    def wspec(which):
        return pl.BlockSpec((tn, k), lambda j, i: (which * nj + j, 0))

    return pl.pallas_call(
        functools.partial(_conv_proj_kernel, tiles_per_seq=tiles_per_seq),
        out_shape=jax.ShapeDtypeStruct((m, n), BF16),
        grid=(n // tn, m // tm),
        in_specs=[pl.BlockSpec((tm, k), lambda j, i: (i, 0)),
                  pl.BlockSpec((HALO, k), lambda j, i: (jnp.maximum(i * halo_per_tile - 1, 0), 0)),
                  wspec(0), wspec(1), wspec(2), wspec(3),
                  pl.BlockSpec((CONV_K, tn), lambda j, i: (0, j))],
        out_specs=pl.BlockSpec((tm, tn), lambda j, i: (i, j)),
        scratch_shapes=[pltpu.VMEM((k, tn), BF16)] * 4,
        compiler_params=_cparams(2),
        name="conv_proj",
    )(h, h, wt, wt, wt, wt, conv_w)


def _rotary(x, c, sa, sb):
    half = ROT_DIM // 2
    return (x * c + pltpu.roll(x, half, axis=1) * sa
            + pltpu.roll(x, HEAD_DIM - half, axis=1) * sb)


def _compress(src_ref, pos_ref, w1_ref, b1_ref, w2_ref):
    n_rows = src_ref.shape[0] // CMP_STRIDE
    p_acc = jnp.zeros((n_rows, HEAD_DIM), F32)
    q_acc = jnp.zeros((n_rows, HEAD_DIM), F32)
    for r in range(CMP_STRIDE):
        s_r = src_ref[pl.ds(r, n_rows, stride=CMP_STRIDE), :]
        a_r = (s_r + pos_ref[r:r + 1, :]).astype(BF16)
        b_r = (s_r + pos_ref[CMP_STRIDE + r:CMP_STRIDE + r + 1, :]).astype(BF16)
        p_acc += jnp.dot(a_r, w1_ref[r], preferred_element_type=F32)
        q_acc += jnp.dot(b_r, w1_ref[CMP_STRIDE + r], preferred_element_type=F32)
    hid = p_acc + pltpu.roll(q_acc, n_rows - 1, axis=0) + b1_ref[...]
    act = (hid * jax.nn.sigmoid(hid)).astype(BF16)
    return jnp.dot(act, w2_ref[...], preferred_element_type=F32)


def _nsa_prep_kernel(kc_ref, vc_ref, ks_ref, vs_ref, kw_ref, vw_ref, c_ref, sa_ref, sb_ref,
                     kpos_ref, kw1_ref, kb1_ref, kw2_ref, vpos_ref, vw1_ref, vb1_ref, vw2_ref,
                     ksx_ref, vst_ref, kwx_ref, vwt_ref, kcmp_ref, vcmp_ref):
    c, sa, sb = c_ref[...], sa_ref[...], sb_ref[...]
    seq = ks_ref.shape[0]
    n_blk = seq // SLC_LEN
    flag_w = EXT_DIM - HEAD_DIM
    blk = lax.broadcasted_iota(jnp.int32, (seq, flag_w), 0) >> SLC_SHIFT
    onehot = jnp.where(blk == lax.broadcasted_iota(jnp.int32, (seq, flag_w), 1), 1.0, 0.0)
    ksx_ref[:, 0:HEAD_DIM] = _rotary(ks_ref[...], c, sa, sb).astype(BF16)
    ksx_ref[:, HEAD_DIM:] = onehot.astype(BF16)
    pad_flag = jnp.where(lax.broadcasted_iota(jnp.int32, (WINDOW, EXT_DIM), 1) == HEAD_DIM + n_blk, 1.0, 0.0)
    kwx_ref[0:WINDOW, :] = pad_flag.astype(BF16)
    kwx_ref[WINDOW:, 0:HEAD_DIM] = _rotary(kw_ref[...], c, sa, sb).astype(BF16)
    kwx_ref[WINDOW:, HEAD_DIM:] = jnp.zeros((seq, flag_w), BF16)
    n_tiles = seq // KEY_TILE
    pad_tiles = WINDOW // KEY_TILE
    for t in range(pad_tiles):
        vwt_ref[t] = jnp.zeros((HEAD_DIM, KEY_TILE), BF16)
    for t in range(n_tiles):
        rows = pl.ds(t * KEY_TILE, KEY_TILE)
        vst_ref[t] = vs_ref[rows, :].T.astype(BF16)
        vwt_ref[pad_tiles + t] = vw_ref[rows, :].T.astype(BF16)
    kcmp_ref[...] = _compress(kc_ref, kpos_ref, kw1_ref, kb1_ref, kw2_ref).astype(BF16)
    vcmp_ref[...] = _compress(vc_ref, vpos_ref, vw1_ref, vb1_ref, vw2_ref).astype(BF16)


def nsa_prep(z, tabs, cmp_k, cmp_v, batch, seq, col0):
    def zspec(which):
        return pl.BlockSpec((seq, HEAD_DIM), lambda b, g, w=which: (b, col0 + w * N_KV + g))

    tab = pl.BlockSpec((seq, HEAD_DIM), lambda b, g: (0, 0))

    def wspecs():
        return [pl.BlockSpec((CMP_LEN, HEAD_DIM), lambda b, g: (0, 0)),
                pl.BlockSpec((CMP_LEN, HEAD_DIM, HEAD_DIM), lambda b, g: (0, 0, 0)),
                pl.BlockSpec((1, HEAD_DIM), lambda b, g: (0, 0)),
                pl.BlockSpec((HEAD_DIM, HEAD_DIM), lambda b, g: (0, 0))]

    def out(shape):
        nd = len(shape)
        spec = pl.BlockSpec((None, None) + shape, lambda b, g: (b, g) + (0,) * nd)
        return spec, jax.ShapeDtypeStruct((batch, N_KV) + shape, BF16)

    n_tiles = seq // KEY_TILE
    pad_tiles = WINDOW // KEY_TILE
    outs = [out((seq, EXT_DIM)), out((n_tiles, HEAD_DIM, KEY_TILE)),
            out((WINDOW + seq, EXT_DIM)), out((pad_tiles + n_tiles, HEAD_DIM, KEY_TILE)),
            out((seq // CMP_STRIDE, HEAD_DIM)), out((seq // CMP_STRIDE, HEAD_DIM))]
    return pl.pallas_call(
        _nsa_prep_kernel,
        out_shape=tuple(o[1] for o in outs),
        grid=(batch, N_KV),
        in_specs=[zspec(0), zspec(1), zspec(2), zspec(3), zspec(4), zspec(5), tab, tab, tab]
        + wspecs() + wspecs(),
        out_specs=tuple(o[0] for o in outs),
        compiler_params=_cparams(2),
        name="nsa_prep",
    )(z, z, z, z, z, z, *tabs, *cmp_k, *cmp_v)


def _nsa_attn_kernel(q_ref, ng_ref, gl_ref, rot_ref, ov_ref,
                     ksx_ref, vst_ref, kwx_ref, vwt_ref, kcmp_ref, vcmp_ref,
                     o_ref, acc_ref, qx_ref, glt_ref, s_ref, m_ref, l_ref, *, tq):
    g = pl.program_id(1)
    i = pl.program_id(2)
    q0 = i * tq
    n_blk = ov_ref.shape[0]
    half = ROT_DIM // 2
    qscale = (HEAD_DIM ** -0.5) * math.log2(math.e)

    def head_lanes(n):
        return slice(n * tq, (n + 1) * tq)

    def lanes4(x):
        return jnp.concatenate([x] * GQA, axis=1)

    q = q_ref[...]
    rot = rot_ref[...]
    cos2, sin2 = rot[:ROT_DIM], rot[ROT_DIM:]
    qts = []
    for n in range(GQA):
        qt = (q[:, n * HEAD_DIM:(n + 1) * HEAD_DIM] * qscale).T
        qts.append(qt)
        top = qt[:ROT_DIM]
        swapped = jnp.concatenate([top[half:], top[:half]], axis=0)
        qrt = jnp.concatenate([top * cos2 + swapped * sin2, qt[ROT_DIM:]], axis=0)
        qx_ref[0:HEAD_DIM, head_lanes(n)] = qrt.astype(BF16)
    qt_all = jnp.concatenate(qts, axis=1).astype(BF16)

    tlane = q0 + lax.broadcasted_iota(jnp.int32, (1, tq), 1)

    n_cmp = kcmp_ref.shape[0]
    s_c = jnp.dot(kcmp_ref[...], qt_all, preferred_element_type=F32)
    cend = lax.broadcasted_iota(jnp.int32, (n_cmp, tq), 0) * CMP_STRIDE + (CMP_LEN - 1)
    s_c = s_c + lanes4(jnp.where(cend <= tlane, 0.0, NEG))
    any_c = lanes4(tlane) >= CMP_LEN - 1
    m_c = jnp.where(any_c, jnp.max(s_c, axis=0, keepdims=True), 0.0)
    e_c = jnp.exp2(s_c - m_c)
    den_c = jnp.sum(e_c, axis=0, keepdims=True)
    p_c = e_c / jnp.where(den_c > 0, den_c, 1.0)
    vcmp_t = vcmp_ref[...].astype(F32).T.astype(BF16)
    o_c = jnp.dot(vcmp_t, p_c.astype(BF16), preferred_element_type=F32)

    p_sum = p_c[:, head_lanes(0)]
    for n in range(1, GQA):
        p_sum = p_sum + p_c[:, head_lanes(n)]
    imp = jnp.dot(ov_ref[...], p_sum, preferred_element_type=F32,
                  precision=lax.Precision.HIGHEST)
    jblk = lax.broadcasted_iota(jnp.int32, (n_blk, tq), 0)
    forced = (jblk == 0) | (jblk == (tlane >> SLC_SHIFT))
    imp = jnp.where(forced, jnp.inf, jnp.where(jblk * SLC_LEN <= tlane, imp, -jnp.inf))
    n_grp = n_blk // SUBLANES
    grp = [imp[a * SUBLANES:(a + 1) * SUBLANES] for a in range(n_grp)]
    sub = lax.broadcasted_iota(jnp.int32, (SUBLANES, tq), 0)
    cnt = [jnp.zeros((SUBLANES, tq), F32) for _ in range(n_grp)]
    for r in range(n_blk):
        row = imp[r:r + 1, :]
        a_r, r_in = divmod(r, SUBLANES)
        for a in range(n_grp):
            if a > a_r:
                cnt[a] = cnt[a] + jnp.where(row >= grp[a], 1.0, 0.0)
            elif a < a_r:
                cnt[a] = cnt[a] + jnp.where(row > grp[a], 1.0, 0.0)
            else:
                cnt[a] = cnt[a] + jnp.where(sub > r_in, jnp.where(row >= grp[a], 1.0, 0.0),
                                            jnp.where(row > grp[a], 1.0, 0.0))
    cnt = jnp.concatenate(cnt, axis=0)
    sel_bias = jnp.where(cnt < N_SEL, jnp.where(imp > -jnp.inf, 0.0, NEG), NEG)

    qx_ref[HEAD_DIM:HEAD_DIM + n_blk, :] = lanes4(sel_bias).astype(BF16)
    n_rest = EXT_DIM - HEAD_DIM - n_blk
    first = lax.broadcasted_iota(jnp.int32, (n_rest, GQA * tq), 0) == 0
    qx_ref[HEAD_DIM + n_blk:, :] = jnp.where(first, NEG, 0.0).astype(BF16)

    kk = lax.broadcasted_iota(jnp.int32, (tq, tq), 0)
    tt = lax.broadcasted_iota(jnp.int32, (tq, tq), 1)
    tri_diag = jnp.where(kk <= tt, 0.0, NEG)
    tri_old = jnp.where(kk > tt, 0.0, NEG)

    tiles = tq // KEY_TILE

    def sel_scores(ci):
        k0 = pl.multiple_of(ci * tq, tq)
        return jnp.dot(ksx_ref[pl.ds(k0, tq), :], qx_ref[...], preferred_element_type=F32)

    def sel_update(ci, s):
        vblk = jnp.concatenate([vst_ref[ci * tiles + r] for r in range(tiles)], axis=1)
        m_old = m_ref[...]
        m_new = jnp.maximum(m_old, jnp.max(s, axis=0, keepdims=True))
        alpha = jnp.exp2(m_old - m_new)
        p = jnp.exp2(s - m_new)
        l_ref[...] = alpha * l_ref[...] + jnp.sum(p, axis=0, keepdims=True)
        acc_ref[...] = alpha * acc_ref[...] + jnp.dot(vblk, p.astype(BF16), preferred_element_type=F32)
        m_ref[...] = m_new

    def sel_pair(pi, carry):
        c0 = 2 * pi
        s_ref[1] = sel_scores(c0 + 1)
        sel_update(c0, s_ref[0])
        s_ref[0] = sel_scores(c0 + 2)
        sel_update(c0 + 1, s_ref[1])
        return carry

    m_ref[...] = jnp.full(m_ref.shape, NEG, F32)
    l_ref[...] = jnp.zeros(l_ref.shape, F32)
    acc_ref[...] = jnp.zeros(acc_ref.shape, F32)
    s_ref[0] = sel_scores(0)
    lax.fori_loop(0, i // 2, sel_pair, 0)

    @pl.when(i % 2 == 1)
    def _():
        s_ref[1] = sel_scores(i)
        sel_update(i - 1, s_ref[0])

    sel_update(i, s_ref[i % 2] + lanes4(tri_diag))
    o_s = acc_ref[...] / l_ref[...]

    n_wt = WINDOW // tq
    kwin = kwx_ref[pl.ds(pl.multiple_of(q0, tq), WINDOW + tq), :]
    vt0 = q0 // KEY_TILE
    vwin = jnp.concatenate([vwt_ref[vt0 + r] for r in range((WINDOW + tq) // KEY_TILE)], axis=1)
    s_w = jnp.dot(kwin, qx_ref[...], preferred_element_type=F32)
    s_w = jnp.concatenate([s_w[:tq] + lanes4(tri_old), s_w[tq:n_wt * tq],
                           s_w[n_wt * tq:] + lanes4(tri_diag)], axis=0)
    e_w = jnp.exp2(s_w - jnp.max(s_w, axis=0, keepdims=True))
    o_w = (jnp.dot(vwin, e_w.astype(BF16), preferred_element_type=F32)
           / jnp.sum(e_w, axis=0, keepdims=True))

    ng = ng_ref[...]
    glt_ref[...] = jax.nn.sigmoid(gl_ref[...]).T
    for n in range(GQA):
        head = g * GQA + n
        gc, gs, gw = (glt_ref[pl.ds(j * N_HEADS + head, 1), :] for j in range(3))
        sl = head_lanes(n)
        o = (gc * o_c[:, sl] + gs * o_s[:, sl] + gw * o_w[:, sl]).T
        ngh = ng[:, n * HEAD_DIM:(n + 1) * HEAD_DIM]
        o_ref[:, n * HEAD_DIM:(n + 1) * HEAD_DIM] = (o * (ngh * jax.nn.sigmoid(ngh))).astype(o_ref.dtype)


def nsa_attention(z, z_ng, gl, rot_t, overlap_t, prep, batch, seq, tq=256):
    ksx, vst, kwx, vwt, kcmp, vcmp = prep
    assert tq % KEY_TILE == 0 and WINDOW % tq == 0 and WINDOW >= 2 * tq
    nq = seq // tq
    qw = GQA * HEAD_DIM

    def whole(a):
        nd = a.ndim - 2
        return pl.BlockSpec((None, None) + a.shape[2:], lambda b, g, i: (b, g) + (0,) * nd)

    return pl.pallas_call(
        functools.partial(_nsa_attn_kernel, tq=tq),
        out_shape=jax.ShapeDtypeStruct((batch * seq, N_HEADS * HEAD_DIM), BF16),
        grid=(batch, N_KV, nq),
        in_specs=[pl.BlockSpec((tq, qw), lambda b, g, i: (b * nq + i, g)),
                  pl.BlockSpec((tq, qw), lambda b, g, i: (b * nq + i, g)),
                  pl.BlockSpec((tq, gl.shape[1]), lambda b, g, i: (b * nq + i, 0)),
                  pl.BlockSpec((rot_t.shape[0], tq), lambda b, g, i: (0, i)),
                  pl.BlockSpec(overlap_t.shape, lambda b, g, i: (0, 0)),
                  whole(ksx), whole(vst), whole(kwx), whole(vwt), whole(kcmp), whole(vcmp)],
        out_specs=pl.BlockSpec((tq, qw), lambda b, g, i: (b * nq + i, g)),
        scratch_shapes=[pltpu.VMEM((HEAD_DIM, GQA * tq), F32),
                        pltpu.VMEM((EXT_DIM, GQA * tq), BF16),
                        pltpu.VMEM((gl.shape[1], tq), F32),
                        pltpu.VMEM((2, tq, GQA * tq), F32),
                        pltpu.VMEM((1, GQA * tq), F32), pltpu.VMEM((1, GQA * tq), F32)],
        compiler_params=_cparams(3),
        name="nsa_attention",
    )(z, z_ng, gl, rot_t, overlap_t, ksx, vst, kwx, vwt, kcmp, vcmp)


def _out_proj_norm_kernel(a0_ref, a1_ref, w_ref, x_ref, g_ref, *o_refs, kh, emit_residual):
    acc = jnp.dot(a0_ref[...], w_ref[0:kh, :], preferred_element_type=F32)
    acc += jnp.dot(a1_ref[...], w_ref[kh:, :], preferred_element_type=F32)
    x = x_ref[...] + acc
    y = x * lax.rsqrt(jnp.mean(x * x, axis=-1, keepdims=True) + EPS)
    o_refs[-1][...] = (y * g_ref[...]).astype(o_refs[-1].dtype)
    if emit_residual:
        o_refs[0][...] = x


def out_proj_norm(a0, a0_blk, a1, a1_blk, w, x2d, g, norm_dtype, emit_residual, tm=512, name="out_proj"):
    m, n = x2d.shape
    kh = w.shape[0] // 2
    row = pl.BlockSpec((tm, n), lambda i: (i, 0))
    out_shape = [jax.ShapeDtypeStruct((m, n), norm_dtype)]
    if emit_residual:
        out_shape.insert(0, jax.ShapeDtypeStruct((m, n), F32))
    return pl.pallas_call(
        functools.partial(_out_proj_norm_kernel, kh=kh, emit_residual=emit_residual),
        out_shape=tuple(out_shape),
        grid=(m // tm,),
        in_specs=[pl.BlockSpec((tm, kh), lambda i: (i, a0_blk)),
                  pl.BlockSpec((tm, kh), lambda i: (i, a1_blk)),
                  pl.BlockSpec(w.shape, lambda i: (0, 0), pipeline_mode=pl.Buffered(1)),
                  row,
                  pl.BlockSpec((1, n), lambda i: (0, 0))],
        out_specs=tuple(row for _ in out_shape),
        compiler_params=_cparams(1),
        name=name,
    )(a0, a1, w, x2d, g.reshape(1, n))


def _sgu_v_kernel(h_ref, w_ref, v_ref, mu_ref, rstd_ref, wb_ref, c_ref, s1_ref, s2_ref, *, n_col_tiles):
    j = pl.program_id(1)

    @pl.when(pl.program_id(0) == 0)
    def _():
        wb_ref[j] = w_ref[...].astype(BF16)

    v = jnp.dot(h_ref[...], wb_ref[j], preferred_element_type=F32)
    v_ref[...] = v
    tm, tn = v.shape
    reps = tn // LANES

    @pl.when(j == 0)
    def _():
        c_ref[...] = jnp.broadcast_to(jnp.sum(v, axis=-1, keepdims=True) / tn, c_ref.shape)
        s1_ref[...] = jnp.zeros(s1_ref.shape, F32)
        s2_ref[...] = jnp.zeros(s2_ref.shape, F32)

    c = c_ref[...]
    s1, s2 = s1_ref[...], s2_ref[...]
    for r in range(reps):
        d = v[:, r * LANES:(r + 1) * LANES] - c
        s1 = s1 + d
        s2 = s2 + d * d
    s1_ref[...] = s1
    s2_ref[...] = s2

    @pl.when(j == n_col_tiles - 1)
    def _():
        width = n_col_tiles * tn
        mean_d = jnp.sum(s1, axis=-1, keepdims=True) / width
        var = jnp.sum(s2, axis=-1, keepdims=True) / width - mean_d * mean_d
        mu_ref[...] = c + mean_d
        rstd_ref[...] = jnp.broadcast_to(lax.rsqrt(var + EPS), rstd_ref.shape)


def sgu_v(h, w, col0, tm=512, tn=512):
    m, k = h.shape
    nj = SGU_W // tn
    j0 = col0 // tn
    stat = jax.ShapeDtypeStruct((m, LANES), F32)
    stat_spec = pl.BlockSpec((tm, LANES), lambda i, j: (i, 0))
    return pl.pallas_call(
        functools.partial(_sgu_v_kernel, n_col_tiles=nj),
        out_shape=(jax.ShapeDtypeStruct((m, SGU_W), F32), stat, stat),
        grid=(m // tm, nj),
        in_specs=[pl.BlockSpec((tm, k), lambda i, j: (i, 0)),
                  pl.BlockSpec((k, tn), lambda i, j: (0, j0 + jnp.where(i == 0, j, 0)))],
        out_specs=(pl.BlockSpec((tm, tn), lambda i, j: (i, j)), stat_spec, stat_spec),
        scratch_shapes=[pltpu.VMEM((nj, k, tn), BF16)] + [pltpu.VMEM((tm, LANES), F32)] * 3,
        compiler_params=_cparams(2),
        name="sgu_v",
    )(h, w)


def _sgu_gate_kernel(h_ref, wu32_ref, wz32_ref, v_ref, mu_ref, rstd_ref, lg_ref, lb_ref, ws_ref, bs_ref,
                     o_ref, wu_ref, wz_ref):
    _cast_once([wu32_ref, wz32_ref], [wu_ref, wz_ref])
    h = h_ref[...]
    tm, tn = v_ref.shape
    reps = GROUP_W // LANES
    mu = jnp.concatenate([mu_ref[...]] * reps, axis=1)
    rstd = jnp.concatenate([rstd_ref[...]] * reps, axis=1)
    tri = (lax.broadcasted_iota(jnp.int32, (CHUNK, CHUNK), 1)
           <= lax.broadcasted_iota(jnp.int32, (CHUNK, CHUNK), 0))
    for gi in range(tn // GROUP_W):
        cols = slice(gi * GROUP_W, (gi + 1) * GROUP_W)
        u = jnp.dot(h, wu_ref[:, cols], preferred_element_type=F32)
        zg = jnp.dot(h, wz_ref[:, cols], preferred_element_type=F32)
        vn = ((v_ref[:, cols] - mu) * rstd * lg_ref[:, cols] + lb_ref[:, cols]).astype(BF16)
        wsm = jnp.where(tri, ws_ref[gi], 0.0).astype(BF16)
        bsg = bs_ref[gi]
        mix = jnp.concatenate(
            [jnp.dot(wsm, vn[c * CHUNK:(c + 1) * CHUNK], preferred_element_type=F32) + bsg
             for c in range(tm // CHUNK)], axis=0)
        o_ref[:, cols] = (u * mix * (zg * jax.nn.sigmoid(zg))).astype(o_ref.dtype)


def sgu_gate(h, w, col_u, col_z, v, mu, rstd, ln_g, ln_b, w_s, b_s, tm=1024, tn=512):
    m, k = h.shape
    n = SGU_W
    nj = n // tn
    ju, jz = col_u // tn, col_z // tn
    gpt = tn // GROUP_W
    stat_spec = pl.BlockSpec((tm, LANES), lambda j, i: (i, 0))
    return pl.pallas_call(
        _sgu_gate_kernel,
        out_shape=jax.ShapeDtypeStruct((m, n), BF16),
        grid=(nj, m // tm),
        in_specs=[pl.BlockSpec((tm, k), lambda j, i: (i, 0)),
                  pl.BlockSpec((k, tn), lambda j, i: (0, ju + j)),
                  pl.BlockSpec((k, tn), lambda j, i: (0, jz + j)),
                  pl.BlockSpec((tm, tn), lambda j, i: (i, j)),
                  stat_spec, stat_spec,
                  pl.BlockSpec((1, tn), lambda j, i: (0, j)),
                  pl.BlockSpec((1, tn), lambda j, i: (0, j)),
                  pl.BlockSpec((gpt, CHUNK, CHUNK), lambda j, i: (j, 0, 0)),
                  pl.BlockSpec((gpt, CHUNK, 1), lambda j, i: (j, 0, 0))],
        out_specs=pl.BlockSpec((tm, tn), lambda j, i: (i, j)),
        scratch_shapes=[pltpu.VMEM((k, tn), BF16)] * 2,
        compiler_params=_cparams(2),
        name="sgu_gate",
    )(h, w, w, v, mu, rstd, ln_g.reshape(1, n), ln_b.reshape(1, n), w_s, b_s.reshape(N_GROUPS, CHUNK, 1))


def _rotary_tables(seq):
    half = ROT_DIM // 2
    inv_freq = jnp.power(ROPE_THETA, -jnp.arange(half, dtype=F32) * 2.0 / ROT_DIM)
    ang = jnp.arange(seq).astype(F32)[:, None] * inv_freq[None, :]
    cos, sin = jnp.cos(ang), jnp.sin(ang)
    rest = HEAD_DIM - ROT_DIM
    c = jnp.concatenate([cos, cos, jnp.ones((seq, rest), F32)], axis=1)
    sa = jnp.concatenate([jnp.zeros((seq, half), F32), sin, jnp.zeros((seq, rest), F32)], axis=1)
    sb = jnp.concatenate([-sin, jnp.zeros((seq, half + rest), F32)], axis=1)
    rot_t = jnp.concatenate([cos, cos, -sin, sin], axis=1).T
    return (c, sa, sb), rot_t


def _overlap_t(seq):
    n_blk = seq // SLC_LEN
    n_rows = seq // CMP_STRIDE
    n_cmp = (seq - CMP_LEN) // CMP_STRIDE + 1
    cs = jnp.arange(n_rows) * CMP_STRIDE
    bs = jnp.arange(n_blk) * SLC_LEN
    ov = (cs[None, :] < bs[:, None] + SLC_LEN) & (cs[None, :] + CMP_LEN > bs[:, None])
    ov = ov & (jnp.arange(n_rows)[None, :] < n_cmp)
    return ov.astype(F32)


def kernel(x, norm_even, w_in_even, conv_w, cmp_k_pos, cmp_k_w1, cmp_k_b1, cmp_k_w2, cmp_v_pos, cmp_v_w1, cmp_v_b1, cmp_v_w2, w_out_even, norm_odd, w_in_odd, sgu_ln_g, sgu_ln_b, sgu_w_s, sgu_b_s, w_out_odd, norm_final):
    batch, seq, d = x.shape
    m = batch * seq
    x2d = x.reshape(m, d)

    wt_in = jnp.swapaxes(w_in_even[0], 0, 1)
    cw = CONV_W
    qw = N_HEADS * HEAD_DIM
    kvw = N_KV * HEAD_DIM
    o_q = 4 * cw
    o_kv = o_q + qw
    o_gl = o_kv + 6 * kvw
    o_ng = o_gl + 3 * N_HEADS

    h0, gl = rmsnorm_proj(x2d, norm_even[0], wt_in, o_gl, LANES)
    y_conv = conv_proj(h0, wt_in, conv_w[0], seq)
    z = matmul_t(h0, wt_in, o_q, qw + 6 * kvw, tn=1024, name="nsa_proj")
    z_ng = matmul_t(h0, wt_in, o_ng, qw, tn=1024, name="nsa_gate_proj")

    tabs, rot_t = _rotary_tables(seq)
    cmp_k = (cmp_k_pos[0], cmp_k_w1[0].astype(BF16).reshape(CMP_LEN, HEAD_DIM, HEAD_DIM),
             cmp_k_b1[0].reshape(1, HEAD_DIM), cmp_k_w2[0].astype(BF16))
    cmp_v = (cmp_v_pos[0], cmp_v_w1[0].astype(BF16).reshape(CMP_LEN, HEAD_DIM, HEAD_DIM),
             cmp_v_b1[0].reshape(1, HEAD_DIM), cmp_v_w2[0].astype(BF16))
    prep = nsa_prep(z, tabs, cmp_k, cmp_v, batch, seq, col0=qw // HEAD_DIM)
    y_nsa = nsa_attention(z, z_ng, gl, rot_t, _overlap_t(seq), prep, batch, seq)

    x1, h1 = out_proj_norm(y_conv, 0, y_nsa, 0, w_out_even[0].astype(BF16), x2d, norm_odd[0],
                           BF16, True, name="out_proj_even")

    w_in1 = w_in_odd[0]
    v, mu, rstd = sgu_v(h1, w_in1, SGU_W)
    act = sgu_gate(h1, w_in1, 0, 2 * SGU_W, v, mu, rstd,
                   sgu_ln_g[0], sgu_ln_b[0], sgu_w_s[0], sgu_b_s[0])
    (out,) = out_proj_norm(act, 0, act, 1, w_out_odd[0].astype(BF16), x1, norm_final,
                           F32, False, name="out_proj_odd")
    return out.reshape(batch, seq, d)
```

```python
import functools
import math

import jax
import jax.numpy as jnp
from jax import lax
from jax.experimental import pallas as pl
from jax.experimental.pallas import tpu as pltpu

F32 = jnp.float32
BF16 = jnp.bfloat16

D_MODEL = 2048
MIX = 2 * D_MODEL
CONV_W = MIX // 2
CONV_K = 3
HEAD_DIM = 128
N_HEADS = 16
N_KV = 4
GQA = N_HEADS // N_KV
ROT_DIM = HEAD_DIM // 4
ROPE_THETA = 500000.0
CMP_LEN = 32
CMP_STRIDE = 16
SLC_LEN = 64
N_SEL = 8
WINDOW = 512
SGU_W = MIX
CHUNK = 128
N_GROUPS = 16
GROUP_W = SGU_W // N_GROUPS
EPS = 1e-6

LANES = 128
SUBLANES = 8
SLC_SHIFT = 6
HALO = 16
KEY_TILE = LANES
XPOSE_ROWS = 256
VMEM_LIMIT = 56 * 1024 * 1024

NEG = -1e30

EXT_DIM = 2 * HEAD_DIM


def _cparams(n_axes):
    return pltpu.CompilerParams(
        dimension_semantics=("arbitrary",) * n_axes, vmem_limit_bytes=VMEM_LIMIT)


def _rmsnorm_kernel(x_ref, g_ref, o_ref):
    x = x_ref[...]
    y = x * lax.rsqrt(jnp.mean(x * x, axis=-1, keepdims=True) + EPS)
    o_ref[...] = (y * g_ref[...]).astype(o_ref.dtype)


def _rmsnorm_proj_kernel(x_ref, g_ref, wt_ref, o_ref, p_ref, wb_ref):
    @pl.when(pl.program_id(0) == 0)
    def _():
        wb_ref[...] = wt_ref[...].T.astype(BF16)

    x = x_ref[...]
    y = x * lax.rsqrt(jnp.mean(x * x, axis=-1, keepdims=True) + EPS)
    hb = (y * g_ref[...]).astype(BF16)
    o_ref[...] = hb
    p_ref[...] = jnp.dot(hb, wb_ref[...], preferred_element_type=F32)


def rmsnorm(x2d, g, out_dtype, tm=512):
    m, d = x2d.shape
    return pl.pallas_call(
        _rmsnorm_kernel,
        out_shape=jax.ShapeDtypeStruct((m, d), out_dtype),
        grid=(m // tm,),
        in_specs=[pl.BlockSpec((tm, d), lambda i: (i, 0)),
                  pl.BlockSpec((1, d), lambda i: (0, 0))],
        out_specs=pl.BlockSpec((tm, d), lambda i: (i, 0)),
        compiler_params=_cparams(1),
        name="rmsnorm",
    )(x2d, g.reshape(1, d))


def rmsnorm_proj(x2d, g, wt, row0, n, tm=512):
    m, d = x2d.shape
    assert row0 % n == 0
    return pl.pallas_call(
        _rmsnorm_proj_kernel,
        out_shape=(jax.ShapeDtypeStruct((m, d), BF16), jax.ShapeDtypeStruct((m, n), F32)),
        grid=(m // tm,),
        in_specs=[pl.BlockSpec((tm, d), lambda i: (i, 0)),
                  pl.BlockSpec((1, d), lambda i: (0, 0)),
                  pl.BlockSpec((n, d), lambda i: (row0 // n, 0))],
        out_specs=(pl.BlockSpec((tm, d), lambda i: (i, 0)),
                   pl.BlockSpec((tm, n), lambda i: (i, 0))),
        scratch_shapes=[pltpu.VMEM((d, n), BF16)],
        compiler_params=_cparams(1),
        name="rmsnorm_gates",
    )(x2d, g.reshape(1, d), wt)


def _cast_once(w_refs, wb_refs, transposed=False):
    @pl.when(pl.program_id(1) == 0)
    def _():
        for w_ref, wb_ref in zip(w_refs, wb_refs):
            if transposed:
                for r in range(0, w_ref.shape[0], XPOSE_ROWS):
                    wb_ref[:, r:r + XPOSE_ROWS] = w_ref[r:r + XPOSE_ROWS, :].T.astype(BF16)
            else:
                wb_ref[...] = w_ref[...].astype(BF16)


def _matmul_t_kernel(a_ref, wt_ref, o_ref, wb_ref):
    _cast_once([wt_ref], [wb_ref], transposed=True)
    o_ref[...] = jnp.dot(a_ref[...], wb_ref[...], preferred_element_type=F32)


def matmul_t(a, wt, row0, n, tm=1024, tn=512, name="proj"):
    m, k = a.shape
    assert n % tn == 0 and row0 % HALO == 0
    if row0 % tn == 0:
        wspec = pl.BlockSpec((tn, k), lambda j, i: (row0 // tn + j, 0))
    else:
        wspec = pl.BlockSpec((pl.Element(tn), pl.Element(k)),
                             lambda j, i: (pl.multiple_of(row0 + j * tn, HALO), 0))
    return pl.pallas_call(
        _matmul_t_kernel,
        out_shape=jax.ShapeDtypeStruct((m, n), F32),
        grid=(n // tn, m // tm),
        in_specs=[pl.BlockSpec((tm, k), lambda j, i: (i, 0)), wspec],
        out_specs=pl.BlockSpec((tm, tn), lambda j, i: (i, j)),
        scratch_shapes=[pltpu.VMEM((k, tn), BF16)],
        compiler_params=_cparams(2),
        name=name,
    )(a, wt)


def _conv_proj_kernel(h_ref, hp_ref, wb32_ref, wc32_ref, wh32_ref, wg32_ref, cw_ref, side32_ref,
                      o_ref, side_ref, wb_ref, wc_ref, wh_ref, wg_ref, *, tiles_per_seq):
    _cast_once([wb32_ref, wc32_ref, wh32_ref, wg32_ref], [wb_ref, wc_ref, wh_ref, wg_ref],
               transposed=True)
    side_ref[...] = side32_ref[...].astype(BF16)
    i = pl.program_id(1)
    cw = cw_ref[...]
    hp = hp_ref[...]
    pp = (jnp.dot(hp, wc_ref[...], preferred_element_type=F32)
          * jnp.dot(hp, wh_ref[...], preferred_element_type=F32))
    pp = jnp.where(i % tiles_per_seq == 0, 0.0, pp)
    prev1, prev2 = pp[HALO - 1:HALO, :], pp[HALO - 2:HALO - 1, :]
    h = h_ref[...]
    cb = jnp.dot(h, wb_ref[...], preferred_element_type=F32)
    cc = jnp.dot(h, wc_ref[...], preferred_element_type=F32)
    ch = jnp.dot(h, wh_ref[...], preferred_element_type=F32)
    cg = jnp.dot(h, wg_ref[...], preferred_element_type=F32)
    p = cc * ch
    row = lax.broadcasted_iota(jnp.int32, p.shape, 0)
    p1 = jnp.where(row == 0, prev1, pltpu.roll(p, 1, axis=0))
    p2 = jnp.where(row == 0, prev2, jnp.where(row == 1, prev1, pltpu.roll(p, 2, axis=0)))
    conv = cw[0:1, :] * p2 + cw[1:2, :] * p1 + cw[2:3, :] * p
    o_ref[...] = (cb * conv * (cg * jax.nn.sigmoid(cg))).astype(o_ref.dtype)


def conv_proj(h, wt, conv_w, seq, side, tm=1024, tn=256):
    m, k = h.shape
    n = conv_w.shape[1]
    tiles_per_seq = seq // tm
    halo_per_tile = tm // HALO
    nj, ni = n // tn, m // tm
    side_rows = side.shape[0] // (nj * ni)
    assert side_rows * nj * ni == side.shape[0] and side_rows % HALO == 0
    side_spec = pl.BlockSpec((side_rows, side.shape[1]), lambda j, i: (j * ni + i, 0))

    def wspec(which):
        return pl.BlockSpec((tn, k), lambda j, i: (which * nj + j, 0))

    return pl.pallas_call(
        functools.partial(_conv_proj_kernel, tiles_per_seq=tiles_per_seq),
        out_shape=(jax.ShapeDtypeStruct((m, n), BF16), jax.ShapeDtypeStruct(side.shape, BF16)),
        grid=(nj, ni),
        in_specs=[pl.BlockSpec((tm, k), lambda j, i: (i, 0)),
                  pl.BlockSpec((HALO, k), lambda j, i: (jnp.maximum(i * halo_per_tile - 1, 0), 0)),
                  wspec(0), wspec(1), wspec(2), wspec(3),
                  pl.BlockSpec((CONV_K, tn), lambda j, i: (0, j)),
                  side_spec],
        out_specs=(pl.BlockSpec((tm, tn), lambda j, i: (i, j)), side_spec),
        scratch_shapes=[pltpu.VMEM((k, tn), BF16)] * 4,
        compiler_params=_cparams(2),
        name="conv_proj",
    )(h, h, wt, wt, wt, wt, conv_w, side)


def _rotary(x, c, sa, sb):
    half = ROT_DIM // 2
    return (x * c + pltpu.roll(x, half, axis=1) * sa
            + pltpu.roll(x, HEAD_DIM - half, axis=1) * sb)


def _compress(src_ref, pos_ref, w1_ref, b1_ref, w2_ref):
    n_rows = src_ref.shape[0] // CMP_STRIDE
    p_acc = jnp.zeros((n_rows, HEAD_DIM), F32)
    q_acc = jnp.zeros((n_rows, HEAD_DIM), F32)
    for r in range(CMP_STRIDE):
        s_r = src_ref[pl.ds(r, n_rows, stride=CMP_STRIDE), :]
        a_r = (s_r + pos_ref[r:r + 1, :]).astype(BF16)
        b_r = (s_r + pos_ref[CMP_STRIDE + r:CMP_STRIDE + r + 1, :]).astype(BF16)
        p_acc += jnp.dot(a_r, w1_ref[r], preferred_element_type=F32)
        q_acc += jnp.dot(b_r, w1_ref[CMP_STRIDE + r], preferred_element_type=F32)
    hid = p_acc + pltpu.roll(q_acc, n_rows - 1, axis=0) + b1_ref[...]
    act = (hid * jax.nn.sigmoid(hid)).astype(BF16)
    return jnp.dot(act, w2_ref[...], preferred_element_type=F32)


def _nsa_prep_kernel(kc_ref, vc_ref, ks_ref, vs_ref, kw_ref, vw_ref, c_ref, sa_ref, sb_ref,
                     kpos_ref, kw1_ref, kb1_ref, kw2_ref, vpos_ref, vw1_ref, vb1_ref, vw2_ref,
                     ksx_ref, vst_ref, kwx_ref, vwt_ref, kcmp_ref, vcmp_ref):
    c, sa, sb = c_ref[...], sa_ref[...], sb_ref[...]
    seq = ks_ref.shape[0]
    n_blk = seq // SLC_LEN
    flag_w = EXT_DIM - HEAD_DIM
    blk = lax.broadcasted_iota(jnp.int32, (seq, flag_w), 0) >> SLC_SHIFT
    onehot = jnp.where(blk == lax.broadcasted_iota(jnp.int32, (seq, flag_w), 1), 1.0, 0.0)
    ksx_ref[:, 0:HEAD_DIM] = _rotary(ks_ref[...], c, sa, sb).astype(BF16)
    ksx_ref[:, HEAD_DIM:] = onehot.astype(BF16)
    pad_flag = jnp.where(lax.broadcasted_iota(jnp.int32, (WINDOW, EXT_DIM), 1) == HEAD_DIM + n_blk, 1.0, 0.0)
    kwx_ref[0:WINDOW, :] = pad_flag.astype(BF16)
    kwx_ref[WINDOW:, 0:HEAD_DIM] = _rotary(kw_ref[...], c, sa, sb).astype(BF16)
    kwx_ref[WINDOW:, HEAD_DIM:] = jnp.zeros((seq, flag_w), BF16)
    n_tiles = seq // KEY_TILE
    pad_tiles = WINDOW // KEY_TILE
    for t in range(pad_tiles):
        vwt_ref[t] = jnp.zeros((HEAD_DIM, KEY_TILE), BF16)
    for t in range(n_tiles):
        rows = pl.ds(t * KEY_TILE, KEY_TILE)
        vst_ref[t] = vs_ref[rows, :].T.astype(BF16)
        vwt_ref[pad_tiles + t] = vw_ref[rows, :].T.astype(BF16)
    kcmp_ref[...] = _compress(kc_ref, kpos_ref, kw1_ref, kb1_ref, kw2_ref).astype(BF16)
    vcmp_ref[...] = _compress(vc_ref, vpos_ref, vw1_ref, vb1_ref, vw2_ref).astype(BF16)


def nsa_prep(z, tabs, cmp_k, cmp_v, batch, seq, col0):
    def zspec(which):
        return pl.BlockSpec((seq, HEAD_DIM), lambda b, g, w=which: (b, col0 + w * N_KV + g))

    tab = pl.BlockSpec((seq, HEAD_DIM), lambda b, g: (0, 0))

    def wspecs():
        return [pl.BlockSpec((CMP_LEN, HEAD_DIM), lambda b, g: (0, 0)),
                pl.BlockSpec((CMP_LEN, HEAD_DIM, HEAD_DIM), lambda b, g: (0, 0, 0)),
                pl.BlockSpec((1, HEAD_DIM), lambda b, g: (0, 0)),
                pl.BlockSpec((HEAD_DIM, HEAD_DIM), lambda b, g: (0, 0))]

    def out(shape):
        nd = len(shape)
        spec = pl.BlockSpec((None, None) + shape, lambda b, g: (b, g) + (0,) * nd)
        return spec, jax.ShapeDtypeStruct((batch, N_KV) + shape, BF16)

    n_tiles = seq // KEY_TILE
    pad_tiles = WINDOW // KEY_TILE
    outs = [out((seq, EXT_DIM)), out((n_tiles, HEAD_DIM, KEY_TILE)),
            out((WINDOW + seq, EXT_DIM)), out((pad_tiles + n_tiles, HEAD_DIM, KEY_TILE)),
            out((seq // CMP_STRIDE, HEAD_DIM)), out((seq // CMP_STRIDE, HEAD_DIM))]
    return pl.pallas_call(
        _nsa_prep_kernel,
        out_shape=tuple(o[1] for o in outs),
        grid=(batch, N_KV),
        in_specs=[zspec(0), zspec(1), zspec(2), zspec(3), zspec(4), zspec(5), tab, tab, tab]
        + wspecs() + wspecs(),
        out_specs=tuple(o[0] for o in outs),
        compiler_params=_cparams(2),
        name="nsa_prep",
    )(z, z, z, z, z, z, *tabs, *cmp_k, *cmp_v)


def _nsa_attn_kernel(q_ref, ng_ref, gl_ref, rot_ref, ov_ref,
                     ksx_ref, vst_ref, kwx_ref, vwt_ref, kcmp_ref, vcmp_ref,
                     o_ref, acc_ref, qx_ref, glt_ref, s_ref, m_ref, l_ref, *, tq):
    g = pl.program_id(1)
    i = pl.program_id(2)
    q0 = i * tq
    n_blk = ov_ref.shape[0]
    half = ROT_DIM // 2
    qscale = (HEAD_DIM ** -0.5) * math.log2(math.e)

    def head_lanes(n):
        return slice(n * tq, (n + 1) * tq)

    def lanes4(x):
        return jnp.concatenate([x] * GQA, axis=1)

    q = q_ref[...]
    rot = rot_ref[...]
    cos2, sin2 = rot[:ROT_DIM], rot[ROT_DIM:]
    qts = []
    for n in range(GQA):
        qt = (q[:, n * HEAD_DIM:(n + 1) * HEAD_DIM] * qscale).T
        qts.append(qt)
        top = qt[:ROT_DIM]
        swapped = jnp.concatenate([top[half:], top[:half]], axis=0)
        qrt = jnp.concatenate([top * cos2 + swapped * sin2, qt[ROT_DIM:]], axis=0)
        qx_ref[0:HEAD_DIM, head_lanes(n)] = qrt.astype(BF16)
    qt_all = jnp.concatenate(qts, axis=1).astype(BF16)

    tlane = q0 + lax.broadcasted_iota(jnp.int32, (1, tq), 1)

    n_cmp = kcmp_ref.shape[0]
    s_c = jnp.dot(kcmp_ref[...], qt_all, preferred_element_type=F32)
    cend = lax.broadcasted_iota(jnp.int32, (n_cmp, tq), 0) * CMP_STRIDE + (CMP_LEN - 1)
    s_c = s_c + lanes4(jnp.where(cend <= tlane, 0.0, NEG))
    any_c = lanes4(tlane) >= CMP_LEN - 1
    m_c = jnp.where(any_c, jnp.max(s_c, axis=0, keepdims=True), 0.0)
    e_c = jnp.exp2(s_c - m_c)
    den_c = jnp.sum(e_c, axis=0, keepdims=True)
    p_c = e_c / jnp.where(den_c > 0, den_c, 1.0)
    vcmp_t = vcmp_ref[...].astype(F32).T.astype(BF16)
    o_c = jnp.dot(vcmp_t, p_c.astype(BF16), preferred_element_type=F32)

    p_sum = p_c[:, head_lanes(0)]
    for n in range(1, GQA):
        p_sum = p_sum + p_c[:, head_lanes(n)]
    imp = jnp.dot(ov_ref[...], p_sum, preferred_element_type=F32,
                  precision=lax.Precision.HIGHEST)
    jblk = lax.broadcasted_iota(jnp.int32, (n_blk, tq), 0)
    forced = (jblk == 0) | (jblk == (tlane >> SLC_SHIFT))
    imp = jnp.where(forced, jnp.inf, jnp.where(jblk * SLC_LEN <= tlane, imp, -jnp.inf))
    n_grp = n_blk // SUBLANES
    grp = [imp[a * SUBLANES:(a + 1) * SUBLANES] for a in range(n_grp)]
    sub = lax.broadcasted_iota(jnp.int32, (SUBLANES, tq), 0)
    cnt = [jnp.zeros((SUBLANES, tq), F32) for _ in range(n_grp)]
    for r in range(n_blk):
        row = imp[r:r + 1, :]
        a_r, r_in = divmod(r, SUBLANES)
        for a in range(n_grp):
            if a > a_r:
                cnt[a] = cnt[a] + jnp.where(row >= grp[a], 1.0, 0.0)
            elif a < a_r:
                cnt[a] = cnt[a] + jnp.where(row > grp[a], 1.0, 0.0)
            else:
                cnt[a] = cnt[a] + jnp.where(sub > r_in, jnp.where(row >= grp[a], 1.0, 0.0),
                                            jnp.where(row > grp[a], 1.0, 0.0))
    cnt = jnp.concatenate(cnt, axis=0)
    sel_bias = jnp.where(cnt < N_SEL, jnp.where(imp > -jnp.inf, 0.0, NEG), NEG)

    qx_ref[HEAD_DIM:HEAD_DIM + n_blk, :] = lanes4(sel_bias).astype(BF16)
    n_rest = EXT_DIM - HEAD_DIM - n_blk
    first = lax.broadcasted_iota(jnp.int32, (n_rest, GQA * tq), 0) == 0
    qx_ref[HEAD_DIM + n_blk:, :] = jnp.where(first, NEG, 0.0).astype(BF16)

    kk = lax.broadcasted_iota(jnp.int32, (tq, tq), 0)
    tt = lax.broadcasted_iota(jnp.int32, (tq, tq), 1)
    tri_diag = jnp.where(kk <= tt, 0.0, NEG)
    tri_old = jnp.where(kk > tt, 0.0, NEG)

    tiles = tq // KEY_TILE

    def sel_scores(ci):
        k0 = pl.multiple_of(ci * tq, tq)
        return jnp.dot(ksx_ref[pl.ds(k0, tq), :], qx_ref[...], preferred_element_type=F32)

    def sel_update(ci, s):
        vblk = jnp.concatenate([vst_ref[ci * tiles + r] for r in range(tiles)], axis=1)
        m_old = m_ref[...]
        m_new = jnp.maximum(m_old, jnp.max(s, axis=0, keepdims=True))
        alpha = jnp.exp2(m_old - m_new)
        p = jnp.exp2(s - m_new)
        l_ref[...] = alpha * l_ref[...] + jnp.sum(p, axis=0, keepdims=True)
        acc_ref[...] = alpha * acc_ref[...] + jnp.dot(vblk, p.astype(BF16), preferred_element_type=F32)
        m_ref[...] = m_new

    def sel_pair(pi, carry):
        c0 = 2 * pi
        s_ref[1] = sel_scores(c0 + 1)
        sel_update(c0, s_ref[0])
        s_ref[0] = sel_scores(c0 + 2)
        sel_update(c0 + 1, s_ref[1])
        return carry

    m_ref[...] = jnp.full(m_ref.shape, NEG, F32)
    l_ref[...] = jnp.zeros(l_ref.shape, F32)
    acc_ref[...] = jnp.zeros(acc_ref.shape, F32)
    s_ref[0] = sel_scores(0)
    lax.fori_loop(0, i // 2, sel_pair, 0)

    @pl.when(i % 2 == 1)
    def _():
        s_ref[1] = sel_scores(i)
        sel_update(i - 1, s_ref[0])

    sel_update(i, s_ref[i % 2] + lanes4(tri_diag))
    o_s = acc_ref[...] / l_ref[...]

    n_wt = WINDOW // tq
    kwin = kwx_ref[pl.ds(pl.multiple_of(q0, tq), WINDOW + tq), :]
    vt0 = q0 // KEY_TILE
    vwin = jnp.concatenate([vwt_ref[vt0 + r] for r in range((WINDOW + tq) // KEY_TILE)], axis=1)
    s_w = jnp.dot(kwin, qx_ref[...], preferred_element_type=F32)
    s_w = jnp.concatenate([s_w[:tq] + lanes4(tri_old), s_w[tq:n_wt * tq],
                           s_w[n_wt * tq:] + lanes4(tri_diag)], axis=0)
    e_w = jnp.exp2(s_w - jnp.max(s_w, axis=0, keepdims=True))
    o_w = (jnp.dot(vwin, e_w.astype(BF16), preferred_element_type=F32)
           / jnp.sum(e_w, axis=0, keepdims=True))

    ng = ng_ref[...]
    glt_ref[...] = jax.nn.sigmoid(gl_ref[...]).T
    for n in range(GQA):
        head = g * GQA + n
        gc, gs, gw = (glt_ref[pl.ds(j * N_HEADS + head, 1), :] for j in range(3))
        sl = head_lanes(n)
        o = (gc * o_c[:, sl] + gs * o_s[:, sl] + gw * o_w[:, sl]).T
        ngh = ng[:, n * HEAD_DIM:(n + 1) * HEAD_DIM]
        o_ref[:, n * HEAD_DIM:(n + 1) * HEAD_DIM] = (o * (ngh * jax.nn.sigmoid(ngh))).astype(o_ref.dtype)


def nsa_attention(z, z_ng, gl, rot_t, overlap_t, prep, batch, seq, tq=256):
    ksx, vst, kwx, vwt, kcmp, vcmp = prep
    assert tq % KEY_TILE == 0 and WINDOW % tq == 0 and WINDOW >= 2 * tq
    nq = seq // tq
    qw = GQA * HEAD_DIM

    def whole(a):
        nd = a.ndim - 2
        return pl.BlockSpec((None, None) + a.shape[2:], lambda b, g, i: (b, g) + (0,) * nd)

    return pl.pallas_call(
        functools.partial(_nsa_attn_kernel, tq=tq),
        out_shape=jax.ShapeDtypeStruct((batch * seq, N_HEADS * HEAD_DIM), BF16),
        grid=(batch, N_KV, nq),
        in_specs=[pl.BlockSpec((tq, qw), lambda b, g, i: (b * nq + i, g)),
                  pl.BlockSpec((tq, qw), lambda b, g, i: (b * nq + i, g)),
                  pl.BlockSpec((tq, gl.shape[1]), lambda b, g, i: (b * nq + i, 0)),
                  pl.BlockSpec((rot_t.shape[0], tq), lambda b, g, i: (0, i)),
                  pl.BlockSpec(overlap_t.shape, lambda b, g, i: (0, 0)),
                  whole(ksx), whole(vst), whole(kwx), whole(vwt), whole(kcmp), whole(vcmp)],
        out_specs=pl.BlockSpec((tq, qw), lambda b, g, i: (b * nq + i, g)),
        scratch_shapes=[pltpu.VMEM((HEAD_DIM, GQA * tq), F32),
                        pltpu.VMEM((EXT_DIM, GQA * tq), BF16),
                        pltpu.VMEM((gl.shape[1], tq), F32),
                        pltpu.VMEM((2, tq, GQA * tq), F32),
                        pltpu.VMEM((1, GQA * tq), F32), pltpu.VMEM((1, GQA * tq), F32)],
        compiler_params=_cparams(3),
        name="nsa_attention",
    )(z, z_ng, gl, rot_t, overlap_t, ksx, vst, kwx, vwt, kcmp, vcmp)


def _out_proj_norm_kernel(a0_ref, a1_ref, w_ref, x_ref, g_ref, *o_refs, kh, emit_residual):
    acc = jnp.dot(a0_ref[...], w_ref[0:kh, :], preferred_element_type=F32)
    acc += jnp.dot(a1_ref[...], w_ref[kh:, :], preferred_element_type=F32)
    x = x_ref[...] + acc
    y = x * lax.rsqrt(jnp.mean(x * x, axis=-1, keepdims=True) + EPS)
    o_refs[-1][...] = (y * g_ref[...]).astype(o_refs[-1].dtype)
    if emit_residual:
        o_refs[0][...] = x


def out_proj_norm(a0, a0_blk, a1, a1_blk, w, x2d, g, norm_dtype, emit_residual, tm=512, name="out_proj"):
    m, n = x2d.shape
    kh = w.shape[0] // 2
    row = pl.BlockSpec((tm, n), lambda i: (i, 0))
    out_shape = [jax.ShapeDtypeStruct((m, n), norm_dtype)]
    if emit_residual:
        out_shape.insert(0, jax.ShapeDtypeStruct((m, n), F32))
    return pl.pallas_call(
        functools.partial(_out_proj_norm_kernel, kh=kh, emit_residual=emit_residual),
        out_shape=tuple(out_shape),
        grid=(m // tm,),
        in_specs=[pl.BlockSpec((tm, kh), lambda i: (i, a0_blk)),
                  pl.BlockSpec((tm, kh), lambda i: (i, a1_blk)),
                  pl.BlockSpec(w.shape, lambda i: (0, 0), pipeline_mode=pl.Buffered(1)),
                  row,
                  pl.BlockSpec((1, n), lambda i: (0, 0))],
        out_specs=tuple(row for _ in out_shape),
        compiler_params=_cparams(1),
        name=name,
    )(a0, a1, w, x2d, g.reshape(1, n))


def _sgu_v_kernel(h_ref, w_ref, v_ref, mu_ref, rstd_ref, wb_ref, c_ref, s1_ref, s2_ref, *, n_col_tiles):
    j = pl.program_id(1)

    @pl.when(pl.program_id(0) == 0)
    def _():
        wb_ref[j] = w_ref[...].astype(BF16)

    v = jnp.dot(h_ref[...], wb_ref[j], preferred_element_type=F32)
    v_ref[...] = v
    tm, tn = v.shape
    reps = tn // LANES

    @pl.when(j == 0)
    def _():
        c_ref[...] = jnp.broadcast_to(jnp.sum(v, axis=-1, keepdims=True) / tn, c_ref.shape)
        s1_ref[...] = jnp.zeros(s1_ref.shape, F32)
        s2_ref[...] = jnp.zeros(s2_ref.shape, F32)

    c = c_ref[...]
    s1, s2 = s1_ref[...], s2_ref[...]
    for r in range(reps):
        d = v[:, r * LANES:(r + 1) * LANES] - c
        s1 = s1 + d
        s2 = s2 + d * d
    s1_ref[...] = s1
    s2_ref[...] = s2

    @pl.when(j == n_col_tiles - 1)
    def _():
        width = n_col_tiles * tn
        mean_d = jnp.sum(s1, axis=-1, keepdims=True) / width
        var = jnp.sum(s2, axis=-1, keepdims=True) / width - mean_d * mean_d
        mu_ref[...] = c + mean_d
        rstd_ref[...] = jnp.broadcast_to(lax.rsqrt(var + EPS), rstd_ref.shape)


def sgu_v(h, w, col0, tm=512, tn=512):
    m, k = h.shape
    nj = SGU_W // tn
    j0 = col0 // tn
    stat = jax.ShapeDtypeStruct((m, LANES), F32)
    stat_spec = pl.BlockSpec((tm, LANES), lambda i, j: (i, 0))
    return pl.pallas_call(
        functools.partial(_sgu_v_kernel, n_col_tiles=nj),
        out_shape=(jax.ShapeDtypeStruct((m, SGU_W), F32), stat, stat),
        grid=(m // tm, nj),
        in_specs=[pl.BlockSpec((tm, k), lambda i, j: (i, 0)),
                  pl.BlockSpec((k, tn), lambda i, j: (0, j0 + jnp.where(i == 0, j, 0)))],
        out_specs=(pl.BlockSpec((tm, tn), lambda i, j: (i, j)), stat_spec, stat_spec),
        scratch_shapes=[pltpu.VMEM((nj, k, tn), BF16)] + [pltpu.VMEM((tm, LANES), F32)] * 3,
        compiler_params=_cparams(2),
        name="sgu_v",
    )(h, w)


def _sgu_gate_kernel(h_ref, wu32_ref, wz32_ref, v_ref, mu_ref, rstd_ref, lg_ref, lb_ref, ws_ref, bs_ref,
                     side32_ref, o_ref, side_ref, wu_ref, wz_ref):
    _cast_once([wu32_ref, wz32_ref], [wu_ref, wz_ref])
    side_ref[...] = side32_ref[...].astype(BF16)
    h = h_ref[...]
    tm, tn = v_ref.shape
    reps = GROUP_W // LANES
    mu = jnp.concatenate([mu_ref[...]] * reps, axis=1)
    rstd = jnp.concatenate([rstd_ref[...]] * reps, axis=1)
    tri = (lax.broadcasted_iota(jnp.int32, (CHUNK, CHUNK), 1)
           <= lax.broadcasted_iota(jnp.int32, (CHUNK, CHUNK), 0))
    for gi in range(tn // GROUP_W):
        cols = slice(gi * GROUP_W, (gi + 1) * GROUP_W)
        u = jnp.dot(h, wu_ref[:, cols], preferred_element_type=F32)
        zg = jnp.dot(h, wz_ref[:, cols], preferred_element_type=F32)
        vn = ((v_ref[:, cols] - mu) * rstd * lg_ref[:, cols] + lb_ref[:, cols]).astype(BF16)
        wsm = jnp.where(tri, ws_ref[gi], 0.0).astype(BF16)
        bsg = bs_ref[gi]
        mix = jnp.concatenate(
            [jnp.dot(wsm, vn[c * CHUNK:(c + 1) * CHUNK], preferred_element_type=F32) + bsg
             for c in range(tm // CHUNK)], axis=0)
        o_ref[:, cols] = (u * mix * (zg * jax.nn.sigmoid(zg))).astype(o_ref.dtype)


def sgu_gate(h, w, col_u, col_z, v, mu, rstd, ln_g, ln_b, w_s, b_s, side, tm=1024, tn=512):
    m, k = h.shape
    n = SGU_W
    nj, ni = n // tn, m // tm
    ju, jz = col_u // tn, col_z // tn
    gpt = tn // GROUP_W
    stat_spec = pl.BlockSpec((tm, LANES), lambda j, i: (i, 0))
    side_rows = side.shape[0] // (nj * ni)
    assert side_rows * nj * ni == side.shape[0] and side_rows % HALO == 0
    side_spec = pl.BlockSpec((side_rows, side.shape[1]), lambda j, i: (j * ni + i, 0))
    return pl.pallas_call(
        _sgu_gate_kernel,
        out_shape=(jax.ShapeDtypeStruct((m, n), BF16), jax.ShapeDtypeStruct(side.shape, BF16)),
        grid=(nj, ni),
        in_specs=[pl.BlockSpec((tm, k), lambda j, i: (i, 0)),
                  pl.BlockSpec((k, tn), lambda j, i: (0, ju + j)),
                  pl.BlockSpec((k, tn), lambda j, i: (0, jz + j)),
                  pl.BlockSpec((tm, tn), lambda j, i: (i, j)),
                  stat_spec, stat_spec,
                  pl.BlockSpec((1, tn), lambda j, i: (0, j)),
                  pl.BlockSpec((1, tn), lambda j, i: (0, j)),
                  pl.BlockSpec((gpt, CHUNK, CHUNK), lambda j, i: (j, 0, 0)),
                  pl.BlockSpec((gpt, CHUNK, 1), lambda j, i: (j, 0, 0)),
                  side_spec],
        out_specs=(pl.BlockSpec((tm, tn), lambda j, i: (i, j)), side_spec),
        scratch_shapes=[pltpu.VMEM((k, tn), BF16)] * 2,
        compiler_params=_cparams(2),
        name="sgu_gate",
    )(h, w, w, v, mu, rstd, ln_g.reshape(1, n), ln_b.reshape(1, n), w_s, b_s.reshape(N_GROUPS, CHUNK, 1),
      side)


def _rotary_tables(seq):
    half = ROT_DIM // 2
    inv_freq = jnp.power(ROPE_THETA, -jnp.arange(half, dtype=F32) * 2.0 / ROT_DIM)
    ang = jnp.arange(seq).astype(F32)[:, None] * inv_freq[None, :]
    cos, sin = jnp.cos(ang), jnp.sin(ang)
    rest = HEAD_DIM - ROT_DIM
    c = jnp.concatenate([cos, cos, jnp.ones((seq, rest), F32)], axis=1)
    sa = jnp.concatenate([jnp.zeros((seq, half), F32), sin, jnp.zeros((seq, rest), F32)], axis=1)
    sb = jnp.concatenate([-sin, jnp.zeros((seq, half + rest), F32)], axis=1)
    rot_t = jnp.concatenate([cos, cos, -sin, sin], axis=1).T
    return (c, sa, sb), rot_t


def _overlap_t(seq):
    n_blk = seq // SLC_LEN
    n_rows = seq // CMP_STRIDE
    n_cmp = (seq - CMP_LEN) // CMP_STRIDE + 1
    cs = jnp.arange(n_rows) * CMP_STRIDE
    bs = jnp.arange(n_blk) * SLC_LEN
    ov = (cs[None, :] < bs[:, None] + SLC_LEN) & (cs[None, :] + CMP_LEN > bs[:, None])
    ov = ov & (jnp.arange(n_rows)[None, :] < n_cmp)
    return ov.astype(F32)


def kernel(x, norm_even, w_in_even, conv_w, cmp_k_pos, cmp_k_w1, cmp_k_b1, cmp_k_w2, cmp_v_pos, cmp_v_w1, cmp_v_b1, cmp_v_w2, w_out_even, norm_odd, w_in_odd, sgu_ln_g, sgu_ln_b, sgu_w_s, sgu_b_s, w_out_odd, norm_final):
    batch, seq, d = x.shape
    m = batch * seq
    x2d = x.reshape(m, d)

    wt_in = jnp.swapaxes(w_in_even[0], 0, 1)
    cw = CONV_W
    qw = N_HEADS * HEAD_DIM
    kvw = N_KV * HEAD_DIM
    o_q = 4 * cw
    o_kv = o_q + qw
    o_gl = o_kv + 6 * kvw
    o_ng = o_gl + 3 * N_HEADS

    h0, gl = rmsnorm_proj(x2d, norm_even[0], wt_in, o_gl, LANES)
    y_conv, w_out0 = conv_proj(h0, wt_in, conv_w[0], seq, w_out_even[0])
    z = matmul_t(h0, wt_in, o_q, qw + 6 * kvw, tn=1024, name="nsa_proj")
    z_ng = matmul_t(h0, wt_in, o_ng, qw, tn=1024, name="nsa_gate_proj")

    tabs, rot_t = _rotary_tables(seq)
    cmp_k = (cmp_k_pos[0], cmp_k_w1[0].astype(BF16).reshape(CMP_LEN, HEAD_DIM, HEAD_DIM),
             cmp_k_b1[0].reshape(1, HEAD_DIM), cmp_k_w2[0].astype(BF16))
    cmp_v = (cmp_v_pos[0], cmp_v_w1[0].astype(BF16).reshape(CMP_LEN, HEAD_DIM, HEAD_DIM),
             cmp_v_b1[0].reshape(1, HEAD_DIM), cmp_v_w2[0].astype(BF16))
    prep = nsa_prep(z, tabs, cmp_k, cmp_v, batch, seq, col0=qw // HEAD_DIM)
    y_nsa = nsa_attention(z, z_ng, gl, rot_t, _overlap_t(seq), prep, batch, seq)

    x1, h1 = out_proj_norm(y_conv, 0, y_nsa, 0, w_out0, x2d, norm_odd[0],
                           BF16, True, name="out_proj_even")

    w_in1 = w_in_odd[0]
    v, mu, rstd = sgu_v(h1, w_in1, SGU_W)
    act, w_out1 = sgu_gate(h1, w_in1, 0, 2 * SGU_W, v, mu, rstd,
                           sgu_ln_g[0], sgu_ln_b[0], sgu_w_s[0], sgu_b_s[0], w_out_odd[0])
    (out,) = out_proj_norm(act, 0, act, 1, w_out1, x1, norm_final,
                           F32, False, name="out_proj_odd")
    return out.reshape(batch, seq, d)
```

```python
import functools
import math

import jax
import jax.numpy as jnp
from jax import lax
from jax.experimental import pallas as pl
from jax.experimental.pallas import tpu as pltpu

F32 = jnp.float32
BF16 = jnp.bfloat16

D_MODEL = 2048
MIX = 2 * D_MODEL
CONV_W = MIX // 2
CONV_K = 3
HEAD_DIM = 128
N_HEADS = 16
N_KV = 4
GQA = N_HEADS // N_KV
ROT_DIM = HEAD_DIM // 4
ROPE_THETA = 500000.0
CMP_LEN = 32
CMP_STRIDE = 16
SLC_LEN = 64
N_SEL = 8
WINDOW = 512
SGU_W = MIX
CHUNK = 128
N_GROUPS = 16
GROUP_W = SGU_W // N_GROUPS
EPS = 1e-6

LANES = 128
SUBLANES = 8
SLC_SHIFT = 6
HALO = 16
KEY_TILE = LANES
XPOSE_ROWS = 256
VMEM_LIMIT = 56 * 1024 * 1024

NEG = -1e30

EXT_DIM = 2 * HEAD_DIM


def _cparams(n_axes):
    return pltpu.CompilerParams(
        dimension_semantics=("arbitrary",) * n_axes, vmem_limit_bytes=VMEM_LIMIT)


def _rmsnorm_kernel(x_ref, g_ref, o_ref):
    x = x_ref[...]
    y = x * lax.rsqrt(jnp.mean(x * x, axis=-1, keepdims=True) + EPS)
    o_ref[...] = (y * g_ref[...]).astype(o_ref.dtype)


def _rmsnorm_proj_kernel(x_ref, g_ref, wt_ref, o_ref, p_ref, wb_ref):
    @pl.when(pl.program_id(0) == 0)
    def _():
        wb_ref[...] = wt_ref[...].T.astype(BF16)

    x = x_ref[...]
    y = x * lax.rsqrt(jnp.mean(x * x, axis=-1, keepdims=True) + EPS)
    hb = (y * g_ref[...]).astype(BF16)
    o_ref[...] = hb
    p_ref[...] = jnp.dot(hb, wb_ref[...], preferred_element_type=F32)


def rmsnorm(x2d, g, out_dtype, tm=512):
    m, d = x2d.shape
    return pl.pallas_call(
        _rmsnorm_kernel,
        out_shape=jax.ShapeDtypeStruct((m, d), out_dtype),
        grid=(m // tm,),
        in_specs=[pl.BlockSpec((tm, d), lambda i: (i, 0)),
                  pl.BlockSpec((1, d), lambda i: (0, 0))],
        out_specs=pl.BlockSpec((tm, d), lambda i: (i, 0)),
        compiler_params=_cparams(1),
        name="rmsnorm",
    )(x2d, g.reshape(1, d))


def rmsnorm_proj(x2d, g, wt, row0, n, tm=512):
    m, d = x2d.shape
    assert row0 % n == 0
    return pl.pallas_call(
        _rmsnorm_proj_kernel,
        out_shape=(jax.ShapeDtypeStruct((m, d), BF16), jax.ShapeDtypeStruct((m, n), F32)),
        grid=(m // tm,),
        in_specs=[pl.BlockSpec((tm, d), lambda i: (i, 0)),
                  pl.BlockSpec((1, d), lambda i: (0, 0)),
                  pl.BlockSpec((n, d), lambda i: (row0 // n, 0))],
        out_specs=(pl.BlockSpec((tm, d), lambda i: (i, 0)),
                   pl.BlockSpec((tm, n), lambda i: (i, 0))),
        scratch_shapes=[pltpu.VMEM((d, n), BF16)],
        compiler_params=_cparams(1),
        name="rmsnorm_gates",
    )(x2d, g.reshape(1, d), wt)


def _cast_once(w_refs, wb_refs, transposed=False):
    @pl.when(pl.program_id(1) == 0)
    def _():
        for w_ref, wb_ref in zip(w_refs, wb_refs):
            if transposed:
                for r in range(0, w_ref.shape[0], XPOSE_ROWS):
                    wb_ref[:, r:r + XPOSE_ROWS] = w_ref[r:r + XPOSE_ROWS, :].T.astype(BF16)
            else:
                wb_ref[...] = w_ref[...].astype(BF16)


def _matmul_t_kernel(a_ref, wt_ref, o_ref, wb_ref):
    _cast_once([wt_ref], [wb_ref], transposed=True)
    o_ref[...] = jnp.dot(a_ref[...], wb_ref[...], preferred_element_type=F32)


def matmul_t(a, wt, row0, n, tm=1024, tn=512, name="proj"):
    m, k = a.shape
    assert n % tn == 0 and row0 % HALO == 0
    if row0 % tn == 0:
        wspec = pl.BlockSpec((tn, k), lambda j, i: (row0 // tn + j, 0))
    else:
        wspec = pl.BlockSpec((pl.Element(tn), pl.Element(k)),
                             lambda j, i: (pl.multiple_of(row0 + j * tn, HALO), 0))
    return pl.pallas_call(
        _matmul_t_kernel,
        out_shape=jax.ShapeDtypeStruct((m, n), F32),
        grid=(n // tn, m // tm),
        in_specs=[pl.BlockSpec((tm, k), lambda j, i: (i, 0)), wspec],
        out_specs=pl.BlockSpec((tm, tn), lambda j, i: (i, j)),
        scratch_shapes=[pltpu.VMEM((k, tn), BF16)],
        compiler_params=_cparams(2),
        name=name,
    )(a, wt)


def _conv_proj_kernel(h_ref, hp_ref, wb32_ref, wc32_ref, wh32_ref, wg32_ref, cw_ref, side32_ref,
                      o_ref, side_ref, wb_ref, wc_ref, wh_ref, wg_ref, *, tiles_per_seq):
    _cast_once([wb32_ref, wc32_ref, wh32_ref, wg32_ref], [wb_ref, wc_ref, wh_ref, wg_ref],
               transposed=True)
    side_ref[...] = side32_ref[...].astype(BF16)
    i = pl.program_id(1)
    cw = cw_ref[...]
    hp = hp_ref[...]
    pp = (jnp.dot(hp, wc_ref[...], preferred_element_type=F32)
          * jnp.dot(hp, wh_ref[...], preferred_element_type=F32))
    pp = jnp.where(i % tiles_per_seq == 0, 0.0, pp)
    prev1, prev2 = pp[HALO - 1:HALO, :], pp[HALO - 2:HALO - 1, :]
    h = h_ref[...]
    cb = jnp.dot(h, wb_ref[...], preferred_element_type=F32)
    cc = jnp.dot(h, wc_ref[...], preferred_element_type=F32)
    ch = jnp.dot(h, wh_ref[...], preferred_element_type=F32)
    cg = jnp.dot(h, wg_ref[...], preferred_element_type=F32)
    p = cc * ch
    row = lax.broadcasted_iota(jnp.int32, p.shape, 0)
    p1 = jnp.where(row == 0, prev1, pltpu.roll(p, 1, axis=0))
    p2 = jnp.where(row == 0, prev2, jnp.where(row == 1, prev1, pltpu.roll(p, 2, axis=0)))
    conv = cw[0:1, :] * p2 + cw[1:2, :] * p1 + cw[2:3, :] * p
    o_ref[...] = (cb * conv * (cg * jax.nn.sigmoid(cg))).astype(o_ref.dtype)


def conv_proj(h, wt, conv_w, seq, side, tm=1024, tn=256):
    m, k = h.shape
    n = conv_w.shape[1]
    tiles_per_seq = seq // tm
    halo_per_tile = tm // HALO
    nj, ni = n // tn, m // tm
    side_rows = side.shape[0] // (nj * ni)
    assert side_rows * nj * ni == side.shape[0] and side_rows % HALO == 0
    side_spec = pl.BlockSpec((side_rows, side.shape[1]), lambda j, i: (j * ni + i, 0))

    def wspec(which):
        return pl.BlockSpec((tn, k), lambda j, i: (which * nj + j, 0))

    return pl.pallas_call(
        functools.partial(_conv_proj_kernel, tiles_per_seq=tiles_per_seq),
        out_shape=(jax.ShapeDtypeStruct((m, n), BF16), jax.ShapeDtypeStruct(side.shape, BF16)),
        grid=(nj, ni),
        in_specs=[pl.BlockSpec((tm, k), lambda j, i: (i, 0)),
                  pl.BlockSpec((HALO, k), lambda j, i: (jnp.maximum(i * halo_per_tile - 1, 0), 0)),
                  wspec(0), wspec(1), wspec(2), wspec(3),
                  pl.BlockSpec((CONV_K, tn), lambda j, i: (0, j)),
                  side_spec],
        out_specs=(pl.BlockSpec((tm, tn), lambda j, i: (i, j)), side_spec),
        scratch_shapes=[pltpu.VMEM((k, tn), BF16)] * 4,
        compiler_params=_cparams(2),
        name="conv_proj",
    )(h, h, wt, wt, wt, wt, conv_w, side)


def _rotary(x, c, sa, sb):
    half = ROT_DIM // 2
    return (x * c + pltpu.roll(x, half, axis=1) * sa
            + pltpu.roll(x, HEAD_DIM - half, axis=1) * sb)


def _compress(src_ref, pos_ref, w1_ref, b1_ref, w2_ref):
    n_rows = src_ref.shape[0] // CMP_STRIDE
    p_acc = jnp.zeros((n_rows, HEAD_DIM), F32)
    q_acc = jnp.zeros((n_rows, HEAD_DIM), F32)
    for r in range(CMP_STRIDE):
        s_r = src_ref[pl.ds(r, n_rows, stride=CMP_STRIDE), :]
        a_r = (s_r + pos_ref[r:r + 1, :]).astype(BF16)
        b_r = (s_r + pos_ref[CMP_STRIDE + r:CMP_STRIDE + r + 1, :]).astype(BF16)
        p_acc += jnp.dot(a_r, w1_ref[r], preferred_element_type=F32)
        q_acc += jnp.dot(b_r, w1_ref[CMP_STRIDE + r], preferred_element_type=F32)
    hid = p_acc + pltpu.roll(q_acc, n_rows - 1, axis=0) + b1_ref[...]
    act = (hid * jax.nn.sigmoid(hid)).astype(BF16)
    return jnp.dot(act, w2_ref[...], preferred_element_type=F32)


def _nsa_prep_kernel(kc_ref, vc_ref, ks_ref, vs_ref, kw_ref, vw_ref, c_ref, sa_ref, sb_ref,
                     kpos_ref, kw1_ref, kb1_ref, kw2_ref, vpos_ref, vw1_ref, vb1_ref, vw2_ref,
                     ksx_ref, vst_ref, kwx_ref, vwt_ref, kcmp_ref, vcmp_ref):
    c, sa, sb = c_ref[...], sa_ref[...], sb_ref[...]
    seq = ks_ref.shape[0]
    n_blk = seq // SLC_LEN
    flag_w = EXT_DIM - HEAD_DIM
    blk = lax.broadcasted_iota(jnp.int32, (seq, flag_w), 0) >> SLC_SHIFT
    onehot = jnp.where(blk == lax.broadcasted_iota(jnp.int32, (seq, flag_w), 1), 1.0, 0.0)
    ksx_ref[:, 0:HEAD_DIM] = _rotary(ks_ref[...], c, sa, sb).astype(BF16)
    ksx_ref[:, HEAD_DIM:] = onehot.astype(BF16)
    pad_flag = jnp.where(lax.broadcasted_iota(jnp.int32, (WINDOW, EXT_DIM), 1) == HEAD_DIM + n_blk, 1.0, 0.0)
    kwx_ref[0:WINDOW, :] = pad_flag.astype(BF16)
    kwx_ref[WINDOW:, 0:HEAD_DIM] = _rotary(kw_ref[...], c, sa, sb).astype(BF16)
    kwx_ref[WINDOW:, HEAD_DIM:] = jnp.zeros((seq, flag_w), BF16)
    n_tiles = seq // KEY_TILE
    pad_tiles = WINDOW // KEY_TILE
    for t in range(pad_tiles):
        vwt_ref[t] = jnp.zeros((HEAD_DIM, KEY_TILE), BF16)
    for t in range(n_tiles):
        rows = pl.ds(t * KEY_TILE, KEY_TILE)
        vst_ref[t] = vs_ref[rows, :].T.astype(BF16)
        vwt_ref[pad_tiles + t] = vw_ref[rows, :].T.astype(BF16)
    kcmp_ref[...] = _compress(kc_ref, kpos_ref, kw1_ref, kb1_ref, kw2_ref).astype(BF16)
    vcmp_ref[...] = _compress(vc_ref, vpos_ref, vw1_ref, vb1_ref, vw2_ref).astype(BF16)


def nsa_prep(z, tabs, cmp_k, cmp_v, batch, seq, col0):
    def zspec(which):
        return pl.BlockSpec((seq, HEAD_DIM), lambda b, g, w=which: (b, col0 + w * N_KV + g))

    tab = pl.BlockSpec((seq, HEAD_DIM), lambda b, g: (0, 0))

    def wspecs():
        return [pl.BlockSpec((CMP_LEN, HEAD_DIM), lambda b, g: (0, 0)),
                pl.BlockSpec((CMP_LEN, HEAD_DIM, HEAD_DIM), lambda b, g: (0, 0, 0)),
                pl.BlockSpec((1, HEAD_DIM), lambda b, g: (0, 0)),
                pl.BlockSpec((HEAD_DIM, HEAD_DIM), lambda b, g: (0, 0))]

    def out(shape):
        nd = len(shape)
        spec = pl.BlockSpec((None, None) + shape, lambda b, g: (b, g) + (0,) * nd)
        return spec, jax.ShapeDtypeStruct((batch, N_KV) + shape, BF16)

    n_tiles = seq // KEY_TILE
    pad_tiles = WINDOW // KEY_TILE
    outs = [out((seq, EXT_DIM)), out((n_tiles, HEAD_DIM, KEY_TILE)),
            out((WINDOW + seq, EXT_DIM)), out((pad_tiles + n_tiles, HEAD_DIM, KEY_TILE)),
            out((seq // CMP_STRIDE, HEAD_DIM)), out((seq // CMP_STRIDE, HEAD_DIM))]
    return pl.pallas_call(
        _nsa_prep_kernel,
        out_shape=tuple(o[1] for o in outs),
        grid=(batch, N_KV),
        in_specs=[zspec(0), zspec(1), zspec(2), zspec(3), zspec(4), zspec(5), tab, tab, tab]
        + wspecs() + wspecs(),
        out_specs=tuple(o[0] for o in outs),
        compiler_params=_cparams(2),
        name="nsa_prep",
    )(z, z, z, z, z, z, *tabs, *cmp_k, *cmp_v)


def _nsa_attn_kernel(q_ref, ng_ref, gl_ref, rot_ref, ov_ref,
                     ksx_ref, vst_ref, kwx_ref, vwt_ref, kcmp_ref, vcmp_ref,
                     o_ref, acc_ref, qx_ref, glt_ref, s_ref, m_ref, l_ref, *, tq):
    g = pl.program_id(1)
    i = pl.program_id(2)
    q0 = i * tq
    n_blk = ov_ref.shape[0]
    half = ROT_DIM // 2
    qscale = (HEAD_DIM ** -0.5) * math.log2(math.e)

    def head_lanes(n):
        return slice(n * tq, (n + 1) * tq)

    def lanes4(x):
        return jnp.concatenate([x] * GQA, axis=1)

    q = q_ref[...]
    rot = rot_ref[...]
    cos2, sin2 = rot[:ROT_DIM], rot[ROT_DIM:]
    qts = []
    for n in range(GQA):
        qt = (q[:, n * HEAD_DIM:(n + 1) * HEAD_DIM] * qscale).T
        qts.append(qt)
        top = qt[:ROT_DIM]
        swapped = jnp.concatenate([top[half:], top[:half]], axis=0)
        qrt = jnp.concatenate([top * cos2 + swapped * sin2, qt[ROT_DIM:]], axis=0)
        qx_ref[0:HEAD_DIM, head_lanes(n)] = qrt.astype(BF16)
    qt_all = jnp.concatenate(qts, axis=1).astype(BF16)

    tlane = q0 + lax.broadcasted_iota(jnp.int32, (1, tq), 1)

    n_cmp = kcmp_ref.shape[0]
    s_c = jnp.dot(kcmp_ref[...], qt_all, preferred_element_type=F32)
    cend = lax.broadcasted_iota(jnp.int32, (n_cmp, tq), 0) * CMP_STRIDE + (CMP_LEN - 1)
    s_c = s_c + lanes4(jnp.where(cend <= tlane, 0.0, NEG))
    any_c = lanes4(tlane) >= CMP_LEN - 1
    m_c = jnp.where(any_c, jnp.max(s_c, axis=0, keepdims=True), 0.0)
    e_c = jnp.exp2(s_c - m_c)
    den_c = jnp.sum(e_c, axis=0, keepdims=True)
    p_c = e_c / jnp.where(den_c > 0, den_c, 1.0)
    vcmp_t = vcmp_ref[...].astype(F32).T.astype(BF16)
    o_c = jnp.dot(vcmp_t, p_c.astype(BF16), preferred_element_type=F32)

    p_sum = p_c[:, head_lanes(0)]
    for n in range(1, GQA):
        p_sum = p_sum + p_c[:, head_lanes(n)]
    imp = jnp.dot(ov_ref[...], p_sum, preferred_element_type=F32,
                  precision=lax.Precision.HIGHEST)
    jblk = lax.broadcasted_iota(jnp.int32, (n_blk, tq), 0)
    forced = (jblk == 0) | (jblk == (tlane >> SLC_SHIFT))
    imp = jnp.where(forced, jnp.inf, jnp.where(jblk * SLC_LEN <= tlane, imp, -jnp.inf))
    n_grp = n_blk // SUBLANES
    grp = [imp[a * SUBLANES:(a + 1) * SUBLANES] for a in range(n_grp)]
    sub = lax.broadcasted_iota(jnp.int32, (SUBLANES, tq), 0)
    cnt = [jnp.zeros((SUBLANES, tq), F32) for _ in range(n_grp)]
    for r in range(n_blk):
        row = imp[r:r + 1, :]
        a_r, r_in = divmod(r, SUBLANES)
        for a in range(n_grp):
            if a > a_r:
                cnt[a] = cnt[a] + jnp.where(row >= grp[a], 1.0, 0.0)
            elif a < a_r:
                cnt[a] = cnt[a] + jnp.where(row > grp[a], 1.0, 0.0)
            else:
                cnt[a] = cnt[a] + jnp.where(sub > r_in, jnp.where(row >= grp[a], 1.0, 0.0),
                                            jnp.where(row > grp[a], 1.0, 0.0))
    cnt = jnp.concatenate(cnt, axis=0)
    sel_bias = jnp.where(cnt < N_SEL, jnp.where(imp > -jnp.inf, 0.0, NEG), NEG)

    qx_ref[HEAD_DIM:HEAD_DIM + n_blk, :] = lanes4(sel_bias).astype(BF16)
    n_rest = EXT_DIM - HEAD_DIM - n_blk
    first = lax.broadcasted_iota(jnp.int32, (n_rest, GQA * tq), 0) == 0
    qx_ref[HEAD_DIM + n_blk:, :] = jnp.where(first, NEG, 0.0).astype(BF16)

    kk = lax.broadcasted_iota(jnp.int32, (tq, tq), 0)
    tt = lax.broadcasted_iota(jnp.int32, (tq, tq), 1)
    tri_diag = jnp.where(kk <= tt, 0.0, NEG)
    tri_old = jnp.where(kk > tt, 0.0, NEG)

    tiles = tq // KEY_TILE

    def sel_scores(ci):
        k0 = pl.multiple_of(ci * tq, tq)
        return jnp.dot(ksx_ref[pl.ds(k0, tq), :], qx_ref[...], preferred_element_type=F32)

    def sel_update(ci, s):
        vblk = jnp.concatenate([vst_ref[ci * tiles + r] for r in range(tiles)], axis=1)
        m_old = m_ref[...]
        m_new = jnp.maximum(m_old, jnp.max(s, axis=0, keepdims=True))
        alpha = jnp.exp2(m_old - m_new)
        p = jnp.exp2(s - m_new)
        l_ref[...] = alpha * l_ref[...] + jnp.sum(p, axis=0, keepdims=True)
        acc_ref[...] = alpha * acc_ref[...] + jnp.dot(vblk, p.astype(BF16), preferred_element_type=F32)
        m_ref[...] = m_new

    def sel_pair(pi, carry):
        c0 = 2 * pi
        s_ref[1] = sel_scores(c0 + 1)
        sel_update(c0, s_ref[0])
        s_ref[0] = sel_scores(c0 + 2)
        sel_update(c0 + 1, s_ref[1])
        return carry

    m_ref[...] = jnp.full(m_ref.shape, NEG, F32)
    l_ref[...] = jnp.zeros(l_ref.shape, F32)
    acc_ref[...] = jnp.zeros(acc_ref.shape, F32)
    s_ref[0] = sel_scores(0)
    lax.fori_loop(0, i // 2, sel_pair, 0)

    @pl.when(i % 2 == 1)
    def _():
        s_ref[1] = sel_scores(i)
        sel_update(i - 1, s_ref[0])

    sel_update(i, s_ref[i % 2] + lanes4(tri_diag))
    o_s = acc_ref[...] / l_ref[...]

    n_wt = WINDOW // tq
    kwin = kwx_ref[pl.ds(pl.multiple_of(q0, tq), WINDOW + tq), :]
    vt0 = q0 // KEY_TILE
    vwin = jnp.concatenate([vwt_ref[vt0 + r] for r in range((WINDOW + tq) // KEY_TILE)], axis=1)
    s_w = jnp.dot(kwin, qx_ref[...], preferred_element_type=F32)
    s_w = jnp.concatenate([s_w[:tq] + lanes4(tri_old), s_w[tq:n_wt * tq],
                           s_w[n_wt * tq:] + lanes4(tri_diag)], axis=0)
    e_w = jnp.exp2(s_w - jnp.max(s_w, axis=0, keepdims=True))
    o_w = (jnp.dot(vwin, e_w.astype(BF16), preferred_element_type=F32)
           / jnp.sum(e_w, axis=0, keepdims=True))

    ng = ng_ref[...]
    glt_ref[...] = jax.nn.sigmoid(gl_ref[...]).T
    for n in range(GQA):
        head = g * GQA + n
        gc, gs, gw = (glt_ref[pl.ds(j * N_HEADS + head, 1), :] for j in range(3))
        sl = head_lanes(n)
        o = (gc * o_c[:, sl] + gs * o_s[:, sl] + gw * o_w[:, sl]).T
        ngh = ng[:, n * HEAD_DIM:(n + 1) * HEAD_DIM]
        o_ref[:, n * HEAD_DIM:(n + 1) * HEAD_DIM] = (o * (ngh * jax.nn.sigmoid(ngh))).astype(o_ref.dtype)


def nsa_attention(z, z_ng, gl, rot_t, overlap_t, prep, batch, seq, tq=256):
    ksx, vst, kwx, vwt, kcmp, vcmp = prep
    assert tq % KEY_TILE == 0 and WINDOW % tq == 0 and WINDOW >= 2 * tq
    nq = seq // tq
    qw = GQA * HEAD_DIM

    def whole(a):
        nd = a.ndim - 2
        return pl.BlockSpec((None, None) + a.shape[2:], lambda b, g, i: (b, g) + (0,) * nd)

    return pl.pallas_call(
        functools.partial(_nsa_attn_kernel, tq=tq),
        out_shape=jax.ShapeDtypeStruct((batch * seq, N_HEADS * HEAD_DIM), BF16),
        grid=(batch, N_KV, nq),
        in_specs=[pl.BlockSpec((tq, qw), lambda b, g, i: (b * nq + i, g)),
                  pl.BlockSpec((tq, qw), lambda b, g, i: (b * nq + i, g)),
                  pl.BlockSpec((tq, gl.shape[1]), lambda b, g, i: (b * nq + i, 0)),
                  pl.BlockSpec((rot_t.shape[0], tq), lambda b, g, i: (0, i)),
                  pl.BlockSpec(overlap_t.shape, lambda b, g, i: (0, 0)),
                  whole(ksx), whole(vst), whole(kwx), whole(vwt), whole(kcmp), whole(vcmp)],
        out_specs=pl.BlockSpec((tq, qw), lambda b, g, i: (b * nq + i, g)),
        scratch_shapes=[pltpu.VMEM((HEAD_DIM, GQA * tq), F32),
                        pltpu.VMEM((EXT_DIM, GQA * tq), BF16),
                        pltpu.VMEM((gl.shape[1], tq), F32),
                        pltpu.VMEM((2, tq, GQA * tq), F32),
                        pltpu.VMEM((1, GQA * tq), F32), pltpu.VMEM((1, GQA * tq), F32)],
        compiler_params=_cparams(3),
        name="nsa_attention",
    )(z, z_ng, gl, rot_t, overlap_t, ksx, vst, kwx, vwt, kcmp, vcmp)


def _nsa_attn_pair_kernel(q_ref, ng_ref, gl_ref, rot_ref, ov_ref,
                          ksx_ref, vst_ref, kwx_ref, vwt_ref, kcmp_ref, vcmp_ref,
                          o_ref, acc_ref, qx_ref, glt_ref, s_ref, m_ref, l_ref, oc_ref, *, tq, n_grp_step):
    gp = pl.program_id(1)
    i = pl.program_id(2)
    q0 = i * tq
    n_blk = ov_ref.shape[0]
    half = ROT_DIM // 2
    qscale = (HEAD_DIM ** -0.5) * math.log2(math.e)
    qw = GQA * HEAD_DIM
    groups = range(n_grp_step)

    def head_lanes(n):
        return slice(n * tq, (n + 1) * tq)

    def lanes4(x):
        return jnp.concatenate([x] * GQA, axis=1)

    rot = rot_ref[...]
    cos2, sin2 = rot[:ROT_DIM], rot[ROT_DIM:]
    tlane = q0 + lax.broadcasted_iota(jnp.int32, (1, tq), 1)
    kk = lax.broadcasted_iota(jnp.int32, (tq, tq), 0)
    tt = lax.broadcasted_iota(jnp.int32, (tq, tq), 1)
    tri_diag = jnp.where(kk <= tt, 0.0, NEG)
    tri_old = jnp.where(kk > tt, 0.0, NEG)
    n_cmp = kcmp_ref.shape[1]
    cend = lax.broadcasted_iota(jnp.int32, (n_cmp, tq), 0) * CMP_STRIDE + (CMP_LEN - 1)
    bias_c = lanes4(jnp.where(cend <= tlane, 0.0, NEG))
    any_c = lanes4(tlane) >= CMP_LEN - 1
    jblk = lax.broadcasted_iota(jnp.int32, (n_blk, tq), 0)
    forced = (jblk == 0) | (jblk == (tlane >> SLC_SHIFT))
    causal_blk = jblk * SLC_LEN <= tlane
    sub = lax.broadcasted_iota(jnp.int32, (SUBLANES, tq), 0)
    n_rest = EXT_DIM - HEAD_DIM - n_blk
    pad_rows = jnp.where(lax.broadcasted_iota(jnp.int32, (n_rest, GQA * tq), 0) == 0, NEG, 0.0).astype(BF16)
    tiles = tq // KEY_TILE

    def sel_scores(gg, ci):
        k0 = pl.multiple_of(ci * tq, tq)
        return jnp.dot(ksx_ref[gg, pl.ds(k0, tq), :], qx_ref[gg], preferred_element_type=F32)

    def sel_update(gg, ci, s):
        vblk = jnp.concatenate([vst_ref[gg, ci * tiles + r] for r in range(tiles)], axis=1)
        m_old = m_ref[gg]
        m_new = jnp.maximum(m_old, jnp.max(s, axis=0, keepdims=True))
        alpha = jnp.exp2(m_old - m_new)
        p = jnp.exp2(s - m_new)
        l_ref[gg] = alpha * l_ref[gg] + jnp.sum(p, axis=0, keepdims=True)
        acc_ref[gg] = alpha * acc_ref[gg] + jnp.dot(vblk, p.astype(BF16), preferred_element_type=F32)
        m_ref[gg] = m_new

    for gg in groups:
        q = q_ref[:, gg * qw:(gg + 1) * qw]
        qts = []
        for n in range(GQA):
            qt = (q[:, n * HEAD_DIM:(n + 1) * HEAD_DIM] * qscale).T
            qts.append(qt)
            top = qt[:ROT_DIM]
            swapped = jnp.concatenate([top[half:], top[:half]], axis=0)
            qrt = jnp.concatenate([top * cos2 + swapped * sin2, qt[ROT_DIM:]], axis=0)
            qx_ref[gg, 0:HEAD_DIM, head_lanes(n)] = qrt.astype(BF16)
        qt_all = jnp.concatenate(qts, axis=1).astype(BF16)

        s_c = jnp.dot(kcmp_ref[gg], qt_all, preferred_element_type=F32) + bias_c
        m_c = jnp.where(any_c, jnp.max(s_c, axis=0, keepdims=True), 0.0)
        e_c = jnp.exp2(s_c - m_c)
        den_c = jnp.sum(e_c, axis=0, keepdims=True)
        p_c = e_c / jnp.where(den_c > 0, den_c, 1.0)
        vcmp_t = vcmp_ref[gg].astype(F32).T.astype(BF16)
        oc_ref[gg] = jnp.dot(vcmp_t, p_c.astype(BF16), preferred_element_type=F32)

        p_sum = p_c[:, head_lanes(0)]
        for n in range(1, GQA):
            p_sum = p_sum + p_c[:, head_lanes(n)]
        imp = jnp.dot(ov_ref[...], p_sum, preferred_element_type=F32,
                      precision=lax.Precision.HIGHEST)
        imp = jnp.where(forced, jnp.inf, jnp.where(causal_blk, imp, -jnp.inf))
        n_sub = n_blk // SUBLANES
        part = [imp[a * SUBLANES:(a + 1) * SUBLANES] for a in range(n_sub)]
        cnt = [jnp.zeros((SUBLANES, tq), F32) for _ in range(n_sub)]
        for r in range(n_blk):
            row = imp[r:r + 1, :]
            a_r, r_in = divmod(r, SUBLANES)
            for a in range(n_sub):
                if a > a_r:
                    cnt[a] = cnt[a] + jnp.where(row >= part[a], 1.0, 0.0)
                elif a < a_r:
                    cnt[a] = cnt[a] + jnp.where(row > part[a], 1.0, 0.0)
                else:
                    cnt[a] = cnt[a] + jnp.where(sub > r_in, jnp.where(row >= part[a], 1.0, 0.0),
                                                jnp.where(row > part[a], 1.0, 0.0))
        cnt = jnp.concatenate(cnt, axis=0)
        sel_bias = jnp.where(cnt < N_SEL, jnp.where(imp > -jnp.inf, 0.0, NEG), NEG)
        qx_ref[gg, HEAD_DIM:HEAD_DIM + n_blk, :] = lanes4(sel_bias).astype(BF16)
        qx_ref[gg, HEAD_DIM + n_blk:, :] = pad_rows

        m_ref[gg] = jnp.full(m_ref.shape[1:], NEG, F32)
        l_ref[gg] = jnp.zeros(l_ref.shape[1:], F32)
        acc_ref[gg] = jnp.zeros(acc_ref.shape[1:], F32)
        s_ref[gg, 0] = sel_scores(gg, 0)

    def sel_pair(pi, carry):
        c0 = 2 * pi
        for gg in groups:
            s_ref[gg, 1] = sel_scores(gg, c0 + 1)
        for gg in groups:
            sel_update(gg, c0, s_ref[gg, 0])
        for gg in groups:
            s_ref[gg, 0] = sel_scores(gg, c0 + 2)
        for gg in groups:
            sel_update(gg, c0 + 1, s_ref[gg, 1])
        return carry

    lax.fori_loop(0, i // 2, sel_pair, 0)

    @pl.when(i % 2 == 1)
    def _():
        for gg in groups:
            s_ref[gg, 1] = sel_scores(gg, i)
        for gg in groups:
            sel_update(gg, i - 1, s_ref[gg, 0])

    for gg in groups:
        sel_update(gg, i, s_ref[gg, i % 2] + lanes4(tri_diag))

    n_wt = WINDOW // tq
    vt0 = q0 // KEY_TILE
    glt_ref[...] = jax.nn.sigmoid(gl_ref[...]).T
    for gg in groups:
        kwin = kwx_ref[gg, pl.ds(pl.multiple_of(q0, tq), WINDOW + tq), :]
        vwin = jnp.concatenate([vwt_ref[gg, vt0 + r] for r in range((WINDOW + tq) // KEY_TILE)], axis=1)
        s_w = jnp.dot(kwin, qx_ref[gg], preferred_element_type=F32)
        s_w = jnp.concatenate([s_w[:tq] + lanes4(tri_old), s_w[tq:n_wt * tq],
                               s_w[n_wt * tq:] + lanes4(tri_diag)], axis=0)
        e_w = jnp.exp2(s_w - jnp.max(s_w, axis=0, keepdims=True))
        o_w = (jnp.dot(vwin, e_w.astype(BF16), preferred_element_type=F32)
               / jnp.sum(e_w, axis=0, keepdims=True))
        o_s = acc_ref[gg] / l_ref[gg]
        o_c = oc_ref[gg]
        ng = ng_ref[:, gg * qw:(gg + 1) * qw]
        for n in range(GQA):
            head = (gp * n_grp_step + gg) * GQA + n
            gc, gs, gw = (glt_ref[pl.ds(j * N_HEADS + head, 1), :] for j in range(3))
            sl = head_lanes(n)
            o = (gc * o_c[:, sl] + gs * o_s[:, sl] + gw * o_w[:, sl]).T
            ngh = ng[:, n * HEAD_DIM:(n + 1) * HEAD_DIM]
            col = gg * qw + n * HEAD_DIM
            o_ref[:, col:col + HEAD_DIM] = (o * (ngh * jax.nn.sigmoid(ngh))).astype(o_ref.dtype)


def nsa_attention_pair(z, z_ng, gl, rot_t, overlap_t, prep, batch, seq, tq=256, n_grp_step=2):
    ksx, vst, kwx, vwt, kcmp, vcmp = prep
    assert tq % KEY_TILE == 0 and WINDOW % tq == 0 and WINDOW >= 2 * tq and N_KV % n_grp_step == 0
    nq = seq // tq
    qw = GQA * HEAD_DIM * n_grp_step

    def whole(a):
        nd = a.ndim - 2
        return pl.BlockSpec((None, n_grp_step) + a.shape[2:], lambda b, g, i: (b, g) + (0,) * nd)

    lanes = GQA * tq
    return pl.pallas_call(
        functools.partial(_nsa_attn_pair_kernel, tq=tq, n_grp_step=n_grp_step),
        out_shape=jax.ShapeDtypeStruct((batch * seq, N_HEADS * HEAD_DIM), BF16),
        grid=(batch, N_KV // n_grp_step, nq),
        in_specs=[pl.BlockSpec((tq, qw), lambda b, g, i: (b * nq + i, g)),
                  pl.BlockSpec((tq, qw), lambda b, g, i: (b * nq + i, g)),
                  pl.BlockSpec((tq, gl.shape[1]), lambda b, g, i: (b * nq + i, 0)),
                  pl.BlockSpec((rot_t.shape[0], tq), lambda b, g, i: (0, i)),
                  pl.BlockSpec(overlap_t.shape, lambda b, g, i: (0, 0)),
                  whole(ksx), whole(vst), whole(kwx), whole(vwt), whole(kcmp), whole(vcmp)],
        out_specs=pl.BlockSpec((tq, qw), lambda b, g, i: (b * nq + i, g)),
        scratch_shapes=[pltpu.VMEM((n_grp_step, HEAD_DIM, lanes), F32),
                        pltpu.VMEM((n_grp_step, EXT_DIM, lanes), BF16),
                        pltpu.VMEM((gl.shape[1], tq), F32),
                        pltpu.VMEM((n_grp_step, 2, tq, lanes), F32),
                        pltpu.VMEM((n_grp_step, 1, lanes), F32),
                        pltpu.VMEM((n_grp_step, 1, lanes), F32),
                        pltpu.VMEM((n_grp_step, HEAD_DIM, lanes), F32)],
        compiler_params=_cparams(3),
        name="nsa_attention",
    )(z, z_ng, gl, rot_t, overlap_t, ksx, vst, kwx, vwt, kcmp, vcmp)


def _out_proj_norm_kernel(a0_ref, a1_ref, w_ref, x_ref, g_ref, *o_refs, kh, emit_residual):
    acc = jnp.dot(a0_ref[...], w_ref[0:kh, :], preferred_element_type=F32)
    acc += jnp.dot(a1_ref[...], w_ref[kh:, :], preferred_element_type=F32)
    x = x_ref[...] + acc
    y = x * lax.rsqrt(jnp.mean(x * x, axis=-1, keepdims=True) + EPS)
    o_refs[-1][...] = (y * g_ref[...]).astype(o_refs[-1].dtype)
    if emit_residual:
        o_refs[0][...] = x


def out_proj_norm(a0, a0_blk, a1, a1_blk, w, x2d, g, norm_dtype, emit_residual, tm=512, name="out_proj"):
    m, n = x2d.shape
    kh = w.shape[0] // 2
    row = pl.BlockSpec((tm, n), lambda i: (i, 0))
    out_shape = [jax.ShapeDtypeStruct((m, n), norm_dtype)]
    if emit_residual:
        out_shape.insert(0, jax.ShapeDtypeStruct((m, n), F32))
    return pl.pallas_call(
        functools.partial(_out_proj_norm_kernel, kh=kh, emit_residual=emit_residual),
        out_shape=tuple(out_shape),
        grid=(m // tm,),
        in_specs=[pl.BlockSpec((tm, kh), lambda i: (i, a0_blk)),
                  pl.BlockSpec((tm, kh), lambda i: (i, a1_blk)),
                  pl.BlockSpec(w.shape, lambda i: (0, 0), pipeline_mode=pl.Buffered(1)),
                  row,
                  pl.BlockSpec((1, n), lambda i: (0, 0))],
        out_specs=tuple(row for _ in out_shape),
        compiler_params=_cparams(1),
        name=name,
    )(a0, a1, w, x2d, g.reshape(1, n))


def _sgu_v_kernel(h_ref, w_ref, v_ref, mu_ref, rstd_ref, wb_ref, c_ref, s1_ref, s2_ref, *, n_col_tiles):
    j = pl.program_id(1)

    @pl.when(pl.program_id(0) == 0)
    def _():
        wb_ref[j] = w_ref[...].astype(BF16)

    v = jnp.dot(h_ref[...], wb_ref[j], preferred_element_type=F32)
    v_ref[...] = v
    tm, tn = v.shape
    reps = tn // LANES

    @pl.when(j == 0)
    def _():
        c_ref[...] = jnp.broadcast_to(jnp.sum(v, axis=-1, keepdims=True) / tn, c_ref.shape)
        s1_ref[...] = jnp.zeros(s1_ref.shape, F32)
        s2_ref[...] = jnp.zeros(s2_ref.shape, F32)

    c = c_ref[...]
    s1, s2 = s1_ref[...], s2_ref[...]
    for r in range(reps):
        d = v[:, r * LANES:(r + 1) * LANES] - c
        s1 = s1 + d
        s2 = s2 + d * d
    s1_ref[...] = s1
    s2_ref[...] = s2

    @pl.when(j == n_col_tiles - 1)
    def _():
        width = n_col_tiles * tn
        mean_d = jnp.sum(s1, axis=-1, keepdims=True) / width
        var = jnp.sum(s2, axis=-1, keepdims=True) / width - mean_d * mean_d
        mu_ref[...] = c + mean_d
        rstd_ref[...] = jnp.broadcast_to(lax.rsqrt(var + EPS), rstd_ref.shape)


def sgu_v(h, w, col0, tm=512, tn=512):
    m, k = h.shape
    nj = SGU_W // tn
    j0 = col0 // tn
    stat = jax.ShapeDtypeStruct((m, LANES), F32)
    stat_spec = pl.BlockSpec((tm, LANES), lambda i, j: (i, 0))
    return pl.pallas_call(
        functools.partial(_sgu_v_kernel, n_col_tiles=nj),
        out_shape=(jax.ShapeDtypeStruct((m, SGU_W), F32), stat, stat),
        grid=(m // tm, nj),
        in_specs=[pl.BlockSpec((tm, k), lambda i, j: (i, 0)),
                  pl.BlockSpec((k, tn), lambda i, j: (0, j0 + jnp.where(i == 0, j, 0)))],
        out_specs=(pl.BlockSpec((tm, tn), lambda i, j: (i, j)), stat_spec, stat_spec),
        scratch_shapes=[pltpu.VMEM((nj, k, tn), BF16)] + [pltpu.VMEM((tm, LANES), F32)] * 3,
        compiler_params=_cparams(2),
        name="sgu_v",
    )(h, w)


def _sgu_gate_kernel(h_ref, wu32_ref, wz32_ref, v_ref, mu_ref, rstd_ref, lg_ref, lb_ref, ws_ref, bs_ref,
                     side32_ref, o_ref, side_ref, wu_ref, wz_ref):
    _cast_once([wu32_ref, wz32_ref], [wu_ref, wz_ref])
    side_ref[...] = side32_ref[...].astype(BF16)
    h = h_ref[...]
    tm, tn = v_ref.shape
    reps = GROUP_W // LANES
    mu = jnp.concatenate([mu_ref[...]] * reps, axis=1)
    rstd = jnp.concatenate([rstd_ref[...]] * reps, axis=1)
    tri = (lax.broadcasted_iota(jnp.int32, (CHUNK, CHUNK), 1)
           <= lax.broadcasted_iota(jnp.int32, (CHUNK, CHUNK), 0))
    for gi in range(tn // GROUP_W):
        cols = slice(gi * GROUP_W, (gi + 1) * GROUP_W)
        u = jnp.dot(h, wu_ref[:, cols], preferred_element_type=F32)
        zg = jnp.dot(h, wz_ref[:, cols], preferred_element_type=F32)
        vn = ((v_ref[:, cols] - mu) * rstd * lg_ref[:, cols] + lb_ref[:, cols]).astype(BF16)
        wsm = jnp.where(tri, ws_ref[gi], 0.0).astype(BF16)
        bsg = bs_ref[gi]
        mix = jnp.concatenate(
            [jnp.dot(wsm, vn[c * CHUNK:(c + 1) * CHUNK], preferred_element_type=F32) + bsg
             for c in range(tm // CHUNK)], axis=0)
        o_ref[:, cols] = (u * mix * (zg * jax.nn.sigmoid(zg))).astype(o_ref.dtype)


def sgu_gate(h, w, col_u, col_z, v, mu, rstd, ln_g, ln_b, w_s, b_s, side, tm=1024, tn=512):
    m, k = h.shape
    n = SGU_W
    nj, ni = n // tn, m // tm
    ju, jz = col_u // tn, col_z // tn
    gpt = tn // GROUP_W
    stat_spec = pl.BlockSpec((tm, LANES), lambda j, i: (i, 0))
    side_rows = side.shape[0] // (nj * ni)
    assert side_rows * nj * ni == side.shape[0] and side_rows % HALO == 0
    side_spec = pl.BlockSpec((side_rows, side.shape[1]), lambda j, i: (j * ni + i, 0))
    return pl.pallas_call(
        _sgu_gate_kernel,
        out_shape=(jax.ShapeDtypeStruct((m, n), BF16), jax.ShapeDtypeStruct(side.shape, BF16)),
        grid=(nj, ni),
        in_specs=[pl.BlockSpec((tm, k), lambda j, i: (i, 0)),
                  pl.BlockSpec((k, tn), lambda j, i: (0, ju + j)),
                  pl.BlockSpec((k, tn), lambda j, i: (0, jz + j)),
                  pl.BlockSpec((tm, tn), lambda j, i: (i, j)),
                  stat_spec, stat_spec,
                  pl.BlockSpec((1, tn), lambda j, i: (0, j)),
                  pl.BlockSpec((1, tn), lambda j, i: (0, j)),
                  pl.BlockSpec((gpt, CHUNK, CHUNK), lambda j, i: (j, 0, 0)),
                  pl.BlockSpec((gpt, CHUNK, 1), lambda j, i: (j, 0, 0)),
                  side_spec],
        out_specs=(pl.BlockSpec((tm, tn), lambda j, i: (i, j)), side_spec),
        scratch_shapes=[pltpu.VMEM((k, tn), BF16)] * 2,
        compiler_params=_cparams(2),
        name="sgu_gate",
    )(h, w, w, v, mu, rstd, ln_g.reshape(1, n), ln_b.reshape(1, n), w_s, b_s.reshape(N_GROUPS, CHUNK, 1),
      side)


def _rotary_tables(seq):
    half = ROT_DIM // 2
    inv_freq = jnp.power(ROPE_THETA, -jnp.arange(half, dtype=F32) * 2.0 / ROT_DIM)
    ang = jnp.arange(seq).astype(F32)[:, None] * inv_freq[None, :]
    cos, sin = jnp.cos(ang), jnp.sin(ang)
    rest = HEAD_DIM - ROT_DIM
    c = jnp.concatenate([cos, cos, jnp.ones((seq, rest), F32)], axis=1)
    sa = jnp.concatenate([jnp.zeros((seq, half), F32), sin, jnp.zeros((seq, rest), F32)], axis=1)
    sb = jnp.concatenate([-sin, jnp.zeros((seq, half + rest), F32)], axis=1)
    rot_t = jnp.concatenate([cos, cos, -sin, sin], axis=1).T
    return (c, sa, sb), rot_t


def _overlap_t(seq):
    n_blk = seq // SLC_LEN
    n_rows = seq // CMP_STRIDE
    n_cmp = (seq - CMP_LEN) // CMP_STRIDE + 1
    cs = jnp.arange(n_rows) * CMP_STRIDE
    bs = jnp.arange(n_blk) * SLC_LEN
    ov = (cs[None, :] < bs[:, None] + SLC_LEN) & (cs[None, :] + CMP_LEN > bs[:, None])
    ov = ov & (jnp.arange(n_rows)[None, :] < n_cmp)
    return ov.astype(F32)


def kernel(x, norm_even, w_in_even, conv_w, cmp_k_pos, cmp_k_w1, cmp_k_b1, cmp_k_w2, cmp_v_pos, cmp_v_w1, cmp_v_b1, cmp_v_w2, w_out_even, norm_odd, w_in_odd, sgu_ln_g, sgu_ln_b, sgu_w_s, sgu_b_s, w_out_odd, norm_final):
    batch, seq, d = x.shape
    m = batch * seq
    x2d = x.reshape(m, d)

    wt_in = jnp.swapaxes(w_in_even[0], 0, 1)
    cw = CONV_W
    qw = N_HEADS * HEAD_DIM
    kvw = N_KV * HEAD_DIM
    o_q = 4 * cw
    o_kv = o_q + qw
    o_gl = o_kv + 6 * kvw
    o_ng = o_gl + 3 * N_HEADS

    h0, gl = rmsnorm_proj(x2d, norm_even[0], wt_in, o_gl, LANES)
    y_conv, w_out0 = conv_proj(h0, wt_in, conv_w[0], seq, w_out_even[0])
    z = matmul_t(h0, wt_in, o_q, qw + 6 * kvw, tn=1024, name="nsa_proj")
    z_ng = matmul_t(h0, wt_in, o_ng, qw, tn=1024, name="nsa_gate_proj")

    tabs, rot_t = _rotary_tables(seq)
    cmp_k = (cmp_k_pos[0], cmp_k_w1[0].astype(BF16).reshape(CMP_LEN, HEAD_DIM, HEAD_DIM),
             cmp_k_b1[0].reshape(1, HEAD_DIM), cmp_k_w2[0].astype(BF16))
    cmp_v = (cmp_v_pos[0], cmp_v_w1[0].astype(BF16).reshape(CMP_LEN, HEAD_DIM, HEAD_DIM),
             cmp_v_b1[0].reshape(1, HEAD_DIM), cmp_v_w2[0].astype(BF16))
    prep = nsa_prep(z, tabs, cmp_k, cmp_v, batch, seq, col0=qw // HEAD_DIM)
    y_nsa = nsa_attention_pair(z, z_ng, gl, rot_t, _overlap_t(seq), prep, batch, seq)

    x1, h1 = out_proj_norm(y_conv, 0, y_nsa, 0, w_out0, x2d, norm_odd[0],
                           BF16, True, name="out_proj_even")

    w_in1 = w_in_odd[0]
    v, mu, rstd = sgu_v(h1, w_in1, SGU_W)
    act, w_out1 = sgu_gate(h1, w_in1, 0, 2 * SGU_W, v, mu, rstd,
                           sgu_ln_g[0], sgu_ln_b[0], sgu_w_s[0], sgu_b_s[0], w_out_odd[0])
    (out,) = out_proj_norm(act, 0, act, 1, w_out1, x1, norm_final,
                           F32, False, name="out_proj_odd")
    return out.reshape(batch, seq, d)
```

```python
import functools
import math

import jax
import jax.numpy as jnp
from jax import lax
from jax.experimental import pallas as pl
from jax.experimental.pallas import tpu as pltpu

F32 = jnp.float32
BF16 = jnp.bfloat16

D_MODEL = 2048
MIX = 2 * D_MODEL
CONV_W = MIX // 2
CONV_K = 3
HEAD_DIM = 128
N_HEADS = 16
N_KV = 4
GQA = N_HEADS // N_KV
ROT_DIM = HEAD_DIM // 4
ROPE_THETA = 500000.0
CMP_LEN = 32
CMP_STRIDE = 16
SLC_LEN = 64
N_SEL = 8
WINDOW = 512
SGU_W = MIX
CHUNK = 128
N_GROUPS = 16
GROUP_W = SGU_W // N_GROUPS
EPS = 1e-6

LANES = 128
SUBLANES = 8
SLC_SHIFT = 6
HALO = 16
KEY_TILE = LANES
V_SUB = 512
XPOSE_ROWS = 256
VMEM_LIMIT = 56 * 1024 * 1024

NEG = -1e30

EXT_DIM = 2 * HEAD_DIM


def _cparams(n_axes):
    return pltpu.CompilerParams(
        dimension_semantics=("arbitrary",) * n_axes, vmem_limit_bytes=VMEM_LIMIT)


def _rmsnorm_kernel(x_ref, g_ref, o_ref):
    x = x_ref[...]
    y = x * lax.rsqrt(jnp.mean(x * x, axis=-1, keepdims=True) + EPS)
    o_ref[...] = (y * g_ref[...]).astype(o_ref.dtype)


def _rmsnorm_proj_kernel(x_ref, g_ref, wt_ref, o_ref, p_ref, wb_ref):
    @pl.when(pl.program_id(0) == 0)
    def _():
        wb_ref[...] = wt_ref[...].T.astype(BF16)

    x = x_ref[...]
    y = x * lax.rsqrt(jnp.mean(x * x, axis=-1, keepdims=True) + EPS)
    hb = (y * g_ref[...]).astype(BF16)
    o_ref[...] = hb
    p_ref[...] = jnp.dot(hb, wb_ref[...], preferred_element_type=F32)


def rmsnorm(x2d, g, out_dtype, tm=512):
    m, d = x2d.shape
    return pl.pallas_call(
        _rmsnorm_kernel,
        out_shape=jax.ShapeDtypeStruct((m, d), out_dtype),
        grid=(m // tm,),
        in_specs=[pl.BlockSpec((tm, d), lambda i: (i, 0)),
                  pl.BlockSpec((1, d), lambda i: (0, 0))],
        out_specs=pl.BlockSpec((tm, d), lambda i: (i, 0)),
        compiler_params=_cparams(1),
        name="rmsnorm",
    )(x2d, g.reshape(1, d))


def rmsnorm_proj(x2d, g, wt, row0, n, tm=512):
    m, d = x2d.shape
    assert row0 % n == 0
    return pl.pallas_call(
        _rmsnorm_proj_kernel,
        out_shape=(jax.ShapeDtypeStruct((m, d), BF16), jax.ShapeDtypeStruct((m, n), F32)),
        grid=(m // tm,),
        in_specs=[pl.BlockSpec((tm, d), lambda i: (i, 0)),
                  pl.BlockSpec((1, d), lambda i: (0, 0)),
                  pl.BlockSpec((n, d), lambda i: (row0 // n, 0))],
        out_specs=(pl.BlockSpec((tm, d), lambda i: (i, 0)),
                   pl.BlockSpec((tm, n), lambda i: (i, 0))),
        scratch_shapes=[pltpu.VMEM((d, n), BF16)],
        compiler_params=_cparams(1),
        name="rmsnorm_gates",
    )(x2d, g.reshape(1, d), wt)


def _cast_once(w_refs, wb_refs, transposed=False):
    @pl.when(pl.program_id(1) == 0)
    def _():
        for w_ref, wb_ref in zip(w_refs, wb_refs):
            if transposed:
                for r in range(0, w_ref.shape[0], XPOSE_ROWS):
                    wb_ref[:, r:r + XPOSE_ROWS] = w_ref[r:r + XPOSE_ROWS, :].T.astype(BF16)
            else:
                wb_ref[...] = w_ref[...].astype(BF16)


def _matmul_t_kernel(a_ref, wt_ref, o_ref, wb_ref):
    _cast_once([wt_ref], [wb_ref], transposed=True)
    o_ref[...] = jnp.dot(a_ref[...], wb_ref[...], preferred_element_type=F32)


def matmul_t(a, wt, row0, n, tm=1024, tn=512, name="proj"):
    m, k = a.shape
    assert n % tn == 0 and row0 % HALO == 0
    if row0 % tn == 0:
        wspec = pl.BlockSpec((tn, k), lambda j, i: (row0 // tn + j, 0))
    else:
        wspec = pl.BlockSpec((pl.Element(tn), pl.Element(k)),
                             lambda j, i: (pl.multiple_of(row0 + j * tn, HALO), 0))
    return pl.pallas_call(
        _matmul_t_kernel,
        out_shape=jax.ShapeDtypeStruct((m, n), F32),
        grid=(n // tn, m // tm),
        in_specs=[pl.BlockSpec((tm, k), lambda j, i: (i, 0)), wspec],
        out_specs=pl.BlockSpec((tm, tn), lambda j, i: (i, j)),
        scratch_shapes=[pltpu.VMEM((k, tn), BF16)],
        compiler_params=_cparams(2),
        name=name,
    )(a, wt)


def _conv_proj_kernel(h_ref, hp_ref, wb32_ref, wc32_ref, wh32_ref, wg32_ref, cw_ref, side32_ref,
                      o_ref, side_ref, wb_ref, wc_ref, wh_ref, wg_ref, *, tiles_per_seq):
    _cast_once([wb32_ref, wc32_ref, wh32_ref, wg32_ref], [wb_ref, wc_ref, wh_ref, wg_ref],
               transposed=True)
    side_ref[...] = side32_ref[...].astype(BF16)
    i = pl.program_id(1)
    cw = cw_ref[...]
    hp = hp_ref[...]
    pp = (jnp.dot(hp, wc_ref[...], preferred_element_type=F32)
          * jnp.dot(hp, wh_ref[...], preferred_element_type=F32))
    pp = jnp.where(i % tiles_per_seq == 0, 0.0, pp)
    prev1, prev2 = pp[HALO - 1:HALO, :], pp[HALO - 2:HALO - 1, :]
    h = h_ref[...]
    cb = jnp.dot(h, wb_ref[...], preferred_element_type=F32)
    cc = jnp.dot(h, wc_ref[...], preferred_element_type=F32)
    ch = jnp.dot(h, wh_ref[...], preferred_element_type=F32)
    cg = jnp.dot(h, wg_ref[...], preferred_element_type=F32)
    p = cc * ch
    row = lax.broadcasted_iota(jnp.int32, (SUBLANES, p.shape[1]), 0)
    p1, p2 = pltpu.roll(p, 1, axis=0), pltpu.roll(p, 2, axis=0)
    p1 = jnp.concatenate([jnp.where(row == 0, prev1, p1[:SUBLANES]), p1[SUBLANES:]], axis=0)
    p2 = jnp.concatenate([jnp.where(row == 0, prev2, jnp.where(row == 1, prev1, p2[:SUBLANES])),
                          p2[SUBLANES:]], axis=0)
    conv = cw[0:1, :] * p2 + cw[1:2, :] * p1 + cw[2:3, :] * p
    o_ref[...] = (cb * conv * (cg * jax.nn.sigmoid(cg))).astype(o_ref.dtype)


def conv_proj(h, wt, conv_w, seq, side, tm=1024, tn=256):
    m, k = h.shape
    n = conv_w.shape[1]
    tiles_per_seq = seq // tm
    halo_per_tile = tm // HALO
    nj, ni = n // tn, m // tm
    side_rows = side.shape[0] // (nj * ni)
    assert side_rows * nj * ni == side.shape[0] and side_rows % HALO == 0
    side_spec = pl.BlockSpec((side_rows, side.shape[1]), lambda j, i: (j * ni + i, 0))

    def wspec(which):
        return pl.BlockSpec((tn, k), lambda j, i: (which * nj + j, 0))

    return pl.pallas_call(
        functools.partial(_conv_proj_kernel, tiles_per_seq=tiles_per_seq),
        out_shape=(jax.ShapeDtypeStruct((m, n), BF16), jax.ShapeDtypeStruct(side.shape, BF16)),
        grid=(nj, ni),
        in_specs=[pl.BlockSpec((tm, k), lambda j, i: (i, 0)),
                  pl.BlockSpec((HALO, k), lambda j, i: (jnp.maximum(i * halo_per_tile - 1, 0), 0)),
                  wspec(0), wspec(1), wspec(2), wspec(3),
                  pl.BlockSpec((CONV_K, tn), lambda j, i: (0, j)),
                  side_spec],
        out_specs=(pl.BlockSpec((tm, tn), lambda j, i: (i, j)), side_spec),
        scratch_shapes=[pltpu.VMEM((k, tn), BF16)] * 4,
        compiler_params=_cparams(2),
        name="conv_proj",
    )(h, h, wt, wt, wt, wt, conv_w, side)


def _rotary(x, c, sa, sb):
    half = ROT_DIM // 2
    return (x * c + pltpu.roll(x, half, axis=1) * sa
            + pltpu.roll(x, HEAD_DIM - half, axis=1) * sb)


def _compress(src_ref, pos_ref, w1_ref, b1_ref, w2_ref):
    n_rows = src_ref.shape[0] // CMP_STRIDE
    p_acc = jnp.zeros((n_rows, HEAD_DIM), F32)
    q_acc = jnp.zeros((n_rows, HEAD_DIM), F32)
    for r in range(CMP_STRIDE):
        s_r = src_ref[pl.ds(r, n_rows, stride=CMP_STRIDE), :]
        a_r = (s_r + pos_ref[r:r + 1, :]).astype(BF16)
        b_r = (s_r + pos_ref[CMP_STRIDE + r:CMP_STRIDE + r + 1, :]).astype(BF16)
        p_acc += jnp.dot(a_r, w1_ref[r], preferred_element_type=F32)
        q_acc += jnp.dot(b_r, w1_ref[CMP_STRIDE + r], preferred_element_type=F32)
    hid = p_acc + pltpu.roll(q_acc, n_rows - 1, axis=0) + b1_ref[...]
    act = (hid * jax.nn.sigmoid(hid)).astype(BF16)
    return jnp.dot(act, w2_ref[...], preferred_element_type=F32)


def _nsa_prep_kernel(kc_ref, vc_ref, ks_ref, vs_ref, kw_ref, vw_ref, c_ref, sa_ref, sb_ref,
                     kpos_ref, kw1_ref, kb1_ref, kw2_ref, vpos_ref, vw1_ref, vb1_ref, vw2_ref,
                     ksx_ref, vst_ref, kwx_ref, vwt_ref, kcmp_ref, vcmp_ref):
    c, sa, sb = c_ref[...], sa_ref[...], sb_ref[...]
    seq = ks_ref.shape[0]
    n_blk = seq // SLC_LEN
    flag_w = EXT_DIM - HEAD_DIM
    blk = lax.broadcasted_iota(jnp.int32, (seq, flag_w), 0) >> SLC_SHIFT
    onehot = jnp.where(blk == lax.broadcasted_iota(jnp.int32, (seq, flag_w), 1), 1.0, 0.0)
    ksx_ref[:, 0:HEAD_DIM] = _rotary(ks_ref[...], c, sa, sb).astype(BF16)
    ksx_ref[:, HEAD_DIM:] = onehot.astype(BF16)
    pad_flag = jnp.where(lax.broadcasted_iota(jnp.int32, (WINDOW, EXT_DIM), 1) == HEAD_DIM + n_blk, 1.0, 0.0)
    kwx_ref[0:WINDOW, :] = pad_flag.astype(BF16)
    kwx_ref[WINDOW:, 0:HEAD_DIM] = _rotary(kw_ref[...], c, sa, sb).astype(BF16)
    kwx_ref[WINDOW:, HEAD_DIM:] = jnp.zeros((seq, flag_w), BF16)
    n_tiles = seq // KEY_TILE
    pad_tiles = WINDOW // KEY_TILE
    for t in range(pad_tiles):
        vwt_ref[t] = jnp.zeros((HEAD_DIM, KEY_TILE), BF16)
    for t in range(n_tiles):
        rows = pl.ds(t * KEY_TILE, KEY_TILE)
        vst_ref[t] = vs_ref[rows, :].T.astype(BF16)
        vwt_ref[pad_tiles + t] = vw_ref[rows, :].T.astype(BF16)
    kcmp_ref[...] = _compress(kc_ref, kpos_ref, kw1_ref, kb1_ref, kw2_ref).astype(BF16)
    vcmp_ref[...] = _compress(vc_ref, vpos_ref, vw1_ref, vb1_ref, vw2_ref).astype(BF16)


def nsa_prep(z, tabs, cmp_k, cmp_v, batch, seq, col0):
    def zspec(which):
        return pl.BlockSpec((seq, HEAD_DIM), lambda b, g, w=which: (b, col0 + w * N_KV + g))

    tab = pl.BlockSpec((seq, HEAD_DIM), lambda b, g: (0, 0))

    def wspecs():
        return [pl.BlockSpec((CMP_LEN, HEAD_DIM), lambda b, g: (0, 0)),
                pl.BlockSpec((CMP_LEN, HEAD_DIM, HEAD_DIM), lambda b, g: (0, 0, 0)),
                pl.BlockSpec((1, HEAD_DIM), lambda b, g: (0, 0)),
                pl.BlockSpec((HEAD_DIM, HEAD_DIM), lambda b, g: (0, 0))]

    def out(shape):
        nd = len(shape)
        spec = pl.BlockSpec((None, None) + shape, lambda b, g: (b, g) + (0,) * nd)
        return spec, jax.ShapeDtypeStruct((batch, N_KV) + shape, BF16)

    n_tiles = seq // KEY_TILE
    pad_tiles = WINDOW // KEY_TILE
    outs = [out((seq, EXT_DIM)), out((n_tiles, HEAD_DIM, KEY_TILE)),
            out((WINDOW + seq, EXT_DIM)), out((pad_tiles + n_tiles, HEAD_DIM, KEY_TILE)),
            out((seq // CMP_STRIDE, HEAD_DIM)), out((seq // CMP_STRIDE, HEAD_DIM))]
    return pl.pallas_call(
        _nsa_prep_kernel,
        out_shape=tuple(o[1] for o in outs),
        grid=(batch, N_KV),
        in_specs=[zspec(0), zspec(1), zspec(2), zspec(3), zspec(4), zspec(5), tab, tab, tab]
        + wspecs() + wspecs(),
        out_specs=tuple(o[0] for o in outs),
        compiler_params=_cparams(2),
        name="nsa_prep",
    )(z, z, z, z, z, z, *tabs, *cmp_k, *cmp_v)


def _nsa_attn_pair_kernel(q_ref, ng_ref, gl_ref, rot_ref, ov_ref,
                          ksx_ref, vst_ref, kwx_ref, vwt_ref, kcmp_ref, vcmp_ref,
                          o_ref, acc_ref, qx_ref, glt_ref, s_ref, m_ref, l_ref, oc_ref, *, tq, n_grp_step):
    gp = pl.program_id(1)
    i = pl.program_id(2)
    q0 = i * tq
    n_blk = ov_ref.shape[0]
    half = ROT_DIM // 2
    qscale = (HEAD_DIM ** -0.5) * math.log2(math.e)
    qw = GQA * HEAD_DIM
    groups = range(n_grp_step)

    def head_lanes(n):
        return slice(n * tq, (n + 1) * tq)

    def lanes4(x):
        return jnp.concatenate([x] * GQA, axis=1)

    rot = rot_ref[...]
    cos2, sin2 = rot[:ROT_DIM], rot[ROT_DIM:]
    tlane = q0 + lax.broadcasted_iota(jnp.int32, (1, tq), 1)
    kk = lax.broadcasted_iota(jnp.int32, (tq, tq), 0)
    tt = lax.broadcasted_iota(jnp.int32, (tq, tq), 1)
    tri_diag = jnp.where(kk <= tt, 0.0, NEG)
    tri_old = jnp.where(kk > tt, 0.0, NEG)
    n_cmp = kcmp_ref.shape[1]
    cend = lax.broadcasted_iota(jnp.int32, (n_cmp, tq), 0) * CMP_STRIDE + (CMP_LEN - 1)
    bias_c = lanes4(jnp.where(cend <= tlane, 0.0, NEG))
    any_c = lanes4(tlane) >= CMP_LEN - 1
    jblk = lax.broadcasted_iota(jnp.int32, (n_blk, tq), 0)
    forced = (jblk == 0) | (jblk == (tlane >> SLC_SHIFT))
    causal_blk = jblk * SLC_LEN <= tlane
    sub = lax.broadcasted_iota(jnp.int32, (SUBLANES, tq), 0)
    n_rest = EXT_DIM - HEAD_DIM - n_blk
    pad_rows = jnp.where(lax.broadcasted_iota(jnp.int32, (n_rest, GQA * tq), 0) == 0, NEG, 0.0).astype(BF16)
    tiles = tq // KEY_TILE

    def sel_scores(gg, ci):
        k0 = pl.multiple_of(ci * tq, tq)
        return jnp.dot(ksx_ref[gg, pl.ds(k0, tq), :], qx_ref[gg], preferred_element_type=F32)

    def sel_update(gg, ci, s):
        vblk = jnp.concatenate([vst_ref[gg, ci * tiles + r] for r in range(tiles)], axis=1)
        m_old = m_ref[gg]
        m_new = jnp.maximum(m_old, jnp.max(s, axis=0, keepdims=True))
        alpha = jnp.exp2(m_old - m_new)
        p = jnp.exp2(s - m_new)
        l_ref[gg] = alpha * l_ref[gg] + jnp.sum(p, axis=0, keepdims=True)
        acc_ref[gg] = alpha * acc_ref[gg] + jnp.dot(vblk, p.astype(BF16), preferred_element_type=F32)
        m_ref[gg] = m_new

    for gg in groups:
        q = q_ref[:, gg * qw:(gg + 1) * qw]
        qts = []
        for n in range(GQA):
            qt = (q[:, n * HEAD_DIM:(n + 1) * HEAD_DIM] * qscale).T
            qts.append(qt)
            top = qt[:ROT_DIM]
            swapped = jnp.concatenate([top[half:], top[:half]], axis=0)
            qrt = jnp.concatenate([top * cos2 + swapped * sin2, qt[ROT_DIM:]], axis=0)
            qx_ref[gg, 0:HEAD_DIM, head_lanes(n)] = qrt.astype(BF16)
        qt_all = jnp.concatenate(qts, axis=1).astype(BF16)

        s_c = jnp.dot(kcmp_ref[gg], qt_all, preferred_element_type=F32) + bias_c
        m_c = jnp.where(any_c, jnp.max(s_c, axis=0, keepdims=True), 0.0)
        e_c = jnp.exp2(s_c - m_c)
        den_c = jnp.sum(e_c, axis=0, keepdims=True)
        p_c = e_c / jnp.where(den_c > 0, den_c, 1.0)
        vcmp_t = vcmp_ref[gg].astype(F32).T.astype(BF16)
        oc_ref[gg] = jnp.dot(vcmp_t, p_c.astype(BF16), preferred_element_type=F32)

        p_sum = p_c[:, head_lanes(0)]
        for n in range(1, GQA):
            p_sum = p_sum + p_c[:, head_lanes(n)]
        imp = jnp.dot(ov_ref[...], p_sum, preferred_element_type=F32,
                      precision=lax.Precision.HIGHEST)
        imp = jnp.where(forced, jnp.inf, jnp.where(causal_blk, imp, -jnp.inf))
        n_sub = n_blk // SUBLANES
        part = [imp[a * SUBLANES:(a + 1) * SUBLANES] for a in range(n_sub)]
        cnt = [jnp.zeros((SUBLANES, tq), F32) for _ in range(n_sub)]
        for r in range(n_blk):
            row = imp[r:r + 1, :]
            a_r, r_in = divmod(r, SUBLANES)
            for a in range(n_sub):
                if a > a_r:
                    cnt[a] = cnt[a] + jnp.where(row >= part[a], 1.0, 0.0)
                elif a < a_r:
                    cnt[a] = cnt[a] + jnp.where(row > part[a], 1.0, 0.0)
                else:
                    cnt[a] = cnt[a] + jnp.where(sub > r_in, jnp.where(row >= part[a], 1.0, 0.0),
                                                jnp.where(row > part[a], 1.0, 0.0))
        cnt = jnp.concatenate(cnt, axis=0)
        sel_bias = jnp.where(cnt < N_SEL, jnp.where(imp > -jnp.inf, 0.0, NEG), NEG)
        qx_ref[gg, HEAD_DIM:HEAD_DIM + n_blk, :] = lanes4(sel_bias).astype(BF16)
        qx_ref[gg, HEAD_DIM + n_blk:, :] = pad_rows

        m_ref[gg] = jnp.full(m_ref.shape[1:], NEG, F32)
        l_ref[gg] = jnp.zeros(l_ref.shape[1:], F32)
        acc_ref[gg] = jnp.zeros(acc_ref.shape[1:], F32)
        s_ref[gg, 0] = sel_scores(gg, 0)

    def sel_pair(pi, carry):
        c0 = 2 * pi
        for gg in groups:
            s_ref[gg, 1] = sel_scores(gg, c0 + 1)
        for gg in groups:
            sel_update(gg, c0, s_ref[gg, 0])
        for gg in groups:
            s_ref[gg, 0] = sel_scores(gg, c0 + 2)
        for gg in groups:
            sel_update(gg, c0 + 1, s_ref[gg, 1])
        return carry

    lax.fori_loop(0, i // 2, sel_pair, 0)

    @pl.when(i % 2 == 1)
    def _():
        for gg in groups:
            s_ref[gg, 1] = sel_scores(gg, i)
        for gg in groups:
            sel_update(gg, i - 1, s_ref[gg, 0])

    for gg in groups:
        sel_update(gg, i, s_ref[gg, i % 2] + lanes4(tri_diag))

    n_wt = WINDOW // tq
    vt0 = q0 // KEY_TILE
    glt_ref[...] = jax.nn.sigmoid(gl_ref[...]).T
    for gg in groups:
        kwin = kwx_ref[gg, pl.ds(pl.multiple_of(q0, tq), WINDOW + tq), :]
        vwin = jnp.concatenate([vwt_ref[gg, vt0 + r] for r in range((WINDOW + tq) // KEY_TILE)], axis=1)
        s_w = jnp.dot(kwin, qx_ref[gg], preferred_element_type=F32)
        s_w = jnp.concatenate([s_w[:tq] + lanes4(tri_old), s_w[tq:n_wt * tq],
                               s_w[n_wt * tq:] + lanes4(tri_diag)], axis=0)
        e_w = jnp.exp2(s_w - jnp.max(s_w, axis=0, keepdims=True))
        o_w = (jnp.dot(vwin, e_w.astype(BF16), preferred_element_type=F32)
               / jnp.sum(e_w, axis=0, keepdims=True))
        o_s = acc_ref[gg] / l_ref[gg]
        o_c = oc_ref[gg]
        ng = ng_ref[:, gg * qw:(gg + 1) * qw]
        for n in range(GQA):
            head = (gp * n_grp_step + gg) * GQA + n
            gc, gs, gw = (glt_ref[pl.ds(j * N_HEADS + head, 1), :] for j in range(3))
            sl = head_lanes(n)
            o = (gc * o_c[:, sl] + gs * o_s[:, sl] + gw * o_w[:, sl]).T
            ngh = ng[:, n * HEAD_DIM:(n + 1) * HEAD_DIM]
            col = gg * qw + n * HEAD_DIM
            o_ref[:, col:col + HEAD_DIM] = (o * (ngh * jax.nn.sigmoid(ngh))).astype(o_ref.dtype)


def nsa_attention_pair(z, z_ng, gl, rot_t, overlap_t, prep, batch, seq, tq=256, n_grp_step=2):
    ksx, vst, kwx, vwt, kcmp, vcmp = prep
    assert tq % KEY_TILE == 0 and WINDOW % tq == 0 and WINDOW >= 2 * tq and N_KV % n_grp_step == 0
    nq = seq // tq
    qw = GQA * HEAD_DIM * n_grp_step

    def whole(a):
        nd = a.ndim - 2
        return pl.BlockSpec((None, n_grp_step) + a.shape[2:], lambda b, g, i: (b, g) + (0,) * nd)

    lanes = GQA * tq
    return pl.pallas_call(
        functools.partial(_nsa_attn_pair_kernel, tq=tq, n_grp_step=n_grp_step),
        out_shape=jax.ShapeDtypeStruct((batch * seq, N_HEADS * HEAD_DIM), BF16),
        grid=(batch, N_KV // n_grp_step, nq),
        in_specs=[pl.BlockSpec((tq, qw), lambda b, g, i: (b * nq + i, g)),
                  pl.BlockSpec((tq, qw), lambda b, g, i: (b * nq + i, g)),
                  pl.BlockSpec((tq, gl.shape[1]), lambda b, g, i: (b * nq + i, 0)),
                  pl.BlockSpec((rot_t.shape[0], tq), lambda b, g, i: (0, i)),
                  pl.BlockSpec(overlap_t.shape, lambda b, g, i: (0, 0)),
                  whole(ksx), whole(vst), whole(kwx), whole(vwt), whole(kcmp), whole(vcmp)],
        out_specs=pl.BlockSpec((tq, qw), lambda b, g, i: (b * nq + i, g)),
        scratch_shapes=[pltpu.VMEM((n_grp_step, HEAD_DIM, lanes), F32),
                        pltpu.VMEM((n_grp_step, EXT_DIM, lanes), BF16),
                        pltpu.VMEM((gl.shape[1], tq), F32),
                        pltpu.VMEM((n_grp_step, 2, tq, lanes), F32),
                        pltpu.VMEM((n_grp_step, 1, lanes), F32),
                        pltpu.VMEM((n_grp_step, 1, lanes), F32),
                        pltpu.VMEM((n_grp_step, HEAD_DIM, lanes), F32)],
        compiler_params=_cparams(3),
        name="nsa_attention",
    )(z, z_ng, gl, rot_t, overlap_t, ksx, vst, kwx, vwt, kcmp, vcmp)


def _out_proj_norm_kernel(a0_ref, a1_ref, w_ref, x_ref, g_ref, *o_refs, kh, emit_residual):
    acc = jnp.dot(a0_ref[...], w_ref[0:kh, :], preferred_element_type=F32)
    acc += jnp.dot(a1_ref[...], w_ref[kh:, :], preferred_element_type=F32)
    x = x_ref[...] + acc
    y = x * lax.rsqrt(jnp.mean(x * x, axis=-1, keepdims=True) + EPS)
    o_refs[-1][...] = (y * g_ref[...]).astype(o_refs[-1].dtype)
    if emit_residual:
        o_refs[0][...] = x


def out_proj_norm(a0, a0_blk, a1, a1_blk, w, x2d, g, norm_dtype, emit_residual, tm=512, name="out_proj"):
    m, n = x2d.shape
    kh = w.shape[0] // 2
    row = pl.BlockSpec((tm, n), lambda i: (i, 0))
    out_shape = [jax.ShapeDtypeStruct((m, n), norm_dtype)]
    if emit_residual:
        out_shape.insert(0, jax.ShapeDtypeStruct((m, n), F32))
    return pl.pallas_call(
        functools.partial(_out_proj_norm_kernel, kh=kh, emit_residual=emit_residual),
        out_shape=tuple(out_shape),
        grid=(m // tm,),
        in_specs=[pl.BlockSpec((tm, kh), lambda i: (i, a0_blk)),
                  pl.BlockSpec((tm, kh), lambda i: (i, a1_blk)),
                  pl.BlockSpec(w.shape, lambda i: (0, 0), pipeline_mode=pl.Buffered(1)),
                  row,
                  pl.BlockSpec((1, n), lambda i: (0, 0))],
        out_specs=tuple(row for _ in out_shape),
        compiler_params=_cparams(1),
        name=name,
    )(a0, a1, w, x2d, g.reshape(1, n))


def _sgu_v_kernel(h_ref, w_ref, v_ref, mu_ref, rstd_ref, wb_ref, c_ref, s1_ref, s2_ref, *, n_col_tiles):
    j = pl.program_id(1)

    @pl.when(pl.program_id(0) == 0)
    def _():
        wb_ref[j] = w_ref[...].astype(BF16)

    tm, tn = v_ref.shape
    h = h_ref[...]
    for c0 in range(0, tn, V_SUB):
        v = jnp.dot(h, wb_ref[j, :, c0:c0 + V_SUB], preferred_element_type=F32)
        v_ref[:, c0:c0 + V_SUB] = v

        if c0 == 0:
            @pl.when(j == 0)
            def _():
                c_ref[...] = jnp.broadcast_to(jnp.sum(v, axis=-1, keepdims=True) / V_SUB, c_ref.shape)
                s1_ref[...] = jnp.zeros(s1_ref.shape, F32)
                s2_ref[...] = jnp.zeros(s2_ref.shape, F32)

        c = c_ref[...]
        s1, s2 = s1_ref[...], s2_ref[...]
        for r in range(V_SUB // LANES):
            d = v[:, r * LANES:(r + 1) * LANES] - c
            s1 = s1 + d
            s2 = s2 + d * d
        s1_ref[...] = s1
        s2_ref[...] = s2

    @pl.when(j == n_col_tiles - 1)
    def _():
        width = n_col_tiles * tn
        mean_d = jnp.sum(s1, axis=-1, keepdims=True) / width
        var = jnp.sum(s2, axis=-1, keepdims=True) / width - mean_d * mean_d
        mu_ref[...] = c + mean_d
        rstd_ref[...] = jnp.broadcast_to(lax.rsqrt(var + EPS), rstd_ref.shape)


def sgu_v(h, w, col0, tm=512, tn=2 * V_SUB):
    m, k = h.shape
    nj = SGU_W // tn
    j0 = col0 // tn
    stat = jax.ShapeDtypeStruct((m, LANES), F32)
    stat_spec = pl.BlockSpec((tm, LANES), lambda i, j: (i, 0))
    return pl.pallas_call(
        functools.partial(_sgu_v_kernel, n_col_tiles=nj),
        out_shape=(jax.ShapeDtypeStruct((m, SGU_W), F32), stat, stat),
        grid=(m // tm, nj),
        in_specs=[pl.BlockSpec((tm, k), lambda i, j: (i, 0)),
                  pl.BlockSpec((k, tn), lambda i, j: (0, j0 + jnp.where(i == 0, j, 0)))],
        out_specs=(pl.BlockSpec((tm, tn), lambda i, j: (i, j)), stat_spec, stat_spec),
        scratch_shapes=[pltpu.VMEM((nj, k, tn), BF16)] + [pltpu.VMEM((tm, LANES), F32)] * 3,
        compiler_params=_cparams(2),
        name="sgu_v",
    )(h, w)


def _sgu_gate_kernel(h_ref, wu32_ref, wz32_ref, v_ref, mu_ref, rstd_ref, lg_ref, lb_ref, ws_ref, bs_ref,
                     side32_ref, o_ref, side_ref, wu_ref, wz_ref):
    _cast_once([wu32_ref, wz32_ref], [wu_ref, wz_ref])
    side_ref[...] = side32_ref[...].astype(BF16)
    h = h_ref[...]
    tm, tn = v_ref.shape
    reps = GROUP_W // LANES
    mu = jnp.concatenate([mu_ref[...]] * reps, axis=1)
    rstd = jnp.concatenate([rstd_ref[...]] * reps, axis=1)
    tri = (lax.broadcasted_iota(jnp.int32, (CHUNK, CHUNK), 1)
           <= lax.broadcasted_iota(jnp.int32, (CHUNK, CHUNK), 0))
    for gi in range(tn // GROUP_W):
        cols = slice(gi * GROUP_W, (gi + 1) * GROUP_W)
        u = jnp.dot(h, wu_ref[:, cols], preferred_element_type=F32)
        zg = jnp.dot(h, wz_ref[:, cols], preferred_element_type=F32)
        vn = ((v_ref[:, cols] - mu) * rstd * lg_ref[:, cols] + lb_ref[:, cols]).astype(BF16)
        wsm = jnp.where(tri, ws_ref[gi], 0.0).astype(BF16)
        bsg = bs_ref[gi]
        mix = jnp.concatenate(
            [jnp.dot(wsm, vn[c * CHUNK:(c + 1) * CHUNK], preferred_element_type=F32) + bsg
             for c in range(tm // CHUNK)], axis=0)
        o_ref[:, cols] = (u * mix * (zg * jax.nn.sigmoid(zg))).astype(o_ref.dtype)


def sgu_gate(h, w, col_u, col_z, v, mu, rstd, ln_g, ln_b, w_s, b_s, side, tm=1024, tn=512):
    m, k = h.shape
    n = SGU_W
    nj, ni = n // tn, m // tm
    ju, jz = col_u // tn, col_z // tn
    gpt = tn // GROUP_W
    stat_spec = pl.BlockSpec((tm, LANES), lambda j, i: (i, 0))
    side_rows = side.shape[0] // (nj * ni)
    assert side_rows * nj * ni == side.shape[0] and side_rows % HALO == 0
    side_spec = pl.BlockSpec((side_rows, side.shape[1]), lambda j, i: (j * ni + i, 0))
    return pl.pallas_call(
        _sgu_gate_kernel,
        out_shape=(jax.ShapeDtypeStruct((m, n), BF16), jax.ShapeDtypeStruct(side.shape, BF16)),
        grid=(nj, ni),
        in_specs=[pl.BlockSpec((tm, k), lambda j, i: (i, 0)),
                  pl.BlockSpec((k, tn), lambda j, i: (0, ju + j)),
                  pl.BlockSpec((k, tn), lambda j, i: (0, jz + j)),
                  pl.BlockSpec((tm, tn), lambda j, i: (i, j)),
                  stat_spec, stat_spec,
                  pl.BlockSpec((1, tn), lambda j, i: (0, j)),
                  pl.BlockSpec((1, tn), lambda j, i: (0, j)),
                  pl.BlockSpec((gpt, CHUNK, CHUNK), lambda j, i: (j, 0, 0)),
                  pl.BlockSpec((gpt, CHUNK, 1), lambda j, i: (j, 0, 0)),
                  side_spec],
        out_specs=(pl.BlockSpec((tm, tn), lambda j, i: (i, j)), side_spec),
        scratch_shapes=[pltpu.VMEM((k, tn), BF16)] * 2,
        compiler_params=_cparams(2),
        name="sgu_gate",
    )(h, w, w, v, mu, rstd, ln_g.reshape(1, n), ln_b.reshape(1, n), w_s, b_s.reshape(N_GROUPS, CHUNK, 1),
      side)


def _rotary_tables(seq):
    half = ROT_DIM // 2
    inv_freq = jnp.power(ROPE_THETA, -jnp.arange(half, dtype=F32) * 2.0 / ROT_DIM)
    ang = jnp.arange(seq).astype(F32)[:, None] * inv_freq[None, :]
    cos, sin = jnp.cos(ang), jnp.sin(ang)
    rest = HEAD_DIM - ROT_DIM
    c = jnp.concatenate([cos, cos, jnp.ones((seq, rest), F32)], axis=1)
    sa = jnp.concatenate([jnp.zeros((seq, half), F32), sin, jnp.zeros((seq, rest), F32)], axis=1)
    sb = jnp.concatenate([-sin, jnp.zeros((seq, half + rest), F32)], axis=1)
    rot_t = jnp.concatenate([cos, cos, -sin, sin], axis=1).T
    return (c, sa, sb), rot_t


def _overlap_t(seq):
    n_blk = seq // SLC_LEN
    n_rows = seq // CMP_STRIDE
    n_cmp = (seq - CMP_LEN) // CMP_STRIDE + 1
    cs = jnp.arange(n_rows) * CMP_STRIDE
    bs = jnp.arange(n_blk) * SLC_LEN
    ov = (cs[None, :] < bs[:, None] + SLC_LEN) & (cs[None, :] + CMP_LEN > bs[:, None])
    ov = ov & (jnp.arange(n_rows)[None, :] < n_cmp)
    return ov.astype(F32)


def kernel(x, norm_even, w_in_even, conv_w, cmp_k_pos, cmp_k_w1, cmp_k_b1, cmp_k_w2, cmp_v_pos, cmp_v_w1, cmp_v_b1, cmp_v_w2, w_out_even, norm_odd, w_in_odd, sgu_ln_g, sgu_ln_b, sgu_w_s, sgu_b_s, w_out_odd, norm_final):
    batch, seq, d = x.shape
    m = batch * seq
    x2d = x.reshape(m, d)

    wt_in = jnp.swapaxes(w_in_even[0], 0, 1)
    cw = CONV_W
    qw = N_HEADS * HEAD_DIM
    kvw = N_KV * HEAD_DIM
    o_q = 4 * cw
    o_kv = o_q + qw
    o_gl = o_kv + 6 * kvw
    o_ng = o_gl + 3 * N_HEADS

    h0, gl = rmsnorm_proj(x2d, norm_even[0], wt_in, o_gl, LANES)
    y_conv, w_out0 = conv_proj(h0, wt_in, conv_w[0], seq, w_out_even[0])
    z = matmul_t(h0, wt_in, o_q, qw + 6 * kvw, tn=1024, name="nsa_proj")
    z_ng = matmul_t(h0, wt_in, o_ng, qw, tn=1024, name="nsa_gate_proj")

    tabs, rot_t = _rotary_tables(seq)
    cmp_k = (cmp_k_pos[0], cmp_k_w1[0].astype(BF16).reshape(CMP_LEN, HEAD_DIM, HEAD_DIM),
             cmp_k_b1[0].reshape(1, HEAD_DIM), cmp_k_w2[0].astype(BF16))
    cmp_v = (cmp_v_pos[0], cmp_v_w1[0].astype(BF16).reshape(CMP_LEN, HEAD_DIM, HEAD_DIM),
             cmp_v_b1[0].reshape(1, HEAD_DIM), cmp_v_w2[0].astype(BF16))
    prep = nsa_prep(z, tabs, cmp_k, cmp_v, batch, seq, col0=qw // HEAD_DIM)
    y_nsa = nsa_attention_pair(z, z_ng, gl, rot_t, _overlap_t(seq), prep, batch, seq)

    x1, h1 = out_proj_norm(y_conv, 0, y_nsa, 0, w_out0, x2d, norm_odd[0],
                           BF16, True, name="out_proj_even")

    w_in1 = w_in_odd[0]
    v, mu, rstd = sgu_v(h1, w_in1, SGU_W)
    act, w_out1 = sgu_gate(h1, w_in1, 0, 2 * SGU_W, v, mu, rstd,
                           sgu_ln_g[0], sgu_ln_b[0], sgu_w_s[0], sgu_b_s[0], w_out_odd[0])
    (out,) = out_proj_norm(act, 0, act, 1, w_out1, x1, norm_final,
                           F32, False, name="out_proj_odd")
    return out.reshape(batch, seq, d)
```

```python
import functools
import math

import jax
import jax.numpy as jnp
from jax import lax
from jax.experimental import pallas as pl
from jax.experimental.pallas import tpu as pltpu

F32 = jnp.float32
BF16 = jnp.bfloat16

D_MODEL = 2048
MIX = 2 * D_MODEL
CONV_W = MIX // 2
CONV_K = 3
HEAD_DIM = 128
N_HEADS = 16
N_KV = 4
GQA = N_HEADS // N_KV
ROT_DIM = HEAD_DIM // 4
ROPE_THETA = 500000.0
CMP_LEN = 32
CMP_STRIDE = 16
SLC_LEN = 64
N_SEL = 8
WINDOW = 512
SGU_W = MIX
CHUNK = 128
N_GROUPS = 16
GROUP_W = SGU_W // N_GROUPS
EPS = 1e-6

LANES = 128
SUBLANES = 8
SLC_SHIFT = 6
HALO = 16
KEY_TILE = LANES
V_SUB = 512
VMEM_LIMIT = 56 * 1024 * 1024

NEG = -1e30

EXT_DIM = 2 * HEAD_DIM


def _cparams(n_axes):
    return pltpu.CompilerParams(
        dimension_semantics=("arbitrary",) * n_axes, vmem_limit_bytes=VMEM_LIMIT)


def _rmsnorm_kernel(x_ref, g_ref, o_ref):
    x = x_ref[...]
    y = x * lax.rsqrt(jnp.mean(x * x, axis=-1, keepdims=True) + EPS)
    o_ref[...] = (y * g_ref[...]).astype(o_ref.dtype)


def _rmsnorm_proj_kernel(x_ref, g_ref, wt_ref, o_ref, p_ref, wb_ref):
    @pl.when(pl.program_id(0) == 0)
    def _():
        wb_ref[...] = wt_ref[...].astype(BF16)

    x = x_ref[...]
    y = x * lax.rsqrt(jnp.mean(x * x, axis=-1, keepdims=True) + EPS)
    hb = (y * g_ref[...]).astype(BF16)
    o_ref[...] = hb
    p_ref[...] = _dot_nt(hb, wb_ref[...])


def rmsnorm(x2d, g, out_dtype, tm=512):
    m, d = x2d.shape
    return pl.pallas_call(
        _rmsnorm_kernel,
        out_shape=jax.ShapeDtypeStruct((m, d), out_dtype),
        grid=(m // tm,),
        in_specs=[pl.BlockSpec((tm, d), lambda i: (i, 0)),
                  pl.BlockSpec((1, d), lambda i: (0, 0))],
        out_specs=pl.BlockSpec((tm, d), lambda i: (i, 0)),
        compiler_params=_cparams(1),
        name="rmsnorm",
    )(x2d, g.reshape(1, d))


def rmsnorm_proj(x2d, g, wt, row0, n, tm=512):
    m, d = x2d.shape
    assert row0 % n == 0
    return pl.pallas_call(
        _rmsnorm_proj_kernel,
        out_shape=(jax.ShapeDtypeStruct((m, d), BF16), jax.ShapeDtypeStruct((m, n), F32)),
        grid=(m // tm,),
        in_specs=[pl.BlockSpec((tm, d), lambda i: (i, 0)),
                  pl.BlockSpec((1, d), lambda i: (0, 0)),
                  pl.BlockSpec((n, d), lambda i: (row0 // n, 0))],
        out_specs=(pl.BlockSpec((tm, d), lambda i: (i, 0)),
                   pl.BlockSpec((tm, n), lambda i: (i, 0))),
        scratch_shapes=[pltpu.VMEM((n, d), BF16)],
        compiler_params=_cparams(1),
        name="rmsnorm_gates",
    )(x2d, g.reshape(1, d), wt)


def _cast_once(w_refs, wb_refs):
    @pl.when(pl.program_id(1) == 0)
    def _():
        for w_ref, wb_ref in zip(w_refs, wb_refs):
            wb_ref[...] = w_ref[...].astype(BF16)


def _dot_nt(a, wt):
    return lax.dot_general(a, wt, (((1,), (1,)), ((), ())), preferred_element_type=F32)


def _matmul_t_kernel(a_ref, wt_ref, o_ref, wb_ref):
    _cast_once([wt_ref], [wb_ref])
    o_ref[...] = _dot_nt(a_ref[...], wb_ref[...])


def matmul_t(a, wt, row0, n, tm=1024, tn=512, name="proj"):
    m, k = a.shape
    assert n % tn == 0 and row0 % HALO == 0
    if row0 % tn == 0:
        wspec = pl.BlockSpec((tn, k), lambda j, i: (row0 // tn + j, 0))
    else:
        wspec = pl.BlockSpec((pl.Element(tn), pl.Element(k)),
                             lambda j, i: (pl.multiple_of(row0 + j * tn, HALO), 0))
    return pl.pallas_call(
        _matmul_t_kernel,
        out_shape=jax.ShapeDtypeStruct((m, n), F32),
        grid=(n // tn, m // tm),
        in_specs=[pl.BlockSpec((tm, k), lambda j, i: (i, 0)), wspec],
        out_specs=pl.BlockSpec((tm, tn), lambda j, i: (i, j)),
        scratch_shapes=[pltpu.VMEM((tn, k), BF16)],
        compiler_params=_cparams(2),
        name=name,
    )(a, wt)


def _conv_proj_kernel(h_ref, hp_ref, wb32_ref, wc32_ref, wh32_ref, wg32_ref, cw_ref, side32_ref,
                      o_ref, side_ref, wb_ref, wc_ref, wh_ref, wg_ref, *, tiles_per_seq):
    _cast_once([wb32_ref, wc32_ref, wh32_ref, wg32_ref], [wb_ref, wc_ref, wh_ref, wg_ref])
    side_ref[...] = side32_ref[...].astype(BF16)
    i = pl.program_id(1)
    cw = cw_ref[...]
    hp = hp_ref[...]
    pp = _dot_nt(hp, wc_ref[...]) * _dot_nt(hp, wh_ref[...])
    pp = jnp.where(i % tiles_per_seq == 0, 0.0, pp)
    prev1, prev2 = pp[HALO - 1:HALO, :], pp[HALO - 2:HALO - 1, :]
    h = h_ref[...]
    cb = _dot_nt(h, wb_ref[...])
    cc = _dot_nt(h, wc_ref[...])
    ch = _dot_nt(h, wh_ref[...])
    cg = _dot_nt(h, wg_ref[...])
    p = cc * ch
    row = lax.broadcasted_iota(jnp.int32, (SUBLANES, p.shape[1]), 0)
    p1, p2 = pltpu.roll(p, 1, axis=0), pltpu.roll(p, 2, axis=0)
    p1 = jnp.concatenate([jnp.where(row == 0, prev1, p1[:SUBLANES]), p1[SUBLANES:]], axis=0)
    p2 = jnp.concatenate([jnp.where(row == 0, prev2, jnp.where(row == 1, prev1, p2[:SUBLANES])),
                          p2[SUBLANES:]], axis=0)
    conv = cw[0:1, :] * p2 + cw[1:2, :] * p1 + cw[2:3, :] * p
    o_ref[...] = (cb * conv * (cg * jax.nn.sigmoid(cg))).astype(o_ref.dtype)


def conv_proj(h, wt, conv_w, seq, side, tm=1024, tn=256):
    m, k = h.shape
    n = conv_w.shape[1]
    tiles_per_seq = seq // tm
    halo_per_tile = tm // HALO
    nj, ni = n // tn, m // tm
    side_rows = side.shape[0] // (nj * ni)
    assert side_rows * nj * ni == side.shape[0] and side_rows % HALO == 0
    side_spec = pl.BlockSpec((side_rows, side.shape[1]), lambda j, i: (j * ni + i, 0))

    def wspec(which):
        return pl.BlockSpec((tn, k), lambda j, i: (which * nj + j, 0))

    return pl.pallas_call(
        functools.partial(_conv_proj_kernel, tiles_per_seq=tiles_per_seq),
        out_shape=(jax.ShapeDtypeStruct((m, n), BF16), jax.ShapeDtypeStruct(side.shape, BF16)),
        grid=(nj, ni),
        in_specs=[pl.BlockSpec((tm, k), lambda j, i: (i, 0)),
                  pl.BlockSpec((HALO, k), lambda j, i: (jnp.maximum(i * halo_per_tile - 1, 0), 0)),
                  wspec(0), wspec(1), wspec(2), wspec(3),
                  pl.BlockSpec((CONV_K, tn), lambda j, i: (0, j)),
                  side_spec],
        out_specs=(pl.BlockSpec((tm, tn), lambda j, i: (i, j)), side_spec),
        scratch_shapes=[pltpu.VMEM((tn, k), BF16)] * 4,
        compiler_params=_cparams(2),
        name="conv_proj",
    )(h, h, wt, wt, wt, wt, conv_w, side)


def _rotary(x, c, sa, sb):
    half = ROT_DIM // 2
    return (x * c + pltpu.roll(x, half, axis=1) * sa
            + pltpu.roll(x, HEAD_DIM - half, axis=1) * sb)


def _compress(src_ref, pos_ref, w1_ref, b1_ref, w2_ref):
    n_rows = src_ref.shape[0] // CMP_STRIDE
    p_acc = jnp.zeros((n_rows, HEAD_DIM), F32)
    q_acc = jnp.zeros((n_rows, HEAD_DIM), F32)
    for r in range(CMP_STRIDE):
        s_r = src_ref[pl.ds(r, n_rows, stride=CMP_STRIDE), :]
        a_r = (s_r + pos_ref[r:r + 1, :]).astype(BF16)
        b_r = (s_r + pos_ref[CMP_STRIDE + r:CMP_STRIDE + r + 1, :]).astype(BF16)
        p_acc += jnp.dot(a_r, w1_ref[r], preferred_element_type=F32)
        q_acc += jnp.dot(b_r, w1_ref[CMP_STRIDE + r], preferred_element_type=F32)
    hid = p_acc + pltpu.roll(q_acc, n_rows - 1, axis=0) + b1_ref[...]
    act = (hid * jax.nn.sigmoid(hid)).astype(BF16)
    return jnp.dot(act, w2_ref[...], preferred_element_type=F32)


def _nsa_prep_kernel(kc_ref, vc_ref, ks_ref, vs_ref, kw_ref, vw_ref, c_ref, sa_ref, sb_ref,
                     kpos_ref, kw1_ref, kb1_ref, kw2_ref, vpos_ref, vw1_ref, vb1_ref, vw2_ref,
                     ksx_ref, vst_ref, kwx_ref, vwt_ref, kcmp_ref, vcmp_ref):
    c, sa, sb = c_ref[...], sa_ref[...], sb_ref[...]
    seq = ks_ref.shape[0]
    n_blk = seq // SLC_LEN
    flag_w = EXT_DIM - HEAD_DIM
    blk = lax.broadcasted_iota(jnp.int32, (seq, flag_w), 0) >> SLC_SHIFT
    onehot = jnp.where(blk == lax.broadcasted_iota(jnp.int32, (seq, flag_w), 1), 1.0, 0.0)
    ksx_ref[:, 0:HEAD_DIM] = _rotary(ks_ref[...], c, sa, sb).astype(BF16)
    ksx_ref[:, HEAD_DIM:] = onehot.astype(BF16)
    pad_flag = jnp.where(lax.broadcasted_iota(jnp.int32, (WINDOW, EXT_DIM), 1) == HEAD_DIM + n_blk, 1.0, 0.0)
    kwx_ref[0:WINDOW, :] = pad_flag.astype(BF16)
    kwx_ref[WINDOW:, 0:HEAD_DIM] = _rotary(kw_ref[...], c, sa, sb).astype(BF16)
    kwx_ref[WINDOW:, HEAD_DIM:] = jnp.zeros((seq, flag_w), BF16)
    n_tiles = seq // KEY_TILE
    pad_tiles = WINDOW // KEY_TILE
    for t in range(pad_tiles):
        vwt_ref[t] = jnp.zeros((HEAD_DIM, KEY_TILE), BF16)
    for t in range(n_tiles):
        rows = pl.ds(t * KEY_TILE, KEY_TILE)
        vst_ref[t] = vs_ref[rows, :].T.astype(BF16)
        vwt_ref[pad_tiles + t] = vw_ref[rows, :].T.astype(BF16)
    kcmp_ref[...] = _compress(kc_ref, kpos_ref, kw1_ref, kb1_ref, kw2_ref).astype(BF16)
    vcmp_ref[...] = _compress(vc_ref, vpos_ref, vw1_ref, vb1_ref, vw2_ref).astype(BF16)


def nsa_prep(z, tabs, cmp_k, cmp_v, batch, seq, col0):
    def zspec(which):
        return pl.BlockSpec((seq, HEAD_DIM), lambda b, g, w=which: (b, col0 + w * N_KV + g))

    tab = pl.BlockSpec((seq, HEAD_DIM), lambda b, g: (0, 0))

    def wspecs():
        return [pl.BlockSpec((CMP_LEN, HEAD_DIM), lambda b, g: (0, 0)),
                pl.BlockSpec((CMP_LEN, HEAD_DIM, HEAD_DIM), lambda b, g: (0, 0, 0)),
                pl.BlockSpec((1, HEAD_DIM), lambda b, g: (0, 0)),
                pl.BlockSpec((HEAD_DIM, HEAD_DIM), lambda b, g: (0, 0))]

    def out(shape):
        nd = len(shape)
        spec = pl.BlockSpec((None, None) + shape, lambda b, g: (b, g) + (0,) * nd)
        return spec, jax.ShapeDtypeStruct((batch, N_KV) + shape, BF16)

    n_tiles = seq // KEY_TILE
    pad_tiles = WINDOW // KEY_TILE
    outs = [out((seq, EXT_DIM)), out((n_tiles, HEAD_DIM, KEY_TILE)),
            out((WINDOW + seq, EXT_DIM)), out((pad_tiles + n_tiles, HEAD_DIM, KEY_TILE)),
            out((seq // CMP_STRIDE, HEAD_DIM)), out((seq // CMP_STRIDE, HEAD_DIM))]
    return pl.pallas_call(
        _nsa_prep_kernel,
        out_shape=tuple(o[1] for o in outs),
        grid=(batch, N_KV),
        in_specs=[zspec(0), zspec(1), zspec(2), zspec(3), zspec(4), zspec(5), tab, tab, tab]
        + wspecs() + wspecs(),
        out_specs=tuple(o[0] for o in outs),
        compiler_params=_cparams(2),
        name="nsa_prep",
    )(z, z, z, z, z, z, *tabs, *cmp_k, *cmp_v)


def _nsa_attn_pair_kernel(q_ref, ng_ref, gl_ref, rot_ref, ov_ref,
                          ksx_ref, vst_ref, kwx_ref, vwt_ref, kcmp_ref, vcmp_ref,
                          o_ref, acc_ref, qx_ref, glt_ref, s_ref, m_ref, l_ref, oc_ref, *, tq, n_grp_step):
    gp = pl.program_id(1)
    i = pl.program_id(2)
    q0 = i * tq
    n_blk = ov_ref.shape[0]
    half = ROT_DIM // 2
    qscale = (HEAD_DIM ** -0.5) * math.log2(math.e)
    qw = GQA * HEAD_DIM
    groups = range(n_grp_step)

    def head_lanes(n):
        return slice(n * tq, (n + 1) * tq)

    def lanes4(x):
        return jnp.concatenate([x] * GQA, axis=1)

    rot = rot_ref[...]
    cos2, sin2 = rot[:ROT_DIM], rot[ROT_DIM:]
    tlane = q0 + lax.broadcasted_iota(jnp.int32, (1, tq), 1)
    kk = lax.broadcasted_iota(jnp.int32, (tq, tq), 0)
    tt = lax.broadcasted_iota(jnp.int32, (tq, tq), 1)
    tri_diag = jnp.where(kk <= tt, 0.0, NEG)
    tri_old = jnp.where(kk > tt, 0.0, NEG)
    n_cmp = kcmp_ref.shape[1]
    cend = lax.broadcasted_iota(jnp.int32, (n_cmp, tq), 0) * CMP_STRIDE + (CMP_LEN - 1)
    bias_c = lanes4(jnp.where(cend <= tlane, 0.0, NEG))
    any_c = lanes4(tlane) >= CMP_LEN - 1
    jblk = lax.broadcasted_iota(jnp.int32, (n_blk, tq), 0)
    forced = (jblk == 0) | (jblk == (tlane >> SLC_SHIFT))
    causal_blk = jblk * SLC_LEN <= tlane
    sub = lax.broadcasted_iota(jnp.int32, (SUBLANES, tq), 0)
    n_rest = EXT_DIM - HEAD_DIM - n_blk
    pad_rows = jnp.where(lax.broadcasted_iota(jnp.int32, (n_rest, GQA * tq), 0) == 0, NEG, 0.0).astype(BF16)
    tiles = tq // KEY_TILE

    def sel_scores(gg, ci):
        k0 = pl.multiple_of(ci * tq, tq)
        return jnp.dot(ksx_ref[gg, pl.ds(k0, tq), :], qx_ref[gg], preferred_element_type=F32)

    def sel_update(gg, ci, s):
        vblk = jnp.concatenate([vst_ref[gg, ci * tiles + r] for r in range(tiles)], axis=1)
        m_old = m_ref[gg]
        m_new = jnp.maximum(m_old, jnp.max(s, axis=0, keepdims=True))
        alpha = jnp.exp2(m_old - m_new)
        p = jnp.exp2(s - m_new)
        l_ref[gg] = alpha * l_ref[gg] + jnp.sum(p, axis=0, keepdims=True)
        acc_ref[gg] = alpha * acc_ref[gg] + jnp.dot(vblk, p.astype(BF16), preferred_element_type=F32)
        m_ref[gg] = m_new

    for gg in groups:
        q = q_ref[:, gg * qw:(gg + 1) * qw]
        qts = []
        for n in range(GQA):
            qt = (q[:, n * HEAD_DIM:(n + 1) * HEAD_DIM] * qscale).T
            qts.append(qt)
            top = qt[:ROT_DIM]
            swapped = jnp.concatenate([top[half:], top[:half]], axis=0)
            qrt = jnp.concatenate([top * cos2 + swapped * sin2, qt[ROT_DIM:]], axis=0)
            qx_ref[gg, 0:HEAD_DIM, head_lanes(n)] = qrt.astype(BF16)
        qt_all = jnp.concatenate(qts, axis=1).astype(BF16)

        s_c = jnp.dot(kcmp_ref[gg], qt_all, preferred_element_type=F32) + bias_c
        m_c = jnp.where(any_c, jnp.max(s_c, axis=0, keepdims=True), 0.0)
        e_c = jnp.exp2(s_c - m_c)
        den_c = jnp.sum(e_c, axis=0, keepdims=True)
        p_c = e_c / jnp.where(den_c > 0, den_c, 1.0)
        vcmp_t = vcmp_ref[gg].astype(F32).T.astype(BF16)
        oc_ref[gg] = jnp.dot(vcmp_t, p_c.astype(BF16), preferred_element_type=F32)

        p_sum = p_c[:, head_lanes(0)]
        for n in range(1, GQA):
            p_sum = p_sum + p_c[:, head_lanes(n)]
        imp = jnp.dot(ov_ref[...], p_sum, preferred_element_type=F32,
                      precision=lax.Precision.HIGHEST)
        imp = jnp.where(forced, jnp.inf, jnp.where(causal_blk, imp, -jnp.inf))
        n_sub = n_blk // SUBLANES
        part = [imp[a * SUBLANES:(a + 1) * SUBLANES] for a in range(n_sub)]
        cnt = [jnp.zeros((SUBLANES, tq), F32) for _ in range(n_sub)]
        for r in range(n_blk):
            row = imp[r:r + 1, :]
            a_r, r_in = divmod(r, SUBLANES)
            for a in range(n_sub):
                if a > a_r:
                    cnt[a] = cnt[a] + jnp.where(row >= part[a], 1.0, 0.0)
                elif a < a_r:
                    cnt[a] = cnt[a] + jnp.where(row > part[a], 1.0, 0.0)
                else:
                    cnt[a] = cnt[a] + jnp.where(sub > r_in, jnp.where(row >= part[a], 1.0, 0.0),
                                                jnp.where(row > part[a], 1.0, 0.0))
        cnt = jnp.concatenate(cnt, axis=0)
        sel_bias = jnp.where(cnt < N_SEL, jnp.where(imp > -jnp.inf, 0.0, NEG), NEG)
        qx_ref[gg, HEAD_DIM:HEAD_DIM + n_blk, :] = lanes4(sel_bias).astype(BF16)
        qx_ref[gg, HEAD_DIM + n_blk:, :] = pad_rows

        m_ref[gg] = jnp.full(m_ref.shape[1:], NEG, F32)
        l_ref[gg] = jnp.zeros(l_ref.shape[1:], F32)
        acc_ref[gg] = jnp.zeros(acc_ref.shape[1:], F32)
        s_ref[gg, 0] = sel_scores(gg, 0)

    def sel_pair(pi, carry):
        c0 = 2 * pi
        for gg in groups:
            s_ref[gg, 1] = sel_scores(gg, c0 + 1)
        for gg in groups:
            sel_update(gg, c0, s_ref[gg, 0])
        for gg in groups:
            s_ref[gg, 0] = sel_scores(gg, c0 + 2)
        for gg in groups:
            sel_update(gg, c0 + 1, s_ref[gg, 1])
        return carry

    lax.fori_loop(0, i // 2, sel_pair, 0)

    @pl.when(i % 2 == 1)
    def _():
        for gg in groups:
            s_ref[gg, 1] = sel_scores(gg, i)
        for gg in groups:
            sel_update(gg, i - 1, s_ref[gg, 0])

    for gg in groups:
        sel_update(gg, i, s_ref[gg, i % 2] + lanes4(tri_diag))

    n_wt = WINDOW // tq
    vt0 = q0 // KEY_TILE
    glt_ref[...] = jax.nn.sigmoid(gl_ref[...]).T
    for gg in groups:
        kwin = kwx_ref[gg, pl.ds(pl.multiple_of(q0, tq), WINDOW + tq), :]
        vwin = jnp.concatenate([vwt_ref[gg, vt0 + r] for r in range((WINDOW + tq) // KEY_TILE)], axis=1)
        s_w = jnp.dot(kwin, qx_ref[gg], preferred_element_type=F32)
        s_w = jnp.concatenate([s_w[:tq] + lanes4(tri_old), s_w[tq:n_wt * tq],
                               s_w[n_wt * tq:] + lanes4(tri_diag)], axis=0)
        e_w = jnp.exp2(s_w - jnp.max(s_w, axis=0, keepdims=True))
        o_w = (jnp.dot(vwin, e_w.astype(BF16), preferred_element_type=F32)
               / jnp.sum(e_w, axis=0, keepdims=True))
        o_s = acc_ref[gg] / l_ref[gg]
        o_c = oc_ref[gg]
        ng = ng_ref[:, gg * qw:(gg + 1) * qw]
        for n in range(GQA):
            head = (gp * n_grp_step + gg) * GQA + n
            gc, gs, gw = (glt_ref[pl.ds(j * N_HEADS + head, 1), :] for j in range(3))
            sl = head_lanes(n)
            o = (gc * o_c[:, sl] + gs * o_s[:, sl] + gw * o_w[:, sl]).T
            ngh = ng[:, n * HEAD_DIM:(n + 1) * HEAD_DIM]
            col = gg * qw + n * HEAD_DIM
            o_ref[:, col:col + HEAD_DIM] = (o * (ngh * jax.nn.sigmoid(ngh))).astype(o_ref.dtype)


def nsa_attention_pair(z, z_ng, gl, rot_t, overlap_t, prep, batch, seq, tq=256, n_grp_step=2):
    ksx, vst, kwx, vwt, kcmp, vcmp = prep
    assert tq % KEY_TILE == 0 and WINDOW % tq == 0 and WINDOW >= 2 * tq and N_KV % n_grp_step == 0
    nq = seq // tq
    qw = GQA * HEAD_DIM * n_grp_step

    def whole(a):
        nd = a.ndim - 2
        return pl.BlockSpec((None, n_grp_step) + a.shape[2:], lambda b, g, i: (b, g) + (0,) * nd)

    lanes = GQA * tq
    return pl.pallas_call(
        functools.partial(_nsa_attn_pair_kernel, tq=tq, n_grp_step=n_grp_step),
        out_shape=jax.ShapeDtypeStruct((batch * seq, N_HEADS * HEAD_DIM), BF16),
        grid=(batch, N_KV // n_grp_step, nq),
        in_specs=[pl.BlockSpec((tq, qw), lambda b, g, i: (b * nq + i, g)),
                  pl.BlockSpec((tq, qw), lambda b, g, i: (b * nq + i, g)),
                  pl.BlockSpec((tq, gl.shape[1]), lambda b, g, i: (b * nq + i, 0)),
                  pl.BlockSpec((rot_t.shape[0], tq), lambda b, g, i: (0, i)),
                  pl.BlockSpec(overlap_t.shape, lambda b, g, i: (0, 0)),
                  whole(ksx), whole(vst), whole(kwx), whole(vwt), whole(kcmp), whole(vcmp)],
        out_specs=pl.BlockSpec((tq, qw), lambda b, g, i: (b * nq + i, g)),
        scratch_shapes=[pltpu.VMEM((n_grp_step, HEAD_DIM, lanes), F32),
                        pltpu.VMEM((n_grp_step, EXT_DIM, lanes), BF16),
                        pltpu.VMEM((gl.shape[1], tq), F32),
                        pltpu.VMEM((n_grp_step, 2, tq, lanes), F32),
                        pltpu.VMEM((n_grp_step, 1, lanes), F32),
                        pltpu.VMEM((n_grp_step, 1, lanes), F32),
                        pltpu.VMEM((n_grp_step, HEAD_DIM, lanes), F32)],
        compiler_params=_cparams(3),
        name="nsa_attention",
    )(z, z_ng, gl, rot_t, overlap_t, ksx, vst, kwx, vwt, kcmp, vcmp)


def _out_proj_norm_kernel(a0_ref, a1_ref, w_ref, x_ref, g_ref, *o_refs, kh, emit_residual):
    acc = jnp.dot(a0_ref[...], w_ref[0:kh, :], preferred_element_type=F32)
    acc += jnp.dot(a1_ref[...], w_ref[kh:, :], preferred_element_type=F32)
    x = x_ref[...] + acc
    y = x * lax.rsqrt(jnp.mean(x * x, axis=-1, keepdims=True) + EPS)
    o_refs[-1][...] = (y * g_ref[...]).astype(o_refs[-1].dtype)
    if emit_residual:
        o_refs[0][...] = x


def out_proj_norm(a0, a0_blk, a1, a1_blk, w, x2d, g, norm_dtype, emit_residual, tm=512, name="out_proj"):
    m, n = x2d.shape
    kh = w.shape[0] // 2
    row = pl.BlockSpec((tm, n), lambda i: (i, 0))
    out_shape = [jax.ShapeDtypeStruct((m, n), norm_dtype)]
    if emit_residual:
        out_shape.insert(0, jax.ShapeDtypeStruct((m, n), F32))
    return pl.pallas_call(
        functools.partial(_out_proj_norm_kernel, kh=kh, emit_residual=emit_residual),
        out_shape=tuple(out_shape),
        grid=(m // tm,),
        in_specs=[pl.BlockSpec((tm, kh), lambda i: (i, a0_blk)),
                  pl.BlockSpec((tm, kh), lambda i: (i, a1_blk)),
                  pl.BlockSpec(w.shape, lambda i: (0, 0), pipeline_mode=pl.Buffered(1)),
                  row,
                  pl.BlockSpec((1, n), lambda i: (0, 0))],
        out_specs=tuple(row for _ in out_shape),
        compiler_params=_cparams(1),
        name=name,
    )(a0, a1, w, x2d, g.reshape(1, n))


def _sgu_v_kernel(h_ref, w_ref, v_ref, mu_ref, rstd_ref, wb_ref, c_ref, s1_ref, s2_ref, *, n_col_tiles):
    j = pl.program_id(1)

    @pl.when(pl.program_id(0) == 0)
    def _():
        wb_ref[j] = w_ref[...].astype(BF16)

    tm, tn = v_ref.shape
    h = h_ref[...]
    for c0 in range(0, tn, V_SUB):
        v = jnp.dot(h, wb_ref[j, :, c0:c0 + V_SUB], preferred_element_type=F32)
        v_ref[:, c0:c0 + V_SUB] = v

        if c0 == 0:
            @pl.when(j == 0)
            def _():
                c_ref[...] = jnp.broadcast_to(jnp.sum(v, axis=-1, keepdims=True) / V_SUB, c_ref.shape)
                s1_ref[...] = jnp.zeros(s1_ref.shape, F32)
                s2_ref[...] = jnp.zeros(s2_ref.shape, F32)

        c = c_ref[...]
        s1, s2 = s1_ref[...], s2_ref[...]
        for r in range(V_SUB // LANES):
            d = v[:, r * LANES:(r + 1) * LANES] - c
            s1 = s1 + d
            s2 = s2 + d * d
        s1_ref[...] = s1
        s2_ref[...] = s2

    @pl.when(j == n_col_tiles - 1)
    def _():
        width = n_col_tiles * tn
        mean_d = jnp.sum(s1, axis=-1, keepdims=True) / width
        var = jnp.sum(s2, axis=-1, keepdims=True) / width - mean_d * mean_d
        mu_ref[...] = c + mean_d
        rstd_ref[...] = jnp.broadcast_to(lax.rsqrt(var + EPS), rstd_ref.shape)


def sgu_v(h, w, col0, tm=512, tn=2 * V_SUB):
    m, k = h.shape
    nj = SGU_W // tn
    j0 = col0 // tn
    stat = jax.ShapeDtypeStruct((m, LANES), F32)
    stat_spec = pl.BlockSpec((tm, LANES), lambda i, j: (i, 0))
    return pl.pallas_call(
        functools.partial(_sgu_v_kernel, n_col_tiles=nj),
        out_shape=(jax.ShapeDtypeStruct((m, SGU_W), F32), stat, stat),
        grid=(m // tm, nj),
        in_specs=[pl.BlockSpec((tm, k), lambda i, j: (i, 0)),
                  pl.BlockSpec((k, tn), lambda i, j: (0, j0 + jnp.where(i == 0, j, 0)))],
        out_specs=(pl.BlockSpec((tm, tn), lambda i, j: (i, j)), stat_spec, stat_spec),
        scratch_shapes=[pltpu.VMEM((nj, k, tn), BF16)] + [pltpu.VMEM((tm, LANES), F32)] * 3,
        compiler_params=_cparams(2),
        name="sgu_v",
    )(h, w)


def _sgu_gate_kernel(h_ref, wu32_ref, wz32_ref, v_ref, mu_ref, rstd_ref, lg_ref, lb_ref, ws_ref, bs_ref,
                     side32_ref, o_ref, side_ref, wu_ref, wz_ref):
    _cast_once([wu32_ref, wz32_ref], [wu_ref, wz_ref])
    side_ref[...] = side32_ref[...].astype(BF16)
    h = h_ref[...]
    tm, tn = v_ref.shape
    reps = GROUP_W // LANES
    mu = jnp.concatenate([mu_ref[...]] * reps, axis=1)
    rstd = jnp.concatenate([rstd_ref[...]] * reps, axis=1)
    tri = (lax.broadcasted_iota(jnp.int32, (CHUNK, CHUNK), 1)
           <= lax.broadcasted_iota(jnp.int32, (CHUNK, CHUNK), 0))
    for gi in range(tn // GROUP_W):
        cols = slice(gi * GROUP_W, (gi + 1) * GROUP_W)
        u = jnp.dot(h, wu_ref[:, cols], preferred_element_type=F32)
        zg = jnp.dot(h, wz_ref[:, cols], preferred_element_type=F32)
        vn = ((v_ref[:, cols] - mu) * rstd * lg_ref[:, cols] + lb_ref[:, cols]).astype(BF16)
        wsm = jnp.where(tri, ws_ref[gi], 0.0).astype(BF16)
        bsg = bs_ref[gi]
        mix = jnp.concatenate(
            [jnp.dot(wsm, vn[c * CHUNK:(c + 1) * CHUNK], preferred_element_type=F32) + bsg
             for c in range(tm // CHUNK)], axis=0)
        o_ref[:, cols] = (u * mix * (zg * jax.nn.sigmoid(zg))).astype(o_ref.dtype)


def sgu_gate(h, w, col_u, col_z, v, mu, rstd, ln_g, ln_b, w_s, b_s, side, tm=1024, tn=512):
    m, k = h.shape
    n = SGU_W
    nj, ni = n // tn, m // tm
    ju, jz = col_u // tn, col_z // tn
    gpt = tn // GROUP_W
    stat_spec = pl.BlockSpec((tm, LANES), lambda j, i: (i, 0))
    side_rows = side.shape[0] // (nj * ni)
    assert side_rows * nj * ni == side.shape[0] and side_rows % HALO == 0
    side_spec = pl.BlockSpec((side_rows, side.shape[1]), lambda j, i: (j * ni + i, 0))
    return pl.pallas_call(
        _sgu_gate_kernel,
        out_shape=(jax.ShapeDtypeStruct((m, n), BF16), jax.ShapeDtypeStruct(side.shape, BF16)),
        grid=(nj, ni),
        in_specs=[pl.BlockSpec((tm, k), lambda j, i: (i, 0)),
                  pl.BlockSpec((k, tn), lambda j, i: (0, ju + j)),
                  pl.BlockSpec((k, tn), lambda j, i: (0, jz + j)),
                  pl.BlockSpec((tm, tn), lambda j, i: (i, j)),
                  stat_spec, stat_spec,
                  pl.BlockSpec((1, tn), lambda j, i: (0, j)),
                  pl.BlockSpec((1, tn), lambda j, i: (0, j)),
                  pl.BlockSpec((gpt, CHUNK, CHUNK), lambda j, i: (j, 0, 0)),
                  pl.BlockSpec((gpt, CHUNK, 1), lambda j, i: (j, 0, 0)),
                  side_spec],
        out_specs=(pl.BlockSpec((tm, tn), lambda j, i: (i, j)), side_spec),
        scratch_shapes=[pltpu.VMEM((k, tn), BF16)] * 2,
        compiler_params=_cparams(2),
        name="sgu_gate",
    )(h, w, w, v, mu, rstd, ln_g.reshape(1, n), ln_b.reshape(1, n), w_s, b_s.reshape(N_GROUPS, CHUNK, 1),
      side)


def _rotary_tables(seq):
    half = ROT_DIM // 2
    inv_freq = jnp.power(ROPE_THETA, -jnp.arange(half, dtype=F32) * 2.0 / ROT_DIM)
    ang = jnp.arange(seq).astype(F32)[:, None] * inv_freq[None, :]
    cos, sin = jnp.cos(ang), jnp.sin(ang)
    rest = HEAD_DIM - ROT_DIM
    c = jnp.concatenate([cos, cos, jnp.ones((seq, rest), F32)], axis=1)
    sa = jnp.concatenate([jnp.zeros((seq, half), F32), sin, jnp.zeros((seq, rest), F32)], axis=1)
    sb = jnp.concatenate([-sin, jnp.zeros((seq, half + rest), F32)], axis=1)
    rot_t = jnp.concatenate([cos, cos, -sin, sin], axis=1).T
    return (c, sa, sb), rot_t


def _overlap_t(seq):
    n_blk = seq // SLC_LEN
    n_rows = seq // CMP_STRIDE
    n_cmp = (seq - CMP_LEN) // CMP_STRIDE + 1
    cs = jnp.arange(n_rows) * CMP_STRIDE
    bs = jnp.arange(n_blk) * SLC_LEN
    ov = (cs[None, :] < bs[:, None] + SLC_LEN) & (cs[None, :] + CMP_LEN > bs[:, None])
    ov = ov & (jnp.arange(n_rows)[None, :] < n_cmp)
    return ov.astype(F32)


def kernel(x, norm_even, w_in_even, conv_w, cmp_k_pos, cmp_k_w1, cmp_k_b1, cmp_k_w2, cmp_v_pos, cmp_v_w1, cmp_v_b1, cmp_v_w2, w_out_even, norm_odd, w_in_odd, sgu_ln_g, sgu_ln_b, sgu_w_s, sgu_b_s, w_out_odd, norm_final):
    batch, seq, d = x.shape
    m = batch * seq
    x2d = x.reshape(m, d)

    wt_in = jnp.swapaxes(w_in_even[0], 0, 1)
    cw = CONV_W
    qw = N_HEADS * HEAD_DIM
    kvw = N_KV * HEAD_DIM
    o_q = 4 * cw
    o_kv = o_q + qw
    o_gl = o_kv + 6 * kvw
    o_ng = o_gl + 3 * N_HEADS

    h0, gl = rmsnorm_proj(x2d, norm_even[0], wt_in, o_gl, LANES)
    y_conv, w_out0 = conv_proj(h0, wt_in, conv_w[0], seq, w_out_even[0])
    z = matmul_t(h0, wt_in, o_q, qw + 6 * kvw, tn=1024, name="nsa_proj")
    z_ng = matmul_t(h0, wt_in, o_ng, qw, tn=1024, name="nsa_gate_proj")

    tabs, rot_t = _rotary_tables(seq)
    cmp_k = (cmp_k_pos[0], cmp_k_w1[0].astype(BF16).reshape(CMP_LEN, HEAD_DIM, HEAD_DIM),
             cmp_k_b1[0].reshape(1, HEAD_DIM), cmp_k_w2[0].astype(BF16))
    cmp_v = (cmp_v_pos[0], cmp_v_w1[0].astype(BF16).reshape(CMP_LEN, HEAD_DIM, HEAD_DIM),
             cmp_v_b1[0].reshape(1, HEAD_DIM), cmp_v_w2[0].astype(BF16))
    prep = nsa_prep(z, tabs, cmp_k, cmp_v, batch, seq, col0=qw // HEAD_DIM)
    y_nsa = nsa_attention_pair(z, z_ng, gl, rot_t, _overlap_t(seq), prep, batch, seq)

    x1, h1 = out_proj_norm(y_conv, 0, y_nsa, 0, w_out0, x2d, norm_odd[0],
                           BF16, True, name="out_proj_even")

    w_in1 = w_in_odd[0]
    v, mu, rstd = sgu_v(h1, w_in1, SGU_W)
    act, w_out1 = sgu_gate(h1, w_in1, 0, 2 * SGU_W, v, mu, rstd,
                           sgu_ln_g[0], sgu_ln_b[0], sgu_w_s[0], sgu_b_s[0], w_out_odd[0])
    (out,) = out_proj_norm(act, 0, act, 1, w_out1, x1, norm_final,
                           F32, False, name="out_proj_odd")
    return out.reshape(batch, seq, d)
```

```python
import functools
import math

import jax
import jax.numpy as jnp
from jax import lax
from jax.experimental import pallas as pl
from jax.experimental.pallas import tpu as pltpu

F32 = jnp.float32
BF16 = jnp.bfloat16

D_MODEL = 2048
MIX = 2 * D_MODEL
CONV_W = MIX // 2
CONV_K = 3
HEAD_DIM = 128
N_HEADS = 16
N_KV = 4
GQA = N_HEADS // N_KV
ROT_DIM = HEAD_DIM // 4
ROPE_THETA = 500000.0
CMP_LEN = 32
CMP_STRIDE = 16
SLC_LEN = 64
N_SEL = 8
WINDOW = 512
SGU_W = MIX
CHUNK = 128
N_GROUPS = 16
GROUP_W = SGU_W // N_GROUPS
EPS = 1e-6

LANES = 128
SUBLANES = 8
SLC_SHIFT = 6
HALO = 16
KEY_TILE = LANES
V_SUB = 512
XPOSE_ROWS = 256
VMEM_LIMIT = 56 * 1024 * 1024

NEG = -1e30

EXT_DIM = 2 * HEAD_DIM


def _cparams(n_axes):
    return pltpu.CompilerParams(
        dimension_semantics=("arbitrary",) * n_axes, vmem_limit_bytes=VMEM_LIMIT)


def _rmsnorm_proj_kernel(x_ref, g_ref, wt_ref, o_ref, p_ref, wb_ref):
    @pl.when(pl.program_id(0) == 0)
    def _():
        wb_ref[...] = wt_ref[...].T.astype(BF16)

    x = x_ref[...]
    y = x * lax.rsqrt(jnp.mean(x * x, axis=-1, keepdims=True) + EPS)
    hb = (y * g_ref[...]).astype(BF16)
    o_ref[...] = hb
    p_ref[...] = jnp.dot(hb, wb_ref[...], preferred_element_type=F32)


def rmsnorm_proj(x2d, g, wt, row0, n, tm=512):
    m, d = x2d.shape
    assert row0 % n == 0
    return pl.pallas_call(
        _rmsnorm_proj_kernel,
        out_shape=(jax.ShapeDtypeStruct((m, d), BF16), jax.ShapeDtypeStruct((m, n), F32)),
        grid=(m // tm,),
        in_specs=[pl.BlockSpec((tm, d), lambda i: (i, 0)),
                  pl.BlockSpec((1, d), lambda i: (0, 0)),
                  pl.BlockSpec((n, d), lambda i: (row0 // n, 0))],
        out_specs=(pl.BlockSpec((tm, d), lambda i: (i, 0)),
                   pl.BlockSpec((tm, n), lambda i: (i, 0))),
        scratch_shapes=[pltpu.VMEM((d, n), BF16)],
        compiler_params=_cparams(1),
        name="rmsnorm_gates",
    )(x2d, g.reshape(1, d), wt)


def _cast_once(w_refs, wb_refs, transposed=False):
    @pl.when(pl.program_id(1) == 0)
    def _():
        for w_ref, wb_ref in zip(w_refs, wb_refs):
            if transposed:
                for r in range(0, w_ref.shape[0], XPOSE_ROWS):
                    wb_ref[:, r:r + XPOSE_ROWS] = w_ref[r:r + XPOSE_ROWS, :].T.astype(BF16)
            else:
                wb_ref[...] = w_ref[...].astype(BF16)


def _matmul_t_kernel(a_ref, wt_ref, o_ref, wb_ref):
    _cast_once([wt_ref], [wb_ref], transposed=True)
    o_ref[...] = jnp.dot(a_ref[...], wb_ref[...], preferred_element_type=F32)


def matmul_t(a, wt, row0, n, tm=1024, tn=512, name="proj"):
    m, k = a.shape
    assert n % tn == 0 and row0 % HALO == 0
    if row0 % tn == 0:
        wspec = pl.BlockSpec((tn, k), lambda j, i: (row0 // tn + j, 0))
    else:
        wspec = pl.BlockSpec((pl.Element(tn), pl.Element(k)),
                             lambda j, i: (pl.multiple_of(row0 + j * tn, HALO), 0))
    return pl.pallas_call(
        _matmul_t_kernel,
        out_shape=jax.ShapeDtypeStruct((m, n), F32),
        grid=(n // tn, m // tm),
        in_specs=[pl.BlockSpec((tm, k), lambda j, i: (i, 0)), wspec],
        out_specs=pl.BlockSpec((tm, tn), lambda j, i: (i, j)),
        scratch_shapes=[pltpu.VMEM((k, tn), BF16)],
        compiler_params=_cparams(2),
        name=name,
    )(a, wt)


def _conv_proj_kernel(h_ref, hp_ref, wb32_ref, wc32_ref, wh32_ref, wg32_ref, cw_ref, side32_ref,
                      o_ref, side_ref, wb_ref, wc_ref, wh_ref, wg_ref, *, tiles_per_seq):
    _cast_once([wb32_ref, wc32_ref, wh32_ref, wg32_ref], [wb_ref, wc_ref, wh_ref, wg_ref],
               transposed=True)
    side_ref[...] = side32_ref[...].astype(BF16)
    i = pl.program_id(1)
    cw = cw_ref[...]
    hp = hp_ref[...]
    pp = (jnp.dot(hp, wc_ref[...], preferred_element_type=F32)
          * jnp.dot(hp, wh_ref[...], preferred_element_type=F32))
    pp = jnp.where(i % tiles_per_seq == 0, 0.0, pp)
    prev1, prev2 = pp[HALO - 1:HALO, :], pp[HALO - 2:HALO - 1, :]
    h = h_ref[...]
    cb = jnp.dot(h, wb_ref[...], preferred_element_type=F32)
    cc = jnp.dot(h, wc_ref[...], preferred_element_type=F32)
    ch = jnp.dot(h, wh_ref[...], preferred_element_type=F32)
    cg = jnp.dot(h, wg_ref[...], preferred_element_type=F32)
    p = cc * ch
    row = lax.broadcasted_iota(jnp.int32, (SUBLANES, p.shape[1]), 0)
    p1, p2 = pltpu.roll(p, 1, axis=0), pltpu.roll(p, 2, axis=0)
    p1 = jnp.concatenate([jnp.where(row == 0, prev1, p1[:SUBLANES]), p1[SUBLANES:]], axis=0)
    p2 = jnp.concatenate([jnp.where(row == 0, prev2, jnp.where(row == 1, prev1, p2[:SUBLANES])),
                          p2[SUBLANES:]], axis=0)
    conv = cw[0:1, :] * p2 + cw[1:2, :] * p1 + cw[2:3, :] * p
    o_ref[...] = (cb * conv * (cg * jax.nn.sigmoid(cg))).astype(o_ref.dtype)


def conv_proj(h, wt, conv_w, seq, side, tm=1024, tn=256):
    m, k = h.shape
    n = conv_w.shape[1]
    tiles_per_seq = seq // tm
    halo_per_tile = tm // HALO
    nj, ni = n // tn, m // tm
    side_rows = side.shape[0] // (nj * ni)
    assert side_rows * nj * ni == side.shape[0] and side_rows % HALO == 0
    side_spec = pl.BlockSpec((side_rows, side.shape[1]), lambda j, i: (j * ni + i, 0))

    def wspec(which):
        return pl.BlockSpec((tn, k), lambda j, i: (which * nj + j, 0))

    return pl.pallas_call(
        functools.partial(_conv_proj_kernel, tiles_per_seq=tiles_per_seq),
        out_shape=(jax.ShapeDtypeStruct((m, n), BF16), jax.ShapeDtypeStruct(side.shape, BF16)),
        grid=(nj, ni),
        in_specs=[pl.BlockSpec((tm, k), lambda j, i: (i, 0)),
                  pl.BlockSpec((HALO, k), lambda j, i: (jnp.maximum(i * halo_per_tile - 1, 0), 0)),
                  wspec(0), wspec(1), wspec(2), wspec(3),
                  pl.BlockSpec((CONV_K, tn), lambda j, i: (0, j)),
                  side_spec],
        out_specs=(pl.BlockSpec((tm, tn), lambda j, i: (i, j)), side_spec),
        scratch_shapes=[pltpu.VMEM((k, tn), BF16)] * 4,
        compiler_params=_cparams(2),
        name="conv_proj",
    )(h, h, wt, wt, wt, wt, conv_w, side)


def _rotary(x, c, sa, sb):
    half = ROT_DIM // 2
    return (x * c + pltpu.roll(x, half, axis=1) * sa
            + pltpu.roll(x, HEAD_DIM - half, axis=1) * sb)


def _compress(src_ref, pos_ref, w1_ref, b1_ref, w2_ref):
    n_rows = src_ref.shape[0] // CMP_STRIDE
    p_acc = jnp.zeros((n_rows, HEAD_DIM), F32)
    q_acc = jnp.zeros((n_rows, HEAD_DIM), F32)
    for r in range(CMP_STRIDE):
        s_r = src_ref[pl.ds(r, n_rows, stride=CMP_STRIDE), :]
        a_r = (s_r + pos_ref[r:r + 1, :]).astype(BF16)
        b_r = (s_r + pos_ref[CMP_STRIDE + r:CMP_STRIDE + r + 1, :]).astype(BF16)
        p_acc += jnp.dot(a_r, w1_ref[r], preferred_element_type=F32)
        q_acc += jnp.dot(b_r, w1_ref[CMP_STRIDE + r], preferred_element_type=F32)
    hid = p_acc + pltpu.roll(q_acc, n_rows - 1, axis=0) + b1_ref[...]
    act = (hid * jax.nn.sigmoid(hid)).astype(BF16)
    return jnp.dot(act, w2_ref[...], preferred_element_type=F32)


def _nsa_prep_kernel(kc_ref, vc_ref, ks_ref, vs_ref, kw_ref, vw_ref, c_ref, sa_ref, sb_ref,
                     kpos_ref, kw1_ref, kb1_ref, kw2_ref, vpos_ref, vw1_ref, vb1_ref, vw2_ref,
                     ksx_ref, vst_ref, kwx_ref, vwt_ref, kcmp_ref, vcmp_ref):
    c, sa, sb = c_ref[...], sa_ref[...], sb_ref[...]
    seq = ks_ref.shape[0]
    n_blk = seq // SLC_LEN
    flag_w = EXT_DIM - HEAD_DIM
    blk = lax.broadcasted_iota(jnp.int32, (seq, flag_w), 0) >> SLC_SHIFT
    onehot = jnp.where(blk == lax.broadcasted_iota(jnp.int32, (seq, flag_w), 1), 1.0, 0.0)
    ksx_ref[:, 0:HEAD_DIM] = _rotary(ks_ref[...], c, sa, sb).astype(BF16)
    ksx_ref[:, HEAD_DIM:] = onehot.astype(BF16)
    pad_flag = jnp.where(lax.broadcasted_iota(jnp.int32, (WINDOW, EXT_DIM), 1) == HEAD_DIM + n_blk, 1.0, 0.0)
    kwx_ref[0:WINDOW, :] = pad_flag.astype(BF16)
    kwx_ref[WINDOW:, 0:HEAD_DIM] = _rotary(kw_ref[...], c, sa, sb).astype(BF16)
    kwx_ref[WINDOW:, HEAD_DIM:] = jnp.zeros((seq, flag_w), BF16)
    n_tiles = seq // KEY_TILE
    pad_tiles = WINDOW // KEY_TILE
    for t in range(pad_tiles):
        vwt_ref[t] = jnp.zeros((HEAD_DIM, KEY_TILE), BF16)
    for t in range(n_tiles):
        rows = pl.ds(t * KEY_TILE, KEY_TILE)
        vst_ref[t] = vs_ref[rows, :].T.astype(BF16)
        vwt_ref[pad_tiles + t] = vw_ref[rows, :].T.astype(BF16)
    kcmp_ref[...] = _compress(kc_ref, kpos_ref, kw1_ref, kb1_ref, kw2_ref).astype(BF16)
    vcmp_ref[...] = _compress(vc_ref, vpos_ref, vw1_ref, vb1_ref, vw2_ref).astype(BF16)


def nsa_prep(z, tabs, cmp_k, cmp_v, batch, seq, col0):
    def zspec(which):
        return pl.BlockSpec((seq, HEAD_DIM), lambda b, g, w=which: (b, col0 + w * N_KV + g))

    tab = pl.BlockSpec((seq, HEAD_DIM), lambda b, g: (0, 0))

    def wspecs():
        return [pl.BlockSpec((CMP_LEN, HEAD_DIM), lambda b, g: (0, 0)),
                pl.BlockSpec((CMP_LEN, HEAD_DIM, HEAD_DIM), lambda b, g: (0, 0, 0)),
                pl.BlockSpec((1, HEAD_DIM), lambda b, g: (0, 0)),
                pl.BlockSpec((HEAD_DIM, HEAD_DIM), lambda b, g: (0, 0))]

    def out(shape):
        nd = len(shape)
        spec = pl.BlockSpec((None, None) + shape, lambda b, g: (b, g) + (0,) * nd)
        return spec, jax.ShapeDtypeStruct((batch, N_KV) + shape, BF16)

    n_tiles = seq // KEY_TILE
    pad_tiles = WINDOW // KEY_TILE
    outs = [out((seq, EXT_DIM)), out((n_tiles, HEAD_DIM, KEY_TILE)),
            out((WINDOW + seq, EXT_DIM)), out((pad_tiles + n_tiles, HEAD_DIM, KEY_TILE)),
            out((seq // CMP_STRIDE, HEAD_DIM)), out((seq // CMP_STRIDE, HEAD_DIM))]
    return pl.pallas_call(
        _nsa_prep_kernel,
        out_shape=tuple(o[1] for o in outs),
        grid=(batch, N_KV),
        in_specs=[zspec(0), zspec(1), zspec(2), zspec(3), zspec(4), zspec(5), tab, tab, tab]
        + wspecs() + wspecs(),
        out_specs=tuple(o[0] for o in outs),
        compiler_params=_cparams(2),
        name="nsa_prep",
    )(z, z, z, z, z, z, *tabs, *cmp_k, *cmp_v)


def _nsa_attn_pair_kernel(q_ref, ng_ref, gl_ref, rot_ref, ov_ref,
                          ksx_ref, vst_ref, kwx_ref, vwt_ref, kcmp_ref, vcmp_ref,
                          o_ref, acc_ref, qx_ref, glt_ref, s_ref, m_ref, l_ref, oc_ref, *, tq, n_grp_step):
    gp = pl.program_id(1)
    i = pl.program_id(2)
    q0 = i * tq
    n_blk = ov_ref.shape[0]
    half = ROT_DIM // 2
    qscale = (HEAD_DIM ** -0.5) * math.log2(math.e)
    qw = GQA * HEAD_DIM
    groups = range(n_grp_step)

    def head_lanes(n):
        return slice(n * tq, (n + 1) * tq)

    def lanes4(x):
        return jnp.concatenate([x] * GQA, axis=1)

    rot = rot_ref[...]
    cos2, sin2 = rot[:ROT_DIM], rot[ROT_DIM:]
    tlane = q0 + lax.broadcasted_iota(jnp.int32, (1, tq), 1)
    kk = lax.broadcasted_iota(jnp.int32, (tq, tq), 0)
    tt = lax.broadcasted_iota(jnp.int32, (tq, tq), 1)
    tri_diag = jnp.where(kk <= tt, 0.0, NEG)
    tri_old = jnp.where(kk > tt, 0.0, NEG)
    n_cmp = kcmp_ref.shape[1]
    cend = lax.broadcasted_iota(jnp.int32, (n_cmp, tq), 0) * CMP_STRIDE + (CMP_LEN - 1)
    bias_c = lanes4(jnp.where(cend <= tlane, 0.0, NEG))
    any_c = lanes4(tlane) >= CMP_LEN - 1
    jblk = lax.broadcasted_iota(jnp.int32, (n_blk, tq), 0)
    forced = (jblk == 0) | (jblk == (tlane >> SLC_SHIFT))
    causal_blk = jblk * SLC_LEN <= tlane
    sub = lax.broadcasted_iota(jnp.int32, (SUBLANES, tq), 0)
    n_rest = EXT_DIM - HEAD_DIM - n_blk
    pad_rows = jnp.where(lax.broadcasted_iota(jnp.int32, (n_rest, GQA * tq), 0) == 0, NEG, 0.0).astype(BF16)
    tiles = tq // KEY_TILE

    def sel_scores(gg, ci):
        k0 = pl.multiple_of(ci * tq, tq)
        return jnp.dot(ksx_ref[gg, pl.ds(k0, tq), :], qx_ref[gg], preferred_element_type=F32)

    def sel_update(gg, ci, s):
        vblk = jnp.concatenate([vst_ref[gg, ci * tiles + r] for r in range(tiles)], axis=1)
        m_old = m_ref[gg]
        m_new = jnp.maximum(m_old, jnp.max(s, axis=0, keepdims=True))
        alpha = jnp.exp2(m_old - m_new)
        p = jnp.exp2(s - m_new)
        l_ref[gg] = alpha * l_ref[gg] + jnp.sum(p, axis=0, keepdims=True)
        acc_ref[gg] = alpha * acc_ref[gg] + jnp.dot(vblk, p.astype(BF16), preferred_element_type=F32)
        m_ref[gg] = m_new

    for gg in groups:
        q = q_ref[:, gg * qw:(gg + 1) * qw]
        qts = []
        for n in range(GQA):
            qt = (q[:, n * HEAD_DIM:(n + 1) * HEAD_DIM] * qscale).T
            qts.append(qt)
            top = qt[:ROT_DIM]
            swapped = jnp.concatenate([top[half:], top[:half]], axis=0)
            qrt = jnp.concatenate([top * cos2 + swapped * sin2, qt[ROT_DIM:]], axis=0)
            qx_ref[gg, 0:HEAD_DIM, head_lanes(n)] = qrt.astype(BF16)
        qt_all = jnp.concatenate(qts, axis=1).astype(BF16)

        s_c = jnp.dot(kcmp_ref[gg], qt_all, preferred_element_type=F32) + bias_c
        m_c = jnp.where(any_c, jnp.max(s_c, axis=0, keepdims=True), 0.0)
        e_c = jnp.exp2(s_c - m_c)
        den_c = jnp.sum(e_c, axis=0, keepdims=True)
        p_c = e_c / jnp.where(den_c > 0, den_c, 1.0)
        vcmp_t = vcmp_ref[gg].astype(F32).T.astype(BF16)
        oc_ref[gg] = jnp.dot(vcmp_t, p_c.astype(BF16), preferred_element_type=F32)

        p_sum = p_c[:, head_lanes(0)]
        for n in range(1, GQA):
            p_sum = p_sum + p_c[:, head_lanes(n)]
        imp = jnp.dot(ov_ref[...], p_sum, preferred_element_type=F32,
                      precision=lax.Precision.HIGHEST)
        imp = jnp.where(forced, jnp.inf, jnp.where(causal_blk, imp, -jnp.inf))
        n_sub = n_blk // SUBLANES
        part = [imp[a * SUBLANES:(a + 1) * SUBLANES] for a in range(n_sub)]
        cnt = [jnp.zeros((SUBLANES, tq), F32) for _ in range(n_sub)]
        for r in range(n_blk):
            row = imp[r:r + 1, :]
            a_r, r_in = divmod(r, SUBLANES)
            for a in range(n_sub):
                if a > a_r:
                    cnt[a] = cnt[a] + jnp.where(row >= part[a], 1.0, 0.0)
                elif a < a_r:
                    cnt[a] = cnt[a] + jnp.where(row > part[a], 1.0, 0.0)
                else:
                    cnt[a] = cnt[a] + jnp.where(sub > r_in, jnp.where(row >= part[a], 1.0, 0.0),
                                                jnp.where(row > part[a], 1.0, 0.0))
        cnt = jnp.concatenate(cnt, axis=0)
        sel_bias = jnp.where(cnt < N_SEL, jnp.where(imp > -jnp.inf, 0.0, NEG), NEG)
        qx_ref[gg, HEAD_DIM:HEAD_DIM + n_blk, :] = lanes4(sel_bias).astype(BF16)
        qx_ref[gg, HEAD_DIM + n_blk:, :] = pad_rows

        m_ref[gg] = jnp.full(m_ref.shape[1:], NEG, F32)
        l_ref[gg] = jnp.zeros(l_ref.shape[1:], F32)
        acc_ref[gg] = jnp.zeros(acc_ref.shape[1:], F32)
        s_ref[gg, 0] = sel_scores(gg, 0)

    def sel_pair(pi, carry):
        c0 = 2 * pi
        for gg in groups:
            s_ref[gg, 1] = sel_scores(gg, c0 + 1)
        for gg in groups:
            sel_update(gg, c0, s_ref[gg, 0])
        for gg in groups:
            s_ref[gg, 0] = sel_scores(gg, c0 + 2)
        for gg in groups:
            sel_update(gg, c0 + 1, s_ref[gg, 1])
        return carry

    lax.fori_loop(0, i // 2, sel_pair, 0)

    @pl.when(i % 2 == 1)
    def _():
        for gg in groups:
            s_ref[gg, 1] = sel_scores(gg, i)
        for gg in groups:
            sel_update(gg, i - 1, s_ref[gg, 0])

    for gg in groups:
        sel_update(gg, i, s_ref[gg, i % 2] + lanes4(tri_diag))

    n_wt = WINDOW // tq
    vt0 = q0 // KEY_TILE
    glt_ref[...] = jax.nn.sigmoid(gl_ref[...]).T
    for gg in groups:
        kwin = kwx_ref[gg, pl.ds(pl.multiple_of(q0, tq), WINDOW + tq), :]
        vwin = jnp.concatenate([vwt_ref[gg, vt0 + r] for r in range((WINDOW + tq) // KEY_TILE)], axis=1)
        s_w = jnp.dot(kwin, qx_ref[gg], preferred_element_type=F32)
        s_w = jnp.concatenate([s_w[:tq] + lanes4(tri_old), s_w[tq:n_wt * tq],
                               s_w[n_wt * tq:] + lanes4(tri_diag)], axis=0)
        e_w = jnp.exp2(s_w - jnp.max(s_w, axis=0, keepdims=True))
        o_w = (jnp.dot(vwin, e_w.astype(BF16), preferred_element_type=F32)
               / jnp.sum(e_w, axis=0, keepdims=True))
        o_s = acc_ref[gg] / l_ref[gg]
        o_c = oc_ref[gg]
        ng = ng_ref[:, gg * qw:(gg + 1) * qw]
        for n in range(GQA):
            head = (gp * n_grp_step + gg) * GQA + n
            gc, gs, gw = (glt_ref[pl.ds(j * N_HEADS + head, 1), :] for j in range(3))
            sl = head_lanes(n)
            o = (gc * o_c[:, sl] + gs * o_s[:, sl] + gw * o_w[:, sl]).T
            ngh = ng[:, n * HEAD_DIM:(n + 1) * HEAD_DIM]
            col = gg * qw + n * HEAD_DIM
            o_ref[:, col:col + HEAD_DIM] = (o * (ngh * jax.nn.sigmoid(ngh))).astype(o_ref.dtype)


def nsa_attention_pair(z, z_ng, gl, rot_t, overlap_t, prep, batch, seq, tq=256, n_grp_step=2):
    ksx, vst, kwx, vwt, kcmp, vcmp = prep
    assert tq % KEY_TILE == 0 and WINDOW % tq == 0 and WINDOW >= 2 * tq and N_KV % n_grp_step == 0
    nq = seq // tq
    qw = GQA * HEAD_DIM * n_grp_step

    def whole(a):
        nd = a.ndim - 2
        return pl.BlockSpec((None, n_grp_step) + a.shape[2:], lambda b, g, i: (b, g) + (0,) * nd)

    lanes = GQA * tq
    return pl.pallas_call(
        functools.partial(_nsa_attn_pair_kernel, tq=tq, n_grp_step=n_grp_step),
        out_shape=jax.ShapeDtypeStruct((batch * seq, N_HEADS * HEAD_DIM), BF16),
        grid=(batch, N_KV // n_grp_step, nq),
        in_specs=[pl.BlockSpec((tq, qw), lambda b, g, i: (b * nq + i, g)),
                  pl.BlockSpec((tq, qw), lambda b, g, i: (b * nq + i, g)),
                  pl.BlockSpec((tq, gl.shape[1]), lambda b, g, i: (b * nq + i, 0)),
                  pl.BlockSpec((rot_t.shape[0], tq), lambda b, g, i: (0, i)),
                  pl.BlockSpec(overlap_t.shape, lambda b, g, i: (0, 0)),
                  whole(ksx), whole(vst), whole(kwx), whole(vwt), whole(kcmp), whole(vcmp)],
        out_specs=pl.BlockSpec((tq, qw), lambda b, g, i: (b * nq + i, g)),
        scratch_shapes=[pltpu.VMEM((n_grp_step, HEAD_DIM, lanes), F32),
                        pltpu.VMEM((n_grp_step, EXT_DIM, lanes), BF16),
                        pltpu.VMEM((gl.shape[1], tq), F32),
                        pltpu.VMEM((n_grp_step, 2, tq, lanes), F32),
                        pltpu.VMEM((n_grp_step, 1, lanes), F32),
                        pltpu.VMEM((n_grp_step, 1, lanes), F32),
                        pltpu.VMEM((n_grp_step, HEAD_DIM, lanes), F32)],
        compiler_params=_cparams(3),
        name="nsa_attention",
    )(z, z_ng, gl, rot_t, overlap_t, ksx, vst, kwx, vwt, kcmp, vcmp)


def _out_proj_norm_kernel(a0_ref, a1_ref, w_ref, x_ref, g_ref, *o_refs, kh, emit_residual):
    acc = jnp.dot(a0_ref[...], w_ref[0:kh, :], preferred_element_type=F32)
    acc += jnp.dot(a1_ref[...], w_ref[kh:, :], preferred_element_type=F32)
    x = x_ref[...] + acc
    y = x * lax.rsqrt(jnp.mean(x * x, axis=-1, keepdims=True) + EPS)
    o_refs[-1][...] = (y * g_ref[...]).astype(o_refs[-1].dtype)
    if emit_residual:
        o_refs[0][...] = x


def out_proj_norm(a0, a0_blk, a1, a1_blk, w, x2d, g, norm_dtype, emit_residual, tm=512, name="out_proj"):
    m, n = x2d.shape
    kh = w.shape[0] // 2
    row = pl.BlockSpec((tm, n), lambda i: (i, 0))
    out_shape = [jax.ShapeDtypeStruct((m, n), norm_dtype)]
    if emit_residual:
        out_shape.insert(0, jax.ShapeDtypeStruct((m, n), F32))
    return pl.pallas_call(
        functools.partial(_out_proj_norm_kernel, kh=kh, emit_residual=emit_residual),
        out_shape=tuple(out_shape),
        grid=(m // tm,),
        in_specs=[pl.BlockSpec((tm, kh), lambda i: (i, a0_blk)),
                  pl.BlockSpec((tm, kh), lambda i: (i, a1_blk)),
                  pl.BlockSpec(w.shape, lambda i: (0, 0), pipeline_mode=pl.Buffered(1)),
                  row,
                  pl.BlockSpec((1, n), lambda i: (0, 0))],
        out_specs=tuple(row for _ in out_shape),
        compiler_params=_cparams(1),
        name=name,
    )(a0, a1, w, x2d, g.reshape(1, n))


def _sgu_v_kernel(h_ref, w_ref, v_ref, mu_ref, rstd_ref, wb_ref, c_ref, s1_ref, s2_ref, *, n_col_tiles):
    j = pl.program_id(1)

    @pl.when(pl.program_id(0) == 0)
    def _():
        wb_ref[j] = w_ref[...].astype(BF16)

    tm, tn = v_ref.shape
    h = h_ref[...]
    for c0 in range(0, tn, V_SUB):
        v = jnp.dot(h, wb_ref[j, :, c0:c0 + V_SUB], preferred_element_type=F32)
        v_ref[:, c0:c0 + V_SUB] = v

        if c0 == 0:
            @pl.when(j == 0)
            def _():
                c_ref[...] = jnp.broadcast_to(jnp.sum(v, axis=-1, keepdims=True) / V_SUB, c_ref.shape)
                s1_ref[...] = jnp.zeros(s1_ref.shape, F32)
                s2_ref[...] = jnp.zeros(s2_ref.shape, F32)

        c = c_ref[...]
        s1, s2 = s1_ref[...], s2_ref[...]
        for r in range(V_SUB // LANES):
            d = v[:, r * LANES:(r + 1) * LANES] - c
            s1 = s1 + d
            s2 = s2 + d * d
        s1_ref[...] = s1
        s2_ref[...] = s2

    @pl.when(j == n_col_tiles - 1)
    def _():
        width = n_col_tiles * tn
        mean_d = jnp.sum(s1, axis=-1, keepdims=True) / width
        var = jnp.sum(s2, axis=-1, keepdims=True) / width - mean_d * mean_d
        mu_ref[...] = c + mean_d
        rstd_ref[...] = jnp.broadcast_to(lax.rsqrt(var + EPS), rstd_ref.shape)


def sgu_v(h, w, col0, tm=512, tn=2 * V_SUB):
    m, k = h.shape
    nj = SGU_W // tn
    j0 = col0 // tn
    stat = jax.ShapeDtypeStruct((m, LANES), F32)
    stat_spec = pl.BlockSpec((tm, LANES), lambda i, j: (i, 0))
    return pl.pallas_call(
        functools.partial(_sgu_v_kernel, n_col_tiles=nj),
        out_shape=(jax.ShapeDtypeStruct((m, SGU_W), F32), stat, stat),
        grid=(m // tm, nj),
        in_specs=[pl.BlockSpec((tm, k), lambda i, j: (i, 0)),
                  pl.BlockSpec((k, tn), lambda i, j: (0, j0 + jnp.where(i == 0, j, 0)))],
        out_specs=(pl.BlockSpec((tm, tn), lambda i, j: (i, j)), stat_spec, stat_spec),
        scratch_shapes=[pltpu.VMEM((nj, k, tn), BF16)] + [pltpu.VMEM((tm, LANES), F32)] * 3,
        compiler_params=_cparams(2),
        name="sgu_v",
    )(h, w)


def _sgu_gate_kernel(h_ref, wu32_ref, wz32_ref, v_ref, mu_ref, rstd_ref, lg_ref, lb_ref, ws_ref, bs_ref,
                     side32_ref, o_ref, side_ref, wu_ref, wz_ref):
    _cast_once([wu32_ref, wz32_ref], [wu_ref, wz_ref])
    side_ref[...] = side32_ref[...].astype(BF16)
    h = h_ref[...]
    tm, tn = v_ref.shape
    reps = GROUP_W // LANES
    mu = jnp.concatenate([mu_ref[...]] * reps, axis=1)
    rstd = jnp.concatenate([rstd_ref[...]] * reps, axis=1)
    tri = (lax.broadcasted_iota(jnp.int32, (CHUNK, CHUNK), 1)
           <= lax.broadcasted_iota(jnp.int32, (CHUNK, CHUNK), 0))
    for gi in range(tn // GROUP_W):
        cols = slice(gi * GROUP_W, (gi + 1) * GROUP_W)
        u = jnp.dot(h, wu_ref[:, cols], preferred_element_type=F32)
        zg = jnp.dot(h, wz_ref[:, cols], preferred_element_type=F32)
        vn = ((v_ref[:, cols] - mu) * rstd * lg_ref[:, cols] + lb_ref[:, cols]).astype(BF16)
        wsm = jnp.where(tri, ws_ref[gi], 0.0).astype(BF16)
        bsg = bs_ref[gi]
        mix = jnp.concatenate(
            [jnp.dot(wsm, vn[c * CHUNK:(c + 1) * CHUNK], preferred_element_type=F32) + bsg
             for c in range(tm // CHUNK)], axis=0)
        o_ref[:, cols] = (u * mix * (zg * jax.nn.sigmoid(zg))).astype(o_ref.dtype)


def sgu_gate(h, w, col_u, col_z, v, mu, rstd, ln_g, ln_b, w_s, b_s, side, tm=1024, tn=512):
    m, k = h.shape
    n = SGU_W
    nj, ni = n // tn, m // tm
    ju, jz = col_u // tn, col_z // tn
    gpt = tn // GROUP_W
    stat_spec = pl.BlockSpec((tm, LANES), lambda j, i: (i, 0))
    side_rows = side.shape[0] // (nj * ni)
    assert side_rows * nj * ni == side.shape[0] and side_rows % HALO == 0
    side_spec = pl.BlockSpec((side_rows, side.shape[1]), lambda j, i: (j * ni + i, 0))
    return pl.pallas_call(
        _sgu_gate_kernel,
        out_shape=(jax.ShapeDtypeStruct((m, n), BF16), jax.ShapeDtypeStruct(side.shape, BF16)),
        grid=(nj, ni),
        in_specs=[pl.BlockSpec((tm, k), lambda j, i: (i, 0)),
                  pl.BlockSpec((k, tn), lambda j, i: (0, ju + j)),
                  pl.BlockSpec((k, tn), lambda j, i: (0, jz + j)),
                  pl.BlockSpec((tm, tn), lambda j, i: (i, j)),
                  stat_spec, stat_spec,
                  pl.BlockSpec((1, tn), lambda j, i: (0, j)),
                  pl.BlockSpec((1, tn), lambda j, i: (0, j)),
                  pl.BlockSpec((gpt, CHUNK, CHUNK), lambda j, i: (j, 0, 0)),
                  pl.BlockSpec((gpt, CHUNK, 1), lambda j, i: (j, 0, 0)),
                  side_spec],
        out_specs=(pl.BlockSpec((tm, tn), lambda j, i: (i, j)), side_spec),
        scratch_shapes=[pltpu.VMEM((k, tn), BF16)] * 2,
        compiler_params=_cparams(2),
        name="sgu_gate",
    )(h, w, w, v, mu, rstd, ln_g.reshape(1, n), ln_b.reshape(1, n), w_s, b_s.reshape(N_GROUPS, CHUNK, 1),
      side)


def _rotary_tables(seq):
    half = ROT_DIM // 2
    inv_freq = jnp.power(ROPE_THETA, -jnp.arange(half, dtype=F32) * 2.0 / ROT_DIM)
    ang = jnp.arange(seq).astype(F32)[:, None] * inv_freq[None, :]
    cos, sin = jnp.cos(ang), jnp.sin(ang)
    rest = HEAD_DIM - ROT_DIM
    c = jnp.concatenate([cos, cos, jnp.ones((seq, rest), F32)], axis=1)
    sa = jnp.concatenate([jnp.zeros((seq, half), F32), sin, jnp.zeros((seq, rest), F32)], axis=1)
    sb = jnp.concatenate([-sin, jnp.zeros((seq, half + rest), F32)], axis=1)
    rot_t = jnp.concatenate([cos, cos, -sin, sin], axis=1).T
    return (c, sa, sb), rot_t


def _overlap_t(seq):
    n_blk = seq // SLC_LEN
    n_rows = seq // CMP_STRIDE
    n_cmp = (seq - CMP_LEN) // CMP_STRIDE + 1
    cs = jnp.arange(n_rows) * CMP_STRIDE
    bs = jnp.arange(n_blk) * SLC_LEN
    ov = (cs[None, :] < bs[:, None] + SLC_LEN) & (cs[None, :] + CMP_LEN > bs[:, None])
    ov = ov & (jnp.arange(n_rows)[None, :] < n_cmp)
    return ov.astype(F32)


def kernel(x, norm_even, w_in_even, conv_w, cmp_k_pos, cmp_k_w1, cmp_k_b1, cmp_k_w2, cmp_v_pos, cmp_v_w1, cmp_v_b1, cmp_v_w2, w_out_even, norm_odd, w_in_odd, sgu_ln_g, sgu_ln_b, sgu_w_s, sgu_b_s, w_out_odd, norm_final):
    batch, seq, d = x.shape
    m = batch * seq
    x2d = x.reshape(m, d)

    wt_in = jnp.swapaxes(w_in_even[0], 0, 1)
    cw = CONV_W
    qw = N_HEADS * HEAD_DIM
    kvw = N_KV * HEAD_DIM
    o_q = 4 * cw
    o_kv = o_q + qw
    o_gl = o_kv + 6 * kvw
    o_ng = o_gl + 3 * N_HEADS

    h0, gl = rmsnorm_proj(x2d, norm_even[0], wt_in, o_gl, LANES)
    y_conv, w_out0 = conv_proj(h0, wt_in, conv_w[0], seq, w_out_even[0])
    z = matmul_t(h0, wt_in, o_q, qw + 6 * kvw, tn=1024, name="nsa_proj")
    z_ng = matmul_t(h0, wt_in, o_ng, qw, tn=1024, name="nsa_gate_proj")

    tabs, rot_t = _rotary_tables(seq)
    cmp_k = (cmp_k_pos[0], cmp_k_w1[0].astype(BF16).reshape(CMP_LEN, HEAD_DIM, HEAD_DIM),
             cmp_k_b1[0].reshape(1, HEAD_DIM), cmp_k_w2[0].astype(BF16))
    cmp_v = (cmp_v_pos[0], cmp_v_w1[0].astype(BF16).reshape(CMP_LEN, HEAD_DIM, HEAD_DIM),
             cmp_v_b1[0].reshape(1, HEAD_DIM), cmp_v_w2[0].astype(BF16))
    prep = nsa_prep(z, tabs, cmp_k, cmp_v, batch, seq, col0=qw // HEAD_DIM)
    y_nsa = nsa_attention_pair(z, z_ng, gl, rot_t, _overlap_t(seq), prep, batch, seq)

    x1, h1 = out_proj_norm(y_conv, 0, y_nsa, 0, w_out0, x2d, norm_odd[0],
                           BF16, True, name="out_proj_even")

    w_in1 = w_in_odd[0]
    v, mu, rstd = sgu_v(h1, w_in1, SGU_W)
    act, w_out1 = sgu_gate(h1, w_in1, 0, 2 * SGU_W, v, mu, rstd,
                           sgu_ln_g[0], sgu_ln_b[0], sgu_w_s[0], sgu_b_s[0], w_out_odd[0])
    (out,) = out_proj_norm(act, 0, act, 1, w_out1, x1, norm_final,
                           F32, False, name="out_proj_odd")
    return out.reshape(batch, seq, d)
```

```python
import functools
import math

import jax
import jax.numpy as jnp
from jax import lax
from jax.experimental import pallas as pl
from jax.experimental.pallas import tpu as pltpu

F32 = jnp.float32
BF16 = jnp.bfloat16

D_MODEL = 2048
MIX = 2 * D_MODEL
CONV_W = MIX // 2
CONV_K = 3
HEAD_DIM = 128
N_HEADS = 16
N_KV = 4
GQA = N_HEADS // N_KV
ROT_DIM = HEAD_DIM // 4
ROPE_THETA = 500000.0
CMP_LEN = 32
CMP_STRIDE = 16
SLC_LEN = 64
N_SEL = 8
CMP_PER_SLC = SLC_LEN // CMP_STRIDE
CMP_BACK = CMP_LEN // CMP_STRIDE - 1
WINDOW = 512
SGU_W = MIX
CHUNK = 128
N_GROUPS = 16
GROUP_W = SGU_W // N_GROUPS
EPS = 1e-6

LANES = 128
SUBLANES = 8
SLC_SHIFT = 6
HALO = 16
KEY_TILE = LANES
V_SUB = 512
XPOSE_ROWS = 256
VMEM_LIMIT = 56 * 1024 * 1024

NEG = -1e30

EXT_DIM = 2 * HEAD_DIM


def _cparams(n_axes):
    return pltpu.CompilerParams(
        dimension_semantics=("arbitrary",) * n_axes, vmem_limit_bytes=VMEM_LIMIT)


def _rmsnorm_proj_kernel(x_ref, g_ref, wt_ref, o_ref, p_ref, wb_ref):
    @pl.when(pl.program_id(0) == 0)
    def _():
        wb_ref[...] = wt_ref[...].T.astype(BF16)

    x = x_ref[...]
    y = x * lax.rsqrt(jnp.mean(x * x, axis=-1, keepdims=True) + EPS)
    hb = (y * g_ref[...]).astype(BF16)
    o_ref[...] = hb
    p_ref[...] = jnp.dot(hb, wb_ref[...], preferred_element_type=F32)


def rmsnorm_proj(x2d, g, wt, row0, n, tm=512):
    m, d = x2d.shape
    assert row0 % n == 0
    return pl.pallas_call(
        _rmsnorm_proj_kernel,
        out_shape=(jax.ShapeDtypeStruct((m, d), BF16), jax.ShapeDtypeStruct((m, n), F32)),
        grid=(m // tm,),
        in_specs=[pl.BlockSpec((tm, d), lambda i: (i, 0)),
                  pl.BlockSpec((1, d), lambda i: (0, 0)),
                  pl.BlockSpec((n, d), lambda i: (row0 // n, 0))],
        out_specs=(pl.BlockSpec((tm, d), lambda i: (i, 0)),
                   pl.BlockSpec((tm, n), lambda i: (i, 0))),
        scratch_shapes=[pltpu.VMEM((d, n), BF16)],
        compiler_params=_cparams(1),
        name="rmsnorm_gates",
    )(x2d, g.reshape(1, d), wt)


def _cast_once(w_refs, wb_refs, transposed=False):
    @pl.when(pl.program_id(1) == 0)
    def _():
        for w_ref, wb_ref in zip(w_refs, wb_refs):
            if transposed:
                for r in range(0, w_ref.shape[0], XPOSE_ROWS):
                    wb_ref[:, r:r + XPOSE_ROWS] = w_ref[r:r + XPOSE_ROWS, :].T.astype(BF16)
            else:
                wb_ref[...] = w_ref[...].astype(BF16)


def _matmul_t_kernel(a_ref, wt_ref, o_ref, wb_ref):
    _cast_once([wt_ref], [wb_ref], transposed=True)
    o_ref[...] = jnp.dot(a_ref[...], wb_ref[...], preferred_element_type=F32)


def matmul_t(a, wt, row0, n, tm=1024, tn=512, name="proj"):
    m, k = a.shape
    assert n % tn == 0 and row0 % HALO == 0
    if row0 % tn == 0:
        wspec = pl.BlockSpec((tn, k), lambda j, i: (row0 // tn + j, 0))
    else:
        wspec = pl.BlockSpec((pl.Element(tn), pl.Element(k)),
                             lambda j, i: (pl.multiple_of(row0 + j * tn, HALO), 0))
    return pl.pallas_call(
        _matmul_t_kernel,
        out_shape=jax.ShapeDtypeStruct((m, n), F32),
        grid=(n // tn, m // tm),
        in_specs=[pl.BlockSpec((tm, k), lambda j, i: (i, 0)), wspec],
        out_specs=pl.BlockSpec((tm, tn), lambda j, i: (i, j)),
        scratch_shapes=[pltpu.VMEM((k, tn), BF16)],
        compiler_params=_cparams(2),
        name=name,
    )(a, wt)


def _conv_proj_kernel(h_ref, hp_ref, wb32_ref, wc32_ref, wh32_ref, wg32_ref, cw_ref, side32_ref,
                      o_ref, side_ref, wb_ref, wc_ref, wh_ref, wg_ref, *, tiles_per_seq):
    _cast_once([wb32_ref, wc32_ref, wh32_ref, wg32_ref], [wb_ref, wc_ref, wh_ref, wg_ref],
               transposed=True)
    side_ref[...] = side32_ref[...].astype(BF16)
    i = pl.program_id(1)
    cw = cw_ref[...]
    hp = hp_ref[...]
    pp = (jnp.dot(hp, wc_ref[...], preferred_element_type=F32)
          * jnp.dot(hp, wh_ref[...], preferred_element_type=F32))
    pp = jnp.where(i % tiles_per_seq == 0, 0.0, pp)
    prev1, prev2 = pp[HALO - 1:HALO, :], pp[HALO - 2:HALO - 1, :]
    h = h_ref[...]
    cb = jnp.dot(h, wb_ref[...], preferred_element_type=F32)
    cc = jnp.dot(h, wc_ref[...], preferred_element_type=F32)
    ch = jnp.dot(h, wh_ref[...], preferred_element_type=F32)
    cg = jnp.dot(h, wg_ref[...], preferred_element_type=F32)
    p = cc * ch
    row = lax.broadcasted_iota(jnp.int32, (SUBLANES, p.shape[1]), 0)
    p1, p2 = pltpu.roll(p, 1, axis=0), pltpu.roll(p, 2, axis=0)
    p1 = jnp.concatenate([jnp.where(row == 0, prev1, p1[:SUBLANES]), p1[SUBLANES:]], axis=0)
    p2 = jnp.concatenate([jnp.where(row == 0, prev2, jnp.where(row == 1, prev1, p2[:SUBLANES])),
                          p2[SUBLANES:]], axis=0)
    conv = cw[0:1, :] * p2 + cw[1:2, :] * p1 + cw[2:3, :] * p
    o_ref[...] = (cb * conv * (cg * jax.nn.sigmoid(cg))).astype(o_ref.dtype)


def conv_proj(h, wt, conv_w, seq, side, tm=1024, tn=256):
    m, k = h.shape
    n = conv_w.shape[1]
    tiles_per_seq = seq // tm
    halo_per_tile = tm // HALO
    nj, ni = n // tn, m // tm
    side_rows = side.shape[0] // (nj * ni)
    assert side_rows * nj * ni == side.shape[0] and side_rows % HALO == 0
    side_spec = pl.BlockSpec((side_rows, side.shape[1]), lambda j, i: (j * ni + i, 0))

    def wspec(which):
        return pl.BlockSpec((tn, k), lambda j, i: (which * nj + j, 0))

    return pl.pallas_call(
        functools.partial(_conv_proj_kernel, tiles_per_seq=tiles_per_seq),
        out_shape=(jax.ShapeDtypeStruct((m, n), BF16), jax.ShapeDtypeStruct(side.shape, BF16)),
        grid=(nj, ni),
        in_specs=[pl.BlockSpec((tm, k), lambda j, i: (i, 0)),
                  pl.BlockSpec((HALO, k), lambda j, i: (jnp.maximum(i * halo_per_tile - 1, 0), 0)),
                  wspec(0), wspec(1), wspec(2), wspec(3),
                  pl.BlockSpec((CONV_K, tn), lambda j, i: (0, j)),
                  side_spec],
        out_specs=(pl.BlockSpec((tm, tn), lambda j, i: (i, j)), side_spec),
        scratch_shapes=[pltpu.VMEM((k, tn), BF16)] * 4,
        compiler_params=_cparams(2),
        name="conv_proj",
    )(h, h, wt, wt, wt, wt, conv_w, side)


def _rotary(x, c, sa, sb):
    half = ROT_DIM // 2
    return (x * c + pltpu.roll(x, half, axis=1) * sa
            + pltpu.roll(x, HEAD_DIM - half, axis=1) * sb)


def _compress(src_ref, pos_ref, w1_ref, b1_ref, w2_ref):
    n_rows = src_ref.shape[0] // CMP_STRIDE
    p_acc = jnp.zeros((n_rows, HEAD_DIM), F32)
    q_acc = jnp.zeros((n_rows, HEAD_DIM), F32)
    for r in range(CMP_STRIDE):
        s_r = src_ref[pl.ds(r, n_rows, stride=CMP_STRIDE), :]
        a_r = (s_r + pos_ref[r:r + 1, :]).astype(BF16)
        b_r = (s_r + pos_ref[CMP_STRIDE + r:CMP_STRIDE + r + 1, :]).astype(BF16)
        p_acc += jnp.dot(a_r, w1_ref[r], preferred_element_type=F32)
        q_acc += jnp.dot(b_r, w1_ref[CMP_STRIDE + r], preferred_element_type=F32)
    hid = p_acc + pltpu.roll(q_acc, n_rows - 1, axis=0) + b1_ref[...]
    act = (hid * jax.nn.sigmoid(hid)).astype(BF16)
    return jnp.dot(act, w2_ref[...], preferred_element_type=F32)


def _nsa_prep_kernel(kc_ref, vc_ref, ks_ref, vs_ref, kw_ref, vw_ref, c_ref, sa_ref, sb_ref,
                     kpos_ref, kw1_ref, kb1_ref, kw2_ref, vpos_ref, vw1_ref, vb1_ref, vw2_ref,
                     ksx_ref, vst_ref, kwx_ref, vwt_ref, kcmp_ref, vcmp_ref):
    c, sa, sb = c_ref[...], sa_ref[...], sb_ref[...]
    seq = ks_ref.shape[0]
    n_blk = seq // SLC_LEN
    flag_w = EXT_DIM - HEAD_DIM
    blk = lax.broadcasted_iota(jnp.int32, (seq, flag_w), 0) >> SLC_SHIFT
    onehot = jnp.where(blk == lax.broadcasted_iota(jnp.int32, (seq, flag_w), 1), 1.0, 0.0)
    ksx_ref[:, 0:HEAD_DIM] = _rotary(ks_ref[...], c, sa, sb).astype(BF16)
    ksx_ref[:, HEAD_DIM:] = onehot.astype(BF16)
    pad_flag = jnp.where(lax.broadcasted_iota(jnp.int32, (WINDOW, EXT_DIM), 1) == HEAD_DIM + n_blk, 1.0, 0.0)
    kwx_ref[0:WINDOW, :] = pad_flag.astype(BF16)
    kwx_ref[WINDOW:, 0:HEAD_DIM] = _rotary(kw_ref[...], c, sa, sb).astype(BF16)
    kwx_ref[WINDOW:, HEAD_DIM:] = jnp.zeros((seq, flag_w), BF16)
    n_tiles = seq // KEY_TILE
    pad_tiles = WINDOW // KEY_TILE
    for t in range(pad_tiles):
        vwt_ref[t] = jnp.zeros((HEAD_DIM, KEY_TILE), BF16)
    for t in range(n_tiles):
        rows = pl.ds(t * KEY_TILE, KEY_TILE)
        vst_ref[t] = vs_ref[rows, :].T.astype(BF16)
        vwt_ref[pad_tiles + t] = vw_ref[rows, :].T.astype(BF16)
    kcmp_ref[...] = _compress(kc_ref, kpos_ref, kw1_ref, kb1_ref, kw2_ref).astype(BF16)
    vcmp_ref[...] = _compress(vc_ref, vpos_ref, vw1_ref, vb1_ref, vw2_ref).astype(BF16)


def nsa_prep(z, tabs, cmp_k, cmp_v, batch, seq, col0):
    def zspec(which):
        return pl.BlockSpec((seq, HEAD_DIM), lambda b, g, w=which: (b, col0 + w * N_KV + g))

    tab = pl.BlockSpec((seq, HEAD_DIM), lambda b, g: (0, 0))

    def wspecs():
        return [pl.BlockSpec((CMP_LEN, HEAD_DIM), lambda b, g: (0, 0)),
                pl.BlockSpec((CMP_LEN, HEAD_DIM, HEAD_DIM), lambda b, g: (0, 0, 0)),
                pl.BlockSpec((1, HEAD_DIM), lambda b, g: (0, 0)),
                pl.BlockSpec((HEAD_DIM, HEAD_DIM), lambda b, g: (0, 0))]

    def out(shape):
        nd = len(shape)
        spec = pl.BlockSpec((None, None) + shape, lambda b, g: (b, g) + (0,) * nd)
        return spec, jax.ShapeDtypeStruct((batch, N_KV) + shape, BF16)

    n_tiles = seq // KEY_TILE
    pad_tiles = WINDOW // KEY_TILE
    outs = [out((seq, EXT_DIM)), out((n_tiles, HEAD_DIM, KEY_TILE)),
            out((WINDOW + seq, EXT_DIM)), out((pad_tiles + n_tiles, HEAD_DIM, KEY_TILE)),
            out((seq // CMP_STRIDE, HEAD_DIM)), out((seq // CMP_STRIDE, HEAD_DIM))]
    return pl.pallas_call(
        _nsa_prep_kernel,
        out_shape=tuple(o[1] for o in outs),
        grid=(batch, N_KV),
        in_specs=[zspec(0), zspec(1), zspec(2), zspec(3), zspec(4), zspec(5), tab, tab, tab]
        + wspecs() + wspecs(),
        out_specs=tuple(o[0] for o in outs),
        compiler_params=_cparams(2),
        name="nsa_prep",
    )(z, z, z, z, z, z, *tabs, *cmp_k, *cmp_v)


def _nsa_attn_pair_kernel(q_ref, ng_ref, gl_ref, rot_ref,
                          ksx_ref, vst_ref, kwx_ref, vwt_ref, kcmp_ref, vcmp_ref,
                          o_ref, acc_ref, qx_ref, glt_ref, s_ref, m_ref, l_ref, oc_ref, ps_ref, *,
                          tq, n_grp_step, n_blk):
    gp = pl.program_id(1)
    i = pl.program_id(2)
    q0 = i * tq
    half = ROT_DIM // 2
    qscale = (HEAD_DIM ** -0.5) * math.log2(math.e)
    qw = GQA * HEAD_DIM
    groups = range(n_grp_step)

    def head_lanes(n):
        return slice(n * tq, (n + 1) * tq)

    def lanes4(x):
        return jnp.concatenate([x] * GQA, axis=1)

    rot = rot_ref[...]
    cos2, sin2 = rot[:ROT_DIM], rot[ROT_DIM:]
    tlane = q0 + lax.broadcasted_iota(jnp.int32, (1, tq), 1)
    kk = lax.broadcasted_iota(jnp.int32, (tq, tq), 0)
    tt = lax.broadcasted_iota(jnp.int32, (tq, tq), 1)
    tri_diag = jnp.where(kk <= tt, 0.0, NEG)
    tri_old = jnp.where(kk > tt, 0.0, NEG)
    n_cmp = kcmp_ref.shape[1]
    cend = lax.broadcasted_iota(jnp.int32, (n_cmp, tq), 0) * CMP_STRIDE + (CMP_LEN - 1)
    bias_c = lanes4(jnp.where(cend <= tlane, 0.0, NEG))
    any_c = lanes4(tlane) >= CMP_LEN - 1
    jblk = lax.broadcasted_iota(jnp.int32, (n_blk, tq), 0)
    forced = (jblk == 0) | (jblk == (tlane >> SLC_SHIFT))
    causal_blk = jblk * SLC_LEN <= tlane
    sub = lax.broadcasted_iota(jnp.int32, (SUBLANES, tq), 0)
    n_rest = EXT_DIM - HEAD_DIM - n_blk
    pad_rows = jnp.where(lax.broadcasted_iota(jnp.int32, (n_rest, GQA * tq), 0) == 0, NEG, 0.0).astype(BF16)
    tiles = tq // KEY_TILE

    def sel_scores(gg, ci):
        k0 = pl.multiple_of(ci * tq, tq)
        return jnp.dot(ksx_ref[gg, pl.ds(k0, tq), :], qx_ref[gg], preferred_element_type=F32)

    def sel_update(gg, ci, s):
        vblk = jnp.concatenate([vst_ref[gg, ci * tiles + r] for r in range(tiles)], axis=1)
        m_old = m_ref[gg]
        m_new = jnp.maximum(m_old, jnp.max(s, axis=0, keepdims=True))
        alpha = jnp.exp2(m_old - m_new)
        p = jnp.exp2(s - m_new)
        l_ref[gg] = alpha * l_ref[gg] + jnp.sum(p, axis=0, keepdims=True)
        acc_ref[gg] = alpha * acc_ref[gg] + jnp.dot(vblk, p.astype(BF16), preferred_element_type=F32)
        m_ref[gg] = m_new

    for gg in groups:
        q = q_ref[:, gg * qw:(gg + 1) * qw]
        qts = []
        for n in range(GQA):
            qt = (q[:, n * HEAD_DIM:(n + 1) * HEAD_DIM] * qscale).T
            qts.append(qt)
            top = qt[:ROT_DIM]
            swapped = jnp.concatenate([top[half:], top[:half]], axis=0)
            qrt = jnp.concatenate([top * cos2 + swapped * sin2, qt[ROT_DIM:]], axis=0)
            qx_ref[gg, 0:HEAD_DIM, head_lanes(n)] = qrt.astype(BF16)
        qt_all = jnp.concatenate(qts, axis=1).astype(BF16)

        s_c = jnp.dot(kcmp_ref[gg], qt_all, preferred_element_type=F32) + bias_c
        m_c = jnp.where(any_c, jnp.max(s_c, axis=0, keepdims=True), 0.0)
        e_c = jnp.exp2(s_c - m_c)
        den_c = jnp.sum(e_c, axis=0, keepdims=True)
        p_c = e_c / jnp.where(den_c > 0, den_c, 1.0)
        vcmp_t = vcmp_ref[gg].astype(F32).T.astype(BF16)
        oc_ref[gg] = jnp.dot(vcmp_t, p_c.astype(BF16), preferred_element_type=F32)

        p_sum = p_c[:, head_lanes(0)]
        for n in range(1, GQA):
            p_sum = p_sum + p_c[:, head_lanes(n)]
        for t in range(tq // LANES):
            ps_ref[gg, t] = p_sum[:, t * LANES:(t + 1) * LANES]
        taps = [jnp.concatenate([ps_ref[gg, t, pl.ds(k, n_blk, stride=CMP_PER_SLC), :]
                                 for t in range(tq // LANES)], axis=1) for k in range(CMP_PER_SLC)]
        imp = taps[0]
        for k in range(1, CMP_PER_SLC):
            imp = imp + taps[k]
        for k in range(CMP_BACK):
            older = pltpu.roll(taps[CMP_PER_SLC - 1 - k], 1, axis=0)
            imp = imp + jnp.where(jblk == 0, 0.0, older)
        imp = jnp.where(forced, jnp.inf, jnp.where(causal_blk, imp, -jnp.inf))
        n_sub = n_blk // SUBLANES
        part = [imp[a * SUBLANES:(a + 1) * SUBLANES] for a in range(n_sub)]
        cnt = [jnp.zeros((SUBLANES, tq), F32) for _ in range(n_sub)]
        for r in range(n_blk):
            row = imp[r:r + 1, :]
            a_r, r_in = divmod(r, SUBLANES)
            for a in range(n_sub):
                if a > a_r:
                    cnt[a] = cnt[a] + jnp.where(row >= part[a], 1.0, 0.0)
                elif a < a_r:
                    cnt[a] = cnt[a] + jnp.where(row > part[a], 1.0, 0.0)
                else:
                    cnt[a] = cnt[a] + jnp.where(sub > r_in, jnp.where(row >= part[a], 1.0, 0.0),
                                                jnp.where(row > part[a], 1.0, 0.0))
        cnt = jnp.concatenate(cnt, axis=0)
        sel_bias = jnp.where(cnt < N_SEL, jnp.where(imp > -jnp.inf, 0.0, NEG), NEG)
        qx_ref[gg, HEAD_DIM:HEAD_DIM + n_blk, :] = lanes4(sel_bias).astype(BF16)
        qx_ref[gg, HEAD_DIM + n_blk:, :] = pad_rows

        m_ref[gg] = jnp.full(m_ref.shape[1:], NEG, F32)
        l_ref[gg] = jnp.zeros(l_ref.shape[1:], F32)
        acc_ref[gg] = jnp.zeros(acc_ref.shape[1:], F32)
        s_ref[gg, 0] = sel_scores(gg, 0)

    def sel_pair(pi, carry):
        c0 = 2 * pi
        for gg in groups:
            s_ref[gg, 1] = sel_scores(gg, c0 + 1)
        for gg in groups:
            sel_update(gg, c0, s_ref[gg, 0])
        for gg in groups:
            s_ref[gg, 0] = sel_scores(gg, c0 + 2)
        for gg in groups:
            sel_update(gg, c0 + 1, s_ref[gg, 1])
        return carry

    lax.fori_loop(0, i // 2, sel_pair, 0)

    @pl.when(i % 2 == 1)
    def _():
        for gg in groups:
            s_ref[gg, 1] = sel_scores(gg, i)
        for gg in groups:
            sel_update(gg, i - 1, s_ref[gg, 0])

    for gg in groups:
        sel_update(gg, i, s_ref[gg, i % 2] + lanes4(tri_diag))

    n_wt = WINDOW // tq
    vt0 = q0 // KEY_TILE
    glt_ref[...] = jax.nn.sigmoid(gl_ref[...]).T
    for gg in groups:
        kwin = kwx_ref[gg, pl.ds(pl.multiple_of(q0, tq), WINDOW + tq), :]
        vwin = jnp.concatenate([vwt_ref[gg, vt0 + r] for r in range((WINDOW + tq) // KEY_TILE)], axis=1)
        s_w = jnp.dot(kwin, qx_ref[gg], preferred_element_type=F32)
        s_w = jnp.concatenate([s_w[:tq] + lanes4(tri_old), s_w[tq:n_wt * tq],
                               s_w[n_wt * tq:] + lanes4(tri_diag)], axis=0)
        e_w = jnp.exp2(s_w - jnp.max(s_w, axis=0, keepdims=True))
        o_w = (jnp.dot(vwin, e_w.astype(BF16), preferred_element_type=F32)
               / jnp.sum(e_w, axis=0, keepdims=True))
        o_s = acc_ref[gg] / l_ref[gg]
        o_c = oc_ref[gg]
        ng = ng_ref[:, gg * qw:(gg + 1) * qw]
        for n in range(GQA):
            head = (gp * n_grp_step + gg) * GQA + n
            gc, gs, gw = (glt_ref[pl.ds(j * N_HEADS + head, 1), :] for j in range(3))
            sl = head_lanes(n)
            o = (gc * o_c[:, sl] + gs * o_s[:, sl] + gw * o_w[:, sl]).T
            ngh = ng[:, n * HEAD_DIM:(n + 1) * HEAD_DIM]
            col = gg * qw + n * HEAD_DIM
            o_ref[:, col:col + HEAD_DIM] = (o * (ngh * jax.nn.sigmoid(ngh))).astype(o_ref.dtype)


def nsa_attention_pair(z, z_ng, gl, rot_t, prep, batch, seq, tq=256, n_grp_step=2):
    ksx, vst, kwx, vwt, kcmp, vcmp = prep
    assert tq % KEY_TILE == 0 and WINDOW % tq == 0 and WINDOW >= 2 * tq and N_KV % n_grp_step == 0
    n_blk, n_cmp_rows = seq // SLC_LEN, kcmp.shape[2]
    assert n_cmp_rows == CMP_PER_SLC * n_blk
    nq = seq // tq
    qw = GQA * HEAD_DIM * n_grp_step

    def whole(a):
        nd = a.ndim - 2
        return pl.BlockSpec((None, n_grp_step) + a.shape[2:], lambda b, g, i: (b, g) + (0,) * nd)

    lanes = GQA * tq
    return pl.pallas_call(
        functools.partial(_nsa_attn_pair_kernel, tq=tq, n_grp_step=n_grp_step, n_blk=n_blk),
        out_shape=jax.ShapeDtypeStruct((batch * seq, N_HEADS * HEAD_DIM), BF16),
        grid=(batch, N_KV // n_grp_step, nq),
        in_specs=[pl.BlockSpec((tq, qw), lambda b, g, i: (b * nq + i, g)),
                  pl.BlockSpec((tq, qw), lambda b, g, i: (b * nq + i, g)),
                  pl.BlockSpec((tq, gl.shape[1]), lambda b, g, i: (b * nq + i, 0)),
                  pl.BlockSpec((rot_t.shape[0], tq), lambda b, g, i: (0, i)),
                  whole(ksx), whole(vst), whole(kwx), whole(vwt), whole(kcmp), whole(vcmp)],
        out_specs=pl.BlockSpec((tq, qw), lambda b, g, i: (b * nq + i, g)),
        scratch_shapes=[pltpu.VMEM((n_grp_step, HEAD_DIM, lanes), F32),
                        pltpu.VMEM((n_grp_step, EXT_DIM, lanes), BF16),
                        pltpu.VMEM((gl.shape[1], tq), F32),
                        pltpu.VMEM((n_grp_step, 2, tq, lanes), F32),
                        pltpu.VMEM((n_grp_step, 1, lanes), F32),
                        pltpu.VMEM((n_grp_step, 1, lanes), F32),
                        pltpu.VMEM((n_grp_step, HEAD_DIM, lanes), F32),
                        pltpu.VMEM((n_grp_step, tq // LANES, n_cmp_rows, LANES), F32)],
        compiler_params=_cparams(3),
        name="nsa_attention",
    )(z, z_ng, gl, rot_t, ksx, vst, kwx, vwt, kcmp, vcmp)


def _out_proj_norm_kernel(a0_ref, a1_ref, w_ref, x_ref, g_ref, *o_refs, kh, emit_residual):
    acc = jnp.dot(a0_ref[...], w_ref[0:kh, :], preferred_element_type=F32)
    acc += jnp.dot(a1_ref[...], w_ref[kh:, :], preferred_element_type=F32)
    x = x_ref[...] + acc
    y = x * lax.rsqrt(jnp.mean(x * x, axis=-1, keepdims=True) + EPS)
    o_refs[-1][...] = (y * g_ref[...]).astype(o_refs[-1].dtype)
    if emit_residual:
        o_refs[0][...] = x


def out_proj_norm(a0, a0_blk, a1, a1_blk, w, x2d, g, norm_dtype, emit_residual, tm=512, name="out_proj"):
    m, n = x2d.shape
    kh = w.shape[0] // 2
    row = pl.BlockSpec((tm, n), lambda i: (i, 0))
    out_shape = [jax.ShapeDtypeStruct((m, n), norm_dtype)]
    if emit_residual:
        out_shape.insert(0, jax.ShapeDtypeStruct((m, n), F32))
    return pl.pallas_call(
        functools.partial(_out_proj_norm_kernel, kh=kh, emit_residual=emit_residual),
        out_shape=tuple(out_shape),
        grid=(m // tm,),
        in_specs=[pl.BlockSpec((tm, kh), lambda i: (i, a0_blk)),
                  pl.BlockSpec((tm, kh), lambda i: (i, a1_blk)),
                  pl.BlockSpec(w.shape, lambda i: (0, 0), pipeline_mode=pl.Buffered(1)),
                  row,
                  pl.BlockSpec((1, n), lambda i: (0, 0))],
        out_specs=tuple(row for _ in out_shape),
        compiler_params=_cparams(1),
        name=name,
    )(a0, a1, w, x2d, g.reshape(1, n))


def _sgu_v_kernel(h_ref, w_ref, v_ref, mu_ref, rstd_ref, wb_ref, c_ref, s1_ref, s2_ref, *, n_col_tiles):
    j = pl.program_id(1)

    @pl.when(pl.program_id(0) == 0)
    def _():
        wb_ref[j] = w_ref[...].astype(BF16)

    tm, tn = v_ref.shape
    h = h_ref[...]
    for c0 in range(0, tn, V_SUB):
        v = jnp.dot(h, wb_ref[j, :, c0:c0 + V_SUB], preferred_element_type=F32)
        v_ref[:, c0:c0 + V_SUB] = v

        if c0 == 0:
            @pl.when(j == 0)
            def _():
                c_ref[...] = jnp.broadcast_to(jnp.sum(v, axis=-1, keepdims=True) / V_SUB, c_ref.shape)
                s1_ref[...] = jnp.zeros(s1_ref.shape, F32)
                s2_ref[...] = jnp.zeros(s2_ref.shape, F32)

        c = c_ref[...]
        s1, s2 = s1_ref[...], s2_ref[...]
        for r in range(V_SUB // LANES):
            d = v[:, r * LANES:(r + 1) * LANES] - c
            s1 = s1 + d
            s2 = s2 + d * d
        s1_ref[...] = s1
        s2_ref[...] = s2

    @pl.when(j == n_col_tiles - 1)
    def _():
        width = n_col_tiles * tn
        mean_d = jnp.sum(s1, axis=-1, keepdims=True) / width
        var = jnp.sum(s2, axis=-1, keepdims=True) / width - mean_d * mean_d
        mu_ref[...] = c + mean_d
        rstd_ref[...] = jnp.broadcast_to(lax.rsqrt(var + EPS), rstd_ref.shape)


def sgu_v(h, w, col0, tm=512, tn=2 * V_SUB):
    m, k = h.shape
    nj = SGU_W // tn
    j0 = col0 // tn
    stat = jax.ShapeDtypeStruct((m, LANES), F32)
    stat_spec = pl.BlockSpec((tm, LANES), lambda i, j: (i, 0))
    return pl.pallas_call(
        functools.partial(_sgu_v_kernel, n_col_tiles=nj),
        out_shape=(jax.ShapeDtypeStruct((m, SGU_W), F32), stat, stat),
        grid=(m // tm, nj),
        in_specs=[pl.BlockSpec((tm, k), lambda i, j: (i, 0)),
                  pl.BlockSpec((k, tn), lambda i, j: (0, j0 + jnp.where(i == 0, j, 0)))],
        out_specs=(pl.BlockSpec((tm, tn), lambda i, j: (i, j)), stat_spec, stat_spec),
        scratch_shapes=[pltpu.VMEM((nj, k, tn), BF16)] + [pltpu.VMEM((tm, LANES), F32)] * 3,
        compiler_params=_cparams(2),
        name="sgu_v",
    )(h, w)


def _sgu_gate_kernel(h_ref, wu32_ref, wz32_ref, v_ref, mu_ref, rstd_ref, lg_ref, lb_ref, ws_ref, bs_ref,
                     side32_ref, o_ref, side_ref, wu_ref, wz_ref):
    _cast_once([wu32_ref, wz32_ref], [wu_ref, wz_ref])
    side_ref[...] = side32_ref[...].astype(BF16)
    h = h_ref[...]
    tm, tn = v_ref.shape
    reps = GROUP_W // LANES
    mu = jnp.concatenate([mu_ref[...]] * reps, axis=1)
    rstd = jnp.concatenate([rstd_ref[...]] * reps, axis=1)
    tri = (lax.broadcasted_iota(jnp.int32, (CHUNK, CHUNK), 1)
           <= lax.broadcasted_iota(jnp.int32, (CHUNK, CHUNK), 0))
    for gi in range(tn // GROUP_W):
        cols = slice(gi * GROUP_W, (gi + 1) * GROUP_W)
        u = jnp.dot(h, wu_ref[:, cols], preferred_element_type=F32)
        zg = jnp.dot(h, wz_ref[:, cols], preferred_element_type=F32)
        vn = ((v_ref[:, cols] - mu) * rstd * lg_ref[:, cols] + lb_ref[:, cols]).astype(BF16)
        wsm = jnp.where(tri, ws_ref[gi], 0.0).astype(BF16)
        bsg = bs_ref[gi]
        mix = jnp.concatenate(
            [jnp.dot(wsm, vn[c * CHUNK:(c + 1) * CHUNK], preferred_element_type=F32) + bsg
             for c in range(tm // CHUNK)], axis=0)
        o_ref[:, cols] = (u * mix * (zg * jax.nn.sigmoid(zg))).astype(o_ref.dtype)


def sgu_gate(h, w, col_u, col_z, v, mu, rstd, ln_g, ln_b, w_s, b_s, side, tm=1024, tn=512):
    m, k = h.shape
    n = SGU_W
    nj, ni = n // tn, m // tm
    ju, jz = col_u // tn, col_z // tn
    gpt = tn // GROUP_W
    stat_spec = pl.BlockSpec((tm, LANES), lambda j, i: (i, 0))
    side_rows = side.shape[0] // (nj * ni)
    assert side_rows * nj * ni == side.shape[0] and side_rows % HALO == 0
    side_spec = pl.BlockSpec((side_rows, side.shape[1]), lambda j, i: (j * ni + i, 0))
    return pl.pallas_call(
        _sgu_gate_kernel,
        out_shape=(jax.ShapeDtypeStruct((m, n), BF16), jax.ShapeDtypeStruct(side.shape, BF16)),
        grid=(nj, ni),
        in_specs=[pl.BlockSpec((tm, k), lambda j, i: (i, 0)),
                  pl.BlockSpec((k, tn), lambda j, i: (0, ju + j)),
                  pl.BlockSpec((k, tn), lambda j, i: (0, jz + j)),
                  pl.BlockSpec((tm, tn), lambda j, i: (i, j)),
                  stat_spec, stat_spec,
                  pl.BlockSpec((1, tn), lambda j, i: (0, j)),
                  pl.BlockSpec((1, tn), lambda j, i: (0, j)),
                  pl.BlockSpec((gpt, CHUNK, CHUNK), lambda j, i: (j, 0, 0)),
                  pl.BlockSpec((gpt, CHUNK, 1), lambda j, i: (j, 0, 0)),
                  side_spec],
        out_specs=(pl.BlockSpec((tm, tn), lambda j, i: (i, j)), side_spec),
        scratch_shapes=[pltpu.VMEM((k, tn), BF16)] * 2,
        compiler_params=_cparams(2),
        name="sgu_gate",
    )(h, w, w, v, mu, rstd, ln_g.reshape(1, n), ln_b.reshape(1, n), w_s, b_s.reshape(N_GROUPS, CHUNK, 1),
      side)


def _rotary_tables(seq):
    half = ROT_DIM // 2
    inv_freq = jnp.power(ROPE_THETA, -jnp.arange(half, dtype=F32) * 2.0 / ROT_DIM)
    ang = jnp.arange(seq).astype(F32)[:, None] * inv_freq[None, :]
    cos, sin = jnp.cos(ang), jnp.sin(ang)
    rest = HEAD_DIM - ROT_DIM
    c = jnp.concatenate([cos, cos, jnp.ones((seq, rest), F32)], axis=1)
    sa = jnp.concatenate([jnp.zeros((seq, half), F32), sin, jnp.zeros((seq, rest), F32)], axis=1)
    sb = jnp.concatenate([-sin, jnp.zeros((seq, half + rest), F32)], axis=1)
    rot_t = jnp.concatenate([cos, cos, -sin, sin], axis=1).T
    return (c, sa, sb), rot_t


def kernel(x, norm_even, w_in_even, conv_w, cmp_k_pos, cmp_k_w1, cmp_k_b1, cmp_k_w2, cmp_v_pos, cmp_v_w1, cmp_v_b1, cmp_v_w2, w_out_even, norm_odd, w_in_odd, sgu_ln_g, sgu_ln_b, sgu_w_s, sgu_b_s, w_out_odd, norm_final):
    batch, seq, d = x.shape
    m = batch * seq
    x2d = x.reshape(m, d)

    wt_in = jnp.swapaxes(w_in_even[0], 0, 1)
    cw = CONV_W
    qw = N_HEADS * HEAD_DIM
    kvw = N_KV * HEAD_DIM
    o_q = 4 * cw
    o_kv = o_q + qw
    o_gl = o_kv + 6 * kvw
    o_ng = o_gl + 3 * N_HEADS

    h0, gl = rmsnorm_proj(x2d, norm_even[0], wt_in, o_gl, LANES)
    y_conv, w_out0 = conv_proj(h0, wt_in, conv_w[0], seq, w_out_even[0])
    z = matmul_t(h0, wt_in, o_q, qw + 6 * kvw, tn=1024, name="nsa_proj")
    z_ng = matmul_t(h0, wt_in, o_ng, qw, tn=1024, name="nsa_gate_proj")

    tabs, rot_t = _rotary_tables(seq)
    cmp_k = (cmp_k_pos[0], cmp_k_w1[0].astype(BF16).reshape(CMP_LEN, HEAD_DIM, HEAD_DIM),
             cmp_k_b1[0].reshape(1, HEAD_DIM), cmp_k_w2[0].astype(BF16))
    cmp_v = (cmp_v_pos[0], cmp_v_w1[0].astype(BF16).reshape(CMP_LEN, HEAD_DIM, HEAD_DIM),
             cmp_v_b1[0].reshape(1, HEAD_DIM), cmp_v_w2[0].astype(BF16))
    prep = nsa_prep(z, tabs, cmp_k, cmp_v, batch, seq, col0=qw // HEAD_DIM)
    y_nsa = nsa_attention_pair(z, z_ng, gl, rot_t, prep, batch, seq)

    x1, h1 = out_proj_norm(y_conv, 0, y_nsa, 0, w_out0, x2d, norm_odd[0],
                           BF16, True, name="out_proj_even")

    w_in1 = w_in_odd[0]
    v, mu, rstd = sgu_v(h1, w_in1, SGU_W)
    act, w_out1 = sgu_gate(h1, w_in1, 0, 2 * SGU_W, v, mu, rstd,
                           sgu_ln_g[0], sgu_ln_b[0], sgu_w_s[0], sgu_b_s[0], w_out_odd[0])
    (out,) = out_proj_norm(act, 0, act, 1, w_out1, x1, norm_final,
                           F32, False, name="out_proj_odd")
    return out.reshape(batch, seq, d)
```

```python
import functools
import math

import jax
import jax.numpy as jnp
from jax import lax
from jax.experimental import pallas as pl
from jax.experimental.pallas import tpu as pltpu

F32 = jnp.float32
BF16 = jnp.bfloat16

D_MODEL = 2048
MIX = 2 * D_MODEL
CONV_W = MIX // 2
CONV_K = 3
HEAD_DIM = 128
N_HEADS = 16
N_KV = 4
GQA = N_HEADS // N_KV
ROT_DIM = HEAD_DIM // 4
ROPE_THETA = 500000.0
CMP_LEN = 32
CMP_STRIDE = 16
SLC_LEN = 64
N_SEL = 8
CMP_PER_SLC = SLC_LEN // CMP_STRIDE
CMP_BACK = CMP_LEN // CMP_STRIDE - 1
WINDOW = 512
SGU_W = MIX
CHUNK = 128
N_GROUPS = 16
GROUP_W = SGU_W // N_GROUPS
EPS = 1e-6

LANES = 128
SUBLANES = 8
SLC_SHIFT = 6
HALO = 16
KEY_TILE = LANES
V_SUB = 512
XPOSE_ROWS = 256
VMEM_LIMIT = 56 * 1024 * 1024

NEG = -1e30

EXT_DIM = 2 * HEAD_DIM


def _cparams(n_axes):
    return pltpu.CompilerParams(
        dimension_semantics=("arbitrary",) * n_axes, vmem_limit_bytes=VMEM_LIMIT)


def _rmsnorm_proj_kernel(x_ref, g_ref, wt_ref, o_ref, p_ref, wb_ref):
    @pl.when(pl.program_id(0) == 0)
    def _():
        wb_ref[...] = wt_ref[...].T.astype(BF16)

    x = x_ref[...]
    y = x * lax.rsqrt(jnp.mean(x * x, axis=-1, keepdims=True) + EPS)
    hb = (y * g_ref[...]).astype(BF16)
    o_ref[...] = hb
    p_ref[...] = jnp.dot(hb, wb_ref[...], preferred_element_type=F32)


def rmsnorm_proj(x2d, g, wt, row0, n, tm=512):
    m, d = x2d.shape
    assert row0 % n == 0
    return pl.pallas_call(
        _rmsnorm_proj_kernel,
        out_shape=(jax.ShapeDtypeStruct((m, d), BF16), jax.ShapeDtypeStruct((m, n), F32)),
        grid=(m // tm,),
        in_specs=[pl.BlockSpec((tm, d), lambda i: (i, 0)),
                  pl.BlockSpec((1, d), lambda i: (0, 0)),
                  pl.BlockSpec((n, d), lambda i: (row0 // n, 0))],
        out_specs=(pl.BlockSpec((tm, d), lambda i: (i, 0)),
                   pl.BlockSpec((tm, n), lambda i: (i, 0))),
        scratch_shapes=[pltpu.VMEM((d, n), BF16)],
        compiler_params=_cparams(1),
        name="rmsnorm_gates",
    )(x2d, g.reshape(1, d), wt)


def _cast_once(w_refs, wb_refs, transposed=False):
    @pl.when(pl.program_id(1) == 0)
    def _():
        for w_ref, wb_ref in zip(w_refs, wb_refs):
            if transposed:
                for r in range(0, w_ref.shape[0], XPOSE_ROWS):
                    wb_ref[:, r:r + XPOSE_ROWS] = w_ref[r:r + XPOSE_ROWS, :].T.astype(BF16)
            else:
                wb_ref[...] = w_ref[...].astype(BF16)


def _matmul_t_kernel(a_ref, wt_ref, o_ref, wb_ref):
    _cast_once([wt_ref], [wb_ref], transposed=True)
    o_ref[...] = jnp.dot(a_ref[...], wb_ref[...], preferred_element_type=F32)


def matmul_t(a, wt, row0, n, tm=1024, tn=512, name="proj"):
    m, k = a.shape
    assert n % tn == 0 and row0 % HALO == 0
    if row0 % tn == 0:
        wspec = pl.BlockSpec((tn, k), lambda j, i: (row0 // tn + j, 0))
    else:
        wspec = pl.BlockSpec((pl.Element(tn), pl.Element(k)),
                             lambda j, i: (pl.multiple_of(row0 + j * tn, HALO), 0))
    return pl.pallas_call(
        _matmul_t_kernel,
        out_shape=jax.ShapeDtypeStruct((m, n), F32),
        grid=(n // tn, m // tm),
        in_specs=[pl.BlockSpec((tm, k), lambda j, i: (i, 0)), wspec],
        out_specs=pl.BlockSpec((tm, tn), lambda j, i: (i, j)),
        scratch_shapes=[pltpu.VMEM((k, tn), BF16)],
        compiler_params=_cparams(2),
        name=name,
    )(a, wt)


def _conv_proj_kernel(h_ref, hp_ref, wb32_ref, wc32_ref, wh32_ref, wg32_ref, cw_ref, side32_ref,
                      o_ref, side_ref, wb_ref, wc_ref, wh_ref, wg_ref, *, tiles_per_seq):
    _cast_once([wb32_ref, wc32_ref, wh32_ref, wg32_ref], [wb_ref, wc_ref, wh_ref, wg_ref],
               transposed=True)
    side_ref[...] = side32_ref[...].astype(BF16)
    i = pl.program_id(1)
    cw = cw_ref[...]
    hp = hp_ref[...]
    pp = (jnp.dot(hp, wc_ref[...], preferred_element_type=F32)
          * jnp.dot(hp, wh_ref[...], preferred_element_type=F32))
    pp = jnp.where(i % tiles_per_seq == 0, 0.0, pp)
    prev1, prev2 = pp[HALO - 1:HALO, :], pp[HALO - 2:HALO - 1, :]
    h = h_ref[...]
    cb = jnp.dot(h, wb_ref[...], preferred_element_type=F32)
    cc = jnp.dot(h, wc_ref[...], preferred_element_type=F32)
    ch = jnp.dot(h, wh_ref[...], preferred_element_type=F32)
    cg = jnp.dot(h, wg_ref[...], preferred_element_type=F32)
    p = cc * ch
    row = lax.broadcasted_iota(jnp.int32, (SUBLANES, p.shape[1]), 0)
    p1, p2 = pltpu.roll(p, 1, axis=0), pltpu.roll(p, 2, axis=0)
    p1 = jnp.concatenate([jnp.where(row == 0, prev1, p1[:SUBLANES]), p1[SUBLANES:]], axis=0)
    p2 = jnp.concatenate([jnp.where(row == 0, prev2, jnp.where(row == 1, prev1, p2[:SUBLANES])),
                          p2[SUBLANES:]], axis=0)
    conv = cw[0:1, :] * p2 + cw[1:2, :] * p1 + cw[2:3, :] * p
    o_ref[...] = (cb * conv * (cg * jax.nn.sigmoid(cg))).astype(o_ref.dtype)


def conv_proj(h, wt, conv_w, seq, side, tm=1024, tn=256):
    m, k = h.shape
    n = conv_w.shape[1]
    tiles_per_seq = seq // tm
    halo_per_tile = tm // HALO
    nj, ni = n // tn, m // tm
    side_rows = side.shape[0] // (nj * ni)
    assert side_rows * nj * ni == side.shape[0] and side_rows % HALO == 0
    side_spec = pl.BlockSpec((side_rows, side.shape[1]), lambda j, i: (j * ni + i, 0))

    def wspec(which):
        return pl.BlockSpec((tn, k), lambda j, i: (which * nj + j, 0))

    return pl.pallas_call(
        functools.partial(_conv_proj_kernel, tiles_per_seq=tiles_per_seq),
        out_shape=(jax.ShapeDtypeStruct((m, n), BF16), jax.ShapeDtypeStruct(side.shape, BF16)),
        grid=(nj, ni),
        in_specs=[pl.BlockSpec((tm, k), lambda j, i: (i, 0)),
                  pl.BlockSpec((HALO, k), lambda j, i: (jnp.maximum(i * halo_per_tile - 1, 0), 0)),
                  wspec(0), wspec(1), wspec(2), wspec(3),
                  pl.BlockSpec((CONV_K, tn), lambda j, i: (0, j)),
                  side_spec],
        out_specs=(pl.BlockSpec((tm, tn), lambda j, i: (i, j)), side_spec),
        scratch_shapes=[pltpu.VMEM((k, tn), BF16)] * 4,
        compiler_params=_cparams(2),
        name="conv_proj",
    )(h, h, wt, wt, wt, wt, conv_w, side)


def _rotary(x, c, sa, sb):
    half = ROT_DIM // 2
    return (x * c + pltpu.roll(x, half, axis=1) * sa
            + pltpu.roll(x, HEAD_DIM - half, axis=1) * sb)


def _compress(src_ref, pos_ref, w1_ref, b1_ref, w2_ref):
    n_rows = src_ref.shape[0] // CMP_STRIDE
    p_acc = jnp.zeros((n_rows, HEAD_DIM), F32)
    q_acc = jnp.zeros((n_rows, HEAD_DIM), F32)
    for r in range(CMP_STRIDE):
        s_r = src_ref[pl.ds(r, n_rows, stride=CMP_STRIDE), :]
        a_r = (s_r + pos_ref[r:r + 1, :]).astype(BF16)
        b_r = (s_r + pos_ref[CMP_STRIDE + r:CMP_STRIDE + r + 1, :]).astype(BF16)
        p_acc += jnp.dot(a_r, w1_ref[r], preferred_element_type=F32)
        q_acc += jnp.dot(b_r, w1_ref[CMP_STRIDE + r], preferred_element_type=F32)
    hid = p_acc + pltpu.roll(q_acc, n_rows - 1, axis=0) + b1_ref[...]
    act = (hid * jax.nn.sigmoid(hid)).astype(BF16)
    return jnp.dot(act, w2_ref[...], preferred_element_type=F32)


def _nsa_prep_kernel(kc_ref, vc_ref, ks_ref, vs_ref, kw_ref, vw_ref, c_ref, sa_ref, sb_ref,
                     kpos_ref, kw1_ref, kb1_ref, kw2_ref, vpos_ref, vw1_ref, vb1_ref, vw2_ref,
                     ksx_ref, vst_ref, kwx_ref, vwt_ref, kcmp_ref, vcmp_ref):
    c, sa, sb = c_ref[...], sa_ref[...], sb_ref[...]
    seq = ks_ref.shape[0]
    n_blk = seq // SLC_LEN
    flag_w = EXT_DIM - HEAD_DIM
    blk = lax.broadcasted_iota(jnp.int32, (seq, flag_w), 0) >> SLC_SHIFT
    onehot = jnp.where(blk == lax.broadcasted_iota(jnp.int32, (seq, flag_w), 1), 1.0, 0.0)
    ksx_ref[:, 0:HEAD_DIM] = _rotary(ks_ref[...], c, sa, sb).astype(BF16)
    ksx_ref[:, HEAD_DIM:] = onehot.astype(BF16)
    pad_flag = jnp.where(lax.broadcasted_iota(jnp.int32, (WINDOW, EXT_DIM), 1) == HEAD_DIM + n_blk, 1.0, 0.0)
    kwx_ref[0:WINDOW, :] = pad_flag.astype(BF16)
    kwx_ref[WINDOW:, 0:HEAD_DIM] = _rotary(kw_ref[...], c, sa, sb).astype(BF16)
    kwx_ref[WINDOW:, HEAD_DIM:] = jnp.zeros((seq, flag_w), BF16)
    n_tiles = seq // KEY_TILE
    pad_tiles = WINDOW // KEY_TILE
    for t in range(pad_tiles):
        vwt_ref[t] = jnp.zeros((HEAD_DIM, KEY_TILE), BF16)
    for t in range(n_tiles):
        rows = pl.ds(t * KEY_TILE, KEY_TILE)
        vst_ref[t] = vs_ref[rows, :].T.astype(BF16)
        vwt_ref[pad_tiles + t] = vw_ref[rows, :].T.astype(BF16)
    kcmp_ref[...] = _compress(kc_ref, kpos_ref, kw1_ref, kb1_ref, kw2_ref).astype(BF16)
    vcmp_ref[...] = _compress(vc_ref, vpos_ref, vw1_ref, vb1_ref, vw2_ref).astype(BF16)


def nsa_prep(z, tabs, cmp_k, cmp_v, batch, seq, col0):
    def zspec(which):
        return pl.BlockSpec((seq, HEAD_DIM), lambda b, g, w=which: (b, col0 + w * N_KV + g))

    tab = pl.BlockSpec((seq, HEAD_DIM), lambda b, g: (0, 0))

    def wspecs():
        return [pl.BlockSpec((CMP_LEN, HEAD_DIM), lambda b, g: (0, 0)),
                pl.BlockSpec((CMP_LEN, HEAD_DIM, HEAD_DIM), lambda b, g: (0, 0, 0)),
                pl.BlockSpec((1, HEAD_DIM), lambda b, g: (0, 0)),
                pl.BlockSpec((HEAD_DIM, HEAD_DIM), lambda b, g: (0, 0))]

    def out(shape):
        nd = len(shape)
        spec = pl.BlockSpec((None, None) + shape, lambda b, g: (b, g) + (0,) * nd)
        return spec, jax.ShapeDtypeStruct((batch, N_KV) + shape, BF16)

    n_tiles = seq // KEY_TILE
    pad_tiles = WINDOW // KEY_TILE
    outs = [out((seq, EXT_DIM)), out((n_tiles, HEAD_DIM, KEY_TILE)),
            out((WINDOW + seq, EXT_DIM)), out((pad_tiles + n_tiles, HEAD_DIM, KEY_TILE)),
            out((seq // CMP_STRIDE, HEAD_DIM)), out((seq // CMP_STRIDE, HEAD_DIM))]
    return pl.pallas_call(
        _nsa_prep_kernel,
        out_shape=tuple(o[1] for o in outs),
        grid=(batch, N_KV),
        in_specs=[zspec(0), zspec(1), zspec(2), zspec(3), zspec(4), zspec(5), tab, tab, tab]
        + wspecs() + wspecs(),
        out_specs=tuple(o[0] for o in outs),
        compiler_params=_cparams(2),
        name="nsa_prep",
    )(z, z, z, z, z, z, *tabs, *cmp_k, *cmp_v)


def _nsa_attn_pair_kernel(q_ref, ng_ref, gl_ref, rot_ref,
                          ksx_ref, vst_ref, kwx_ref, vwt_ref, kcmp_ref, vcmp_ref,
                          o_ref, acc_ref, qx_ref, glt_ref, s_ref, m_ref, l_ref, oc_ref, ps_ref, *,
                          tq, n_grp_step, n_blk):
    gp = pl.program_id(1)
    i = pl.program_id(2)
    q0 = i * tq
    half = ROT_DIM // 2
    qscale = (HEAD_DIM ** -0.5) * math.log2(math.e)
    qw = GQA * HEAD_DIM
    groups = range(n_grp_step)

    def head_lanes(n):
        return slice(n * tq, (n + 1) * tq)

    def lanes4(x):
        return jnp.concatenate([x] * GQA, axis=1)

    rot = rot_ref[...]
    cos2, sin2 = rot[:ROT_DIM], rot[ROT_DIM:]
    tlane = q0 + lax.broadcasted_iota(jnp.int32, (1, tq), 1)
    kk = lax.broadcasted_iota(jnp.int32, (tq, tq), 0)
    tt = lax.broadcasted_iota(jnp.int32, (tq, tq), 1)
    tri_diag = jnp.where(kk <= tt, 0.0, NEG)
    tri_old = jnp.where(kk > tt, 0.0, NEG)
    n_cmp = kcmp_ref.shape[1]
    cend = lax.broadcasted_iota(jnp.int32, (n_cmp, tq), 0) * CMP_STRIDE + (CMP_LEN - 1)
    bias_c = lanes4(jnp.where(cend <= tlane, 0.0, NEG))
    any_c = lanes4(tlane) >= CMP_LEN - 1
    jblk = lax.broadcasted_iota(jnp.int32, (n_blk, tq), 0)
    forced = (jblk == 0) | (jblk == (tlane >> SLC_SHIFT))
    causal_blk = jblk * SLC_LEN <= tlane
    sub = lax.broadcasted_iota(jnp.int32, (SUBLANES, tq), 0)
    n_rest = EXT_DIM - HEAD_DIM - n_blk
    pad_rows = jnp.where(lax.broadcasted_iota(jnp.int32, (n_rest, GQA * tq), 0) == 0, NEG, 0.0).astype(BF16)
    tiles = tq // KEY_TILE

    def sel_scores(gg, ci):
        k0 = pl.multiple_of(ci * tq, tq)
        return jnp.dot(ksx_ref[gg, pl.ds(k0, tq), :], qx_ref[gg], preferred_element_type=F32)

    def online_update(sid, vblk, s):
        m_old = m_ref[sid]
        m_new = jnp.maximum(m_old, jnp.max(s, axis=0, keepdims=True))
        alpha = jnp.exp2(m_old - m_new)
        p = jnp.exp2(s - m_new)
        l_ref[sid] = alpha * l_ref[sid] + jnp.sum(p, axis=0, keepdims=True)
        acc_ref[sid] = alpha * acc_ref[sid] + jnp.dot(vblk, p.astype(BF16), preferred_element_type=F32)
        m_ref[sid] = m_new

    def init_state(sid):
        m_ref[sid] = jnp.full(m_ref.shape[1:], NEG, F32)
        l_ref[sid] = jnp.zeros(l_ref.shape[1:], F32)
        acc_ref[sid] = jnp.zeros(acc_ref.shape[1:], F32)

    def sel_update(gg, ci, s):
        online_update(gg, jnp.concatenate([vst_ref[gg, ci * tiles + r] for r in range(tiles)], axis=1), s)

    for gg in groups:
        q = q_ref[:, gg * qw:(gg + 1) * qw]
        qts = []
        for n in range(GQA):
            qt = (q[:, n * HEAD_DIM:(n + 1) * HEAD_DIM] * qscale).T
            qts.append(qt)
            top = qt[:ROT_DIM]
            swapped = jnp.concatenate([top[half:], top[:half]], axis=0)
            qrt = jnp.concatenate([top * cos2 + swapped * sin2, qt[ROT_DIM:]], axis=0)
            qx_ref[gg, 0:HEAD_DIM, head_lanes(n)] = qrt.astype(BF16)
        qt_all = jnp.concatenate(qts, axis=1).astype(BF16)

        s_c = jnp.dot(kcmp_ref[gg], qt_all, preferred_element_type=F32) + bias_c
        m_c = jnp.where(any_c, jnp.max(s_c, axis=0, keepdims=True), 0.0)
        e_c = jnp.exp2(s_c - m_c)
        den_c = jnp.sum(e_c, axis=0, keepdims=True)
        p_c = e_c * (1.0 / jnp.where(den_c > 0, den_c, 1.0))
        vcmp_t = vcmp_ref[gg].astype(F32).T.astype(BF16)
        oc_ref[gg] = jnp.dot(vcmp_t, p_c.astype(BF16), preferred_element_type=F32)

        p_sum = p_c[:, head_lanes(0)]
        for n in range(1, GQA):
            p_sum = p_sum + p_c[:, head_lanes(n)]
        for t in range(tq // LANES):
            ps_ref[gg, t] = p_sum[:, t * LANES:(t + 1) * LANES]
        taps = [jnp.concatenate([ps_ref[gg, t, pl.ds(k, n_blk, stride=CMP_PER_SLC), :]
                                 for t in range(tq // LANES)], axis=1) for k in range(CMP_PER_SLC)]
        imp = taps[0]
        for k in range(1, CMP_PER_SLC):
            imp = imp + taps[k]
        for k in range(CMP_BACK):
            older = pltpu.roll(taps[CMP_PER_SLC - 1 - k], 1, axis=0)
            imp = imp + jnp.where(jblk == 0, 0.0, older)
        imp = jnp.where(forced, jnp.inf, jnp.where(causal_blk, imp, -jnp.inf))
        n_sub = n_blk // SUBLANES
        part = [imp[a * SUBLANES:(a + 1) * SUBLANES] for a in range(n_sub)]
        cnt = [jnp.zeros((SUBLANES, tq), F32) for _ in range(n_sub)]
        for r in range(n_blk):
            row = imp[r:r + 1, :]
            a_r, r_in = divmod(r, SUBLANES)
            for a in range(n_sub):
                if a > a_r:
                    cnt[a] = cnt[a] + jnp.where(row >= part[a], 1.0, 0.0)
                elif a < a_r:
                    cnt[a] = cnt[a] + jnp.where(row > part[a], 1.0, 0.0)
                else:
                    cnt[a] = cnt[a] + jnp.where(sub > r_in, jnp.where(row >= part[a], 1.0, 0.0),
                                                jnp.where(row > part[a], 1.0, 0.0))
        cnt = jnp.concatenate(cnt, axis=0)
        sel_bias = jnp.where(cnt < N_SEL, jnp.where(imp > -jnp.inf, 0.0, NEG), NEG)
        qx_ref[gg, HEAD_DIM:HEAD_DIM + n_blk, :] = lanes4(sel_bias).astype(BF16)
        qx_ref[gg, HEAD_DIM + n_blk:, :] = pad_rows

        init_state(gg)
        s_ref[gg, 0] = sel_scores(gg, 0)

    def sel_pair(pi, carry):
        c0 = 2 * pi
        for gg in groups:
            s_ref[gg, 1] = sel_scores(gg, c0 + 1)
        for gg in groups:
            sel_update(gg, c0, s_ref[gg, 0])
        for gg in groups:
            s_ref[gg, 0] = sel_scores(gg, c0 + 2)
        for gg in groups:
            sel_update(gg, c0 + 1, s_ref[gg, 1])
        return carry

    lax.fori_loop(0, i // 2, sel_pair, 0)

    @pl.when(i % 2 == 1)
    def _():
        for gg in groups:
            s_ref[gg, 1] = sel_scores(gg, i)
        for gg in groups:
            sel_update(gg, i - 1, s_ref[gg, 0])

    for gg in groups:
        sel_update(gg, i, s_ref[gg, i % 2] + lanes4(tri_diag))

    n_wt = WINDOW // tq
    vt0 = q0 // KEY_TILE

    def win_scores(gg, t):
        rows = pl.ds(pl.multiple_of(q0 + t * tq, tq), tq)
        return jnp.dot(kwx_ref[gg, rows, :], qx_ref[gg], preferred_element_type=F32)

    for gg in groups:
        init_state(n_grp_step + gg)
        s_ref[gg, 0] = win_scores(gg, 0)
    for t in range(n_wt + 1):
        if t < n_wt:
            for gg in groups:
                s_ref[gg, (t + 1) % 2] = win_scores(gg, t + 1)
        for gg in groups:
            s = s_ref[gg, t % 2]
            if t == 0:
                s = s + lanes4(tri_old)
            if t == n_wt:
                s = s + lanes4(tri_diag)
            vblk = jnp.concatenate([vwt_ref[gg, vt0 + t * tiles + r] for r in range(tiles)], axis=1)
            online_update(n_grp_step + gg, vblk, s)

    glt_ref[...] = jax.nn.sigmoid(gl_ref[...]).T
    for gg in groups:
        inv_w = 1.0 / l_ref[n_grp_step + gg]
        pv_w = acc_ref[n_grp_step + gg]
        inv_s = 1.0 / l_ref[gg]
        pv_s = acc_ref[gg]
        o_c = oc_ref[gg]
        ng = ng_ref[:, gg * qw:(gg + 1) * qw]
        for n in range(GQA):
            head = (gp * n_grp_step + gg) * GQA + n
            gc, gs, gw = (glt_ref[pl.ds(j * N_HEADS + head, 1), :] for j in range(3))
            sl = head_lanes(n)
            o = (gc * o_c[:, sl] + (gs * inv_s[:, sl]) * pv_s[:, sl]
                 + (gw * inv_w[:, sl]) * pv_w[:, sl]).T
            ngh = ng[:, n * HEAD_DIM:(n + 1) * HEAD_DIM]
            col = gg * qw + n * HEAD_DIM
            o_ref[:, col:col + HEAD_DIM] = (o * (ngh * jax.nn.sigmoid(ngh))).astype(o_ref.dtype)


def nsa_attention_pair(z, z_ng, gl, rot_t, prep, batch, seq, tq=256, n_grp_step=2):
    ksx, vst, kwx, vwt, kcmp, vcmp = prep
    assert tq % KEY_TILE == 0 and WINDOW % tq == 0 and WINDOW >= 2 * tq and N_KV % n_grp_step == 0
    n_blk, n_cmp_rows = seq // SLC_LEN, kcmp.shape[2]
    assert n_cmp_rows == CMP_PER_SLC * n_blk
    nq = seq // tq
    qw = GQA * HEAD_DIM * n_grp_step

    def whole(a):
        nd = a.ndim - 2
        return pl.BlockSpec((None, n_grp_step) + a.shape[2:], lambda b, g, i: (b, g) + (0,) * nd)

    lanes = GQA * tq
    return pl.pallas_call(
        functools.partial(_nsa_attn_pair_kernel, tq=tq, n_grp_step=n_grp_step, n_blk=n_blk),
        out_shape=jax.ShapeDtypeStruct((batch * seq, N_HEADS * HEAD_DIM), BF16),
        grid=(batch, N_KV // n_grp_step, nq),
        in_specs=[pl.BlockSpec((tq, qw), lambda b, g, i: (b * nq + i, g)),
                  pl.BlockSpec((tq, qw), lambda b, g, i: (b * nq + i, g)),
                  pl.BlockSpec((tq, gl.shape[1]), lambda b, g, i: (b * nq + i, 0)),
                  pl.BlockSpec((rot_t.shape[0], tq), lambda b, g, i: (0, i)),
                  whole(ksx), whole(vst), whole(kwx), whole(vwt), whole(kcmp), whole(vcmp)],
        out_specs=pl.BlockSpec((tq, qw), lambda b, g, i: (b * nq + i, g)),
        scratch_shapes=[pltpu.VMEM((2 * n_grp_step, HEAD_DIM, lanes), F32),
                        pltpu.VMEM((n_grp_step, EXT_DIM, lanes), BF16),
                        pltpu.VMEM((gl.shape[1], tq), F32),
                        pltpu.VMEM((n_grp_step, 2, tq, lanes), F32),
                        pltpu.VMEM((2 * n_grp_step, 1, lanes), F32),
                        pltpu.VMEM((2 * n_grp_step, 1, lanes), F32),
                        pltpu.VMEM((n_grp_step, HEAD_DIM, lanes), F32),
                        pltpu.VMEM((n_grp_step, tq // LANES, n_cmp_rows, LANES), F32)],
        compiler_params=_cparams(3),
        name="nsa_attention",
    )(z, z_ng, gl, rot_t, ksx, vst, kwx, vwt, kcmp, vcmp)


def _out_proj_norm_kernel(a0_ref, a1_ref, w_ref, x_ref, g_ref, *o_refs, kh, emit_residual):
    acc = jnp.dot(a0_ref[...], w_ref[0:kh, :], preferred_element_type=F32)
    acc += jnp.dot(a1_ref[...], w_ref[kh:, :], preferred_element_type=F32)
    x = x_ref[...] + acc
    y = x * lax.rsqrt(jnp.mean(x * x, axis=-1, keepdims=True) + EPS)
    o_refs[-1][...] = (y * g_ref[...]).astype(o_refs[-1].dtype)
    if emit_residual:
        o_refs[0][...] = x


def out_proj_norm(a0, a0_blk, a1, a1_blk, w, x2d, g, norm_dtype, emit_residual, tm=512, name="out_proj"):
    m, n = x2d.shape
    kh = w.shape[0] // 2
    row = pl.BlockSpec((tm, n), lambda i: (i, 0))
    out_shape = [jax.ShapeDtypeStruct((m, n), norm_dtype)]
    if emit_residual:
        out_shape.insert(0, jax.ShapeDtypeStruct((m, n), F32))
    return pl.pallas_call(
        functools.partial(_out_proj_norm_kernel, kh=kh, emit_residual=emit_residual),
        out_shape=tuple(out_shape),
        grid=(m // tm,),
        in_specs=[pl.BlockSpec((tm, kh), lambda i: (i, a0_blk)),
                  pl.BlockSpec((tm, kh), lambda i: (i, a1_blk)),
                  pl.BlockSpec(w.shape, lambda i: (0, 0), pipeline_mode=pl.Buffered(1)),
                  row,
                  pl.BlockSpec((1, n), lambda i: (0, 0))],
        out_specs=tuple(row for _ in out_shape),
        compiler_params=_cparams(1),
        name=name,
    )(a0, a1, w, x2d, g.reshape(1, n))


def _sgu_v_kernel(h_ref, w_ref, v_ref, mu_ref, rstd_ref, wb_ref, c_ref, s1_ref, s2_ref, *, n_col_tiles):
    j = pl.program_id(1)

    @pl.when(pl.program_id(0) == 0)
    def _():
        wb_ref[j] = w_ref[...].astype(BF16)

    tm, tn = v_ref.shape
    h = h_ref[...]
    for c0 in range(0, tn, V_SUB):
        v = jnp.dot(h, wb_ref[j, :, c0:c0 + V_SUB], preferred_element_type=F32)
        v_ref[:, c0:c0 + V_SUB] = v

        if c0 == 0:
            @pl.when(j == 0)
            def _():
                c_ref[...] = jnp.broadcast_to(jnp.sum(v, axis=-1, keepdims=True) / V_SUB, c_ref.shape)
                s1_ref[...] = jnp.zeros(s1_ref.shape, F32)
                s2_ref[...] = jnp.zeros(s2_ref.shape, F32)

        c = c_ref[...]
        s1, s2 = s1_ref[...], s2_ref[...]
        for r in range(V_SUB // LANES):
            d = v[:, r * LANES:(r + 1) * LANES] - c
            s1 = s1 + d
            s2 = s2 + d * d
        s1_ref[...] = s1
        s2_ref[...] = s2

    @pl.when(j == n_col_tiles - 1)
    def _():
        width = n_col_tiles * tn
        mean_d = jnp.sum(s1, axis=-1, keepdims=True) / width
        var = jnp.sum(s2, axis=-1, keepdims=True) / width - mean_d * mean_d
        mu_ref[...] = c + mean_d
        rstd_ref[...] = jnp.broadcast_to(lax.rsqrt(var + EPS), rstd_ref.shape)


def sgu_v(h, w, col0, tm=512, tn=2 * V_SUB):
    m, k = h.shape
    nj = SGU_W // tn
    j0 = col0 // tn
    stat = jax.ShapeDtypeStruct((m, LANES), F32)
    stat_spec = pl.BlockSpec((tm, LANES), lambda i, j: (i, 0))
    return pl.pallas_call(
        functools.partial(_sgu_v_kernel, n_col_tiles=nj),
        out_shape=(jax.ShapeDtypeStruct((m, SGU_W), F32), stat, stat),
        grid=(m // tm, nj),
        in_specs=[pl.BlockSpec((tm, k), lambda i, j: (i, 0)),
                  pl.BlockSpec((k, tn), lambda i, j: (0, j0 + jnp.where(i == 0, j, 0)))],
        out_specs=(pl.BlockSpec((tm, tn), lambda i, j: (i, j)), stat_spec, stat_spec),
        scratch_shapes=[pltpu.VMEM((nj, k, tn), BF16)] + [pltpu.VMEM((tm, LANES), F32)] * 3,
        compiler_params=_cparams(2),
        name="sgu_v",
    )(h, w)


def _sgu_gate_kernel(h_ref, wu32_ref, wz32_ref, v_ref, mu_ref, rstd_ref, lg_ref, lb_ref, ws_ref, bs_ref,
                     side32_ref, o_ref, side_ref, wu_ref, wz_ref):
    _cast_once([wu32_ref, wz32_ref], [wu_ref, wz_ref])
    side_ref[...] = side32_ref[...].astype(BF16)
    h = h_ref[...]
    tm, tn = v_ref.shape
    reps = GROUP_W // LANES
    mu = jnp.concatenate([mu_ref[...]] * reps, axis=1)
    rstd = jnp.concatenate([rstd_ref[...]] * reps, axis=1)
    tri = (lax.broadcasted_iota(jnp.int32, (CHUNK, CHUNK), 1)
           <= lax.broadcasted_iota(jnp.int32, (CHUNK, CHUNK), 0))
    for gi in range(tn // GROUP_W):
        cols = slice(gi * GROUP_W, (gi + 1) * GROUP_W)
        u = jnp.dot(h, wu_ref[:, cols], preferred_element_type=F32)
        zg = jnp.dot(h, wz_ref[:, cols], preferred_element_type=F32)
        vn = ((v_ref[:, cols] - mu) * rstd * lg_ref[:, cols] + lb_ref[:, cols]).astype(BF16)
        wsm = jnp.where(tri, ws_ref[gi], 0.0).astype(BF16)
        bsg = bs_ref[gi]
        mix = jnp.concatenate(
            [jnp.dot(wsm, vn[c * CHUNK:(c + 1) * CHUNK], preferred_element_type=F32) + bsg
             for c in range(tm // CHUNK)], axis=0)
        o_ref[:, cols] = (u * mix * (zg * jax.nn.sigmoid(zg))).astype(o_ref.dtype)


def sgu_gate(h, w, col_u, col_z, v, mu, rstd, ln_g, ln_b, w_s, b_s, side, tm=1024, tn=512):
    m, k = h.shape
    n = SGU_W
    nj, ni = n // tn, m // tm
    ju, jz = col_u // tn, col_z // tn
    gpt = tn // GROUP_W
    stat_spec = pl.BlockSpec((tm, LANES), lambda j, i: (i, 0))
    side_rows = side.shape[0] // (nj * ni)
    assert side_rows * nj * ni == side.shape[0] and side_rows % HALO == 0
    side_spec = pl.BlockSpec((side_rows, side.shape[1]), lambda j, i: (j * ni + i, 0))
    return pl.pallas_call(
        _sgu_gate_kernel,
        out_shape=(jax.ShapeDtypeStruct((m, n), BF16), jax.ShapeDtypeStruct(side.shape, BF16)),
        grid=(nj, ni),
        in_specs=[pl.BlockSpec((tm, k), lambda j, i: (i, 0)),
                  pl.BlockSpec((k, tn), lambda j, i: (0, ju + j)),
                  pl.BlockSpec((k, tn), lambda j, i: (0, jz + j)),
                  pl.BlockSpec((tm, tn), lambda j, i: (i, j)),
                  stat_spec, stat_spec,
                  pl.BlockSpec((1, tn), lambda j, i: (0, j)),
                  pl.BlockSpec((1, tn), lambda j, i: (0, j)),
                  pl.BlockSpec((gpt, CHUNK, CHUNK), lambda j, i: (j, 0, 0)),
                  pl.BlockSpec((gpt, CHUNK, 1), lambda j, i: (j, 0, 0)),
                  side_spec],
        out_specs=(pl.BlockSpec((tm, tn), lambda j, i: (i, j)), side_spec),
        scratch_shapes=[pltpu.VMEM((k, tn), BF16)] * 2,
        compiler_params=_cparams(2),
        name="sgu_gate",
    )(h, w, w, v, mu, rstd, ln_g.reshape(1, n), ln_b.reshape(1, n), w_s, b_s.reshape(N_GROUPS, CHUNK, 1),
      side)


def _rotary_tables(seq):
    half = ROT_DIM // 2
    inv_freq = jnp.power(ROPE_THETA, -jnp.arange(half, dtype=F32) * 2.0 / ROT_DIM)
    ang = jnp.arange(seq).astype(F32)[:, None] * inv_freq[None, :]
    cos, sin = jnp.cos(ang), jnp.sin(ang)
    rest = HEAD_DIM - ROT_DIM
    c = jnp.concatenate([cos, cos, jnp.ones((seq, rest), F32)], axis=1)
    sa = jnp.concatenate([jnp.zeros((seq, half), F32), sin, jnp.zeros((seq, rest), F32)], axis=1)
    sb = jnp.concatenate([-sin, jnp.zeros((seq, half + rest), F32)], axis=1)
    rot_t = jnp.concatenate([cos, cos, -sin, sin], axis=1).T
    return (c, sa, sb), rot_t


def kernel(x, norm_even, w_in_even, conv_w, cmp_k_pos, cmp_k_w1, cmp_k_b1, cmp_k_w2, cmp_v_pos, cmp_v_w1, cmp_v_b1, cmp_v_w2, w_out_even, norm_odd, w_in_odd, sgu_ln_g, sgu_ln_b, sgu_w_s, sgu_b_s, w_out_odd, norm_final):
    batch, seq, d = x.shape
    m = batch * seq
    x2d = x.reshape(m, d)

    wt_in = jnp.swapaxes(w_in_even[0], 0, 1)
    cw = CONV_W
    qw = N_HEADS * HEAD_DIM
    kvw = N_KV * HEAD_DIM
    o_q = 4 * cw
    o_kv = o_q + qw
    o_gl = o_kv + 6 * kvw
    o_ng = o_gl + 3 * N_HEADS

    h0, gl = rmsnorm_proj(x2d, norm_even[0], wt_in, o_gl, LANES)
    y_conv, w_out0 = conv_proj(h0, wt_in, conv_w[0], seq, w_out_even[0])
    z = matmul_t(h0, wt_in, o_q, qw + 6 * kvw, tn=1024, name="nsa_proj")
    z_ng = matmul_t(h0, wt_in, o_ng, qw, tn=1024, name="nsa_gate_proj")

    tabs, rot_t = _rotary_tables(seq)
    cmp_k = (cmp_k_pos[0], cmp_k_w1[0].astype(BF16).reshape(CMP_LEN, HEAD_DIM, HEAD_DIM),
             cmp_k_b1[0].reshape(1, HEAD_DIM), cmp_k_w2[0].astype(BF16))
    cmp_v = (cmp_v_pos[0], cmp_v_w1[0].astype(BF16).reshape(CMP_LEN, HEAD_DIM, HEAD_DIM),
             cmp_v_b1[0].reshape(1, HEAD_DIM), cmp_v_w2[0].astype(BF16))
    prep = nsa_prep(z, tabs, cmp_k, cmp_v, batch, seq, col0=qw // HEAD_DIM)
    y_nsa = nsa_attention_pair(z, z_ng, gl, rot_t, prep, batch, seq)

    x1, h1 = out_proj_norm(y_conv, 0, y_nsa, 0, w_out0, x2d, norm_odd[0],
                           BF16, True, name="out_proj_even")

    w_in1 = w_in_odd[0]
    v, mu, rstd = sgu_v(h1, w_in1, SGU_W)
    act, w_out1 = sgu_gate(h1, w_in1, 0, 2 * SGU_W, v, mu, rstd,
                           sgu_ln_g[0], sgu_ln_b[0], sgu_w_s[0], sgu_b_s[0], w_out_odd[0])
    (out,) = out_proj_norm(act, 0, act, 1, w_out1, x1, norm_final,
                           F32, False, name="out_proj_odd")
    return out.reshape(batch, seq, d)
```

```python
import functools
import math

import jax
import jax.numpy as jnp
from jax import lax
from jax.experimental import pallas as pl
from jax.experimental.pallas import tpu as pltpu

F32 = jnp.float32
BF16 = jnp.bfloat16

D_MODEL = 2048
MIX = 2 * D_MODEL
CONV_W = MIX // 2
CONV_K = 3
HEAD_DIM = 128
N_HEADS = 16
N_KV = 4
GQA = N_HEADS // N_KV
ROT_DIM = HEAD_DIM // 4
ROPE_THETA = 500000.0
CMP_LEN = 32
CMP_STRIDE = 16
SLC_LEN = 64
N_SEL = 8
CMP_PER_SLC = SLC_LEN // CMP_STRIDE
CMP_BACK = CMP_LEN // CMP_STRIDE - 1
WINDOW = 512
SGU_W = MIX
CHUNK = 128
N_GROUPS = 16
GROUP_W = SGU_W // N_GROUPS
EPS = 1e-6

LANES = 128
SUBLANES = 8
SLC_SHIFT = 6
HALO = 16
KEY_TILE = LANES
V_SUB = 512
XPOSE_ROWS = 256
VMEM_LIMIT = 56 * 1024 * 1024

NEG = -1e30

EXT_DIM = 2 * HEAD_DIM


def _cparams(n_axes):
    return pltpu.CompilerParams(
        dimension_semantics=("arbitrary",) * n_axes, vmem_limit_bytes=VMEM_LIMIT)


def _rmsnorm_proj_kernel(x_ref, g_ref, wt_ref, o_ref, p_ref, wb_ref):
    @pl.when(pl.program_id(0) == 0)
    def _():
        wb_ref[...] = wt_ref[...].T.astype(BF16)

    x = x_ref[...]
    y = x * lax.rsqrt(jnp.mean(x * x, axis=-1, keepdims=True) + EPS)
    hb = (y * g_ref[...]).astype(BF16)
    o_ref[...] = hb
    p_ref[...] = jnp.dot(hb, wb_ref[...], preferred_element_type=F32)


def rmsnorm_proj(x2d, g, wt, row0, n, tm=512):
    m, d = x2d.shape
    assert row0 % n == 0
    return pl.pallas_call(
        _rmsnorm_proj_kernel,
        out_shape=(jax.ShapeDtypeStruct((m, d), BF16), jax.ShapeDtypeStruct((m, n), F32)),
        grid=(m // tm,),
        in_specs=[pl.BlockSpec((tm, d), lambda i: (i, 0)),
                  pl.BlockSpec((1, d), lambda i: (0, 0)),
                  pl.BlockSpec((n, d), lambda i: (row0 // n, 0))],
        out_specs=(pl.BlockSpec((tm, d), lambda i: (i, 0)),
                   pl.BlockSpec((tm, n), lambda i: (i, 0))),
        scratch_shapes=[pltpu.VMEM((d, n), BF16)],
        compiler_params=_cparams(1),
        name="rmsnorm_gates",
    )(x2d, g.reshape(1, d), wt)


def _cast_once(w_refs, wb_refs, transposed=False):
    @pl.when(pl.program_id(1) == 0)
    def _():
        for w_ref, wb_ref in zip(w_refs, wb_refs):
            if transposed:
                for r in range(0, w_ref.shape[0], XPOSE_ROWS):
                    wb_ref[:, r:r + XPOSE_ROWS] = w_ref[r:r + XPOSE_ROWS, :].T.astype(BF16)
            else:
                wb_ref[...] = w_ref[...].astype(BF16)


def _matmul_t_kernel(a_ref, wt_ref, o_ref, wb_ref, *, slabs):
    _cast_once([wt_ref], [wb_ref], transposed=True)
    res = jnp.dot(a_ref[...], wb_ref[...], preferred_element_type=F32)
    if slabs:
        for s in range(o_ref.shape[0]):
            o_ref[s] = res[:, s * LANES:(s + 1) * LANES]
    else:
        o_ref[...] = res


def matmul_t(a, wt, row0, n, tm=1024, tn=512, slabs=False, name="proj"):
    m, k = a.shape
    assert n % tn == 0 and row0 % HALO == 0
    if row0 % tn == 0:
        wspec = pl.BlockSpec((tn, k), lambda j, i: (row0 // tn + j, 0))
    else:
        wspec = pl.BlockSpec((pl.Element(tn), pl.Element(k)),
                             lambda j, i: (pl.multiple_of(row0 + j * tn, HALO), 0))
    if slabs:
        out_shape = jax.ShapeDtypeStruct((n // LANES, m, LANES), F32)
        out_spec = pl.BlockSpec((tn // LANES, tm, LANES), lambda j, i: (j, i, 0))
    else:
        out_shape = jax.ShapeDtypeStruct((m, n), F32)
        out_spec = pl.BlockSpec((tm, tn), lambda j, i: (i, j))
    return pl.pallas_call(
        functools.partial(_matmul_t_kernel, slabs=slabs),
        out_shape=out_shape,
        grid=(n // tn, m // tm),
        in_specs=[pl.BlockSpec((tm, k), lambda j, i: (i, 0)), wspec],
        out_specs=out_spec,
        scratch_shapes=[pltpu.VMEM((k, tn), BF16)],
        compiler_params=_cparams(2),
        name=name,
    )(a, wt)


def _conv_proj_kernel(h_ref, hp_ref, wb32_ref, wc32_ref, wh32_ref, wg32_ref, cw_ref, side32_ref,
                      o_ref, side_ref, wb_ref, wc_ref, wh_ref, wg_ref, *, tiles_per_seq):
    _cast_once([wb32_ref, wc32_ref, wh32_ref, wg32_ref], [wb_ref, wc_ref, wh_ref, wg_ref],
               transposed=True)
    side_ref[...] = side32_ref[...].astype(BF16)
    i = pl.program_id(1)
    cw = cw_ref[...]
    hp = hp_ref[...]
    pp = (jnp.dot(hp, wc_ref[...], preferred_element_type=F32)
          * jnp.dot(hp, wh_ref[...], preferred_element_type=F32))
    pp = jnp.where(i % tiles_per_seq == 0, 0.0, pp)
    prev1, prev2 = pp[HALO - 1:HALO, :], pp[HALO - 2:HALO - 1, :]
    h = h_ref[...]
    cb = jnp.dot(h, wb_ref[...], preferred_element_type=F32)
    cc = jnp.dot(h, wc_ref[...], preferred_element_type=F32)
    ch = jnp.dot(h, wh_ref[...], preferred_element_type=F32)
    cg = jnp.dot(h, wg_ref[...], preferred_element_type=F32)
    p = cc * ch
    row = lax.broadcasted_iota(jnp.int32, (SUBLANES, p.shape[1]), 0)
    p1, p2 = pltpu.roll(p, 1, axis=0), pltpu.roll(p, 2, axis=0)
    p1 = jnp.concatenate([jnp.where(row == 0, prev1, p1[:SUBLANES]), p1[SUBLANES:]], axis=0)
    p2 = jnp.concatenate([jnp.where(row == 0, prev2, jnp.where(row == 1, prev1, p2[:SUBLANES])),
                          p2[SUBLANES:]], axis=0)
    conv = cw[0:1, :] * p2 + cw[1:2, :] * p1 + cw[2:3, :] * p
    o_ref[...] = (cb * conv * (cg * jax.nn.sigmoid(cg))).astype(o_ref.dtype)


def conv_proj(h, wt, conv_w, seq, side, tm=1024, tn=256):
    m, k = h.shape
    n = conv_w.shape[1]
    tiles_per_seq = seq // tm
    halo_per_tile = tm // HALO
    nj, ni = n // tn, m // tm
    side_rows = side.shape[0] // (nj * ni)
    assert side_rows * nj * ni == side.shape[0] and side_rows % HALO == 0
    side_spec = pl.BlockSpec((side_rows, side.shape[1]), lambda j, i: (j * ni + i, 0))

    def wspec(which):
        return pl.BlockSpec((tn, k), lambda j, i: (which * nj + j, 0))

    return pl.pallas_call(
        functools.partial(_conv_proj_kernel, tiles_per_seq=tiles_per_seq),
        out_shape=(jax.ShapeDtypeStruct((m, n), BF16), jax.ShapeDtypeStruct(side.shape, BF16)),
        grid=(nj, ni),
        in_specs=[pl.BlockSpec((tm, k), lambda j, i: (i, 0)),
                  pl.BlockSpec((HALO, k), lambda j, i: (jnp.maximum(i * halo_per_tile - 1, 0), 0)),
                  wspec(0), wspec(1), wspec(2), wspec(3),
                  pl.BlockSpec((CONV_K, tn), lambda j, i: (0, j)),
                  side_spec],
        out_specs=(pl.BlockSpec((tm, tn), lambda j, i: (i, j)), side_spec),
        scratch_shapes=[pltpu.VMEM((k, tn), BF16)] * 4,
        compiler_params=_cparams(2),
        name="conv_proj",
    )(h, h, wt, wt, wt, wt, conv_w, side)


def _rotary(x, c, sa, sb):
    half = ROT_DIM // 2
    return (x * c + pltpu.roll(x, half, axis=1) * sa
            + pltpu.roll(x, HEAD_DIM - half, axis=1) * sb)


def _compress(src_ref, pos_ref, w1_ref, b1_ref, w2_ref):
    n_rows = src_ref.shape[0] // CMP_STRIDE
    p_acc = jnp.zeros((n_rows, HEAD_DIM), F32)
    q_acc = jnp.zeros((n_rows, HEAD_DIM), F32)
    for r in range(CMP_STRIDE):
        s_r = src_ref[pl.ds(r, n_rows, stride=CMP_STRIDE), :]
        a_r = (s_r + pos_ref[r:r + 1, :]).astype(BF16)
        b_r = (s_r + pos_ref[CMP_STRIDE + r:CMP_STRIDE + r + 1, :]).astype(BF16)
        p_acc += jnp.dot(a_r, w1_ref[r], preferred_element_type=F32)
        q_acc += jnp.dot(b_r, w1_ref[CMP_STRIDE + r], preferred_element_type=F32)
    hid = p_acc + pltpu.roll(q_acc, n_rows - 1, axis=0) + b1_ref[...]
    act = (hid * jax.nn.sigmoid(hid)).astype(BF16)
    return jnp.dot(act, w2_ref[...], preferred_element_type=F32)


def _nsa_prep_kernel(kc_ref, vc_ref, ks_ref, vs_ref, kw_ref, vw_ref, c_ref, sa_ref, sb_ref,
                     kpos_ref, kw1_ref, kb1_ref, kw2_ref, vpos_ref, vw1_ref, vb1_ref, vw2_ref,
                     ksx_ref, vst_ref, kwx_ref, vwt_ref, kcmp_ref, vcmp_ref):
    c, sa, sb = c_ref[...], sa_ref[...], sb_ref[...]
    seq = ks_ref.shape[0]
    n_blk = seq // SLC_LEN
    flag_w = EXT_DIM - HEAD_DIM
    blk = lax.broadcasted_iota(jnp.int32, (seq, flag_w), 0) >> SLC_SHIFT
    onehot = jnp.where(blk == lax.broadcasted_iota(jnp.int32, (seq, flag_w), 1), 1.0, 0.0)
    ksx_ref[:, 0:HEAD_DIM] = _rotary(ks_ref[...], c, sa, sb).astype(BF16)
    ksx_ref[:, HEAD_DIM:] = onehot.astype(BF16)
    pad_flag = jnp.where(lax.broadcasted_iota(jnp.int32, (WINDOW, EXT_DIM), 1) == HEAD_DIM + n_blk, 1.0, 0.0)
    kwx_ref[0:WINDOW, :] = pad_flag.astype(BF16)
    kwx_ref[WINDOW:, 0:HEAD_DIM] = _rotary(kw_ref[...], c, sa, sb).astype(BF16)
    kwx_ref[WINDOW:, HEAD_DIM:] = jnp.zeros((seq, flag_w), BF16)
    n_tiles = seq // KEY_TILE
    pad_tiles = WINDOW // KEY_TILE
    for t in range(pad_tiles):
        vwt_ref[t] = jnp.zeros((HEAD_DIM, KEY_TILE), BF16)
    for t in range(n_tiles):
        rows = pl.ds(t * KEY_TILE, KEY_TILE)
        vst_ref[t] = vs_ref[rows, :].T.astype(BF16)
        vwt_ref[pad_tiles + t] = vw_ref[rows, :].T.astype(BF16)
    kcmp_ref[...] = _compress(kc_ref, kpos_ref, kw1_ref, kb1_ref, kw2_ref).astype(BF16)
    vcmp_ref[...] = _compress(vc_ref, vpos_ref, vw1_ref, vb1_ref, vw2_ref).astype(BF16)


def nsa_prep(z, tabs, cmp_k, cmp_v, batch, seq, col0):
    def zspec(which):
        return pl.BlockSpec((None, seq, HEAD_DIM), lambda b, g, w=which: (col0 + w * N_KV + g, b, 0))

    tab = pl.BlockSpec((seq, HEAD_DIM), lambda b, g: (0, 0))

    def wspecs():
        return [pl.BlockSpec((CMP_LEN, HEAD_DIM), lambda b, g: (0, 0)),
                pl.BlockSpec((CMP_LEN, HEAD_DIM, HEAD_DIM), lambda b, g: (0, 0, 0)),
                pl.BlockSpec((1, HEAD_DIM), lambda b, g: (0, 0)),
                pl.BlockSpec((HEAD_DIM, HEAD_DIM), lambda b, g: (0, 0))]

    def out(shape):
        nd = len(shape)
        spec = pl.BlockSpec((None, None) + shape, lambda b, g: (b, g) + (0,) * nd)
        return spec, jax.ShapeDtypeStruct((batch, N_KV) + shape, BF16)

    n_tiles = seq // KEY_TILE
    pad_tiles = WINDOW // KEY_TILE
    outs = [out((seq, EXT_DIM)), out((n_tiles, HEAD_DIM, KEY_TILE)),
            out((WINDOW + seq, EXT_DIM)), out((pad_tiles + n_tiles, HEAD_DIM, KEY_TILE)),
            out((seq // CMP_STRIDE, HEAD_DIM)), out((seq // CMP_STRIDE, HEAD_DIM))]
    return pl.pallas_call(
        _nsa_prep_kernel,
        out_shape=tuple(o[1] for o in outs),
        grid=(batch, N_KV),
        in_specs=[zspec(0), zspec(1), zspec(2), zspec(3), zspec(4), zspec(5), tab, tab, tab]
        + wspecs() + wspecs(),
        out_specs=tuple(o[0] for o in outs),
        compiler_params=_cparams(2),
        name="nsa_prep",
    )(z, z, z, z, z, z, *tabs, *cmp_k, *cmp_v)


def _nsa_attn_pair_kernel(q_ref, ng_ref, gl_ref, rot_ref,
                          ksx_ref, vst_ref, kwx_ref, vwt_ref, kcmp_ref, vcmp_ref,
                          o_ref, acc_ref, qx_ref, glt_ref, s_ref, m_ref, l_ref, oc_ref, ps_ref, *,
                          tq, n_grp_step, n_blk):
    gp = pl.program_id(1)
    i = pl.program_id(2)
    q0 = i * tq
    half = ROT_DIM // 2
    qscale = (HEAD_DIM ** -0.5) * math.log2(math.e)
    qw = GQA * HEAD_DIM
    groups = range(n_grp_step)

    def head_lanes(n):
        return slice(n * tq, (n + 1) * tq)

    def lanes4(x):
        return jnp.concatenate([x] * GQA, axis=1)

    rot = rot_ref[...]
    cos2, sin2 = rot[:ROT_DIM], rot[ROT_DIM:]
    tlane = q0 + lax.broadcasted_iota(jnp.int32, (1, tq), 1)
    kk = lax.broadcasted_iota(jnp.int32, (tq, tq), 0)
    tt = lax.broadcasted_iota(jnp.int32, (tq, tq), 1)
    tri_diag = jnp.where(kk <= tt, 0.0, NEG)
    tri_old = jnp.where(kk > tt, 0.0, NEG)
    n_cmp = kcmp_ref.shape[1]
    cend = lax.broadcasted_iota(jnp.int32, (n_cmp, tq), 0) * CMP_STRIDE + (CMP_LEN - 1)
    bias_c = lanes4(jnp.where(cend <= tlane, 0.0, NEG))
    any_c = lanes4(tlane) >= CMP_LEN - 1
    jblk = lax.broadcasted_iota(jnp.int32, (n_blk, tq), 0)
    forced = (jblk == 0) | (jblk == (tlane >> SLC_SHIFT))
    causal_blk = jblk * SLC_LEN <= tlane
    sub = lax.broadcasted_iota(jnp.int32, (SUBLANES, tq), 0)
    n_rest = EXT_DIM - HEAD_DIM - n_blk
    pad_rows = jnp.where(lax.broadcasted_iota(jnp.int32, (n_rest, GQA * tq), 0) == 0, NEG, 0.0).astype(BF16)
    tiles = tq // KEY_TILE

    def sel_scores(gg, ci):
        k0 = pl.multiple_of(ci * tq, tq)
        return jnp.dot(ksx_ref[gg, pl.ds(k0, tq), :], qx_ref[gg], preferred_element_type=F32)

    def sel_update(gg, ci, s):
        vblk = jnp.concatenate([vst_ref[gg, ci * tiles + r] for r in range(tiles)], axis=1)
        m_old = m_ref[gg]
        m_new = jnp.maximum(m_old, jnp.max(s, axis=0, keepdims=True))
        alpha = jnp.exp2(m_old - m_new)
        p = jnp.exp2(s - m_new)
        l_ref[gg] = alpha * l_ref[gg] + jnp.sum(p, axis=0, keepdims=True)
        acc_ref[gg] = alpha * acc_ref[gg] + jnp.dot(vblk, p.astype(BF16), preferred_element_type=F32)
        m_ref[gg] = m_new

    for gg in groups:
        qts = []
        for n in range(GQA):
            qt = (q_ref[gg * GQA + n] * qscale).T
            qts.append(qt)
            top = qt[:ROT_DIM]
            swapped = jnp.concatenate([top[half:], top[:half]], axis=0)
            qrt = jnp.concatenate([top * cos2 + swapped * sin2, qt[ROT_DIM:]], axis=0)
            qx_ref[gg, 0:HEAD_DIM, head_lanes(n)] = qrt.astype(BF16)
        qt_all = jnp.concatenate(qts, axis=1).astype(BF16)

        s_c = jnp.dot(kcmp_ref[gg], qt_all, preferred_element_type=F32) + bias_c
        m_c = jnp.where(any_c, jnp.max(s_c, axis=0, keepdims=True), 0.0)
        e_c = jnp.exp2(s_c - m_c)
        den_c = jnp.sum(e_c, axis=0, keepdims=True)
        p_c = e_c / jnp.where(den_c > 0, den_c, 1.0)
        vcmp_t = vcmp_ref[gg].astype(F32).T.astype(BF16)
        oc_ref[gg] = jnp.dot(vcmp_t, p_c.astype(BF16), preferred_element_type=F32)

        p_sum = p_c[:, head_lanes(0)]
        for n in range(1, GQA):
            p_sum = p_sum + p_c[:, head_lanes(n)]
        for t in range(tq // LANES):
            ps_ref[gg, t] = p_sum[:, t * LANES:(t + 1) * LANES]
        taps = [jnp.concatenate([ps_ref[gg, t, pl.ds(k, n_blk, stride=CMP_PER_SLC), :]
                                 for t in range(tq // LANES)], axis=1) for k in range(CMP_PER_SLC)]
        imp = taps[0]
        for k in range(1, CMP_PER_SLC):
            imp = imp + taps[k]
        for k in range(CMP_BACK):
            older = pltpu.roll(taps[CMP_PER_SLC - 1 - k], 1, axis=0)
            imp = imp + jnp.where(jblk == 0, 0.0, older)
        imp = jnp.where(forced, jnp.inf, jnp.where(causal_blk, imp, -jnp.inf))
        n_sub = n_blk // SUBLANES
        part = [imp[a * SUBLANES:(a + 1) * SUBLANES] for a in range(n_sub)]
        cnt = [jnp.zeros((SUBLANES, tq), F32) for _ in range(n_sub)]
        for r in range(n_blk):
            row = imp[r:r + 1, :]
            a_r, r_in = divmod(r, SUBLANES)
            for a in range(n_sub):
                if a > a_r:
                    cnt[a] = cnt[a] + jnp.where(row >= part[a], 1.0, 0.0)
                elif a < a_r:
                    cnt[a] = cnt[a] + jnp.where(row > part[a], 1.0, 0.0)
                else:
                    cnt[a] = cnt[a] + jnp.where(sub > r_in, jnp.where(row >= part[a], 1.0, 0.0),
                                                jnp.where(row > part[a], 1.0, 0.0))
        cnt = jnp.concatenate(cnt, axis=0)
        sel_bias = jnp.where(cnt < N_SEL, jnp.where(imp > -jnp.inf, 0.0, NEG), NEG)
        qx_ref[gg, HEAD_DIM:HEAD_DIM + n_blk, :] = lanes4(sel_bias).astype(BF16)
        qx_ref[gg, HEAD_DIM + n_blk:, :] = pad_rows

        m_ref[gg] = jnp.full(m_ref.shape[1:], NEG, F32)
        l_ref[gg] = jnp.zeros(l_ref.shape[1:], F32)
        acc_ref[gg] = jnp.zeros(acc_ref.shape[1:], F32)
        s_ref[gg, 0] = sel_scores(gg, 0)

    def sel_pair(pi, carry):
        c0 = 2 * pi
        for gg in groups:
            s_ref[gg, 1] = sel_scores(gg, c0 + 1)
        for gg in groups:
            sel_update(gg, c0, s_ref[gg, 0])
        for gg in groups:
            s_ref[gg, 0] = sel_scores(gg, c0 + 2)
        for gg in groups:
            sel_update(gg, c0 + 1, s_ref[gg, 1])
        return carry

    lax.fori_loop(0, i // 2, sel_pair, 0)

    @pl.when(i % 2 == 1)
    def _():
        for gg in groups:
            s_ref[gg, 1] = sel_scores(gg, i)
        for gg in groups:
            sel_update(gg, i - 1, s_ref[gg, 0])

    for gg in groups:
        sel_update(gg, i, s_ref[gg, i % 2] + lanes4(tri_diag))

    n_wt = WINDOW // tq
    vt0 = q0 // KEY_TILE
    glt_ref[...] = jax.nn.sigmoid(gl_ref[...]).T
    for gg in groups:
        kwin = kwx_ref[gg, pl.ds(pl.multiple_of(q0, tq), WINDOW + tq), :]
        vwin = jnp.concatenate([vwt_ref[gg, vt0 + r] for r in range((WINDOW + tq) // KEY_TILE)], axis=1)
        s_w = jnp.dot(kwin, qx_ref[gg], preferred_element_type=F32)
        s_w = jnp.concatenate([s_w[:tq] + lanes4(tri_old), s_w[tq:n_wt * tq],
                               s_w[n_wt * tq:] + lanes4(tri_diag)], axis=0)
        e_w = jnp.exp2(s_w - jnp.max(s_w, axis=0, keepdims=True))
        o_w = (jnp.dot(vwin, e_w.astype(BF16), preferred_element_type=F32)
               / jnp.sum(e_w, axis=0, keepdims=True))
        o_s = acc_ref[gg] / l_ref[gg]
        o_c = oc_ref[gg]
        ng = ng_ref[:, gg * qw:(gg + 1) * qw]
        for n in range(GQA):
            head = (gp * n_grp_step + gg) * GQA + n
            gc, gs, gw = (glt_ref[pl.ds(j * N_HEADS + head, 1), :] for j in range(3))
            sl = head_lanes(n)
            o = (gc * o_c[:, sl] + gs * o_s[:, sl] + gw * o_w[:, sl]).T
            ngh = ng[:, n * HEAD_DIM:(n + 1) * HEAD_DIM]
            col = gg * qw + n * HEAD_DIM
            o_ref[:, col:col + HEAD_DIM] = (o * (ngh * jax.nn.sigmoid(ngh))).astype(o_ref.dtype)


def nsa_attention_pair(z, z_ng, gl, rot_t, prep, batch, seq, tq=256, n_grp_step=2):
    ksx, vst, kwx, vwt, kcmp, vcmp = prep
    assert tq % KEY_TILE == 0 and WINDOW % tq == 0 and WINDOW >= 2 * tq and N_KV % n_grp_step == 0
    n_blk, n_cmp_rows = seq // SLC_LEN, kcmp.shape[2]
    assert n_cmp_rows == CMP_PER_SLC * n_blk
    nq = seq // tq
    qw = GQA * HEAD_DIM * n_grp_step

    def whole(a):
        nd = a.ndim - 2
        return pl.BlockSpec((None, n_grp_step) + a.shape[2:], lambda b, g, i: (b, g) + (0,) * nd)

    lanes = GQA * tq
    return pl.pallas_call(
        functools.partial(_nsa_attn_pair_kernel, tq=tq, n_grp_step=n_grp_step, n_blk=n_blk),
        out_shape=jax.ShapeDtypeStruct((batch * seq, N_HEADS * HEAD_DIM), BF16),
        grid=(batch, N_KV // n_grp_step, nq),
        in_specs=[pl.BlockSpec((GQA * n_grp_step, tq, HEAD_DIM), lambda b, g, i: (g, b * nq + i, 0)),
                  pl.BlockSpec((tq, qw), lambda b, g, i: (b * nq + i, g)),
                  pl.BlockSpec((tq, gl.shape[1]), lambda b, g, i: (b * nq + i, 0)),
                  pl.BlockSpec((rot_t.shape[0], tq), lambda b, g, i: (0, i)),
                  whole(ksx), whole(vst), whole(kwx), whole(vwt), whole(kcmp), whole(vcmp)],
        out_specs=pl.BlockSpec((tq, qw), lambda b, g, i: (b * nq + i, g)),
        scratch_shapes=[pltpu.VMEM((n_grp_step, HEAD_DIM, lanes), F32),
                        pltpu.VMEM((n_grp_step, EXT_DIM, lanes), BF16),
                        pltpu.VMEM((gl.shape[1], tq), F32),
                        pltpu.VMEM((n_grp_step, 2, tq, lanes), F32),
                        pltpu.VMEM((n_grp_step, 1, lanes), F32),
                        pltpu.VMEM((n_grp_step, 1, lanes), F32),
                        pltpu.VMEM((n_grp_step, HEAD_DIM, lanes), F32),
                        pltpu.VMEM((n_grp_step, tq // LANES, n_cmp_rows, LANES), F32)],
        compiler_params=_cparams(3),
        name="nsa_attention",
    )(z, z_ng, gl, rot_t, ksx, vst, kwx, vwt, kcmp, vcmp)


def _out_proj_norm_kernel(a0_ref, a1_ref, w_ref, x_ref, g_ref, *o_refs, kh, emit_residual):
    acc = jnp.dot(a0_ref[...], w_ref[0:kh, :], preferred_element_type=F32)
    acc += jnp.dot(a1_ref[...], w_ref[kh:, :], preferred_element_type=F32)
    x = x_ref[...] + acc
    y = x * lax.rsqrt(jnp.mean(x * x, axis=-1, keepdims=True) + EPS)
    o_refs[-1][...] = (y * g_ref[...]).astype(o_refs[-1].dtype)
    if emit_residual:
        o_refs[0][...] = x


def out_proj_norm(a0, a0_blk, a1, a1_blk, w, x2d, g, norm_dtype, emit_residual, tm=512, name="out_proj"):
    m, n = x2d.shape
    kh = w.shape[0] // 2
    row = pl.BlockSpec((tm, n), lambda i: (i, 0))
    out_shape = [jax.ShapeDtypeStruct((m, n), norm_dtype)]
    if emit_residual:
        out_shape.insert(0, jax.ShapeDtypeStruct((m, n), F32))
    return pl.pallas_call(
        functools.partial(_out_proj_norm_kernel, kh=kh, emit_residual=emit_residual),
        out_shape=tuple(out_shape),
        grid=(m // tm,),
        in_specs=[pl.BlockSpec((tm, kh), lambda i: (i, a0_blk)),
                  pl.BlockSpec((tm, kh), lambda i: (i, a1_blk)),
                  pl.BlockSpec(w.shape, lambda i: (0, 0), pipeline_mode=pl.Buffered(1)),
                  row,
                  pl.BlockSpec((1, n), lambda i: (0, 0))],
        out_specs=tuple(row for _ in out_shape),
        compiler_params=_cparams(1),
        name=name,
    )(a0, a1, w, x2d, g.reshape(1, n))


def _sgu_v_kernel(h_ref, w_ref, v_ref, mu_ref, rstd_ref, wb_ref, c_ref, s1_ref, s2_ref, *, n_col_tiles):
    j = pl.program_id(1)

    @pl.when(pl.program_id(0) == 0)
    def _():
        wb_ref[j] = w_ref[...].astype(BF16)

    tm, tn = v_ref.shape
    h = h_ref[...]
    for c0 in range(0, tn, V_SUB):
        v = jnp.dot(h, wb_ref[j, :, c0:c0 + V_SUB], preferred_element_type=F32)
        v_ref[:, c0:c0 + V_SUB] = v

        if c0 == 0:
            @pl.when(j == 0)
            def _():
                c_ref[...] = jnp.broadcast_to(jnp.sum(v, axis=-1, keepdims=True) / V_SUB, c_ref.shape)
                s1_ref[...] = jnp.zeros(s1_ref.shape, F32)
                s2_ref[...] = jnp.zeros(s2_ref.shape, F32)

        c = c_ref[...]
        s1, s2 = s1_ref[...], s2_ref[...]
        for r in range(V_SUB // LANES):
            d = v[:, r * LANES:(r + 1) * LANES] - c
            s1 = s1 + d
            s2 = s2 + d * d
        s1_ref[...] = s1
        s2_ref[...] = s2

    @pl.when(j == n_col_tiles - 1)
    def _():
        width = n_col_tiles * tn
        mean_d = jnp.sum(s1, axis=-1, keepdims=True) / width
        var = jnp.sum(s2, axis=-1, keepdims=True) / width - mean_d * mean_d
        mu_ref[...] = c + mean_d
        rstd_ref[...] = jnp.broadcast_to(lax.rsqrt(var + EPS), rstd_ref.shape)


def sgu_v(h, w, col0, tm=512, tn=2 * V_SUB):
    m, k = h.shape
    nj = SGU_W // tn
    j0 = col0 // tn
    stat = jax.ShapeDtypeStruct((m, LANES), F32)
    stat_spec = pl.BlockSpec((tm, LANES), lambda i, j: (i, 0))
    return pl.pallas_call(
        functools.partial(_sgu_v_kernel, n_col_tiles=nj),
        out_shape=(jax.ShapeDtypeStruct((m, SGU_W), F32), stat, stat),
        grid=(m // tm, nj),
        in_specs=[pl.BlockSpec((tm, k), lambda i, j: (i, 0)),
                  pl.BlockSpec((k, tn), lambda i, j: (0, j0 + jnp.where(i == 0, j, 0)))],
        out_specs=(pl.BlockSpec((tm, tn), lambda i, j: (i, j)), stat_spec, stat_spec),
        scratch_shapes=[pltpu.VMEM((nj, k, tn), BF16)] + [pltpu.VMEM((tm, LANES), F32)] * 3,
        compiler_params=_cparams(2),
        name="sgu_v",
    )(h, w)


def _sgu_gate_kernel(h_ref, wu32_ref, wz32_ref, v_ref, mu_ref, rstd_ref, lg_ref, lb_ref, ws_ref, bs_ref,
                     side32_ref, o_ref, side_ref, wu_ref, wz_ref):
    _cast_once([wu32_ref, wz32_ref], [wu_ref, wz_ref])
    side_ref[...] = side32_ref[...].astype(BF16)
    h = h_ref[...]
    tm, tn = v_ref.shape
    reps = GROUP_W // LANES
    mu = jnp.concatenate([mu_ref[...]] * reps, axis=1)
    rstd = jnp.concatenate([rstd_ref[...]] * reps, axis=1)
    tri = (lax.broadcasted_iota(jnp.int32, (CHUNK, CHUNK), 1)
           <= lax.broadcasted_iota(jnp.int32, (CHUNK, CHUNK), 0))
    for gi in range(tn // GROUP_W):
        cols = slice(gi * GROUP_W, (gi + 1) * GROUP_W)
        u = jnp.dot(h, wu_ref[:, cols], preferred_element_type=F32)
        zg = jnp.dot(h, wz_ref[:, cols], preferred_element_type=F32)
        vn = ((v_ref[:, cols] - mu) * rstd * lg_ref[:, cols] + lb_ref[:, cols]).astype(BF16)
        wsm = jnp.where(tri, ws_ref[gi], 0.0).astype(BF16)
        bsg = bs_ref[gi]
        mix = jnp.concatenate(
            [jnp.dot(wsm, vn[c * CHUNK:(c + 1) * CHUNK], preferred_element_type=F32) + bsg
             for c in range(tm // CHUNK)], axis=0)
        o_ref[:, cols] = (u * mix * (zg * jax.nn.sigmoid(zg))).astype(o_ref.dtype)


def sgu_gate(h, w, col_u, col_z, v, mu, rstd, ln_g, ln_b, w_s, b_s, side, tm=1024, tn=512):
    m, k = h.shape
    n = SGU_W
    nj, ni = n // tn, m // tm
    ju, jz = col_u // tn, col_z // tn
    gpt = tn // GROUP_W
    stat_spec = pl.BlockSpec((tm, LANES), lambda j, i: (i, 0))
    side_rows = side.shape[0] // (nj * ni)
    assert side_rows * nj * ni == side.shape[0] and side_rows % HALO == 0
    side_spec = pl.BlockSpec((side_rows, side.shape[1]), lambda j, i: (j * ni + i, 0))
    return pl.pallas_call(
        _sgu_gate_kernel,
        out_shape=(jax.ShapeDtypeStruct((m, n), BF16), jax.ShapeDtypeStruct(side.shape, BF16)),
        grid=(nj, ni),
        in_specs=[pl.BlockSpec((tm, k), lambda j, i: (i, 0)),
                  pl.BlockSpec((k, tn), lambda j, i: (0, ju + j)),
                  pl.BlockSpec((k, tn), lambda j, i: (0, jz + j)),
                  pl.BlockSpec((tm, tn), lambda j, i: (i, j)),
                  stat_spec, stat_spec,
                  pl.BlockSpec((1, tn), lambda j, i: (0, j)),
                  pl.BlockSpec((1, tn), lambda j, i: (0, j)),
                  pl.BlockSpec((gpt, CHUNK, CHUNK), lambda j, i: (j, 0, 0)),
                  pl.BlockSpec((gpt, CHUNK, 1), lambda j, i: (j, 0, 0)),
                  side_spec],
        out_specs=(pl.BlockSpec((tm, tn), lambda j, i: (i, j)), side_spec),
        scratch_shapes=[pltpu.VMEM((k, tn), BF16)] * 2,
        compiler_params=_cparams(2),
        name="sgu_gate",
    )(h, w, w, v, mu, rstd, ln_g.reshape(1, n), ln_b.reshape(1, n), w_s, b_s.reshape(N_GROUPS, CHUNK, 1),
      side)


def _rotary_tables(seq):
    half = ROT_DIM // 2
    inv_freq = jnp.power(ROPE_THETA, -jnp.arange(half, dtype=F32) * 2.0 / ROT_DIM)
    ang = jnp.arange(seq).astype(F32)[:, None] * inv_freq[None, :]
    cos, sin = jnp.cos(ang), jnp.sin(ang)
    rest = HEAD_DIM - ROT_DIM
    c = jnp.concatenate([cos, cos, jnp.ones((seq, rest), F32)], axis=1)
    sa = jnp.concatenate([jnp.zeros((seq, half), F32), sin, jnp.zeros((seq, rest), F32)], axis=1)
    sb = jnp.concatenate([-sin, jnp.zeros((seq, half + rest), F32)], axis=1)
    rot_t = jnp.concatenate([cos, cos, -sin, sin], axis=1).T
    return (c, sa, sb), rot_t


def kernel(x, norm_even, w_in_even, conv_w, cmp_k_pos, cmp_k_w1, cmp_k_b1, cmp_k_w2, cmp_v_pos, cmp_v_w1, cmp_v_b1, cmp_v_w2, w_out_even, norm_odd, w_in_odd, sgu_ln_g, sgu_ln_b, sgu_w_s, sgu_b_s, w_out_odd, norm_final):
    batch, seq, d = x.shape
    m = batch * seq
    x2d = x.reshape(m, d)

    wt_in = jnp.swapaxes(w_in_even[0], 0, 1)
    cw = CONV_W
    qw = N_HEADS * HEAD_DIM
    kvw = N_KV * HEAD_DIM
    o_q = 4 * cw
    o_kv = o_q + qw
    o_gl = o_kv + 6 * kvw
    o_ng = o_gl + 3 * N_HEADS

    h0, gl = rmsnorm_proj(x2d, norm_even[0], wt_in, o_gl, LANES)
    y_conv, w_out0 = conv_proj(h0, wt_in, conv_w[0], seq, w_out_even[0])
    z = matmul_t(h0, wt_in, o_q, qw + 6 * kvw, tn=1024, slabs=True, name="nsa_proj")
    z_ng = matmul_t(h0, wt_in, o_ng, qw, tn=1024, name="nsa_gate_proj")

    tabs, rot_t = _rotary_tables(seq)
    cmp_k = (cmp_k_pos[0], cmp_k_w1[0].astype(BF16).reshape(CMP_LEN, HEAD_DIM, HEAD_DIM),
             cmp_k_b1[0].reshape(1, HEAD_DIM), cmp_k_w2[0].astype(BF16))
    cmp_v = (cmp_v_pos[0], cmp_v_w1[0].astype(BF16).reshape(CMP_LEN, HEAD_DIM, HEAD_DIM),
             cmp_v_b1[0].reshape(1, HEAD_DIM), cmp_v_w2[0].astype(BF16))
    prep = nsa_prep(z, tabs, cmp_k, cmp_v, batch, seq, col0=qw // HEAD_DIM)
    y_nsa = nsa_attention_pair(z, z_ng, gl, rot_t, prep, batch, seq)

    x1, h1 = out_proj_norm(y_conv, 0, y_nsa, 0, w_out0, x2d, norm_odd[0],
                           BF16, True, name="out_proj_even")

    w_in1 = w_in_odd[0]
    v, mu, rstd = sgu_v(h1, w_in1, SGU_W)
    act, w_out1 = sgu_gate(h1, w_in1, 0, 2 * SGU_W, v, mu, rstd,
                           sgu_ln_g[0], sgu_ln_b[0], sgu_w_s[0], sgu_b_s[0], w_out_odd[0])
    (out,) = out_proj_norm(act, 0, act, 1, w_out1, x1, norm_final,
                           F32, False, name="out_proj_odd")
    return out.reshape(batch, seq, d)
```

```python
import functools
import math

import jax
import jax.numpy as jnp
from jax import lax
from jax.experimental import pallas as pl
from jax.experimental.pallas import tpu as pltpu

F32 = jnp.float32
BF16 = jnp.bfloat16

D_MODEL = 2048
MIX = 2 * D_MODEL
CONV_W = MIX // 2
CONV_K = 3
HEAD_DIM = 128
N_HEADS = 16
N_KV = 4
GQA = N_HEADS // N_KV
ROT_DIM = HEAD_DIM // 4
ROPE_THETA = 500000.0
CMP_LEN = 32
CMP_STRIDE = 16
SLC_LEN = 64
N_SEL = 8
CMP_PER_SLC = SLC_LEN // CMP_STRIDE
CMP_BACK = CMP_LEN // CMP_STRIDE - 1
WINDOW = 512
SGU_W = MIX
CHUNK = 128
N_GROUPS = 16
GROUP_W = SGU_W // N_GROUPS
EPS = 1e-6

LANES = 128
SUBLANES = 8
SLC_SHIFT = 6
HALO = 16
KEY_TILE = LANES
V_SUB = 512
XPOSE_ROWS = 256
VMEM_LIMIT = 56 * 1024 * 1024

NEG = -1e30

EXT_DIM = 2 * HEAD_DIM


def _cparams(n_axes):
    return pltpu.CompilerParams(
        dimension_semantics=("arbitrary",) * n_axes, vmem_limit_bytes=VMEM_LIMIT)


def _rmsnorm_proj_kernel(x_ref, g_ref, wt_ref, o_ref, p_ref, wb_ref):
    @pl.when(pl.program_id(0) == 0)
    def _():
        wb_ref[...] = wt_ref[...].T.astype(BF16)

    x = x_ref[...]
    y = x * lax.rsqrt(jnp.mean(x * x, axis=-1, keepdims=True) + EPS)
    hb = (y * g_ref[...]).astype(BF16)
    o_ref[...] = hb
    p_ref[...] = jnp.dot(hb, wb_ref[...], preferred_element_type=F32)


def rmsnorm_proj(x2d, g, wt, row0, n, tm=512):
    m, d = x2d.shape
    assert row0 % n == 0
    return pl.pallas_call(
        _rmsnorm_proj_kernel,
        out_shape=(jax.ShapeDtypeStruct((m, d), BF16), jax.ShapeDtypeStruct((m, n), F32)),
        grid=(m // tm,),
        in_specs=[pl.BlockSpec((tm, d), lambda i: (i, 0)),
                  pl.BlockSpec((1, d), lambda i: (0, 0)),
                  pl.BlockSpec((n, d), lambda i: (row0 // n, 0))],
        out_specs=(pl.BlockSpec((tm, d), lambda i: (i, 0)),
                   pl.BlockSpec((tm, n), lambda i: (i, 0))),
        scratch_shapes=[pltpu.VMEM((d, n), BF16)],
        compiler_params=_cparams(1),
        name="rmsnorm_gates",
    )(x2d, g.reshape(1, d), wt)


def _cast_once(w_refs, wb_refs, transposed=False):
    @pl.when(pl.program_id(1) == 0)
    def _():
        for w_ref, wb_ref in zip(w_refs, wb_refs):
            if transposed:
                for r in range(0, w_ref.shape[0], XPOSE_ROWS):
                    wb_ref[:, r:r + XPOSE_ROWS] = w_ref[r:r + XPOSE_ROWS, :].T.astype(BF16)
            else:
                wb_ref[...] = w_ref[...].astype(BF16)


def _matmul_t_kernel(a_ref, wt_ref, o_ref, wb_ref):
    _cast_once([wt_ref], [wb_ref], transposed=True)
    o_ref[...] = jnp.dot(a_ref[...], wb_ref[...], preferred_element_type=F32)


def matmul_t(a, wt, row0, n, tm=1024, tn=512, name="proj"):
    m, k = a.shape
    assert n % tn == 0 and row0 % HALO == 0
    if row0 % tn == 0:
        wspec = pl.BlockSpec((tn, k), lambda j, i: (row0 // tn + j, 0))
    else:
        wspec = pl.BlockSpec((pl.Element(tn), pl.Element(k)),
                             lambda j, i: (pl.multiple_of(row0 + j * tn, HALO), 0))
    return pl.pallas_call(
        _matmul_t_kernel,
        out_shape=jax.ShapeDtypeStruct((m, n), F32),
        grid=(n // tn, m // tm),
        in_specs=[pl.BlockSpec((tm, k), lambda j, i: (i, 0)), wspec],
        out_specs=pl.BlockSpec((tm, tn), lambda j, i: (i, j)),
        scratch_shapes=[pltpu.VMEM((k, tn), BF16)],
        compiler_params=_cparams(2),
        name=name,
    )(a, wt)


def _conv_proj_kernel(h_ref, hp_ref, wb32_ref, wc32_ref, wh32_ref, wg32_ref, cw_ref, side32_ref,
                      o_ref, side_ref, wb_ref, wc_ref, wh_ref, wg_ref, *, tiles_per_seq):
    _cast_once([wb32_ref, wc32_ref, wh32_ref, wg32_ref], [wb_ref, wc_ref, wh_ref, wg_ref],
               transposed=True)
    side_ref[...] = side32_ref[...].astype(BF16)
    i = pl.program_id(1)
    cw = cw_ref[...]
    hp = hp_ref[...]
    pp = (jnp.dot(hp, wc_ref[...], preferred_element_type=F32)
          * jnp.dot(hp, wh_ref[...], preferred_element_type=F32))
    pp = jnp.where(i % tiles_per_seq == 0, 0.0, pp)
    prev1, prev2 = pp[HALO - 1:HALO, :], pp[HALO - 2:HALO - 1, :]
    h = h_ref[...]
    cb = jnp.dot(h, wb_ref[...], preferred_element_type=F32)
    cc = jnp.dot(h, wc_ref[...], preferred_element_type=F32)
    ch = jnp.dot(h, wh_ref[...], preferred_element_type=F32)
    cg = jnp.dot(h, wg_ref[...], preferred_element_type=F32)
    p = cc * ch
    row = lax.broadcasted_iota(jnp.int32, (SUBLANES, p.shape[1]), 0)
    p1, p2 = pltpu.roll(p, 1, axis=0), pltpu.roll(p, 2, axis=0)
    p1 = jnp.concatenate([jnp.where(row == 0, prev1, p1[:SUBLANES]), p1[SUBLANES:]], axis=0)
    p2 = jnp.concatenate([jnp.where(row == 0, prev2, jnp.where(row == 1, prev1, p2[:SUBLANES])),
                          p2[SUBLANES:]], axis=0)
    conv = cw[0:1, :] * p2 + cw[1:2, :] * p1 + cw[2:3, :] * p
    o_ref[...] = (cb * conv * (cg * jax.nn.sigmoid(cg))).astype(o_ref.dtype)


def conv_proj(h, wt, conv_w, seq, side, tm=1024, tn=256):
    m, k = h.shape
    n = conv_w.shape[1]
    tiles_per_seq = seq // tm
    halo_per_tile = tm // HALO
    nj, ni = n // tn, m // tm
    side_rows = side.shape[0] // (nj * ni)
    assert side_rows * nj * ni == side.shape[0] and side_rows % HALO == 0
    side_spec = pl.BlockSpec((side_rows, side.shape[1]), lambda j, i: (j * ni + i, 0))

    def wspec(which):
        return pl.BlockSpec((tn, k), lambda j, i: (which * nj + j, 0))

    return pl.pallas_call(
        functools.partial(_conv_proj_kernel, tiles_per_seq=tiles_per_seq),
        out_shape=(jax.ShapeDtypeStruct((m, n), BF16), jax.ShapeDtypeStruct(side.shape, BF16)),
        grid=(nj, ni),
        in_specs=[pl.BlockSpec((tm, k), lambda j, i: (i, 0)),
                  pl.BlockSpec((HALO, k), lambda j, i: (jnp.maximum(i * halo_per_tile - 1, 0), 0)),
                  wspec(0), wspec(1), wspec(2), wspec(3),
                  pl.BlockSpec((CONV_K, tn), lambda j, i: (0, j)),
                  side_spec],
        out_specs=(pl.BlockSpec((tm, tn), lambda j, i: (i, j)), side_spec),
        scratch_shapes=[pltpu.VMEM((k, tn), BF16)] * 4,
        compiler_params=_cparams(2),
        name="conv_proj",
    )(h, h, wt, wt, wt, wt, conv_w, side)


def _rotary(x, c, sa, sb):
    half = ROT_DIM // 2
    return (x * c + pltpu.roll(x, half, axis=1) * sa
            + pltpu.roll(x, HEAD_DIM - half, axis=1) * sb)


def _compress(src_ref, pos_ref, w1_ref, b1_ref, w2_ref):
    n_rows = src_ref.shape[0] // CMP_STRIDE
    p_acc = jnp.zeros((n_rows, HEAD_DIM), F32)
    q_acc = jnp.zeros((n_rows, HEAD_DIM), F32)
    for r in range(CMP_STRIDE):
        s_r = src_ref[pl.ds(r, n_rows, stride=CMP_STRIDE), :]
        a_r = (s_r + pos_ref[r:r + 1, :]).astype(BF16)
        b_r = (s_r + pos_ref[CMP_STRIDE + r:CMP_STRIDE + r + 1, :]).astype(BF16)
        p_acc += jnp.dot(a_r, w1_ref[r], preferred_element_type=F32)
        q_acc += jnp.dot(b_r, w1_ref[CMP_STRIDE + r], preferred_element_type=F32)
    hid = p_acc + pltpu.roll(q_acc, n_rows - 1, axis=0) + b1_ref[...]
    act = (hid * jax.nn.sigmoid(hid)).astype(BF16)
    return jnp.dot(act, w2_ref[...], preferred_element_type=F32)


def _nsa_prep_kernel(kc_ref, vc_ref, ks_ref, vs_ref, kw_ref, vw_ref, c_ref, sa_ref, sb_ref,
                     kpos_ref, kw1_ref, kb1_ref, kw2_ref, vpos_ref, vw1_ref, vb1_ref, vw2_ref,
                     ksx_ref, vst_ref, kwx_ref, vwt_ref, kcmp_ref, vcmp_ref):
    c, sa, sb = c_ref[...], sa_ref[...], sb_ref[...]
    seq = ks_ref.shape[0]
    n_blk = seq // SLC_LEN
    flag_w = EXT_DIM - HEAD_DIM
    blk = lax.broadcasted_iota(jnp.int32, (seq, flag_w), 0) >> SLC_SHIFT
    onehot = jnp.where(blk == lax.broadcasted_iota(jnp.int32, (seq, flag_w), 1), 1.0, 0.0)
    ksx_ref[:, 0:HEAD_DIM] = _rotary(ks_ref[...], c, sa, sb).astype(BF16)
    ksx_ref[:, HEAD_DIM:] = onehot.astype(BF16)
    pad_flag = jnp.where(lax.broadcasted_iota(jnp.int32, (WINDOW, EXT_DIM), 1) == HEAD_DIM + n_blk, 1.0, 0.0)
    kwx_ref[0:WINDOW, :] = pad_flag.astype(BF16)
    kwx_ref[WINDOW:, 0:HEAD_DIM] = _rotary(kw_ref[...], c, sa, sb).astype(BF16)
    kwx_ref[WINDOW:, HEAD_DIM:] = jnp.zeros((seq, flag_w), BF16)
    n_tiles = seq // KEY_TILE
    pad_tiles = WINDOW // KEY_TILE
    for t in range(pad_tiles):
        vwt_ref[t] = jnp.zeros((HEAD_DIM, KEY_TILE), BF16)
    for t in range(n_tiles):
        rows = pl.ds(t * KEY_TILE, KEY_TILE)
        vst_ref[t] = vs_ref[rows, :].T.astype(BF16)
        vwt_ref[pad_tiles + t] = vw_ref[rows, :].T.astype(BF16)
    kcmp_ref[...] = _compress(kc_ref, kpos_ref, kw1_ref, kb1_ref, kw2_ref).astype(BF16)
    vcmp_ref[...] = _compress(vc_ref, vpos_ref, vw1_ref, vb1_ref, vw2_ref).astype(BF16)


def nsa_prep(z, tabs, cmp_k, cmp_v, batch, seq, col0):
    def zspec(which):
        return pl.BlockSpec((seq, HEAD_DIM), lambda b, g, w=which: (b, col0 + w * N_KV + g))

    tab = pl.BlockSpec((seq, HEAD_DIM), lambda b, g: (0, 0))

    def wspecs():
        return [pl.BlockSpec((CMP_LEN, HEAD_DIM), lambda b, g: (0, 0)),
                pl.BlockSpec((CMP_LEN, HEAD_DIM, HEAD_DIM), lambda b, g: (0, 0, 0)),
                pl.BlockSpec((1, HEAD_DIM), lambda b, g: (0, 0)),
                pl.BlockSpec((HEAD_DIM, HEAD_DIM), lambda b, g: (0, 0))]

    def out(shape):
        nd = len(shape)
        spec = pl.BlockSpec((None, None) + shape, lambda b, g: (b, g) + (0,) * nd)
        return spec, jax.ShapeDtypeStruct((batch, N_KV) + shape, BF16)

    n_tiles = seq // KEY_TILE
    pad_tiles = WINDOW // KEY_TILE
    outs = [out((seq, EXT_DIM)), out((n_tiles, HEAD_DIM, KEY_TILE)),
            out((WINDOW + seq, EXT_DIM)), out((pad_tiles + n_tiles, HEAD_DIM, KEY_TILE)),
            out((seq // CMP_STRIDE, HEAD_DIM)), out((seq // CMP_STRIDE, HEAD_DIM))]
    return pl.pallas_call(
        _nsa_prep_kernel,
        out_shape=tuple(o[1] for o in outs),
        grid=(batch, N_KV),
        in_specs=[zspec(0), zspec(1), zspec(2), zspec(3), zspec(4), zspec(5), tab, tab, tab]
        + wspecs() + wspecs(),
        out_specs=tuple(o[0] for o in outs),
        compiler_params=_cparams(2),
        name="nsa_prep",
    )(z, z, z, z, z, z, *tabs, *cmp_k, *cmp_v)


def _nsa_attn_pair_kernel(q_ref, ng_ref, gl_ref, rot_ref,
                          ksx_ref, vst_ref, kwx_ref, vwt_ref, kcmp_ref, vcmp_ref,
                          o_ref, acc_ref, qx_ref, glt_ref, s_ref, m_ref, l_ref, oc_ref, ps_ref, *,
                          tq, n_grp_step, n_blk):
    gp = pl.program_id(1)
    i = pl.program_id(2)
    q0 = i * tq
    half = ROT_DIM // 2
    qscale = (HEAD_DIM ** -0.5) * math.log2(math.e)
    qw = GQA * HEAD_DIM
    groups = range(n_grp_step)

    def head_lanes(n):
        return slice(n * tq, (n + 1) * tq)

    def lanes4(x):
        return jnp.concatenate([x] * GQA, axis=1)

    rot = rot_ref[...]
    cos2, sin2 = rot[:ROT_DIM], rot[ROT_DIM:]
    tlane = q0 + lax.broadcasted_iota(jnp.int32, (1, tq), 1)
    kk = lax.broadcasted_iota(jnp.int32, (tq, tq), 0)
    tt = lax.broadcasted_iota(jnp.int32, (tq, tq), 1)
    tri_diag = jnp.where(kk <= tt, 0.0, NEG)
    tri_old = jnp.where(kk > tt, 0.0, NEG)
    n_cmp = kcmp_ref.shape[1]
    cend = lax.broadcasted_iota(jnp.int32, (n_cmp, tq), 0) * CMP_STRIDE + (CMP_LEN - 1)
    bias_c = lanes4(jnp.where(cend <= tlane, 0.0, NEG))
    any_c = lanes4(tlane) >= CMP_LEN - 1
    jblk = lax.broadcasted_iota(jnp.int32, (n_blk, tq), 0)
    forced = (jblk == 0) | (jblk == (tlane >> SLC_SHIFT))
    causal_blk = jblk * SLC_LEN <= tlane
    sub = lax.broadcasted_iota(jnp.int32, (SUBLANES, tq), 0)
    n_rest = EXT_DIM - HEAD_DIM - n_blk
    pad_rows = jnp.where(lax.broadcasted_iota(jnp.int32, (n_rest, GQA * tq), 0) == 0, NEG, 0.0).astype(BF16)
    tiles = tq // KEY_TILE

    def sel_scores(gg, ci):
        k0 = pl.multiple_of(ci * tq, tq)
        return jnp.dot(ksx_ref[gg, pl.ds(k0, tq), :], qx_ref[gg], preferred_element_type=F32)

    def sel_update(gg, ci, s):
        vblk = jnp.concatenate([vst_ref[gg, ci * tiles + r] for r in range(tiles)], axis=1)
        m_old = m_ref[gg]
        m_new = jnp.maximum(m_old, jnp.max(s, axis=0, keepdims=True))
        alpha = jnp.exp2(m_old - m_new)
        p = jnp.exp2(s - m_new)
        l_ref[gg] = alpha * l_ref[gg] + jnp.sum(p, axis=0, keepdims=True)
        acc_ref[gg] = alpha * acc_ref[gg] + jnp.dot(vblk, p.astype(BF16), preferred_element_type=F32)
        m_ref[gg] = m_new

    for gg in groups:
        q = q_ref[:, gg * qw:(gg + 1) * qw]
        qts = []
        for n in range(GQA):
            qt = (q[:, n * HEAD_DIM:(n + 1) * HEAD_DIM] * qscale).T
            qts.append(qt)
            top = qt[:ROT_DIM]
            swapped = jnp.concatenate([top[half:], top[:half]], axis=0)
            qrt = jnp.concatenate([top * cos2 + swapped * sin2, qt[ROT_DIM:]], axis=0)
            qx_ref[gg, 0:HEAD_DIM, head_lanes(n)] = qrt.astype(BF16)
        qt_all = jnp.concatenate(qts, axis=1).astype(BF16)

        s_c = jnp.dot(kcmp_ref[gg], qt_all, preferred_element_type=F32) + bias_c
        m_c = jnp.where(any_c, jnp.max(s_c, axis=0, keepdims=True), 0.0)
        e_c = jnp.exp2(s_c - m_c)
        den_c = jnp.sum(e_c, axis=0, keepdims=True)
        p_c = e_c / jnp.where(den_c > 0, den_c, 1.0)
        vcmp_t = vcmp_ref[gg].astype(F32).T.astype(BF16)
        oc_ref[gg] = jnp.dot(vcmp_t, p_c.astype(BF16), preferred_element_type=F32)

        p_sum = p_c[:, head_lanes(0)]
        for n in range(1, GQA):
            p_sum = p_sum + p_c[:, head_lanes(n)]
        for t in range(tq // LANES):
            ps_ref[gg, t] = p_sum[:, t * LANES:(t + 1) * LANES]
        taps = [jnp.concatenate([ps_ref[gg, t, pl.ds(k, n_blk, stride=CMP_PER_SLC), :]
                                 for t in range(tq // LANES)], axis=1) for k in range(CMP_PER_SLC)]
        imp = taps[0]
        for k in range(1, CMP_PER_SLC):
            imp = imp + taps[k]
        for k in range(CMP_BACK):
            older = pltpu.roll(taps[CMP_PER_SLC - 1 - k], 1, axis=0)
            imp = imp + jnp.where(jblk == 0, 0.0, older)
        imp = jnp.where(forced, jnp.inf, jnp.where(causal_blk, imp, -jnp.inf))
        n_sub = n_blk // SUBLANES
        part = [imp[a * SUBLANES:(a + 1) * SUBLANES] for a in range(n_sub)]
        cnt = [jnp.zeros((SUBLANES, tq), F32) for _ in range(n_sub)]
        for r in range(n_blk):
            row = imp[r:r + 1, :]
            a_r, r_in = divmod(r, SUBLANES)
            for a in range(n_sub):
                if a > a_r:
                    cnt[a] = cnt[a] + jnp.where(row >= part[a], 1.0, 0.0)
                elif a < a_r:
                    cnt[a] = cnt[a] + jnp.where(row > part[a], 1.0, 0.0)
                else:
                    cnt[a] = cnt[a] + jnp.where(sub > r_in, jnp.where(row >= part[a], 1.0, 0.0),
                                                jnp.where(row > part[a], 1.0, 0.0))
        cnt = jnp.concatenate(cnt, axis=0)
        sel_bias = jnp.where(cnt < N_SEL, jnp.where(imp > -jnp.inf, 0.0, NEG), NEG)
        qx_ref[gg, HEAD_DIM:HEAD_DIM + n_blk, :] = lanes4(sel_bias).astype(BF16)
        qx_ref[gg, HEAD_DIM + n_blk:, :] = pad_rows

        m_ref[gg] = jnp.full(m_ref.shape[1:], NEG, F32)
        l_ref[gg] = jnp.zeros(l_ref.shape[1:], F32)
        acc_ref[gg] = jnp.zeros(acc_ref.shape[1:], F32)
        s_ref[gg, 0] = sel_scores(gg, 0)

    def sel_pair(pi, carry):
        c0 = 2 * pi
        for gg in groups:
            s_ref[gg, 1] = sel_scores(gg, c0 + 1)
        for gg in groups:
            sel_update(gg, c0, s_ref[gg, 0])
        for gg in groups:
            s_ref[gg, 0] = sel_scores(gg, c0 + 2)
        for gg in groups:
            sel_update(gg, c0 + 1, s_ref[gg, 1])
        return carry

    lax.fori_loop(0, i // 2, sel_pair, 0)

    @pl.when(i % 2 == 1)
    def _():
        for gg in groups:
            s_ref[gg, 1] = sel_scores(gg, i)
        for gg in groups:
            sel_update(gg, i - 1, s_ref[gg, 0])

    for gg in groups:
        sel_update(gg, i, s_ref[gg, i % 2] + lanes4(tri_diag))

    n_wt = WINDOW // tq
    vt0 = q0 // KEY_TILE
    glt_ref[...] = jax.nn.sigmoid(gl_ref[...]).T
    for gg in groups:
        kwin = kwx_ref[gg, pl.ds(pl.multiple_of(q0, tq), WINDOW + tq), :]
        vwin = jnp.concatenate([vwt_ref[gg, vt0 + r] for r in range((WINDOW + tq) // KEY_TILE)], axis=1)
        s_w = jnp.dot(kwin, qx_ref[gg], preferred_element_type=F32)
        s_w = jnp.concatenate([s_w[:tq] + lanes4(tri_old), s_w[tq:n_wt * tq],
                               s_w[n_wt * tq:] + lanes4(tri_diag)], axis=0)
        e_w = jnp.exp2(s_w - jnp.max(s_w, axis=0, keepdims=True))
        o_w = (jnp.dot(vwin, e_w.astype(BF16), preferred_element_type=F32)
               / jnp.sum(e_w, axis=0, keepdims=True))
        o_s = acc_ref[gg] / l_ref[gg]
        o_c = oc_ref[gg]
        ng = ng_ref[:, gg * qw:(gg + 1) * qw]
        for n in range(GQA):
            head = (gp * n_grp_step + gg) * GQA + n
            gc, gs, gw = (glt_ref[pl.ds(j * N_HEADS + head, 1), :] for j in range(3))
            sl = head_lanes(n)
            o = (gc * o_c[:, sl] + gs * o_s[:, sl] + gw * o_w[:, sl]).T
            ngh = ng[:, n * HEAD_DIM:(n + 1) * HEAD_DIM]
            col = gg * qw + n * HEAD_DIM
            o_ref[:, col:col + HEAD_DIM] = (o * (ngh * jax.nn.sigmoid(ngh))).astype(o_ref.dtype)


def nsa_attention_pair(z, z_ng, gl, rot_t, prep, batch, seq, tq=256, n_grp_step=2):
    ksx, vst, kwx, vwt, kcmp, vcmp = prep
    assert tq % KEY_TILE == 0 and WINDOW % tq == 0 and WINDOW >= 2 * tq and N_KV % n_grp_step == 0
    n_blk, n_cmp_rows = seq // SLC_LEN, kcmp.shape[2]
    assert n_cmp_rows == CMP_PER_SLC * n_blk
    nq = seq // tq
    qw = GQA * HEAD_DIM * n_grp_step

    def whole(a):
        nd = a.ndim - 2
        return pl.BlockSpec((None, n_grp_step) + a.shape[2:], lambda b, g, i: (b, g) + (0,) * nd)

    lanes = GQA * tq
    return pl.pallas_call(
        functools.partial(_nsa_attn_pair_kernel, tq=tq, n_grp_step=n_grp_step, n_blk=n_blk),
        out_shape=jax.ShapeDtypeStruct((batch * seq, N_HEADS * HEAD_DIM), BF16),
        grid=(batch, N_KV // n_grp_step, nq),
        in_specs=[pl.BlockSpec((tq, qw), lambda b, g, i: (b * nq + i, g)),
                  pl.BlockSpec((tq, qw), lambda b, g, i: (b * nq + i, g)),
                  pl.BlockSpec((tq, gl.shape[1]), lambda b, g, i: (b * nq + i, 0)),
                  pl.BlockSpec((rot_t.shape[0], tq), lambda b, g, i: (0, i)),
                  whole(ksx), whole(vst), whole(kwx), whole(vwt), whole(kcmp), whole(vcmp)],
        out_specs=pl.BlockSpec((tq, qw), lambda b, g, i: (b * nq + i, g)),
        scratch_shapes=[pltpu.VMEM((n_grp_step, HEAD_DIM, lanes), F32),
                        pltpu.VMEM((n_grp_step, EXT_DIM, lanes), BF16),
                        pltpu.VMEM((gl.shape[1], tq), F32),
                        pltpu.VMEM((n_grp_step, 2, tq, lanes), F32),
                        pltpu.VMEM((n_grp_step, 1, lanes), F32),
                        pltpu.VMEM((n_grp_step, 1, lanes), F32),
                        pltpu.VMEM((n_grp_step, HEAD_DIM, lanes), F32),
                        pltpu.VMEM((n_grp_step, tq // LANES, n_cmp_rows, LANES), F32)],
        compiler_params=_cparams(3),
        name="nsa_attention",
    )(z, z_ng, gl, rot_t, ksx, vst, kwx, vwt, kcmp, vcmp)


def _out_proj_norm_kernel(a0_ref, a1_ref, w_ref, x_ref, g_ref, *o_refs, kh, emit_residual):
    acc = jnp.dot(a0_ref[...], w_ref[0:kh, :], preferred_element_type=F32)
    acc += jnp.dot(a1_ref[...], w_ref[kh:, :], preferred_element_type=F32)
    x = x_ref[...] + acc
    y = x * lax.rsqrt(jnp.mean(x * x, axis=-1, keepdims=True) + EPS)
    o_refs[-1][...] = (y * g_ref[...]).astype(o_refs[-1].dtype)
    if emit_residual:
        o_refs[0][...] = x


def out_proj_norm(a0, a0_blk, a1, a1_blk, w, x2d, g, norm_dtype, emit_residual, tm=512, name="out_proj"):
    m, n = x2d.shape
    kh = w.shape[0] // 2
    row = pl.BlockSpec((tm, n), lambda i: (i, 0))
    out_shape = [jax.ShapeDtypeStruct((m, n), norm_dtype)]
    if emit_residual:
        out_shape.insert(0, jax.ShapeDtypeStruct((m, n), F32))
    return pl.pallas_call(
        functools.partial(_out_proj_norm_kernel, kh=kh, emit_residual=emit_residual),
        out_shape=tuple(out_shape),
        grid=(m // tm,),
        in_specs=[pl.BlockSpec((tm, kh), lambda i: (i, a0_blk)),
                  pl.BlockSpec((tm, kh), lambda i: (i, a1_blk)),
                  pl.BlockSpec(w.shape, lambda i: (0, 0), pipeline_mode=pl.Buffered(1)),
                  row,
                  pl.BlockSpec((1, n), lambda i: (0, 0))],
        out_specs=tuple(row for _ in out_shape),
        compiler_params=_cparams(1),
        name=name,
    )(a0, a1, w, x2d, g.reshape(1, n))


def _sgu_v_kernel(h_ref, w_ref, v_ref, mu_ref, rstd_ref, wb_ref, c_ref, s1_ref, s2_ref, *, n_col_tiles):
    j = pl.program_id(1)

    @pl.when(pl.program_id(0) == 0)
    def _():
        wb_ref[j] = w_ref[...].astype(BF16)

    tm, tn = v_ref.shape
    h = h_ref[...]
    for c0 in range(0, tn, V_SUB):
        v = jnp.dot(h, wb_ref[j, :, c0:c0 + V_SUB], preferred_element_type=F32)
        v_ref[:, c0:c0 + V_SUB] = v

        if c0 == 0:
            @pl.when(j == 0)
            def _():
                c_ref[...] = jnp.broadcast_to(jnp.sum(v, axis=-1, keepdims=True) / V_SUB, c_ref.shape)
                s1_ref[...] = jnp.zeros(s1_ref.shape, F32)
                s2_ref[...] = jnp.zeros(s2_ref.shape, F32)

        c = c_ref[...]
        s1, s2 = s1_ref[...], s2_ref[...]
        for r in range(V_SUB // LANES):
            d = v[:, r * LANES:(r + 1) * LANES] - c
            s1 = s1 + d
            s2 = s2 + d * d
        s1_ref[...] = s1
        s2_ref[...] = s2

    @pl.when(j == n_col_tiles - 1)
    def _():
        width = n_col_tiles * tn
        mean_d = jnp.sum(s1, axis=-1, keepdims=True) / width
        var = jnp.sum(s2, axis=-1, keepdims=True) / width - mean_d * mean_d
        mu_ref[...] = c + mean_d
        rstd_ref[...] = jnp.broadcast_to(lax.rsqrt(var + EPS), rstd_ref.shape)


def sgu_v(h, w, col0, tm=512, tn=4 * V_SUB):
    m, k = h.shape
    nj = SGU_W // tn
    j0 = col0 // tn
    stat = jax.ShapeDtypeStruct((m, LANES), F32)
    stat_spec = pl.BlockSpec((tm, LANES), lambda i, j: (i, 0))
    return pl.pallas_call(
        functools.partial(_sgu_v_kernel, n_col_tiles=nj),
        out_shape=(jax.ShapeDtypeStruct((m, SGU_W), F32), stat, stat),
        grid=(m // tm, nj),
        in_specs=[pl.BlockSpec((tm, k), lambda i, j: (i, 0)),
                  pl.BlockSpec((k, tn), lambda i, j: (0, j0 + jnp.where(i == 0, j, 0)),
                               pipeline_mode=pl.Buffered(1))],
        out_specs=(pl.BlockSpec((tm, tn), lambda i, j: (i, j)), stat_spec, stat_spec),
        scratch_shapes=[pltpu.VMEM((nj, k, tn), BF16)] + [pltpu.VMEM((tm, LANES), F32)] * 3,
        compiler_params=_cparams(2),
        name="sgu_v",
    )(h, w)


def _sgu_gate_kernel(h_ref, wu32_ref, wz32_ref, v_ref, mu_ref, rstd_ref, lg_ref, lb_ref, ws_ref, bs_ref,
                     side32_ref, o_ref, side_ref, wu_ref, wz_ref):
    _cast_once([wu32_ref, wz32_ref], [wu_ref, wz_ref])
    side_ref[...] = side32_ref[...].astype(BF16)
    h = h_ref[...]
    tm, tn = v_ref.shape
    reps = GROUP_W // LANES
    mu = jnp.concatenate([mu_ref[...]] * reps, axis=1)
    rstd = jnp.concatenate([rstd_ref[...]] * reps, axis=1)
    tri = (lax.broadcasted_iota(jnp.int32, (CHUNK, CHUNK), 1)
           <= lax.broadcasted_iota(jnp.int32, (CHUNK, CHUNK), 0))
    for gi in range(tn // GROUP_W):
        cols = slice(gi * GROUP_W, (gi + 1) * GROUP_W)
        u = jnp.dot(h, wu_ref[:, cols], preferred_element_type=F32)
        zg = jnp.dot(h, wz_ref[:, cols], preferred_element_type=F32)
        vn = ((v_ref[:, cols] - mu) * rstd * lg_ref[:, cols] + lb_ref[:, cols]).astype(BF16)
        wsm = jnp.where(tri, ws_ref[gi], 0.0).astype(BF16)
        bsg = bs_ref[gi]
        mix = jnp.concatenate(
            [jnp.dot(wsm, vn[c * CHUNK:(c + 1) * CHUNK], preferred_element_type=F32) + bsg
             for c in range(tm // CHUNK)], axis=0)
        o_ref[:, cols] = (u * mix * (zg * jax.nn.sigmoid(zg))).astype(o_ref.dtype)


def sgu_gate(h, w, col_u, col_z, v, mu, rstd, ln_g, ln_b, w_s, b_s, side, tm=1024, tn=512):
    m, k = h.shape
    n = SGU_W
    nj, ni = n // tn, m // tm
    ju, jz = col_u // tn, col_z // tn
    gpt = tn // GROUP_W
    stat_spec = pl.BlockSpec((tm, LANES), lambda j, i: (i, 0))
    side_rows = side.shape[0] // (nj * ni)
    assert side_rows * nj * ni == side.shape[0] and side_rows % HALO == 0
    side_spec = pl.BlockSpec((side_rows, side.shape[1]), lambda j, i: (j * ni + i, 0))
    return pl.pallas_call(
        _sgu_gate_kernel,
        out_shape=(jax.ShapeDtypeStruct((m, n), BF16), jax.ShapeDtypeStruct(side.shape, BF16)),
        grid=(nj, ni),
        in_specs=[pl.BlockSpec((tm, k), lambda j, i: (i, 0)),
                  pl.BlockSpec((k, tn), lambda j, i: (0, ju + j)),
                  pl.BlockSpec((k, tn), lambda j, i: (0, jz + j)),
                  pl.BlockSpec((tm, tn), lambda j, i: (i, j)),
                  stat_spec, stat_spec,
                  pl.BlockSpec((1, tn), lambda j, i: (0, j)),
                  pl.BlockSpec((1, tn), lambda j, i: (0, j)),
                  pl.BlockSpec((gpt, CHUNK, CHUNK), lambda j, i: (j, 0, 0)),
                  pl.BlockSpec((gpt, CHUNK, 1), lambda j, i: (j, 0, 0)),
                  side_spec],
        out_specs=(pl.BlockSpec((tm, tn), lambda j, i: (i, j)), side_spec),
        scratch_shapes=[pltpu.VMEM((k, tn), BF16)] * 2,
        compiler_params=_cparams(2),
        name="sgu_gate",
    )(h, w, w, v, mu, rstd, ln_g.reshape(1, n), ln_b.reshape(1, n), w_s, b_s.reshape(N_GROUPS, CHUNK, 1),
      side)


def _rotary_tables(seq):
    half = ROT_DIM // 2
    inv_freq = jnp.power(ROPE_THETA, -jnp.arange(half, dtype=F32) * 2.0 / ROT_DIM)
    ang = jnp.arange(seq).astype(F32)[:, None] * inv_freq[None, :]
    cos, sin = jnp.cos(ang), jnp.sin(ang)
    rest = HEAD_DIM - ROT_DIM
    c = jnp.concatenate([cos, cos, jnp.ones((seq, rest), F32)], axis=1)
    sa = jnp.concatenate([jnp.zeros((seq, half), F32), sin, jnp.zeros((seq, rest), F32)], axis=1)
    sb = jnp.concatenate([-sin, jnp.zeros((seq, half + rest), F32)], axis=1)
    rot_t = jnp.concatenate([cos, cos, -sin, sin], axis=1).T
    return (c, sa, sb), rot_t


def kernel(x, norm_even, w_in_even, conv_w, cmp_k_pos, cmp_k_w1, cmp_k_b1, cmp_k_w2, cmp_v_pos, cmp_v_w1, cmp_v_b1, cmp_v_w2, w_out_even, norm_odd, w_in_odd, sgu_ln_g, sgu_ln_b, sgu_w_s, sgu_b_s, w_out_odd, norm_final):
    batch, seq, d = x.shape
    m = batch * seq
    x2d = x.reshape(m, d)

    wt_in = jnp.swapaxes(w_in_even[0], 0, 1)
    cw = CONV_W
    qw = N_HEADS * HEAD_DIM
    kvw = N_KV * HEAD_DIM
    o_q = 4 * cw
    o_kv = o_q + qw
    o_gl = o_kv + 6 * kvw
    o_ng = o_gl + 3 * N_HEADS

    h0, gl = rmsnorm_proj(x2d, norm_even[0], wt_in, o_gl, LANES)
    y_conv, w_out0 = conv_proj(h0, wt_in, conv_w[0], seq, w_out_even[0])
    z = matmul_t(h0, wt_in, o_q, qw + 6 * kvw, tn=1024, name="nsa_proj")
    z_ng = matmul_t(h0, wt_in, o_ng, qw, tn=1024, name="nsa_gate_proj")

    tabs, rot_t = _rotary_tables(seq)
    cmp_k = (cmp_k_pos[0], cmp_k_w1[0].astype(BF16).reshape(CMP_LEN, HEAD_DIM, HEAD_DIM),
             cmp_k_b1[0].reshape(1, HEAD_DIM), cmp_k_w2[0].astype(BF16))
    cmp_v = (cmp_v_pos[0], cmp_v_w1[0].astype(BF16).reshape(CMP_LEN, HEAD_DIM, HEAD_DIM),
             cmp_v_b1[0].reshape(1, HEAD_DIM), cmp_v_w2[0].astype(BF16))
    prep = nsa_prep(z, tabs, cmp_k, cmp_v, batch, seq, col0=qw // HEAD_DIM)
    y_nsa = nsa_attention_pair(z, z_ng, gl, rot_t, prep, batch, seq)

    x1, h1 = out_proj_norm(y_conv, 0, y_nsa, 0, w_out0, x2d, norm_odd[0],
                           BF16, True, name="out_proj_even")

    w_in1 = w_in_odd[0]
    v, mu, rstd = sgu_v(h1, w_in1, SGU_W)
    act, w_out1 = sgu_gate(h1, w_in1, 0, 2 * SGU_W, v, mu, rstd,
                           sgu_ln_g[0], sgu_ln_b[0], sgu_w_s[0], sgu_b_s[0], w_out_odd[0])
    (out,) = out_proj_norm(act, 0, act, 1, w_out1, x1, norm_final,
                           F32, False, name="out_proj_odd")
    return out.reshape(batch, seq, d)
```

```python
import functools
import math

import jax
import jax.numpy as jnp
from jax import lax
from jax.experimental import pallas as pl
from jax.experimental.pallas import tpu as pltpu

F32 = jnp.float32
BF16 = jnp.bfloat16

D_MODEL = 2048
MIX = 2 * D_MODEL
CONV_W = MIX // 2
CONV_K = 3
HEAD_DIM = 128
N_HEADS = 16
N_KV = 4
GQA = N_HEADS // N_KV
ROT_DIM = HEAD_DIM // 4
ROPE_THETA = 500000.0
CMP_LEN = 32
CMP_STRIDE = 16
SLC_LEN = 64
N_SEL = 8
CMP_PER_SLC = SLC_LEN // CMP_STRIDE
CMP_BACK = CMP_LEN // CMP_STRIDE - 1
WINDOW = 512
SGU_W = MIX
CHUNK = 128
N_GROUPS = 16
GROUP_W = SGU_W // N_GROUPS
EPS = 1e-6

LANES = 128
SUBLANES = 8
SLC_SHIFT = 6
HALO = 16
KEY_TILE = LANES
V_SUB = 512
XPOSE_ROWS = 256
VMEM_LIMIT = 56 * 1024 * 1024

NEG = -1e30

EXT_DIM = 2 * HEAD_DIM


def _cparams(n_axes):
    return pltpu.CompilerParams(
        dimension_semantics=("arbitrary",) * n_axes, vmem_limit_bytes=VMEM_LIMIT)


def _rmsnorm_proj_kernel(x_ref, g_ref, wt_ref, o_ref, p_ref, wb_ref):
    @pl.when(pl.program_id(0) == 0)
    def _():
        wb_ref[...] = wt_ref[...].T.astype(BF16)

    x = x_ref[...]
    y = x * lax.rsqrt(jnp.mean(x * x, axis=-1, keepdims=True) + EPS)
    hb = (y * g_ref[...]).astype(BF16)
    o_ref[...] = hb
    p_ref[...] = jnp.dot(hb, wb_ref[...], preferred_element_type=F32)


def rmsnorm_proj(x2d, g, wt, row0, n, tm=512):
    m, d = x2d.shape
    assert row0 % n == 0
    return pl.pallas_call(
        _rmsnorm_proj_kernel,
        out_shape=(jax.ShapeDtypeStruct((m, d), BF16), jax.ShapeDtypeStruct((m, n), F32)),
        grid=(m // tm,),
        in_specs=[pl.BlockSpec((tm, d), lambda i: (i, 0)),
                  pl.BlockSpec((1, d), lambda i: (0, 0)),
                  pl.BlockSpec((n, d), lambda i: (row0 // n, 0))],
        out_specs=(pl.BlockSpec((tm, d), lambda i: (i, 0)),
                   pl.BlockSpec((tm, n), lambda i: (i, 0))),
        scratch_shapes=[pltpu.VMEM((d, n), BF16)],
        compiler_params=_cparams(1),
        name="rmsnorm_gates",
    )(x2d, g.reshape(1, d), wt)


def _cast_once(w_refs, wb_refs, transposed=False):
    @pl.when(pl.program_id(1) == 0)
    def _():
        for w_ref, wb_ref in zip(w_refs, wb_refs):
            if transposed:
                for r in range(0, w_ref.shape[0], XPOSE_ROWS):
                    wb_ref[:, r:r + XPOSE_ROWS] = w_ref[r:r + XPOSE_ROWS, :].T.astype(BF16)
            else:
                wb_ref[...] = w_ref[...].astype(BF16)


def _matmul_t_kernel(a_ref, wt_ref, o_ref, wb_ref):
    _cast_once([wt_ref], [wb_ref], transposed=True)
    o_ref[...] = jnp.dot(a_ref[...], wb_ref[...], preferred_element_type=F32)


def matmul_t(a, wt, segments, tm=1024, tn=512, name="proj"):
    m, k = a.shape
    starts = []
    for row0, seg_n in segments:
        assert seg_n % tn == 0 and row0 % HALO == 0
        starts += [row0 + t * tn for t in range(seg_n // tn)]
    n = tn * len(starts)

    def wrow(j):
        row = starts[-1]
        for idx in range(len(starts) - 2, -1, -1):
            row = jnp.where(j <= idx, starts[idx], row)
        return pl.multiple_of(row, HALO)

    wspec = pl.BlockSpec((pl.Element(tn), pl.Element(k)), lambda j, i: (wrow(j), 0))
    return pl.pallas_call(
        _matmul_t_kernel,
        out_shape=jax.ShapeDtypeStruct((m, n), F32),
        grid=(n // tn, m // tm),
        in_specs=[pl.BlockSpec((tm, k), lambda j, i: (i, 0)), wspec],
        out_specs=pl.BlockSpec((tm, tn), lambda j, i: (i, j)),
        scratch_shapes=[pltpu.VMEM((k, tn), BF16)],
        compiler_params=_cparams(2),
        name=name,
    )(a, wt)


def _conv_proj_kernel(h_ref, hp_ref, wb32_ref, wc32_ref, wh32_ref, wg32_ref, cw_ref, side32_ref,
                      o_ref, side_ref, wb_ref, wc_ref, wh_ref, wg_ref, *, tiles_per_seq):
    _cast_once([wb32_ref, wc32_ref, wh32_ref, wg32_ref], [wb_ref, wc_ref, wh_ref, wg_ref],
               transposed=True)
    side_ref[...] = side32_ref[...].astype(BF16)
    i = pl.program_id(1)
    cw = cw_ref[...]
    hp = hp_ref[...]
    pp = (jnp.dot(hp, wc_ref[...], preferred_element_type=F32)
          * jnp.dot(hp, wh_ref[...], preferred_element_type=F32))
    pp = jnp.where(i % tiles_per_seq == 0, 0.0, pp)
    prev1, prev2 = pp[HALO - 1:HALO, :], pp[HALO - 2:HALO - 1, :]
    h = h_ref[...]
    cb = jnp.dot(h, wb_ref[...], preferred_element_type=F32)
    cc = jnp.dot(h, wc_ref[...], preferred_element_type=F32)
    ch = jnp.dot(h, wh_ref[...], preferred_element_type=F32)
    cg = jnp.dot(h, wg_ref[...], preferred_element_type=F32)
    p = cc * ch
    row = lax.broadcasted_iota(jnp.int32, (SUBLANES, p.shape[1]), 0)
    p1, p2 = pltpu.roll(p, 1, axis=0), pltpu.roll(p, 2, axis=0)
    p1 = jnp.concatenate([jnp.where(row == 0, prev1, p1[:SUBLANES]), p1[SUBLANES:]], axis=0)
    p2 = jnp.concatenate([jnp.where(row == 0, prev2, jnp.where(row == 1, prev1, p2[:SUBLANES])),
                          p2[SUBLANES:]], axis=0)
    conv = cw[0:1, :] * p2 + cw[1:2, :] * p1 + cw[2:3, :] * p
    o_ref[...] = (cb * conv * (cg * jax.nn.sigmoid(cg))).astype(o_ref.dtype)


def conv_proj(h, wt, conv_w, seq, side, tm=1024, tn=256):
    m, k = h.shape
    n = conv_w.shape[1]
    tiles_per_seq = seq // tm
    halo_per_tile = tm // HALO
    nj, ni = n // tn, m // tm
    side_rows = side.shape[0] // (nj * ni)
    assert side_rows * nj * ni == side.shape[0] and side_rows % HALO == 0
    side_spec = pl.BlockSpec((side_rows, side.shape[1]), lambda j, i: (j * ni + i, 0))

    def wspec(which):
        return pl.BlockSpec((tn, k), lambda j, i: (which * nj + j, 0))

    return pl.pallas_call(
        functools.partial(_conv_proj_kernel, tiles_per_seq=tiles_per_seq),
        out_shape=(jax.ShapeDtypeStruct((m, n), BF16), jax.ShapeDtypeStruct(side.shape, BF16)),
        grid=(nj, ni),
        in_specs=[pl.BlockSpec((tm, k), lambda j, i: (i, 0)),
                  pl.BlockSpec((HALO, k), lambda j, i: (jnp.maximum(i * halo_per_tile - 1, 0), 0)),
                  wspec(0), wspec(1), wspec(2), wspec(3),
                  pl.BlockSpec((CONV_K, tn), lambda j, i: (0, j)),
                  side_spec],
        out_specs=(pl.BlockSpec((tm, tn), lambda j, i: (i, j)), side_spec),
        scratch_shapes=[pltpu.VMEM((k, tn), BF16)] * 4,
        compiler_params=_cparams(2),
        name="conv_proj",
    )(h, h, wt, wt, wt, wt, conv_w, side)


def _rotary(x, c, sa, sb):
    half = ROT_DIM // 2
    return (x * c + pltpu.roll(x, half, axis=1) * sa
            + pltpu.roll(x, HEAD_DIM - half, axis=1) * sb)


def _compress(src_ref, pos_ref, w1_ref, b1_ref, w2_ref):
    n_rows = src_ref.shape[0] // CMP_STRIDE
    p_acc = jnp.zeros((n_rows, HEAD_DIM), F32)
    q_acc = jnp.zeros((n_rows, HEAD_DIM), F32)
    for r in range(CMP_STRIDE):
        s_r = src_ref[pl.ds(r, n_rows, stride=CMP_STRIDE), :]
        a_r = (s_r + pos_ref[r:r + 1, :]).astype(BF16)
        b_r = (s_r + pos_ref[CMP_STRIDE + r:CMP_STRIDE + r + 1, :]).astype(BF16)
        p_acc += jnp.dot(a_r, w1_ref[r], preferred_element_type=F32)
        q_acc += jnp.dot(b_r, w1_ref[CMP_STRIDE + r], preferred_element_type=F32)
    hid = p_acc + pltpu.roll(q_acc, n_rows - 1, axis=0) + b1_ref[...]
    act = (hid * jax.nn.sigmoid(hid)).astype(BF16)
    return jnp.dot(act, w2_ref[...], preferred_element_type=F32)


def _nsa_prep_kernel(kc_ref, vc_ref, ks_ref, vs_ref, kw_ref, vw_ref, c_ref, sa_ref, sb_ref,
                     kpos_ref, kw1_ref, kb1_ref, kw2_ref, vpos_ref, vw1_ref, vb1_ref, vw2_ref,
                     ksx_ref, vst_ref, kwx_ref, vwt_ref, kcmp_ref, vcmp_ref):
    c, sa, sb = c_ref[...], sa_ref[...], sb_ref[...]
    seq = ks_ref.shape[0]
    n_blk = seq // SLC_LEN
    flag_w = EXT_DIM - HEAD_DIM
    blk = lax.broadcasted_iota(jnp.int32, (seq, flag_w), 0) >> SLC_SHIFT
    onehot = jnp.where(blk == lax.broadcasted_iota(jnp.int32, (seq, flag_w), 1), 1.0, 0.0)
    ksx_ref[:, 0:HEAD_DIM] = _rotary(ks_ref[...], c, sa, sb).astype(BF16)
    ksx_ref[:, HEAD_DIM:] = onehot.astype(BF16)
    pad_flag = jnp.where(lax.broadcasted_iota(jnp.int32, (WINDOW, EXT_DIM), 1) == HEAD_DIM + n_blk, 1.0, 0.0)
    kwx_ref[0:WINDOW, :] = pad_flag.astype(BF16)
    kwx_ref[WINDOW:, 0:HEAD_DIM] = _rotary(kw_ref[...], c, sa, sb).astype(BF16)
    kwx_ref[WINDOW:, HEAD_DIM:] = jnp.zeros((seq, flag_w), BF16)
    n_tiles = seq // KEY_TILE
    pad_tiles = WINDOW // KEY_TILE
    for t in range(pad_tiles):
        vwt_ref[t] = jnp.zeros((HEAD_DIM, KEY_TILE), BF16)
    for t in range(n_tiles):
        rows = pl.ds(t * KEY_TILE, KEY_TILE)
        vst_ref[t] = vs_ref[rows, :].T.astype(BF16)
        vwt_ref[pad_tiles + t] = vw_ref[rows, :].T.astype(BF16)
    kcmp_ref[...] = _compress(kc_ref, kpos_ref, kw1_ref, kb1_ref, kw2_ref).astype(BF16)
    vcmp_ref[...] = _compress(vc_ref, vpos_ref, vw1_ref, vb1_ref, vw2_ref).astype(BF16)


def nsa_prep(z, tabs, cmp_k, cmp_v, batch, seq, col0):
    def zspec(which):
        return pl.BlockSpec((seq, HEAD_DIM), lambda b, g, w=which: (b, col0 + w * N_KV + g))

    tab = pl.BlockSpec((seq, HEAD_DIM), lambda b, g: (0, 0))

    def wspecs():
        return [pl.BlockSpec((CMP_LEN, HEAD_DIM), lambda b, g: (0, 0)),
                pl.BlockSpec((CMP_LEN, HEAD_DIM, HEAD_DIM), lambda b, g: (0, 0, 0)),
                pl.BlockSpec((1, HEAD_DIM), lambda b, g: (0, 0)),
                pl.BlockSpec((HEAD_DIM, HEAD_DIM), lambda b, g: (0, 0))]

    def out(shape):
        nd = len(shape)
        spec = pl.BlockSpec((None, None) + shape, lambda b, g: (b, g) + (0,) * nd)
        return spec, jax.ShapeDtypeStruct((batch, N_KV) + shape, BF16)

    n_tiles = seq // KEY_TILE
    pad_tiles = WINDOW // KEY_TILE
    outs = [out((seq, EXT_DIM)), out((n_tiles, HEAD_DIM, KEY_TILE)),
            out((WINDOW + seq, EXT_DIM)), out((pad_tiles + n_tiles, HEAD_DIM, KEY_TILE)),
            out((seq // CMP_STRIDE, HEAD_DIM)), out((seq // CMP_STRIDE, HEAD_DIM))]
    return pl.pallas_call(
        _nsa_prep_kernel,
        out_shape=tuple(o[1] for o in outs),
        grid=(batch, N_KV),
        in_specs=[zspec(0), zspec(1), zspec(2), zspec(3), zspec(4), zspec(5), tab, tab, tab]
        + wspecs() + wspecs(),
        out_specs=tuple(o[0] for o in outs),
        compiler_params=_cparams(2),
        name="nsa_prep",
    )(z, z, z, z, z, z, *tabs, *cmp_k, *cmp_v)


def _nsa_attn_pair_kernel(q_ref, ng_ref, gl_ref, rot_ref,
                          ksx_ref, vst_ref, kwx_ref, vwt_ref, kcmp_ref, vcmp_ref,
                          o_ref, acc_ref, qx_ref, glt_ref, s_ref, m_ref, l_ref, oc_ref, ps_ref, *,
                          tq, n_grp_step, n_blk):
    gp = pl.program_id(1)
    i = pl.program_id(2)
    q0 = i * tq
    half = ROT_DIM // 2
    qscale = (HEAD_DIM ** -0.5) * math.log2(math.e)
    qw = GQA * HEAD_DIM
    groups = range(n_grp_step)

    def head_lanes(n):
        return slice(n * tq, (n + 1) * tq)

    def lanes4(x):
        return jnp.concatenate([x] * GQA, axis=1)

    rot = rot_ref[...]
    cos2, sin2 = rot[:ROT_DIM], rot[ROT_DIM:]
    tlane = q0 + lax.broadcasted_iota(jnp.int32, (1, tq), 1)
    kk = lax.broadcasted_iota(jnp.int32, (tq, tq), 0)
    tt = lax.broadcasted_iota(jnp.int32, (tq, tq), 1)
    tri_diag = jnp.where(kk <= tt, 0.0, NEG)
    tri_old = jnp.where(kk > tt, 0.0, NEG)
    n_cmp = kcmp_ref.shape[1]
    cend = lax.broadcasted_iota(jnp.int32, (n_cmp, tq), 0) * CMP_STRIDE + (CMP_LEN - 1)
    bias_c = lanes4(jnp.where(cend <= tlane, 0.0, NEG))
    any_c = lanes4(tlane) >= CMP_LEN - 1
    jblk = lax.broadcasted_iota(jnp.int32, (n_blk, tq), 0)
    forced = (jblk == 0) | (jblk == (tlane >> SLC_SHIFT))
    causal_blk = jblk * SLC_LEN <= tlane
    sub = lax.broadcasted_iota(jnp.int32, (SUBLANES, tq), 0)
    n_rest = EXT_DIM - HEAD_DIM - n_blk
    pad_rows = jnp.where(lax.broadcasted_iota(jnp.int32, (n_rest, GQA * tq), 0) == 0, NEG, 0.0).astype(BF16)
    tiles = tq // KEY_TILE

    def sel_scores(gg, ci):
        k0 = pl.multiple_of(ci * tq, tq)
        return jnp.dot(ksx_ref[gg, pl.ds(k0, tq), :], qx_ref[gg], preferred_element_type=F32)

    def sel_update(gg, ci, s):
        vblk = jnp.concatenate([vst_ref[gg, ci * tiles + r] for r in range(tiles)], axis=1)
        m_old = m_ref[gg]
        m_new = jnp.maximum(m_old, jnp.max(s, axis=0, keepdims=True))
        alpha = jnp.exp2(m_old - m_new)
        p = jnp.exp2(s - m_new)
        l_ref[gg] = alpha * l_ref[gg] + jnp.sum(p, axis=0, keepdims=True)
        acc_ref[gg] = alpha * acc_ref[gg] + jnp.dot(vblk, p.astype(BF16), preferred_element_type=F32)
        m_ref[gg] = m_new

    for gg in groups:
        q = q_ref[:, gg * qw:(gg + 1) * qw]
        qts = []
        for n in range(GQA):
            qt = (q[:, n * HEAD_DIM:(n + 1) * HEAD_DIM] * qscale).T
            qts.append(qt)
            top = qt[:ROT_DIM]
            swapped = jnp.concatenate([top[half:], top[:half]], axis=0)
            qrt = jnp.concatenate([top * cos2 + swapped * sin2, qt[ROT_DIM:]], axis=0)
            qx_ref[gg, 0:HEAD_DIM, head_lanes(n)] = qrt.astype(BF16)
        qt_all = jnp.concatenate(qts, axis=1).astype(BF16)

        s_c = jnp.dot(kcmp_ref[gg], qt_all, preferred_element_type=F32) + bias_c
        m_c = jnp.where(any_c, jnp.max(s_c, axis=0, keepdims=True), 0.0)
        e_c = jnp.exp2(s_c - m_c)
        den_c = jnp.sum(e_c, axis=0, keepdims=True)
        p_c = e_c / jnp.where(den_c > 0, den_c, 1.0)
        vcmp_t = vcmp_ref[gg].astype(F32).T.astype(BF16)
        oc_ref[gg] = jnp.dot(vcmp_t, p_c.astype(BF16), preferred_element_type=F32)

        p_sum = p_c[:, head_lanes(0)]
        for n in range(1, GQA):
            p_sum = p_sum + p_c[:, head_lanes(n)]
        for t in range(tq // LANES):
            ps_ref[gg, t] = p_sum[:, t * LANES:(t + 1) * LANES]
        taps = [jnp.concatenate([ps_ref[gg, t, pl.ds(k, n_blk, stride=CMP_PER_SLC), :]
                                 for t in range(tq // LANES)], axis=1) for k in range(CMP_PER_SLC)]
        imp = taps[0]
        for k in range(1, CMP_PER_SLC):
            imp = imp + taps[k]
        for k in range(CMP_BACK):
            older = pltpu.roll(taps[CMP_PER_SLC - 1 - k], 1, axis=0)
            imp = imp + jnp.where(jblk == 0, 0.0, older)
        imp = jnp.where(forced, jnp.inf, jnp.where(causal_blk, imp, -jnp.inf))
        n_sub = n_blk // SUBLANES
        part = [imp[a * SUBLANES:(a + 1) * SUBLANES] for a in range(n_sub)]
        cnt = [jnp.zeros((SUBLANES, tq), F32) for _ in range(n_sub)]
        for r in range(n_blk):
            row = imp[r:r + 1, :]
            a_r, r_in = divmod(r, SUBLANES)
            for a in range(n_sub):
                if a > a_r:
                    cnt[a] = cnt[a] + jnp.where(row >= part[a], 1.0, 0.0)
                elif a < a_r:
                    cnt[a] = cnt[a] + jnp.where(row > part[a], 1.0, 0.0)
                else:
                    cnt[a] = cnt[a] + jnp.where(sub > r_in, jnp.where(row >= part[a], 1.0, 0.0),
                                                jnp.where(row > part[a], 1.0, 0.0))
        cnt = jnp.concatenate(cnt, axis=0)
        sel_bias = jnp.where(cnt < N_SEL, jnp.where(imp > -jnp.inf, 0.0, NEG), NEG)
        qx_ref[gg, HEAD_DIM:HEAD_DIM + n_blk, :] = lanes4(sel_bias).astype(BF16)
        qx_ref[gg, HEAD_DIM + n_blk:, :] = pad_rows

        m_ref[gg] = jnp.full(m_ref.shape[1:], NEG, F32)
        l_ref[gg] = jnp.zeros(l_ref.shape[1:], F32)
        acc_ref[gg] = jnp.zeros(acc_ref.shape[1:], F32)
        s_ref[gg, 0] = sel_scores(gg, 0)

    def sel_pair(pi, carry):
        c0 = 2 * pi
        for gg in groups:
            s_ref[gg, 1] = sel_scores(gg, c0 + 1)
        for gg in groups:
            sel_update(gg, c0, s_ref[gg, 0])
        for gg in groups:
            s_ref[gg, 0] = sel_scores(gg, c0 + 2)
        for gg in groups:
            sel_update(gg, c0 + 1, s_ref[gg, 1])
        return carry

    lax.fori_loop(0, i // 2, sel_pair, 0)

    @pl.when(i % 2 == 1)
    def _():
        for gg in groups:
            s_ref[gg, 1] = sel_scores(gg, i)
        for gg in groups:
            sel_update(gg, i - 1, s_ref[gg, 0])

    for gg in groups:
        sel_update(gg, i, s_ref[gg, i % 2] + lanes4(tri_diag))

    n_wt = WINDOW // tq
    vt0 = q0 // KEY_TILE
    glt_ref[...] = jax.nn.sigmoid(gl_ref[...]).T
    for gg in groups:
        kwin = kwx_ref[gg, pl.ds(pl.multiple_of(q0, tq), WINDOW + tq), :]
        vwin = jnp.concatenate([vwt_ref[gg, vt0 + r] for r in range((WINDOW + tq) // KEY_TILE)], axis=1)
        s_w = jnp.dot(kwin, qx_ref[gg], preferred_element_type=F32)
        s_w = jnp.concatenate([s_w[:tq] + lanes4(tri_old), s_w[tq:n_wt * tq],
                               s_w[n_wt * tq:] + lanes4(tri_diag)], axis=0)
        e_w = jnp.exp2(s_w - jnp.max(s_w, axis=0, keepdims=True))
        o_w = (jnp.dot(vwin, e_w.astype(BF16), preferred_element_type=F32)
               / jnp.sum(e_w, axis=0, keepdims=True))
        o_s = acc_ref[gg] / l_ref[gg]
        o_c = oc_ref[gg]
        ng = ng_ref[:, gg * qw:(gg + 1) * qw]
        for n in range(GQA):
            head = (gp * n_grp_step + gg) * GQA + n
            gc, gs, gw = (glt_ref[pl.ds(j * N_HEADS + head, 1), :] for j in range(3))
            sl = head_lanes(n)
            o = (gc * o_c[:, sl] + gs * o_s[:, sl] + gw * o_w[:, sl]).T
            ngh = ng[:, n * HEAD_DIM:(n + 1) * HEAD_DIM]
            col = gg * qw + n * HEAD_DIM
            o_ref[:, col:col + HEAD_DIM] = (o * (ngh * jax.nn.sigmoid(ngh))).astype(o_ref.dtype)


def nsa_attention_pair(z, ng_col, gl, rot_t, prep, batch, seq, tq=256, n_grp_step=2):
    ksx, vst, kwx, vwt, kcmp, vcmp = prep
    assert tq % KEY_TILE == 0 and WINDOW % tq == 0 and WINDOW >= 2 * tq and N_KV % n_grp_step == 0
    n_blk, n_cmp_rows = seq // SLC_LEN, kcmp.shape[2]
    assert n_cmp_rows == CMP_PER_SLC * n_blk
    nq = seq // tq
    qw = GQA * HEAD_DIM * n_grp_step

    def whole(a):
        nd = a.ndim - 2
        return pl.BlockSpec((None, n_grp_step) + a.shape[2:], lambda b, g, i: (b, g) + (0,) * nd)

    lanes = GQA * tq
    return pl.pallas_call(
        functools.partial(_nsa_attn_pair_kernel, tq=tq, n_grp_step=n_grp_step, n_blk=n_blk),
        out_shape=jax.ShapeDtypeStruct((batch * seq, N_HEADS * HEAD_DIM), BF16),
        grid=(batch, N_KV // n_grp_step, nq),
        in_specs=[pl.BlockSpec((tq, qw), lambda b, g, i: (b * nq + i, g)),
                  pl.BlockSpec((tq, qw), lambda b, g, i: (b * nq + i, ng_col + g)),
                  pl.BlockSpec((tq, gl.shape[1]), lambda b, g, i: (b * nq + i, 0)),
                  pl.BlockSpec((rot_t.shape[0], tq), lambda b, g, i: (0, i)),
                  whole(ksx), whole(vst), whole(kwx), whole(vwt), whole(kcmp), whole(vcmp)],
        out_specs=pl.BlockSpec((tq, qw), lambda b, g, i: (b * nq + i, g)),
        scratch_shapes=[pltpu.VMEM((n_grp_step, HEAD_DIM, lanes), F32),
                        pltpu.VMEM((n_grp_step, EXT_DIM, lanes), BF16),
                        pltpu.VMEM((gl.shape[1], tq), F32),
                        pltpu.VMEM((n_grp_step, 2, tq, lanes), F32),
                        pltpu.VMEM((n_grp_step, 1, lanes), F32),
                        pltpu.VMEM((n_grp_step, 1, lanes), F32),
                        pltpu.VMEM((n_grp_step, HEAD_DIM, lanes), F32),
                        pltpu.VMEM((n_grp_step, tq // LANES, n_cmp_rows, LANES), F32)],
        compiler_params=_cparams(3),
        name="nsa_attention",
    )(z, z, gl, rot_t, ksx, vst, kwx, vwt, kcmp, vcmp)


def _out_proj_norm_kernel(a0_ref, a1_ref, w_ref, x_ref, g_ref, *o_refs, kh, emit_residual):
    acc = jnp.dot(a0_ref[...], w_ref[0:kh, :], preferred_element_type=F32)
    acc += jnp.dot(a1_ref[...], w_ref[kh:, :], preferred_element_type=F32)
    x = x_ref[...] + acc
    y = x * lax.rsqrt(jnp.mean(x * x, axis=-1, keepdims=True) + EPS)
    o_refs[-1][...] = (y * g_ref[...]).astype(o_refs[-1].dtype)
    if emit_residual:
        o_refs[0][...] = x


def out_proj_norm(a0, a0_blk, a1, a1_blk, w, x2d, g, norm_dtype, emit_residual, tm=512, name="out_proj"):
    m, n = x2d.shape
    kh = w.shape[0] // 2
    row = pl.BlockSpec((tm, n), lambda i: (i, 0))
    out_shape = [jax.ShapeDtypeStruct((m, n), norm_dtype)]
    if emit_residual:
        out_shape.insert(0, jax.ShapeDtypeStruct((m, n), F32))
    return pl.pallas_call(
        functools.partial(_out_proj_norm_kernel, kh=kh, emit_residual=emit_residual),
        out_shape=tuple(out_shape),
        grid=(m // tm,),
        in_specs=[pl.BlockSpec((tm, kh), lambda i: (i, a0_blk)),
                  pl.BlockSpec((tm, kh), lambda i: (i, a1_blk)),
                  pl.BlockSpec(w.shape, lambda i: (0, 0), pipeline_mode=pl.Buffered(1)),
                  row,
                  pl.BlockSpec((1, n), lambda i: (0, 0))],
        out_specs=tuple(row for _ in out_shape),
        compiler_params=_cparams(1),
        name=name,
    )(a0, a1, w, x2d, g.reshape(1, n))


def _sgu_v_kernel(h_ref, w_ref, v_ref, mu_ref, rstd_ref, wb_ref, c_ref, s1_ref, s2_ref, *, n_col_tiles):
    j = pl.program_id(1)

    @pl.when(pl.program_id(0) == 0)
    def _():
        wb_ref[j] = w_ref[...].astype(BF16)

    tm, tn = v_ref.shape
    h = h_ref[...]
    for c0 in range(0, tn, V_SUB):
        v = jnp.dot(h, wb_ref[j, :, c0:c0 + V_SUB], preferred_element_type=F32)
        v_ref[:, c0:c0 + V_SUB] = v

        if c0 == 0:
            @pl.when(j == 0)
            def _():
                c_ref[...] = jnp.broadcast_to(jnp.sum(v, axis=-1, keepdims=True) / V_SUB, c_ref.shape)
                s1_ref[...] = jnp.zeros(s1_ref.shape, F32)
                s2_ref[...] = jnp.zeros(s2_ref.shape, F32)

        c = c_ref[...]
        s1, s2 = s1_ref[...], s2_ref[...]
        for r in range(V_SUB // LANES):
            d = v[:, r * LANES:(r + 1) * LANES] - c
            s1 = s1 + d
            s2 = s2 + d * d
        s1_ref[...] = s1
        s2_ref[...] = s2

    @pl.when(j == n_col_tiles - 1)
    def _():
        width = n_col_tiles * tn
        mean_d = jnp.sum(s1, axis=-1, keepdims=True) / width
        var = jnp.sum(s2, axis=-1, keepdims=True) / width - mean_d * mean_d
        mu_ref[...] = c + mean_d
        rstd_ref[...] = jnp.broadcast_to(lax.rsqrt(var + EPS), rstd_ref.shape)


def sgu_v(h, w, col0, tm=512, tn=4 * V_SUB):
    m, k = h.shape
    nj = SGU_W // tn
    j0 = col0 // tn
    stat = jax.ShapeDtypeStruct((m, LANES), F32)
    stat_spec = pl.BlockSpec((tm, LANES), lambda i, j: (i, 0))
    return pl.pallas_call(
        functools.partial(_sgu_v_kernel, n_col_tiles=nj),
        out_shape=(jax.ShapeDtypeStruct((m, SGU_W), F32), stat, stat),
        grid=(m // tm, nj),
        in_specs=[pl.BlockSpec((tm, k), lambda i, j: (i, 0)),
                  pl.BlockSpec((k, tn), lambda i, j: (0, j0 + jnp.where(i == 0, j, 0)),
                               pipeline_mode=pl.Buffered(1))],
        out_specs=(pl.BlockSpec((tm, tn), lambda i, j: (i, j)), stat_spec, stat_spec),
        scratch_shapes=[pltpu.VMEM((nj, k, tn), BF16)] + [pltpu.VMEM((tm, LANES), F32)] * 3,
        compiler_params=_cparams(2),
        name="sgu_v",
    )(h, w)


def _sgu_gate_kernel(h_ref, wu32_ref, wz32_ref, v_ref, mu_ref, rstd_ref, lg_ref, lb_ref, ws_ref, bs_ref,
                     side32_ref, o_ref, side_ref, wu_ref, wz_ref):
    _cast_once([wu32_ref, wz32_ref], [wu_ref, wz_ref])
    side_ref[...] = side32_ref[...].astype(BF16)
    h = h_ref[...]
    tm, tn = v_ref.shape
    reps = GROUP_W // LANES
    mu = jnp.concatenate([mu_ref[...]] * reps, axis=1)
    rstd = jnp.concatenate([rstd_ref[...]] * reps, axis=1)
    tri = (lax.broadcasted_iota(jnp.int32, (CHUNK, CHUNK), 1)
           <= lax.broadcasted_iota(jnp.int32, (CHUNK, CHUNK), 0))
    for gi in range(tn // GROUP_W):
        cols = slice(gi * GROUP_W, (gi + 1) * GROUP_W)
        u = jnp.dot(h, wu_ref[:, cols], preferred_element_type=F32)
        zg = jnp.dot(h, wz_ref[:, cols], preferred_element_type=F32)
        vn = ((v_ref[:, cols] - mu) * rstd * lg_ref[:, cols] + lb_ref[:, cols]).astype(BF16)
        wsm = jnp.where(tri, ws_ref[gi], 0.0).astype(BF16)
        bsg = bs_ref[gi]
        mix = jnp.concatenate(
            [jnp.dot(wsm, vn[c * CHUNK:(c + 1) * CHUNK], preferred_element_type=F32) + bsg
             for c in range(tm // CHUNK)], axis=0)
        o_ref[:, cols] = (u * mix * (zg * jax.nn.sigmoid(zg))).astype(o_ref.dtype)


def sgu_gate(h, w, col_u, col_z, v, mu, rstd, ln_g, ln_b, w_s, b_s, side, tm=1024, tn=512):
    m, k = h.shape
    n = SGU_W
    nj, ni = n // tn, m // tm
    ju, jz = col_u // tn, col_z // tn
    gpt = tn // GROUP_W
    stat_spec = pl.BlockSpec((tm, LANES), lambda j, i: (i, 0))
    side_rows = side.shape[0] // (nj * ni)
    assert side_rows * nj * ni == side.shape[0] and side_rows % HALO == 0
    side_spec = pl.BlockSpec((side_rows, side.shape[1]), lambda j, i: (j * ni + i, 0))
    return pl.pallas_call(
        _sgu_gate_kernel,
        out_shape=(jax.ShapeDtypeStruct((m, n), BF16), jax.ShapeDtypeStruct(side.shape, BF16)),
        grid=(nj, ni),
        in_specs=[pl.BlockSpec((tm, k), lambda j, i: (i, 0)),
                  pl.BlockSpec((k, tn), lambda j, i: (0, ju + j)),
                  pl.BlockSpec((k, tn), lambda j, i: (0, jz + j)),
                  pl.BlockSpec((tm, tn), lambda j, i: (i, j)),
                  stat_spec, stat_spec,
                  pl.BlockSpec((1, tn), lambda j, i: (0, j)),
                  pl.BlockSpec((1, tn), lambda j, i: (0, j)),
                  pl.BlockSpec((gpt, CHUNK, CHUNK), lambda j, i: (j, 0, 0)),
                  pl.BlockSpec((gpt, CHUNK, 1), lambda j, i: (j, 0, 0)),
                  side_spec],
        out_specs=(pl.BlockSpec((tm, tn), lambda j, i: (i, j)), side_spec),
        scratch_shapes=[pltpu.VMEM((k, tn), BF16)] * 2,
        compiler_params=_cparams(2),
        name="sgu_gate",
    )(h, w, w, v, mu, rstd, ln_g.reshape(1, n), ln_b.reshape(1, n), w_s, b_s.reshape(N_GROUPS, CHUNK, 1),
      side)


def _rotary_tables(seq):
    half = ROT_DIM // 2
    inv_freq = jnp.power(ROPE_THETA, -jnp.arange(half, dtype=F32) * 2.0 / ROT_DIM)
    ang = jnp.arange(seq).astype(F32)[:, None] * inv_freq[None, :]
    cos, sin = jnp.cos(ang), jnp.sin(ang)
    rest = HEAD_DIM - ROT_DIM
    c = jnp.concatenate([cos, cos, jnp.ones((seq, rest), F32)], axis=1)
    sa = jnp.concatenate([jnp.zeros((seq, half), F32), sin, jnp.zeros((seq, rest), F32)], axis=1)
    sb = jnp.concatenate([-sin, jnp.zeros((seq, half + rest), F32)], axis=1)
    rot_t = jnp.concatenate([cos, cos, -sin, sin], axis=1).T
    return (c, sa, sb), rot_t


def kernel(x, norm_even, w_in_even, conv_w, cmp_k_pos, cmp_k_w1, cmp_k_b1, cmp_k_w2, cmp_v_pos, cmp_v_w1, cmp_v_b1, cmp_v_w2, w_out_even, norm_odd, w_in_odd, sgu_ln_g, sgu_ln_b, sgu_w_s, sgu_b_s, w_out_odd, norm_final):
    batch, seq, d = x.shape
    m = batch * seq
    x2d = x.reshape(m, d)

    wt_in = jnp.swapaxes(w_in_even[0], 0, 1)
    cw = CONV_W
    qw = N_HEADS * HEAD_DIM
    kvw = N_KV * HEAD_DIM
    o_q = 4 * cw
    o_kv = o_q + qw
    o_gl = o_kv + 6 * kvw
    o_ng = o_gl + 3 * N_HEADS

    h0, gl = rmsnorm_proj(x2d, norm_even[0], wt_in, o_gl, LANES)
    y_conv, w_out0 = conv_proj(h0, wt_in, conv_w[0], seq, w_out_even[0])
    z = matmul_t(h0, wt_in, [(o_q, qw + 6 * kvw), (o_ng, qw)], tn=1024, name="nsa_proj")
    ng_col = (qw + 6 * kvw) // (2 * GQA * HEAD_DIM)

    tabs, rot_t = _rotary_tables(seq)
    cmp_k = (cmp_k_pos[0], cmp_k_w1[0].astype(BF16).reshape(CMP_LEN, HEAD_DIM, HEAD_DIM),
             cmp_k_b1[0].reshape(1, HEAD_DIM), cmp_k_w2[0].astype(BF16))
    cmp_v = (cmp_v_pos[0], cmp_v_w1[0].astype(BF16).reshape(CMP_LEN, HEAD_DIM, HEAD_DIM),
             cmp_v_b1[0].reshape(1, HEAD_DIM), cmp_v_w2[0].astype(BF16))
    prep = nsa_prep(z, tabs, cmp_k, cmp_v, batch, seq, col0=qw // HEAD_DIM)
    y_nsa = nsa_attention_pair(z, ng_col, gl, rot_t, prep, batch, seq)

    x1, h1 = out_proj_norm(y_conv, 0, y_nsa, 0, w_out0, x2d, norm_odd[0],
                           BF16, True, name="out_proj_even")

    w_in1 = w_in_odd[0]
    v, mu, rstd = sgu_v(h1, w_in1, SGU_W)
    act, w_out1 = sgu_gate(h1, w_in1, 0, 2 * SGU_W, v, mu, rstd,
                           sgu_ln_g[0], sgu_ln_b[0], sgu_w_s[0], sgu_b_s[0], w_out_odd[0])
    (out,) = out_proj_norm(act, 0, act, 1, w_out1, x1, norm_final,
                           F32, False, name="out_proj_odd")
    return out.reshape(batch, seq, d)
```

```python
import functools
import math

import jax
import jax.numpy as jnp
from jax import lax
from jax.experimental import pallas as pl
from jax.experimental.pallas import tpu as pltpu

F32 = jnp.float32
BF16 = jnp.bfloat16

D_MODEL = 2048
MIX = 2 * D_MODEL
CONV_W = MIX // 2
CONV_K = 3
HEAD_DIM = 128
N_HEADS = 16
N_KV = 4
GQA = N_HEADS // N_KV
ROT_DIM = HEAD_DIM // 4
ROPE_THETA = 500000.0
CMP_LEN = 32
CMP_STRIDE = 16
SLC_LEN = 64
N_SEL = 8
CMP_PER_SLC = SLC_LEN // CMP_STRIDE
CMP_BACK = CMP_LEN // CMP_STRIDE - 1
WINDOW = 512
SGU_W = MIX
CHUNK = 128
N_GROUPS = 16
GROUP_W = SGU_W // N_GROUPS
EPS = 1e-6

LANES = 128
SUBLANES = 8
SLC_SHIFT = 6
HALO = 16
KEY_TILE = LANES
V_SUB = 512
XPOSE_ROWS = 256
VMEM_LIMIT = 56 * 1024 * 1024

NEG = -1e30

EXT_DIM = 2 * HEAD_DIM


def _cparams(n_axes):
    return pltpu.CompilerParams(
        dimension_semantics=("arbitrary",) * n_axes, vmem_limit_bytes=VMEM_LIMIT)


def _rmsnorm_proj_kernel(x_ref, g_ref, wt_ref, o_ref, p_ref, wb_ref):
    @pl.when(pl.program_id(0) == 0)
    def _():
        wb_ref[...] = wt_ref[...].T.astype(BF16)

    x = x_ref[...]
    y = x * lax.rsqrt(jnp.mean(x * x, axis=-1, keepdims=True) + EPS)
    hb = (y * g_ref[...]).astype(BF16)
    o_ref[...] = hb
    p_ref[...] = jnp.dot(hb, wb_ref[...], preferred_element_type=F32)


def rmsnorm_proj(x2d, g, wt, row0, n, tm=512):
    m, d = x2d.shape
    assert row0 % n == 0
    return pl.pallas_call(
        _rmsnorm_proj_kernel,
        out_shape=(jax.ShapeDtypeStruct((m, d), BF16), jax.ShapeDtypeStruct((m, n), F32)),
        grid=(m // tm,),
        in_specs=[pl.BlockSpec((tm, d), lambda i: (i, 0)),
                  pl.BlockSpec((1, d), lambda i: (0, 0)),
                  pl.BlockSpec((n, d), lambda i: (row0 // n, 0))],
        out_specs=(pl.BlockSpec((tm, d), lambda i: (i, 0)),
                   pl.BlockSpec((tm, n), lambda i: (i, 0))),
        scratch_shapes=[pltpu.VMEM((d, n), BF16)],
        compiler_params=_cparams(1),
        name="rmsnorm_gates",
    )(x2d, g.reshape(1, d), wt)


def _cast_once(w_refs, wb_refs, transposed=False):
    @pl.when(pl.program_id(1) == 0)
    def _():
        for w_ref, wb_ref in zip(w_refs, wb_refs):
            if transposed:
                for r in range(0, w_ref.shape[0], XPOSE_ROWS):
                    wb_ref[:, r:r + XPOSE_ROWS] = w_ref[r:r + XPOSE_ROWS, :].T.astype(BF16)
            else:
                wb_ref[...] = w_ref[...].astype(BF16)


def _matmul_t_kernel(a_ref, wt_ref, o_ref, wb_ref):
    _cast_once([wt_ref], [wb_ref], transposed=True)
    o_ref[...] = jnp.dot(a_ref[...], wb_ref[...], preferred_element_type=F32)


def matmul_t(a, wt, segments, tm=1024, tn=512, name="proj"):
    m, k = a.shape
    starts = []
    for row0, seg_n in segments:
        assert seg_n % tn == 0 and row0 % HALO == 0
        starts += [row0 + t * tn for t in range(seg_n // tn)]
    n = tn * len(starts)

    def wrow(j):
        row = starts[-1]
        for idx in range(len(starts) - 2, -1, -1):
            row = jnp.where(j <= idx, starts[idx], row)
        return pl.multiple_of(row, HALO)

    wspec = pl.BlockSpec((pl.Element(tn), pl.Element(k)), lambda j, i: (wrow(j), 0))
    return pl.pallas_call(
        _matmul_t_kernel,
        out_shape=jax.ShapeDtypeStruct((m, n), F32),
        grid=(n // tn, m // tm),
        in_specs=[pl.BlockSpec((tm, k), lambda j, i: (i, 0)), wspec],
        out_specs=pl.BlockSpec((tm, tn), lambda j, i: (i, j)),
        scratch_shapes=[pltpu.VMEM((k, tn), BF16)],
        compiler_params=_cparams(2),
        name=name,
    )(a, wt)


def _conv_proj_kernel(h_ref, hp_ref, wb32_ref, wc32_ref, wh32_ref, wg32_ref, cw_ref, side32_ref,
                      o_ref, side_ref, wb_ref, wc_ref, wh_ref, wg_ref, *, tiles_per_seq):
    _cast_once([wb32_ref, wc32_ref, wh32_ref, wg32_ref], [wb_ref, wc_ref, wh_ref, wg_ref],
               transposed=True)
    side_ref[...] = side32_ref[...].astype(BF16)
    i = pl.program_id(1)
    cw = cw_ref[...]
    hp = hp_ref[...]
    pp = (jnp.dot(hp, wc_ref[...], preferred_element_type=F32)
          * jnp.dot(hp, wh_ref[...], preferred_element_type=F32))
    pp = jnp.where(i % tiles_per_seq == 0, 0.0, pp)
    prev1, prev2 = pp[HALO - 1:HALO, :], pp[HALO - 2:HALO - 1, :]
    h = h_ref[...]
    cb = jnp.dot(h, wb_ref[...], preferred_element_type=F32)
    cc = jnp.dot(h, wc_ref[...], preferred_element_type=F32)
    ch = jnp.dot(h, wh_ref[...], preferred_element_type=F32)
    cg = jnp.dot(h, wg_ref[...], preferred_element_type=F32)
    p = cc * ch
    row = lax.broadcasted_iota(jnp.int32, (SUBLANES, p.shape[1]), 0)
    p1, p2 = pltpu.roll(p, 1, axis=0), pltpu.roll(p, 2, axis=0)
    p1 = jnp.concatenate([jnp.where(row == 0, prev1, p1[:SUBLANES]), p1[SUBLANES:]], axis=0)
    p2 = jnp.concatenate([jnp.where(row == 0, prev2, jnp.where(row == 1, prev1, p2[:SUBLANES])),
                          p2[SUBLANES:]], axis=0)
    conv = cw[0:1, :] * p2 + cw[1:2, :] * p1 + cw[2:3, :] * p
    o_ref[...] = (cb * conv * (cg * jax.nn.sigmoid(cg))).astype(o_ref.dtype)


def conv_proj(h, wt, conv_w, seq, side, tm=1024, tn=256):
    m, k = h.shape
    n = conv_w.shape[1]
    tiles_per_seq = seq // tm
    halo_per_tile = tm // HALO
    nj, ni = n // tn, m // tm
    side_rows = side.shape[0] // (nj * ni)
    assert side_rows * nj * ni == side.shape[0] and side_rows % HALO == 0
    side_spec = pl.BlockSpec((side_rows, side.shape[1]), lambda j, i: (j * ni + i, 0))

    def wspec(which):
        return pl.BlockSpec((tn, k), lambda j, i: (which * nj + j, 0))

    return pl.pallas_call(
        functools.partial(_conv_proj_kernel, tiles_per_seq=tiles_per_seq),
        out_shape=(jax.ShapeDtypeStruct((m, n), BF16), jax.ShapeDtypeStruct(side.shape, BF16)),
        grid=(nj, ni),
        in_specs=[pl.BlockSpec((tm, k), lambda j, i: (i, 0)),
                  pl.BlockSpec((HALO, k), lambda j, i: (jnp.maximum(i * halo_per_tile - 1, 0), 0)),
                  wspec(0), wspec(1), wspec(2), wspec(3),
                  pl.BlockSpec((CONV_K, tn), lambda j, i: (0, j)),
                  side_spec],
        out_specs=(pl.BlockSpec((tm, tn), lambda j, i: (i, j)), side_spec),
        scratch_shapes=[pltpu.VMEM((k, tn), BF16)] * 4,
        compiler_params=_cparams(2),
        name="conv_proj",
    )(h, h, wt, wt, wt, wt, conv_w, side)


def _rotary(x, c, sa, sb):
    half = ROT_DIM // 2
    return (x * c + pltpu.roll(x, half, axis=1) * sa
            + pltpu.roll(x, HEAD_DIM - half, axis=1) * sb)


def _compress(src_ref, pos_ref, w1_ref, b1_ref, w2_ref):
    n_rows = src_ref.shape[0] // CMP_STRIDE
    p_acc = jnp.zeros((n_rows, HEAD_DIM), F32)
    q_acc = jnp.zeros((n_rows, HEAD_DIM), F32)
    for r in range(CMP_STRIDE):
        s_r = src_ref[pl.ds(r, n_rows, stride=CMP_STRIDE), :]
        a_r = (s_r + pos_ref[r:r + 1, :]).astype(BF16)
        b_r = (s_r + pos_ref[CMP_STRIDE + r:CMP_STRIDE + r + 1, :]).astype(BF16)
        p_acc += jnp.dot(a_r, w1_ref[r], preferred_element_type=F32)
        q_acc += jnp.dot(b_r, w1_ref[CMP_STRIDE + r], preferred_element_type=F32)
    hid = p_acc + pltpu.roll(q_acc, n_rows - 1, axis=0) + b1_ref[...]
    act = (hid * jax.nn.sigmoid(hid)).astype(BF16)
    return jnp.dot(act, w2_ref[...], preferred_element_type=F32)


def _nsa_prep_kernel(kc_ref, vc_ref, ks_ref, vs_ref, kw_ref, vw_ref, c_ref, sa_ref, sb_ref,
                     kpos_ref, kw1_ref, kb1_ref, kw2_ref, vpos_ref, vw1_ref, vb1_ref, vw2_ref,
                     ksx_ref, vst_ref, kwx_ref, vwt_ref, kcmp_ref, vcmp_ref):
    c, sa, sb = c_ref[...], sa_ref[...], sb_ref[...]
    seq = ks_ref.shape[0]
    n_blk = seq // SLC_LEN
    flag_w = EXT_DIM - HEAD_DIM
    blk = lax.broadcasted_iota(jnp.int32, (seq, flag_w), 0) >> SLC_SHIFT
    onehot = jnp.where(blk == lax.broadcasted_iota(jnp.int32, (seq, flag_w), 1), 1.0, 0.0)
    ksx_ref[:, 0:HEAD_DIM] = _rotary(ks_ref[...], c, sa, sb).astype(BF16)
    ksx_ref[:, HEAD_DIM:] = onehot.astype(BF16)
    pad_flag = jnp.where(lax.broadcasted_iota(jnp.int32, (WINDOW, EXT_DIM), 1) == HEAD_DIM + n_blk, 1.0, 0.0)
    kwx_ref[0:WINDOW, :] = pad_flag.astype(BF16)
    kwx_ref[WINDOW:, 0:HEAD_DIM] = _rotary(kw_ref[...], c, sa, sb).astype(BF16)
    kwx_ref[WINDOW:, HEAD_DIM:] = jnp.zeros((seq, flag_w), BF16)
    n_tiles = seq // KEY_TILE
    pad_tiles = WINDOW // KEY_TILE
    for t in range(pad_tiles):
        vwt_ref[t] = jnp.zeros((HEAD_DIM, KEY_TILE), BF16)
    for t in range(n_tiles):
        rows = pl.ds(t * KEY_TILE, KEY_TILE)
        vst_ref[t] = vs_ref[rows, :].T.astype(BF16)
        vwt_ref[pad_tiles + t] = vw_ref[rows, :].T.astype(BF16)
    kcmp_ref[...] = _compress(kc_ref, kpos_ref, kw1_ref, kb1_ref, kw2_ref).astype(BF16)
    vcmp_ref[...] = _compress(vc_ref, vpos_ref, vw1_ref, vb1_ref, vw2_ref).astype(BF16)


def nsa_prep(z, tabs, cmp_k, cmp_v, batch, seq, col0):
    def zspec(which):
        return pl.BlockSpec((seq, HEAD_DIM), lambda b, g, w=which: (b, col0 + w * N_KV + g))

    tab = pl.BlockSpec((seq, HEAD_DIM), lambda b, g: (0, 0))

    def wspecs():
        return [pl.BlockSpec((CMP_LEN, HEAD_DIM), lambda b, g: (0, 0)),
                pl.BlockSpec((CMP_LEN, HEAD_DIM, HEAD_DIM), lambda b, g: (0, 0, 0)),
                pl.BlockSpec((1, HEAD_DIM), lambda b, g: (0, 0)),
                pl.BlockSpec((HEAD_DIM, HEAD_DIM), lambda b, g: (0, 0))]

    def out(shape):
        nd = len(shape)
        spec = pl.BlockSpec((None, None) + shape, lambda b, g: (b, g) + (0,) * nd)
        return spec, jax.ShapeDtypeStruct((batch, N_KV) + shape, BF16)

    n_tiles = seq // KEY_TILE
    pad_tiles = WINDOW // KEY_TILE
    outs = [out((seq, EXT_DIM)), out((n_tiles, HEAD_DIM, KEY_TILE)),
            out((WINDOW + seq, EXT_DIM)), out((pad_tiles + n_tiles, HEAD_DIM, KEY_TILE)),
            out((seq // CMP_STRIDE, HEAD_DIM)), out((seq // CMP_STRIDE, HEAD_DIM))]
    return pl.pallas_call(
        _nsa_prep_kernel,
        out_shape=tuple(o[1] for o in outs),
        grid=(batch, N_KV),
        in_specs=[zspec(0), zspec(1), zspec(2), zspec(3), zspec(4), zspec(5), tab, tab, tab]
        + wspecs() + wspecs(),
        out_specs=tuple(o[0] for o in outs),
        compiler_params=_cparams(2),
        name="nsa_prep",
    )(z, z, z, z, z, z, *tabs, *cmp_k, *cmp_v)


def _nsa_attn_pair_kernel(q_ref, ng_ref, gl_ref, rot_ref,
                          ksx_ref, vst_ref, kwx_ref, vwt_ref, kcmp_ref, vcmp_ref,
                          o_ref, acc_ref, qx_ref, glt_ref, s_ref, m_ref, l_ref, oc_ref, ps_ref, *,
                          tq, n_grp_step, n_blk):
    gp = pl.program_id(1)
    i = pl.program_id(2)
    q0 = i * tq
    half = ROT_DIM // 2
    qscale = (HEAD_DIM ** -0.5) * math.log2(math.e)
    qw = GQA * HEAD_DIM
    groups = range(n_grp_step)

    def head_lanes(n):
        return slice(n * tq, (n + 1) * tq)

    def lanes4(x):
        return jnp.concatenate([x] * GQA, axis=1)

    rot = rot_ref[...]
    cos2, sin2 = rot[:ROT_DIM], rot[ROT_DIM:]
    tlane = q0 + lax.broadcasted_iota(jnp.int32, (1, tq), 1)
    kk = lax.broadcasted_iota(jnp.int32, (tq, tq), 0)
    tt = lax.broadcasted_iota(jnp.int32, (tq, tq), 1)
    tri_diag = jnp.where(kk <= tt, 0.0, NEG)
    tri_old = jnp.where(kk > tt, 0.0, NEG)
    n_cmp = kcmp_ref.shape[1]
    cend = lax.broadcasted_iota(jnp.int32, (n_cmp, tq), 0) * CMP_STRIDE + (CMP_LEN - 1)
    bias_c = lanes4(jnp.where(cend <= tlane, 0.0, NEG))
    any_c = lanes4(tlane) >= CMP_LEN - 1
    jblk = lax.broadcasted_iota(jnp.int32, (n_blk, tq), 0)
    forced = (jblk == 0) | (jblk == (tlane >> SLC_SHIFT))
    causal_blk = jblk * SLC_LEN <= tlane
    sub = lax.broadcasted_iota(jnp.int32, (SUBLANES, tq), 0)
    n_rest = EXT_DIM - HEAD_DIM - n_blk
    pad_rows = jnp.where(lax.broadcasted_iota(jnp.int32, (n_rest, GQA * tq), 0) == 0, NEG, 0.0).astype(BF16)
    tiles = tq // KEY_TILE

    def sel_scores(gg, ci):
        k0 = pl.multiple_of(ci * tq, tq)
        return jnp.dot(ksx_ref[gg, pl.ds(k0, tq), :], qx_ref[gg], preferred_element_type=F32)

    def sel_update(gg, ci, s):
        vblk = jnp.concatenate([vst_ref[gg, ci * tiles + r] for r in range(tiles)], axis=1)
        m_old = m_ref[gg]
        m_new = jnp.maximum(m_old, jnp.max(s, axis=0, keepdims=True))
        alpha = jnp.exp2(m_old - m_new)
        p = jnp.exp2(s - m_new)
        l_ref[gg] = alpha * l_ref[gg] + jnp.sum(p, axis=0, keepdims=True)
        acc_ref[gg] = alpha * acc_ref[gg] + jnp.dot(vblk, p.astype(BF16), preferred_element_type=F32)
        m_ref[gg] = m_new

    for gg in groups:
        q = q_ref[:, gg * qw:(gg + 1) * qw]
        qts = []
        for n in range(GQA):
            qt = (q[:, n * HEAD_DIM:(n + 1) * HEAD_DIM] * qscale).T
            qts.append(qt)
            top = qt[:ROT_DIM]
            swapped = jnp.concatenate([top[half:], top[:half]], axis=0)
            qrt = jnp.concatenate([top * cos2 + swapped * sin2, qt[ROT_DIM:]], axis=0)
            qx_ref[gg, 0:HEAD_DIM, head_lanes(n)] = qrt.astype(BF16)
        qt_all = jnp.concatenate(qts, axis=1).astype(BF16)

        s_c = jnp.dot(kcmp_ref[gg], qt_all, preferred_element_type=F32) + bias_c
        m_c = jnp.where(any_c, jnp.max(s_c, axis=0, keepdims=True), 0.0)
        e_c = jnp.exp2(s_c - m_c)
        den_c = jnp.sum(e_c, axis=0, keepdims=True)
        p_c = e_c / jnp.where(den_c > 0, den_c, 1.0)
        vcmp_t = vcmp_ref[gg].astype(F32).T.astype(BF16)
        oc_ref[gg] = jnp.dot(vcmp_t, p_c.astype(BF16), preferred_element_type=F32)

        p_sum = p_c[:, head_lanes(0)]
        for n in range(1, GQA):
            p_sum = p_sum + p_c[:, head_lanes(n)]
        for t in range(tq // LANES):
            ps_ref[gg, t] = p_sum[:, t * LANES:(t + 1) * LANES]
        taps = [jnp.concatenate([ps_ref[gg, t, pl.ds(k, n_blk, stride=CMP_PER_SLC), :]
                                 for t in range(tq // LANES)], axis=1) for k in range(CMP_PER_SLC)]
        imp = taps[0]
        for k in range(1, CMP_PER_SLC):
            imp = imp + taps[k]
        for k in range(CMP_BACK):
            older = pltpu.roll(taps[CMP_PER_SLC - 1 - k], 1, axis=0)
            imp = imp + jnp.where(jblk == 0, 0.0, older)
        imp = jnp.where(forced, jnp.inf, jnp.where(causal_blk, imp, -jnp.inf))
        n_sub = n_blk // SUBLANES
        part = [imp[a * SUBLANES:(a + 1) * SUBLANES] for a in range(n_sub)]
        cnt = [jnp.zeros((SUBLANES, tq), F32) for _ in range(n_sub)]
        for r in range(n_blk):
            row = imp[r:r + 1, :]
            a_r, r_in = divmod(r, SUBLANES)
            for a in range(n_sub):
                if a > a_r:
                    cnt[a] = cnt[a] + jnp.where(row >= part[a], 1.0, 0.0)
                elif a < a_r:
                    cnt[a] = cnt[a] + jnp.where(row > part[a], 1.0, 0.0)
                else:
                    cnt[a] = cnt[a] + jnp.where(sub > r_in, jnp.where(row >= part[a], 1.0, 0.0),
                                                jnp.where(row > part[a], 1.0, 0.0))
        cnt = jnp.concatenate(cnt, axis=0)
        sel_bias = jnp.where(cnt < N_SEL, jnp.where(imp > -jnp.inf, 0.0, NEG), NEG)
        qx_ref[gg, HEAD_DIM:HEAD_DIM + n_blk, :] = lanes4(sel_bias).astype(BF16)
        qx_ref[gg, HEAD_DIM + n_blk:, :] = pad_rows

        m_ref[gg] = jnp.full(m_ref.shape[1:], NEG, F32)
        l_ref[gg] = jnp.zeros(l_ref.shape[1:], F32)
        acc_ref[gg] = jnp.zeros(acc_ref.shape[1:], F32)
        s_ref[gg, 0] = sel_scores(gg, 0)

    def sel_pair(pi, carry):
        c0 = 2 * pi
        for gg in groups:
            s_ref[gg, 1] = sel_scores(gg, c0 + 1)
        for gg in groups:
            sel_update(gg, c0, s_ref[gg, 0])
        for gg in groups:
            s_ref[gg, 0] = sel_scores(gg, c0 + 2)
        for gg in groups:
            sel_update(gg, c0 + 1, s_ref[gg, 1])
        return carry

    lax.fori_loop(0, i // 2, sel_pair, 0)

    @pl.when(i % 2 == 1)
    def _():
        for gg in groups:
            s_ref[gg, 1] = sel_scores(gg, i)
        for gg in groups:
            sel_update(gg, i - 1, s_ref[gg, 0])

    for gg in groups:
        sel_update(gg, i, s_ref[gg, i % 2] + lanes4(tri_diag))

    n_wt = WINDOW // tq
    vt0 = q0 // KEY_TILE
    glt_ref[...] = jax.nn.sigmoid(gl_ref[...]).T
    for gg in groups:
        kwin = kwx_ref[gg, pl.ds(pl.multiple_of(q0, tq), WINDOW + tq), :]
        vwin = jnp.concatenate([vwt_ref[gg, vt0 + r] for r in range((WINDOW + tq) // KEY_TILE)], axis=1)
        s_w = jnp.dot(kwin, qx_ref[gg], preferred_element_type=F32)
        s_w = jnp.concatenate([s_w[:tq] + lanes4(tri_old), s_w[tq:n_wt * tq],
                               s_w[n_wt * tq:] + lanes4(tri_diag)], axis=0)
        e_w = jnp.exp2(s_w - jnp.max(s_w, axis=0, keepdims=True))
        o_w = (jnp.dot(vwin, e_w.astype(BF16), preferred_element_type=F32)
               / jnp.sum(e_w, axis=0, keepdims=True))
        o_s = acc_ref[gg] / l_ref[gg]
        o_c = oc_ref[gg]
        ng = ng_ref[:, gg * qw:(gg + 1) * qw]
        for n in range(GQA):
            head = (gp * n_grp_step + gg) * GQA + n
            gc, gs, gw = (glt_ref[pl.ds(j * N_HEADS + head, 1), :] for j in range(3))
            sl = head_lanes(n)
            o = (gc * o_c[:, sl] + gs * o_s[:, sl] + gw * o_w[:, sl]).T
            ngh = ng[:, n * HEAD_DIM:(n + 1) * HEAD_DIM]
            col = gg * qw + n * HEAD_DIM
            o_ref[:, col:col + HEAD_DIM] = (o * (ngh * jax.nn.sigmoid(ngh))).astype(o_ref.dtype)


def nsa_attention_pair(z, ng_col, gl, rot_t, prep, batch, seq, tq=256, n_grp_step=2):
    ksx, vst, kwx, vwt, kcmp, vcmp = prep
    assert tq % KEY_TILE == 0 and WINDOW % tq == 0 and WINDOW >= 2 * tq and N_KV % n_grp_step == 0
    n_blk, n_cmp_rows = seq // SLC_LEN, kcmp.shape[2]
    assert n_cmp_rows == CMP_PER_SLC * n_blk
    nq = seq // tq
    qw = GQA * HEAD_DIM * n_grp_step

    def whole(a):
        nd = a.ndim - 2
        return pl.BlockSpec((None, n_grp_step) + a.shape[2:], lambda b, g, i: (b, g) + (0,) * nd)

    lanes = GQA * tq
    return pl.pallas_call(
        functools.partial(_nsa_attn_pair_kernel, tq=tq, n_grp_step=n_grp_step, n_blk=n_blk),
        out_shape=jax.ShapeDtypeStruct((batch * seq, N_HEADS * HEAD_DIM), BF16),
        grid=(batch, N_KV // n_grp_step, nq),
        in_specs=[pl.BlockSpec((tq, qw), lambda b, g, i: (b * nq + i, g)),
                  pl.BlockSpec((tq, qw), lambda b, g, i: (b * nq + i, ng_col + g)),
                  pl.BlockSpec((tq, gl.shape[1]), lambda b, g, i: (b * nq + i, 0)),
                  pl.BlockSpec((rot_t.shape[0], tq), lambda b, g, i: (0, i)),
                  whole(ksx), whole(vst), whole(kwx), whole(vwt), whole(kcmp), whole(vcmp)],
        out_specs=pl.BlockSpec((tq, qw), lambda b, g, i: (b * nq + i, g)),
        scratch_shapes=[pltpu.VMEM((n_grp_step, HEAD_DIM, lanes), F32),
                        pltpu.VMEM((n_grp_step, EXT_DIM, lanes), BF16),
                        pltpu.VMEM((gl.shape[1], tq), F32),
                        pltpu.VMEM((n_grp_step, 2, tq, lanes), F32),
                        pltpu.VMEM((n_grp_step, 1, lanes), F32),
                        pltpu.VMEM((n_grp_step, 1, lanes), F32),
                        pltpu.VMEM((n_grp_step, HEAD_DIM, lanes), F32),
                        pltpu.VMEM((n_grp_step, tq // LANES, n_cmp_rows, LANES), F32)],
        compiler_params=_cparams(3),
        name="nsa_attention",
    )(z, z, gl, rot_t, ksx, vst, kwx, vwt, kcmp, vcmp)


def _out_proj_norm_kernel(a0_ref, a1_ref, w_ref, x_ref, g_ref, *o_refs, kh, emit_residual, has_side):
    if has_side:
        o_refs[-1][...] = o_refs[0][...].astype(BF16)
        o_refs = o_refs[1:-1]
    acc = jnp.dot(a0_ref[...], w_ref[0:kh, :], preferred_element_type=F32)
    acc += jnp.dot(a1_ref[...], w_ref[kh:, :], preferred_element_type=F32)
    x = x_ref[...] + acc
    y = x * lax.rsqrt(jnp.mean(x * x, axis=-1, keepdims=True) + EPS)
    o_refs[-1][...] = (y * g_ref[...]).astype(o_refs[-1].dtype)
    if emit_residual:
        o_refs[0][...] = x


def out_proj_norm(a0, a0_blk, a1, a1_blk, w, x2d, g, norm_dtype, emit_residual, side=None, tm=512,
                  name="out_proj"):
    m, n = x2d.shape
    kh = w.shape[0] // 2
    row = pl.BlockSpec((tm, n), lambda i: (i, 0))
    out_shape = [jax.ShapeDtypeStruct((m, n), norm_dtype)]
    if emit_residual:
        out_shape.insert(0, jax.ShapeDtypeStruct((m, n), F32))
    in_specs = [pl.BlockSpec((tm, kh), lambda i: (i, a0_blk)),
                pl.BlockSpec((tm, kh), lambda i: (i, a1_blk)),
                pl.BlockSpec(w.shape, lambda i: (0, 0), pipeline_mode=pl.Buffered(1)),
                row,
                pl.BlockSpec((1, n), lambda i: (0, 0))]
    out_specs = [row for _ in out_shape]
    args = [a0, a1, w, x2d, g.reshape(1, n)]
    if side is not None:
        side32, side_w, side_blk = side
        side_rows = side32.shape[0] // (m // tm)
        assert side_rows * (m // tm) == side32.shape[0] and side_rows % HALO == 0
        in_specs.append(pl.BlockSpec((side_rows, side_w), lambda i: (i, side_blk)))
        out_specs.append(pl.BlockSpec((side_rows, side_w), lambda i: (i, 0)))
        out_shape.append(jax.ShapeDtypeStruct((side32.shape[0], side_w), BF16))
        args.append(side32)
    return pl.pallas_call(
        functools.partial(_out_proj_norm_kernel, kh=kh, emit_residual=emit_residual,
                          has_side=side is not None),
        out_shape=tuple(out_shape),
        grid=(m // tm,),
        in_specs=in_specs,
        out_specs=tuple(out_specs),
        compiler_params=_cparams(1),
        name=name,
    )(*args)


def _sgu_v_kernel(h_ref, w_ref, v_ref, mu_ref, rstd_ref, c_ref, s1_ref, s2_ref):
    tm, tn = v_ref.shape
    h = h_ref[...]
    for c0 in range(0, tn, V_SUB):
        v = jnp.dot(h, w_ref[:, c0:c0 + V_SUB], preferred_element_type=F32)
        v_ref[:, c0:c0 + V_SUB] = v

        if c0 == 0:
            c_ref[...] = jnp.broadcast_to(jnp.sum(v, axis=-1, keepdims=True) / V_SUB, c_ref.shape)
            s1_ref[...] = jnp.zeros(s1_ref.shape, F32)
            s2_ref[...] = jnp.zeros(s2_ref.shape, F32)

        c = c_ref[...]
        s1, s2 = s1_ref[...], s2_ref[...]
        for r in range(V_SUB // LANES):
            d = v[:, r * LANES:(r + 1) * LANES] - c
            s1 = s1 + d
            s2 = s2 + d * d
        s1_ref[...] = s1
        s2_ref[...] = s2

    mean_d = jnp.sum(s1, axis=-1, keepdims=True) / tn
    var = jnp.sum(s2, axis=-1, keepdims=True) / tn - mean_d * mean_d
    mu_ref[...] = c + mean_d
    rstd_ref[...] = jnp.broadcast_to(lax.rsqrt(var + EPS), rstd_ref.shape)


def sgu_v(h, wb, tm=512):
    m, k = h.shape
    tn = wb.shape[1]
    stat = jax.ShapeDtypeStruct((m, LANES), F32)
    stat_spec = pl.BlockSpec((tm, LANES), lambda i: (i, 0))
    return pl.pallas_call(
        _sgu_v_kernel,
        out_shape=(jax.ShapeDtypeStruct((m, tn), F32), stat, stat),
        grid=(m // tm,),
        in_specs=[pl.BlockSpec((tm, k), lambda i: (i, 0)),
                  pl.BlockSpec(wb.shape, lambda i: (0, 0), pipeline_mode=pl.Buffered(1))],
        out_specs=(pl.BlockSpec((tm, tn), lambda i: (i, 0)), stat_spec, stat_spec),
        scratch_shapes=[pltpu.VMEM((tm, LANES), F32)] * 3,
        compiler_params=_cparams(1),
        name="sgu_v",
    )(h, wb)


def _sgu_gate_kernel(h_ref, wu32_ref, wz32_ref, v_ref, mu_ref, rstd_ref, lg_ref, lb_ref, ws_ref, bs_ref,
                     side32_ref, o_ref, side_ref, wu_ref, wz_ref):
    _cast_once([wu32_ref, wz32_ref], [wu_ref, wz_ref])
    side_ref[...] = side32_ref[...].astype(BF16)
    h = h_ref[...]
    tm, tn = v_ref.shape
    reps = GROUP_W // LANES
    mu = jnp.concatenate([mu_ref[...]] * reps, axis=1)
    rstd = jnp.concatenate([rstd_ref[...]] * reps, axis=1)
    tri = (lax.broadcasted_iota(jnp.int32, (CHUNK, CHUNK), 1)
           <= lax.broadcasted_iota(jnp.int32, (CHUNK, CHUNK), 0))
    for gi in range(tn // GROUP_W):
        cols = slice(gi * GROUP_W, (gi + 1) * GROUP_W)
        u = jnp.dot(h, wu_ref[:, cols], preferred_element_type=F32)
        zg = jnp.dot(h, wz_ref[:, cols], preferred_element_type=F32)
        vn = ((v_ref[:, cols] - mu) * rstd * lg_ref[:, cols] + lb_ref[:, cols]).astype(BF16)
        wsm = jnp.where(tri, ws_ref[gi], 0.0).astype(BF16)
        bsg = bs_ref[gi]
        mix = jnp.concatenate(
            [jnp.dot(wsm, vn[c * CHUNK:(c + 1) * CHUNK], preferred_element_type=F32) + bsg
             for c in range(tm // CHUNK)], axis=0)
        o_ref[:, cols] = (u * mix * (zg * jax.nn.sigmoid(zg))).astype(o_ref.dtype)


def sgu_gate(h, w, col_u, col_z, v, mu, rstd, ln_g, ln_b, w_s, b_s, side, tm=1024, tn=512):
    m, k = h.shape
    n = SGU_W
    nj, ni = n // tn, m // tm
    ju, jz = col_u // tn, col_z // tn
    gpt = tn // GROUP_W
    stat_spec = pl.BlockSpec((tm, LANES), lambda j, i: (i, 0))
    side_rows = side.shape[0] // (nj * ni)
    assert side_rows * nj * ni == side.shape[0] and side_rows % HALO == 0
    side_spec = pl.BlockSpec((side_rows, side.shape[1]), lambda j, i: (j * ni + i, 0))
    return pl.pallas_call(
        _sgu_gate_kernel,
        out_shape=(jax.ShapeDtypeStruct((m, n), BF16), jax.ShapeDtypeStruct(side.shape, BF16)),
        grid=(nj, ni),
        in_specs=[pl.BlockSpec((tm, k), lambda j, i: (i, 0)),
                  pl.BlockSpec((k, tn), lambda j, i: (0, ju + j)),
                  pl.BlockSpec((k, tn), lambda j, i: (0, jz + j)),
                  pl.BlockSpec((tm, tn), lambda j, i: (i, j)),
                  stat_spec, stat_spec,
                  pl.BlockSpec((1, tn), lambda j, i: (0, j)),
                  pl.BlockSpec((1, tn), lambda j, i: (0, j)),
                  pl.BlockSpec((gpt, CHUNK, CHUNK), lambda j, i: (j, 0, 0)),
                  pl.BlockSpec((gpt, CHUNK, 1), lambda j, i: (j, 0, 0)),
                  side_spec],
        out_specs=(pl.BlockSpec((tm, tn), lambda j, i: (i, j)), side_spec),
        scratch_shapes=[pltpu.VMEM((k, tn), BF16)] * 2,
        compiler_params=_cparams(2),
        name="sgu_gate",
    )(h, w, w, v, mu, rstd, ln_g.reshape(1, n), ln_b.reshape(1, n), w_s, b_s.reshape(N_GROUPS, CHUNK, 1),
      side)


def _rotary_tables(seq):
    half = ROT_DIM // 2
    inv_freq = jnp.power(ROPE_THETA, -jnp.arange(half, dtype=F32) * 2.0 / ROT_DIM)
    ang = jnp.arange(seq).astype(F32)[:, None] * inv_freq[None, :]
    cos, sin = jnp.cos(ang), jnp.sin(ang)
    rest = HEAD_DIM - ROT_DIM
    c = jnp.concatenate([cos, cos, jnp.ones((seq, rest), F32)], axis=1)
    sa = jnp.concatenate([jnp.zeros((seq, half), F32), sin, jnp.zeros((seq, rest), F32)], axis=1)
    sb = jnp.concatenate([-sin, jnp.zeros((seq, half + rest), F32)], axis=1)
    rot_t = jnp.concatenate([cos, cos, -sin, sin], axis=1).T
    return (c, sa, sb), rot_t


def kernel(x, norm_even, w_in_even, conv_w, cmp_k_pos, cmp_k_w1, cmp_k_b1, cmp_k_w2, cmp_v_pos, cmp_v_w1, cmp_v_b1, cmp_v_w2, w_out_even, norm_odd, w_in_odd, sgu_ln_g, sgu_ln_b, sgu_w_s, sgu_b_s, w_out_odd, norm_final):
    batch, seq, d = x.shape
    m = batch * seq
    x2d = x.reshape(m, d)

    wt_in = jnp.swapaxes(w_in_even[0], 0, 1)
    cw = CONV_W
    qw = N_HEADS * HEAD_DIM
    kvw = N_KV * HEAD_DIM
    o_q = 4 * cw
    o_kv = o_q + qw
    o_gl = o_kv + 6 * kvw
    o_ng = o_gl + 3 * N_HEADS

    h0, gl = rmsnorm_proj(x2d, norm_even[0], wt_in, o_gl, LANES)
    y_conv, w_out0 = conv_proj(h0, wt_in, conv_w[0], seq, w_out_even[0])
    z = matmul_t(h0, wt_in, [(o_q, qw + 6 * kvw), (o_ng, qw)], tn=1024, name="nsa_proj")
    ng_col = (qw + 6 * kvw) // (2 * GQA * HEAD_DIM)

    tabs, rot_t = _rotary_tables(seq)
    cmp_k = (cmp_k_pos[0], cmp_k_w1[0].astype(BF16).reshape(CMP_LEN, HEAD_DIM, HEAD_DIM),
             cmp_k_b1[0].reshape(1, HEAD_DIM), cmp_k_w2[0].astype(BF16))
    cmp_v = (cmp_v_pos[0], cmp_v_w1[0].astype(BF16).reshape(CMP_LEN, HEAD_DIM, HEAD_DIM),
             cmp_v_b1[0].reshape(1, HEAD_DIM), cmp_v_w2[0].astype(BF16))
    prep = nsa_prep(z, tabs, cmp_k, cmp_v, batch, seq, col0=qw // HEAD_DIM)
    y_nsa = nsa_attention_pair(z, ng_col, gl, rot_t, prep, batch, seq)

    w_in1 = w_in_odd[0]
    x1, h1, w_v = out_proj_norm(y_conv, 0, y_nsa, 0, w_out0, x2d, norm_odd[0],
                                BF16, True, side=(w_in1, SGU_W, 1), name="out_proj_even")

    v, mu, rstd = sgu_v(h1, w_v)
    act, w_out1 = sgu_gate(h1, w_in1, 0, 2 * SGU_W, v, mu, rstd,
                           sgu_ln_g[0], sgu_ln_b[0], sgu_w_s[0], sgu_b_s[0], w_out_odd[0])
    (out,) = out_proj_norm(act, 0, act, 1, w_out1, x1, norm_final,
                           F32, False, name="out_proj_odd")
    return out.reshape(batch, seq, d)
```

```python
import functools
import math

import jax
import jax.numpy as jnp
from jax import lax
from jax.experimental import pallas as pl
from jax.experimental.pallas import tpu as pltpu

F32 = jnp.float32
BF16 = jnp.bfloat16

D_MODEL = 2048
MIX = 2 * D_MODEL
CONV_W = MIX // 2
CONV_K = 3
HEAD_DIM = 128
N_HEADS = 16
N_KV = 4
GQA = N_HEADS // N_KV
ROT_DIM = HEAD_DIM // 4
ROPE_THETA = 500000.0
CMP_LEN = 32
CMP_STRIDE = 16
SLC_LEN = 64
N_SEL = 8
CMP_PER_SLC = SLC_LEN // CMP_STRIDE
CMP_BACK = CMP_LEN // CMP_STRIDE - 1
WINDOW = 512
SGU_W = MIX
CHUNK = 128
N_GROUPS = 16
GROUP_W = SGU_W // N_GROUPS
EPS = 1e-6

LANES = 128
SUBLANES = 8
SLC_SHIFT = 6
HALO = 16
KEY_TILE = LANES
V_SUB = 512
XPOSE_ROWS = 256
VMEM_LIMIT = 56 * 1024 * 1024

NEG = -1e30

EXT_DIM = 2 * HEAD_DIM


def _cparams(n_axes):
    return pltpu.CompilerParams(
        dimension_semantics=("arbitrary",) * n_axes, vmem_limit_bytes=VMEM_LIMIT)


def _rmsnorm_proj_kernel(x_ref, g_ref, wt_ref, o_ref, p_ref, wb_ref):
    @pl.when(pl.program_id(0) == 0)
    def _():
        wb_ref[...] = wt_ref[...].T.astype(BF16)

    x = x_ref[...]
    y = x * lax.rsqrt(jnp.mean(x * x, axis=-1, keepdims=True) + EPS)
    hb = (y * g_ref[...]).astype(BF16)
    o_ref[...] = hb
    p_ref[...] = jnp.dot(hb, wb_ref[...], preferred_element_type=F32)


def rmsnorm_proj(x2d, g, wt, row0, n, tm=512):
    m, d = x2d.shape
    assert row0 % n == 0
    return pl.pallas_call(
        _rmsnorm_proj_kernel,
        out_shape=(jax.ShapeDtypeStruct((m, d), BF16), jax.ShapeDtypeStruct((m, n), F32)),
        grid=(m // tm,),
        in_specs=[pl.BlockSpec((tm, d), lambda i: (i, 0)),
                  pl.BlockSpec((1, d), lambda i: (0, 0)),
                  pl.BlockSpec((n, d), lambda i: (row0 // n, 0))],
        out_specs=(pl.BlockSpec((tm, d), lambda i: (i, 0)),
                   pl.BlockSpec((tm, n), lambda i: (i, 0))),
        scratch_shapes=[pltpu.VMEM((d, n), BF16)],
        compiler_params=_cparams(1),
        name="rmsnorm_gates",
    )(x2d, g.reshape(1, d), wt)


def _cast_once(w_refs, wb_refs, transposed=False):
    @pl.when(pl.program_id(1) == 0)
    def _():
        for w_ref, wb_ref in zip(w_refs, wb_refs):
            if transposed:
                for r in range(0, w_ref.shape[0], XPOSE_ROWS):
                    wb_ref[:, r:r + XPOSE_ROWS] = w_ref[r:r + XPOSE_ROWS, :].T.astype(BF16)
            else:
                wb_ref[...] = w_ref[...].astype(BF16)


def _matmul_t_kernel(a_ref, wt_ref, o_ref, wb_ref):
    _cast_once([wt_ref], [wb_ref], transposed=True)
    o_ref[...] = jnp.dot(a_ref[...], wb_ref[...], preferred_element_type=F32)


def matmul_t(a, wt, segments, tm=1024, tn=512, name="proj"):
    m, k = a.shape
    starts = []
    for row0, seg_n in segments:
        assert seg_n % tn == 0 and row0 % HALO == 0
        starts += [row0 + t * tn for t in range(seg_n // tn)]
    n = tn * len(starts)

    def wrow(j):
        row = starts[-1]
        for idx in range(len(starts) - 2, -1, -1):
            row = jnp.where(j <= idx, starts[idx], row)
        return pl.multiple_of(row, HALO)

    wspec = pl.BlockSpec((pl.Element(tn), pl.Element(k)), lambda j, i: (wrow(j), 0))
    return pl.pallas_call(
        _matmul_t_kernel,
        out_shape=jax.ShapeDtypeStruct((m, n), F32),
        grid=(n // tn, m // tm),
        in_specs=[pl.BlockSpec((tm, k), lambda j, i: (i, 0)), wspec],
        out_specs=pl.BlockSpec((tm, tn), lambda j, i: (i, j)),
        scratch_shapes=[pltpu.VMEM((k, tn), BF16)],
        compiler_params=_cparams(2),
        name=name,
    )(a, wt)


def _conv_proj_kernel(h_ref, hp_ref, wb32_ref, wc32_ref, wh32_ref, wg32_ref, cw_ref, side32_ref,
                      o_ref, side_ref, wb_ref, wc_ref, wh_ref, wg_ref, *, tiles_per_seq):
    _cast_once([wb32_ref, wc32_ref, wh32_ref, wg32_ref], [wb_ref, wc_ref, wh_ref, wg_ref],
               transposed=True)
    side_ref[...] = side32_ref[...].astype(BF16)
    i = pl.program_id(1)
    cw = cw_ref[...]
    hp = hp_ref[...]
    pp = (jnp.dot(hp, wc_ref[...], preferred_element_type=F32)
          * jnp.dot(hp, wh_ref[...], preferred_element_type=F32))
    pp = jnp.where(i % tiles_per_seq == 0, 0.0, pp)
    prev1, prev2 = pp[HALO - 1:HALO, :], pp[HALO - 2:HALO - 1, :]
    h = h_ref[...]
    cb = jnp.dot(h, wb_ref[...], preferred_element_type=F32)
    cc = jnp.dot(h, wc_ref[...], preferred_element_type=F32)
    ch = jnp.dot(h, wh_ref[...], preferred_element_type=F32)
    cg = jnp.dot(h, wg_ref[...], preferred_element_type=F32)
    p = cc * ch
    row = lax.broadcasted_iota(jnp.int32, (SUBLANES, p.shape[1]), 0)
    p1, p2 = pltpu.roll(p, 1, axis=0), pltpu.roll(p, 2, axis=0)
    p1 = jnp.concatenate([jnp.where(row == 0, prev1, p1[:SUBLANES]), p1[SUBLANES:]], axis=0)
    p2 = jnp.concatenate([jnp.where(row == 0, prev2, jnp.where(row == 1, prev1, p2[:SUBLANES])),
                          p2[SUBLANES:]], axis=0)
    conv = cw[0:1, :] * p2 + cw[1:2, :] * p1 + cw[2:3, :] * p
    o_ref[...] = (cb * conv * (cg * jax.nn.sigmoid(cg))).astype(o_ref.dtype)


def conv_proj(h, wt, conv_w, seq, side, tm=1024, tn=256):
    m, k = h.shape
    n = conv_w.shape[1]
    tiles_per_seq = seq // tm
    halo_per_tile = tm // HALO
    nj, ni = n // tn, m // tm
    side_rows = side.shape[0] // (nj * ni)
    assert side_rows * nj * ni == side.shape[0] and side_rows % HALO == 0
    side_spec = pl.BlockSpec((side_rows, side.shape[1]), lambda j, i: (j * ni + i, 0))

    def wspec(which):
        return pl.BlockSpec((tn, k), lambda j, i: (which * nj + j, 0))

    return pl.pallas_call(
        functools.partial(_conv_proj_kernel, tiles_per_seq=tiles_per_seq),
        out_shape=(jax.ShapeDtypeStruct((m, n), BF16), jax.ShapeDtypeStruct(side.shape, BF16)),
        grid=(nj, ni),
        in_specs=[pl.BlockSpec((tm, k), lambda j, i: (i, 0)),
                  pl.BlockSpec((HALO, k), lambda j, i: (jnp.maximum(i * halo_per_tile - 1, 0), 0)),
                  wspec(0), wspec(1), wspec(2), wspec(3),
                  pl.BlockSpec((CONV_K, tn), lambda j, i: (0, j)),
                  side_spec],
        out_specs=(pl.BlockSpec((tm, tn), lambda j, i: (i, j)), side_spec),
        scratch_shapes=[pltpu.VMEM((k, tn), BF16)] * 4,
        compiler_params=_cparams(2),
        name="conv_proj",
    )(h, h, wt, wt, wt, wt, conv_w, side)


def _rotary(x, c, sa, sb):
    half = ROT_DIM // 2
    return (x * c + pltpu.roll(x, half, axis=1) * sa
            + pltpu.roll(x, HEAD_DIM - half, axis=1) * sb)


def _compress(src_ref, pos_ref, w1_ref, b1_ref, w2_ref):
    n_rows = src_ref.shape[0] // CMP_STRIDE
    p_acc = jnp.zeros((n_rows, HEAD_DIM), F32)
    q_acc = jnp.zeros((n_rows, HEAD_DIM), F32)
    for r in range(CMP_STRIDE):
        s_r = src_ref[pl.ds(r, n_rows, stride=CMP_STRIDE), :]
        a_r = (s_r + pos_ref[r:r + 1, :]).astype(BF16)
        b_r = (s_r + pos_ref[CMP_STRIDE + r:CMP_STRIDE + r + 1, :]).astype(BF16)
        p_acc += jnp.dot(a_r, w1_ref[r], preferred_element_type=F32)
        q_acc += jnp.dot(b_r, w1_ref[CMP_STRIDE + r], preferred_element_type=F32)
    hid = p_acc + pltpu.roll(q_acc, n_rows - 1, axis=0) + b1_ref[...]
    act = (hid * jax.nn.sigmoid(hid)).astype(BF16)
    return jnp.dot(act, w2_ref[...], preferred_element_type=F32)


def _nsa_prep_kernel(kc_ref, vc_ref, ks_ref, vs_ref, kw_ref, vw_ref, c_ref, sa_ref, sb_ref,
                     kpos_ref, kw1_ref, kb1_ref, kw2_ref, vpos_ref, vw1_ref, vb1_ref, vw2_ref,
                     ksx_ref, vst_ref, kwx_ref, vwt_ref, kcmp_ref, vcmp_ref,
                     kw1b_ref, kw2b_ref, vw1b_ref, vw2b_ref):
    @pl.when((pl.program_id(0) == 0) & (pl.program_id(1) == 0))
    def _():
        for w_ref, wb_ref in ((kw1_ref, kw1b_ref), (kw2_ref, kw2b_ref), (vw1_ref, vw1b_ref), (vw2_ref, vw2b_ref)):
            wb_ref[...] = w_ref[...].astype(BF16)

    c, sa, sb = c_ref[...], sa_ref[...], sb_ref[...]
    seq = ks_ref.shape[0]
    n_blk = seq // SLC_LEN
    flag_w = EXT_DIM - HEAD_DIM
    blk = lax.broadcasted_iota(jnp.int32, (seq, flag_w), 0) >> SLC_SHIFT
    onehot = jnp.where(blk == lax.broadcasted_iota(jnp.int32, (seq, flag_w), 1), 1.0, 0.0)
    ksx_ref[:, 0:HEAD_DIM] = _rotary(ks_ref[...], c, sa, sb).astype(BF16)
    ksx_ref[:, HEAD_DIM:] = onehot.astype(BF16)
    pad_flag = jnp.where(lax.broadcasted_iota(jnp.int32, (WINDOW, EXT_DIM), 1) == HEAD_DIM + n_blk, 1.0, 0.0)
    kwx_ref[0:WINDOW, :] = pad_flag.astype(BF16)
    kwx_ref[WINDOW:, 0:HEAD_DIM] = _rotary(kw_ref[...], c, sa, sb).astype(BF16)
    kwx_ref[WINDOW:, HEAD_DIM:] = jnp.zeros((seq, flag_w), BF16)
    n_tiles = seq // KEY_TILE
    pad_tiles = WINDOW // KEY_TILE
    for t in range(pad_tiles):
        vwt_ref[t] = jnp.zeros((HEAD_DIM, KEY_TILE), BF16)
    for t in range(n_tiles):
        rows = pl.ds(t * KEY_TILE, KEY_TILE)
        vst_ref[t] = vs_ref[rows, :].T.astype(BF16)
        vwt_ref[pad_tiles + t] = vw_ref[rows, :].T.astype(BF16)
    kcmp_ref[...] = _compress(kc_ref, kpos_ref, kw1b_ref, kb1_ref, kw2b_ref).astype(BF16)
    vcmp_ref[...] = _compress(vc_ref, vpos_ref, vw1b_ref, vb1_ref, vw2b_ref).astype(BF16)


def nsa_prep(z, tabs, cmp_k, cmp_v, batch, seq, col0):
    def zspec(which):
        return pl.BlockSpec((seq, HEAD_DIM), lambda b, g, w=which: (b, col0 + w * N_KV + g))

    tab = pl.BlockSpec((seq, HEAD_DIM), lambda b, g: (0, 0))

    def wspecs():
        return [pl.BlockSpec((CMP_LEN, HEAD_DIM), lambda b, g: (0, 0)),
                pl.BlockSpec((CMP_LEN, HEAD_DIM, HEAD_DIM), lambda b, g: (0, 0, 0)),
                pl.BlockSpec((1, HEAD_DIM), lambda b, g: (0, 0)),
                pl.BlockSpec((HEAD_DIM, HEAD_DIM), lambda b, g: (0, 0))]

    def out(shape):
        nd = len(shape)
        spec = pl.BlockSpec((None, None) + shape, lambda b, g: (b, g) + (0,) * nd)
        return spec, jax.ShapeDtypeStruct((batch, N_KV) + shape, BF16)

    n_tiles = seq // KEY_TILE
    pad_tiles = WINDOW // KEY_TILE
    outs = [out((seq, EXT_DIM)), out((n_tiles, HEAD_DIM, KEY_TILE)),
            out((WINDOW + seq, EXT_DIM)), out((pad_tiles + n_tiles, HEAD_DIM, KEY_TILE)),
            out((seq // CMP_STRIDE, HEAD_DIM)), out((seq // CMP_STRIDE, HEAD_DIM))]
    return pl.pallas_call(
        _nsa_prep_kernel,
        out_shape=tuple(o[1] for o in outs),
        grid=(batch, N_KV),
        in_specs=[zspec(0), zspec(1), zspec(2), zspec(3), zspec(4), zspec(5), tab, tab, tab]
        + wspecs() + wspecs(),
        out_specs=tuple(o[0] for o in outs),
        scratch_shapes=[pltpu.VMEM((CMP_LEN, HEAD_DIM, HEAD_DIM), BF16), pltpu.VMEM((HEAD_DIM, HEAD_DIM), BF16)] * 2,
        compiler_params=_cparams(2),
        name="nsa_prep",
    )(z, z, z, z, z, z, *tabs, *cmp_k, *cmp_v)


def _nsa_attn_pair_kernel(q_ref, ng_ref, gl_ref, rot_ref,
                          ksx_ref, vst_ref, kwx_ref, vwt_ref, kcmp_ref, vcmp_ref,
                          o_ref, acc_ref, qx_ref, glt_ref, s_ref, m_ref, l_ref, oc_ref, ps_ref, *,
                          tq, n_grp_step, n_blk):
    gp = pl.program_id(1)
    i = pl.program_id(2)
    q0 = i * tq
    half = ROT_DIM // 2
    qscale = (HEAD_DIM ** -0.5) * math.log2(math.e)
    qw = GQA * HEAD_DIM
    groups = range(n_grp_step)

    def head_lanes(n):
        return slice(n * tq, (n + 1) * tq)

    def lanes4(x):
        return jnp.concatenate([x] * GQA, axis=1)

    rot = rot_ref[...]
    cos2, sin2 = rot[:ROT_DIM], rot[ROT_DIM:]
    tlane = q0 + lax.broadcasted_iota(jnp.int32, (1, tq), 1)
    kk = lax.broadcasted_iota(jnp.int32, (tq, tq), 0)
    tt = lax.broadcasted_iota(jnp.int32, (tq, tq), 1)
    tri_diag = jnp.where(kk <= tt, 0.0, NEG)
    tri_old = jnp.where(kk > tt, 0.0, NEG)
    n_cmp = kcmp_ref.shape[1]
    cend = lax.broadcasted_iota(jnp.int32, (n_cmp, tq), 0) * CMP_STRIDE + (CMP_LEN - 1)
    bias_c = lanes4(jnp.where(cend <= tlane, 0.0, NEG))
    any_c = lanes4(tlane) >= CMP_LEN - 1
    jblk = lax.broadcasted_iota(jnp.int32, (n_blk, tq), 0)
    forced = (jblk == 0) | (jblk == (tlane >> SLC_SHIFT))
    causal_blk = jblk * SLC_LEN <= tlane
    sub = lax.broadcasted_iota(jnp.int32, (SUBLANES, tq), 0)
    n_rest = EXT_DIM - HEAD_DIM - n_blk
    pad_rows = jnp.where(lax.broadcasted_iota(jnp.int32, (n_rest, GQA * tq), 0) == 0, NEG, 0.0).astype(BF16)
    tiles = tq // KEY_TILE

    def sel_scores(gg, ci):
        k0 = pl.multiple_of(ci * tq, tq)
        return jnp.dot(ksx_ref[gg, pl.ds(k0, tq), :], qx_ref[gg], preferred_element_type=F32)

    def sel_update(gg, ci, s):
        vblk = jnp.concatenate([vst_ref[gg, ci * tiles + r] for r in range(tiles)], axis=1)
        m_old = m_ref[gg]
        m_new = jnp.maximum(m_old, jnp.max(s, axis=0, keepdims=True))
        alpha = jnp.exp2(m_old - m_new)
        p = jnp.exp2(s - m_new)
        l_ref[gg] = alpha * l_ref[gg] + jnp.sum(p, axis=0, keepdims=True)
        acc_ref[gg] = alpha * acc_ref[gg] + jnp.dot(vblk, p.astype(BF16), preferred_element_type=F32)
        m_ref[gg] = m_new

    for gg in groups:
        q = q_ref[:, gg * qw:(gg + 1) * qw]
        qts = []
        for n in range(GQA):
            qt = (q[:, n * HEAD_DIM:(n + 1) * HEAD_DIM] * qscale).T
            qts.append(qt)
            top = qt[:ROT_DIM]
            swapped = jnp.concatenate([top[half:], top[:half]], axis=0)
            qrt = jnp.concatenate([top * cos2 + swapped * sin2, qt[ROT_DIM:]], axis=0)
            qx_ref[gg, 0:HEAD_DIM, head_lanes(n)] = qrt.astype(BF16)
        qt_all = jnp.concatenate(qts, axis=1).astype(BF16)

        s_c = jnp.dot(kcmp_ref[gg], qt_all, preferred_element_type=F32) + bias_c
        m_c = jnp.where(any_c, jnp.max(s_c, axis=0, keepdims=True), 0.0)
        e_c = jnp.exp2(s_c - m_c)
        den_c = jnp.sum(e_c, axis=0, keepdims=True)
        p_c = e_c / jnp.where(den_c > 0, den_c, 1.0)
        vcmp_t = vcmp_ref[gg].astype(F32).T.astype(BF16)
        oc_ref[gg] = jnp.dot(vcmp_t, p_c.astype(BF16), preferred_element_type=F32)

        p_sum = p_c[:, head_lanes(0)]
        for n in range(1, GQA):
            p_sum = p_sum + p_c[:, head_lanes(n)]
        for t in range(tq // LANES):
            ps_ref[gg, t] = p_sum[:, t * LANES:(t + 1) * LANES]
        taps = [jnp.concatenate([ps_ref[gg, t, pl.ds(k, n_blk, stride=CMP_PER_SLC), :]
                                 for t in range(tq // LANES)], axis=1) for k in range(CMP_PER_SLC)]
        imp = taps[0]
        for k in range(1, CMP_PER_SLC):
            imp = imp + taps[k]
        for k in range(CMP_BACK):
            older = pltpu.roll(taps[CMP_PER_SLC - 1 - k], 1, axis=0)
            imp = imp + jnp.where(jblk == 0, 0.0, older)
        imp = jnp.where(forced, jnp.inf, jnp.where(causal_blk, imp, -jnp.inf))
        n_sub = n_blk // SUBLANES
        part = [imp[a * SUBLANES:(a + 1) * SUBLANES] for a in range(n_sub)]
        cnt = [jnp.zeros((SUBLANES, tq), F32) for _ in range(n_sub)]
        for r in range(n_blk):
            row = imp[r:r + 1, :]
            a_r, r_in = divmod(r, SUBLANES)
            for a in range(n_sub):
                if a > a_r:
                    cnt[a] = cnt[a] + jnp.where(row >= part[a], 1.0, 0.0)
                elif a < a_r:
                    cnt[a] = cnt[a] + jnp.where(row > part[a], 1.0, 0.0)
                else:
                    cnt[a] = cnt[a] + jnp.where(sub > r_in, jnp.where(row >= part[a], 1.0, 0.0),
                                                jnp.where(row > part[a], 1.0, 0.0))
        cnt = jnp.concatenate(cnt, axis=0)
        sel_bias = jnp.where(cnt < N_SEL, jnp.where(imp > -jnp.inf, 0.0, NEG), NEG)
        qx_ref[gg, HEAD_DIM:HEAD_DIM + n_blk, :] = lanes4(sel_bias).astype(BF16)
        qx_ref[gg, HEAD_DIM + n_blk:, :] = pad_rows

        m_ref[gg] = jnp.full(m_ref.shape[1:], NEG, F32)
        l_ref[gg] = jnp.zeros(l_ref.shape[1:], F32)
        acc_ref[gg] = jnp.zeros(acc_ref.shape[1:], F32)
        s_ref[gg, 0] = sel_scores(gg, 0)

    def sel_pair(pi, carry):
        c0 = 2 * pi
        for gg in groups:
            s_ref[gg, 1] = sel_scores(gg, c0 + 1)
        for gg in groups:
            sel_update(gg, c0, s_ref[gg, 0])
        for gg in groups:
            s_ref[gg, 0] = sel_scores(gg, c0 + 2)
        for gg in groups:
            sel_update(gg, c0 + 1, s_ref[gg, 1])
        return carry

    lax.fori_loop(0, i // 2, sel_pair, 0)

    @pl.when(i % 2 == 1)
    def _():
        for gg in groups:
            s_ref[gg, 1] = sel_scores(gg, i)
        for gg in groups:
            sel_update(gg, i - 1, s_ref[gg, 0])

    for gg in groups:
        sel_update(gg, i, s_ref[gg, i % 2] + lanes4(tri_diag))

    n_wt = WINDOW // tq
    vt0 = q0 // KEY_TILE
    glt_ref[...] = jax.nn.sigmoid(gl_ref[...]).T
    for gg in groups:
        kwin = kwx_ref[gg, pl.ds(pl.multiple_of(q0, tq), WINDOW + tq), :]
        vwin = jnp.concatenate([vwt_ref[gg, vt0 + r] for r in range((WINDOW + tq) // KEY_TILE)], axis=1)
        s_w = jnp.dot(kwin, qx_ref[gg], preferred_element_type=F32)
        s_w = jnp.concatenate([s_w[:tq] + lanes4(tri_old), s_w[tq:n_wt * tq],
                               s_w[n_wt * tq:] + lanes4(tri_diag)], axis=0)
        e_w = jnp.exp2(s_w - jnp.max(s_w, axis=0, keepdims=True))
        o_w = (jnp.dot(vwin, e_w.astype(BF16), preferred_element_type=F32)
               / jnp.sum(e_w, axis=0, keepdims=True))
        o_s = acc_ref[gg] / l_ref[gg]
        o_c = oc_ref[gg]
        ng = ng_ref[:, gg * qw:(gg + 1) * qw]
        for n in range(GQA):
            head = (gp * n_grp_step + gg) * GQA + n
            gc, gs, gw = (glt_ref[pl.ds(j * N_HEADS + head, 1), :] for j in range(3))
            sl = head_lanes(n)
            o = (gc * o_c[:, sl] + gs * o_s[:, sl] + gw * o_w[:, sl]).T
            ngh = ng[:, n * HEAD_DIM:(n + 1) * HEAD_DIM]
            col = gg * qw + n * HEAD_DIM
            o_ref[:, col:col + HEAD_DIM] = (o * (ngh * jax.nn.sigmoid(ngh))).astype(o_ref.dtype)


def nsa_attention_pair(z, ng_col, gl, rot_t, prep, batch, seq, tq=256, n_grp_step=2):
    ksx, vst, kwx, vwt, kcmp, vcmp = prep
    assert tq % KEY_TILE == 0 and WINDOW % tq == 0 and WINDOW >= 2 * tq and N_KV % n_grp_step == 0
    n_blk, n_cmp_rows = seq // SLC_LEN, kcmp.shape[2]
    assert n_cmp_rows == CMP_PER_SLC * n_blk
    nq = seq // tq
    qw = GQA * HEAD_DIM * n_grp_step

    def whole(a):
        nd = a.ndim - 2
        return pl.BlockSpec((None, n_grp_step) + a.shape[2:], lambda b, g, i: (b, g) + (0,) * nd)

    lanes = GQA * tq
    return pl.pallas_call(
        functools.partial(_nsa_attn_pair_kernel, tq=tq, n_grp_step=n_grp_step, n_blk=n_blk),
        out_shape=jax.ShapeDtypeStruct((batch * seq, N_HEADS * HEAD_DIM), BF16),
        grid=(batch, N_KV // n_grp_step, nq),
        in_specs=[pl.BlockSpec((tq, qw), lambda b, g, i: (b * nq + i, g)),
                  pl.BlockSpec((tq, qw), lambda b, g, i: (b * nq + i, ng_col + g)),
                  pl.BlockSpec((tq, gl.shape[1]), lambda b, g, i: (b * nq + i, 0)),
                  pl.BlockSpec((rot_t.shape[0], tq), lambda b, g, i: (0, i)),
                  whole(ksx), whole(vst), whole(kwx), whole(vwt), whole(kcmp), whole(vcmp)],
        out_specs=pl.BlockSpec((tq, qw), lambda b, g, i: (b * nq + i, g)),
        scratch_shapes=[pltpu.VMEM((n_grp_step, HEAD_DIM, lanes), F32),
                        pltpu.VMEM((n_grp_step, EXT_DIM, lanes), BF16),
                        pltpu.VMEM((gl.shape[1], tq), F32),
                        pltpu.VMEM((n_grp_step, 2, tq, lanes), F32),
                        pltpu.VMEM((n_grp_step, 1, lanes), F32),
                        pltpu.VMEM((n_grp_step, 1, lanes), F32),
                        pltpu.VMEM((n_grp_step, HEAD_DIM, lanes), F32),
                        pltpu.VMEM((n_grp_step, tq // LANES, n_cmp_rows, LANES), F32)],
        compiler_params=_cparams(3),
        name="nsa_attention",
    )(z, z, gl, rot_t, ksx, vst, kwx, vwt, kcmp, vcmp)


def _out_proj_norm_kernel(a0_ref, a1_ref, w_ref, x_ref, g_ref, *o_refs, kh, emit_residual, has_side):
    if has_side:
        o_refs[-1][...] = o_refs[0][...].astype(BF16)
        o_refs = o_refs[1:-1]
    acc = jnp.dot(a0_ref[...], w_ref[0:kh, :], preferred_element_type=F32)
    acc += jnp.dot(a1_ref[...], w_ref[kh:, :], preferred_element_type=F32)
    x = x_ref[...] + acc
    y = x * lax.rsqrt(jnp.mean(x * x, axis=-1, keepdims=True) + EPS)
    o_refs[-1][...] = (y * g_ref[...]).astype(o_refs[-1].dtype)
    if emit_residual:
        o_refs[0][...] = x


def out_proj_norm(a0, a0_blk, a1, a1_blk, w, x2d, g, norm_dtype, emit_residual, side=None, tm=512,
                  name="out_proj"):
    m, n = x2d.shape
    kh = w.shape[0] // 2
    row = pl.BlockSpec((tm, n), lambda i: (i, 0))
    out_shape = [jax.ShapeDtypeStruct((m, n), norm_dtype)]
    if emit_residual:
        out_shape.insert(0, jax.ShapeDtypeStruct((m, n), F32))
    in_specs = [pl.BlockSpec((tm, kh), lambda i: (i, a0_blk)),
                pl.BlockSpec((tm, kh), lambda i: (i, a1_blk)),
                pl.BlockSpec(w.shape, lambda i: (0, 0), pipeline_mode=pl.Buffered(1)),
                row,
                pl.BlockSpec((1, n), lambda i: (0, 0))]
    out_specs = [row for _ in out_shape]
    args = [a0, a1, w, x2d, g.reshape(1, n)]
    if side is not None:
        side32, side_w, side_blk = side
        side_rows = side32.shape[0] // (m // tm)
        assert side_rows * (m // tm) == side32.shape[0] and side_rows % HALO == 0
        in_specs.append(pl.BlockSpec((side_rows, side_w), lambda i: (i, side_blk)))
        out_specs.append(pl.BlockSpec((side_rows, side_w), lambda i: (i, 0)))
        out_shape.append(jax.ShapeDtypeStruct((side32.shape[0], side_w), BF16))
        args.append(side32)
    return pl.pallas_call(
        functools.partial(_out_proj_norm_kernel, kh=kh, emit_residual=emit_residual,
                          has_side=side is not None),
        out_shape=tuple(out_shape),
        grid=(m // tm,),
        in_specs=in_specs,
        out_specs=tuple(out_specs),
        compiler_params=_cparams(1),
        name=name,
    )(*args)


def _sgu_v_kernel(h_ref, w_ref, v_ref, mu_ref, rstd_ref, c_ref, s1_ref, s2_ref):
    tm, tn = v_ref.shape
    h = h_ref[...]
    for c0 in range(0, tn, V_SUB):
        v = jnp.dot(h, w_ref[:, c0:c0 + V_SUB], preferred_element_type=F32)
        v_ref[:, c0:c0 + V_SUB] = v

        if c0 == 0:
            c_ref[...] = jnp.broadcast_to(jnp.sum(v, axis=-1, keepdims=True) / V_SUB, c_ref.shape)
            s1_ref[...] = jnp.zeros(s1_ref.shape, F32)
            s2_ref[...] = jnp.zeros(s2_ref.shape, F32)

        c = c_ref[...]
        s1, s2 = s1_ref[...], s2_ref[...]
        for r in range(V_SUB // LANES):
            d = v[:, r * LANES:(r + 1) * LANES] - c
            s1 = s1 + d
            s2 = s2 + d * d
        s1_ref[...] = s1
        s2_ref[...] = s2

    mean_d = jnp.sum(s1, axis=-1, keepdims=True) / tn
    var = jnp.sum(s2, axis=-1, keepdims=True) / tn - mean_d * mean_d
    mu_ref[...] = c + mean_d
    rstd_ref[...] = jnp.broadcast_to(lax.rsqrt(var + EPS), rstd_ref.shape)


def sgu_v(h, wb, tm=512):
    m, k = h.shape
    tn = wb.shape[1]
    stat = jax.ShapeDtypeStruct((m, LANES), F32)
    stat_spec = pl.BlockSpec((tm, LANES), lambda i: (i, 0))
    return pl.pallas_call(
        _sgu_v_kernel,
        out_shape=(jax.ShapeDtypeStruct((m, tn), F32), stat, stat),
        grid=(m // tm,),
        in_specs=[pl.BlockSpec((tm, k), lambda i: (i, 0)),
                  pl.BlockSpec(wb.shape, lambda i: (0, 0), pipeline_mode=pl.Buffered(1))],
        out_specs=(pl.BlockSpec((tm, tn), lambda i: (i, 0)), stat_spec, stat_spec),
        scratch_shapes=[pltpu.VMEM((tm, LANES), F32)] * 3,
        compiler_params=_cparams(1),
        name="sgu_v",
    )(h, wb)


def _sgu_gate_kernel(h_ref, wu32_ref, wz32_ref, v_ref, mu_ref, rstd_ref, lg_ref, lb_ref, ws_ref, bs_ref,
                     side32_ref, o_ref, side_ref, wu_ref, wz_ref):
    _cast_once([wu32_ref, wz32_ref], [wu_ref, wz_ref])
    side_ref[...] = side32_ref[...].astype(BF16)
    h = h_ref[...]
    tm, tn = v_ref.shape
    reps = GROUP_W // LANES
    mu = jnp.concatenate([mu_ref[...]] * reps, axis=1)
    rstd = jnp.concatenate([rstd_ref[...]] * reps, axis=1)
    tri = (lax.broadcasted_iota(jnp.int32, (CHUNK, CHUNK), 1)
           <= lax.broadcasted_iota(jnp.int32, (CHUNK, CHUNK), 0))
    for gi in range(tn // GROUP_W):
        cols = slice(gi * GROUP_W, (gi + 1) * GROUP_W)
        u = jnp.dot(h, wu_ref[:, cols], preferred_element_type=F32)
        zg = jnp.dot(h, wz_ref[:, cols], preferred_element_type=F32)
        vn = ((v_ref[:, cols] - mu) * rstd * lg_ref[:, cols] + lb_ref[:, cols]).astype(BF16)
        wsm = jnp.where(tri, ws_ref[gi], 0.0).astype(BF16)
        bsg = bs_ref[gi]
        mix = jnp.concatenate(
            [jnp.dot(wsm, vn[c * CHUNK:(c + 1) * CHUNK], preferred_element_type=F32) + bsg
             for c in range(tm // CHUNK)], axis=0)
        o_ref[:, cols] = (u * mix * (zg * jax.nn.sigmoid(zg))).astype(o_ref.dtype)


def sgu_gate(h, w, col_u, col_z, v, mu, rstd, ln_g, ln_b, w_s, b_s, side, tm=1024, tn=512):
    m, k = h.shape
    n = SGU_W
    nj, ni = n // tn, m // tm
    ju, jz = col_u // tn, col_z // tn
    gpt = tn // GROUP_W
    stat_spec = pl.BlockSpec((tm, LANES), lambda j, i: (i, 0))
    side_rows = side.shape[0] // (nj * ni)
    assert side_rows * nj * ni == side.shape[0] and side_rows % HALO == 0
    side_spec = pl.BlockSpec((side_rows, side.shape[1]), lambda j, i: (j * ni + i, 0))
    return pl.pallas_call(
        _sgu_gate_kernel,
        out_shape=(jax.ShapeDtypeStruct((m, n), BF16), jax.ShapeDtypeStruct(side.shape, BF16)),
        grid=(nj, ni),
        in_specs=[pl.BlockSpec((tm, k), lambda j, i: (i, 0)),
                  pl.BlockSpec((k, tn), lambda j, i: (0, ju + j)),
                  pl.BlockSpec((k, tn), lambda j, i: (0, jz + j)),
                  pl.BlockSpec((tm, tn), lambda j, i: (i, j)),
                  stat_spec, stat_spec,
                  pl.BlockSpec((1, tn), lambda j, i: (0, j)),
                  pl.BlockSpec((1, tn), lambda j, i: (0, j)),
                  pl.BlockSpec((gpt, CHUNK, CHUNK), lambda j, i: (j, 0, 0)),
                  pl.BlockSpec((gpt, CHUNK, 1), lambda j, i: (j, 0, 0)),
                  side_spec],
        out_specs=(pl.BlockSpec((tm, tn), lambda j, i: (i, j)), side_spec),
        scratch_shapes=[pltpu.VMEM((k, tn), BF16)] * 2,
        compiler_params=_cparams(2),
        name="sgu_gate",
    )(h, w, w, v, mu, rstd, ln_g.reshape(1, n), ln_b.reshape(1, n), w_s, b_s.reshape(N_GROUPS, CHUNK, 1),
      side)


def _rotary_tables(seq):
    half = ROT_DIM // 2
    inv_freq = jnp.power(ROPE_THETA, -jnp.arange(half, dtype=F32) * 2.0 / ROT_DIM)
    ang = jnp.arange(seq).astype(F32)[:, None] * inv_freq[None, :]
    cos, sin = jnp.cos(ang), jnp.sin(ang)
    rest = HEAD_DIM - ROT_DIM
    c = jnp.concatenate([cos, cos, jnp.ones((seq, rest), F32)], axis=1)
    sa = jnp.concatenate([jnp.zeros((seq, half), F32), sin, jnp.zeros((seq, rest), F32)], axis=1)
    sb = jnp.concatenate([-sin, jnp.zeros((seq, half + rest), F32)], axis=1)
    rot_t = jnp.concatenate([cos, cos, -sin, sin], axis=1).T
    return (c, sa, sb), rot_t


def kernel(x, norm_even, w_in_even, conv_w, cmp_k_pos, cmp_k_w1, cmp_k_b1, cmp_k_w2, cmp_v_pos, cmp_v_w1, cmp_v_b1, cmp_v_w2, w_out_even, norm_odd, w_in_odd, sgu_ln_g, sgu_ln_b, sgu_w_s, sgu_b_s, w_out_odd, norm_final):
    batch, seq, d = x.shape
    m = batch * seq
    x2d = x.reshape(m, d)

    wt_in = jnp.swapaxes(w_in_even[0], 0, 1)
    cw = CONV_W
    qw = N_HEADS * HEAD_DIM
    kvw = N_KV * HEAD_DIM
    o_q = 4 * cw
    o_kv = o_q + qw
    o_gl = o_kv + 6 * kvw
    o_ng = o_gl + 3 * N_HEADS

    h0, gl = rmsnorm_proj(x2d, norm_even[0], wt_in, o_gl, LANES)
    y_conv, w_out0 = conv_proj(h0, wt_in, conv_w[0], seq, w_out_even[0])
    z = matmul_t(h0, wt_in, [(o_q, qw + 6 * kvw), (o_ng, qw)], tn=1024, name="nsa_proj")
    ng_col = (qw + 6 * kvw) // (2 * GQA * HEAD_DIM)

    tabs, rot_t = _rotary_tables(seq)
    cmp_k = (cmp_k_pos[0], cmp_k_w1[0].reshape(CMP_LEN, HEAD_DIM, HEAD_DIM),
             cmp_k_b1[0].reshape(1, HEAD_DIM), cmp_k_w2[0])
    cmp_v = (cmp_v_pos[0], cmp_v_w1[0].reshape(CMP_LEN, HEAD_DIM, HEAD_DIM),
             cmp_v_b1[0].reshape(1, HEAD_DIM), cmp_v_w2[0])
    prep = nsa_prep(z, tabs, cmp_k, cmp_v, batch, seq, col0=qw // HEAD_DIM)
    y_nsa = nsa_attention_pair(z, ng_col, gl, rot_t, prep, batch, seq)

    w_in1 = w_in_odd[0]
    x1, h1, w_v = out_proj_norm(y_conv, 0, y_nsa, 0, w_out0, x2d, norm_odd[0],
                                BF16, True, side=(w_in1, SGU_W, 1), name="out_proj_even")

    v, mu, rstd = sgu_v(h1, w_v)
    act, w_out1 = sgu_gate(h1, w_in1, 0, 2 * SGU_W, v, mu, rstd,
                           sgu_ln_g[0], sgu_ln_b[0], sgu_w_s[0], sgu_b_s[0], w_out_odd[0])
    (out,) = out_proj_norm(act, 0, act, 1, w_out1, x1, norm_final,
                           F32, False, name="out_proj_odd")
    return out.reshape(batch, seq, d)
```
